```python
import math
import jax
import jax.numpy as jnp
from jax import lax
import numpy as np

D_MODEL = 1024
BATCH = 8
SEQ = 2048
DEPTH = 2
DEC_BATCH = 128
DEC_SEQ = 4
PAST_LEN = 16384
PAGE_SIZE = 128

N_META = 16
N_BRANCH = 4
N_HEADS = 4
BRANCH_W = D_MODEL // N_BRANCH
HEAD_V = BRANCH_W // N_HEADS
HGRN_DK = HEAD_V
RWKV_N = HEAD_V
GLA_DK = HEAD_V // 2
MLSTM_DK = HEAD_V // 2
RWKV_W_LORA = D_MODEL // 16
RWKV_A_LORA = D_MODEL // 16
RWKV_G_LORA = D_MODEL // 8
RWKV_LN_EPS = 64e-5
GLA_GATE_RANK = 16
GLA_GATE_TAU = 16.0
CONV_W = 4
D_FF = 4 * D_MODEL
CHUNK = 64
EPS = 1e-6

HG_SIZES = (N_HEADS * HGRN_DK, N_HEADS * HGRN_DK, BRANCH_W, BRANCH_W)
RW_SIZES = (BRANCH_W, BRANCH_W, BRANCH_W, RWKV_W_LORA, RWKV_A_LORA, RWKV_G_LORA)
GL_SIZES = (N_HEADS * GLA_DK, N_HEADS * GLA_DK, BRANCH_W, GLA_GATE_RANK, BRANCH_W)
ML_SIZES = (2 * N_HEADS * MLSTM_DK, BRANCH_W, N_HEADS, N_HEADS, BRANCH_W)
RW_W = sum(RW_SIZES)
IN_SIZES = (sum(HG_SIZES), RW_W, sum(GL_SIZES), sum(ML_SIZES), N_BRANCH * D_MODEL)
IN_COLS = sum(IN_SIZES)

kernel_name = 'hybrid_hgrn2_rwkv7_gla_mlstm_step'


def _rms(x, g):
    xf = x.astype(jnp.float32)
    y = xf * lax.rsqrt(jnp.mean(xf * xf, axis=-1, keepdims=True) + EPS)
    return (y * g.astype(jnp.float32)).astype(x.dtype)


def _heads(t):
    return t.reshape(t.shape[:-1] + (N_HEADS, t.shape[-1] // N_HEADS))


def _head_rms(o, g):
    y = o * lax.rsqrt(jnp.mean(o * o, axis=-1, keepdims=True) + EPS)
    return y.reshape(y.shape[:-2] + (-1,)) * g


def _head_ln(o, w, b):
    mu = jnp.mean(o, axis=-1, keepdims=True)
    d = o - mu
    y = d * lax.rsqrt(jnp.mean(d * d, axis=-1, keepdims=True) + RWKV_LN_EPS)
    return y.reshape(y.shape[:-2] + (-1,)) * w + b


def _split(z, sizes):
    return jnp.split(z, np.cumsum(sizes)[:-1].tolist(), axis=-1)


def _to_chunks(t, c):
    b, T = t.shape[:2]
    return jnp.moveaxis(t.reshape((b, T // c, c) + t.shape[2:]), 1, 0)


def _from_chunks(t):
    n, b, c = t.shape[:3]
    return jnp.moveaxis(t, 0, 1).reshape((b, n * c) + t.shape[3:])


def _gla_core(seqs, s0):
    T = seqs[0].shape[1]
    c = math.gcd(T, CHUNK)
    mask = jnp.tril(jnp.ones((c, c), dtype=bool))[None, :, :, None, None]

    def step(s, inp):
        qc, kc, vc, ac = inp
        b = jnp.cumsum(ac, axis=1)
        decay = jnp.exp(jnp.where(mask, b[:, :, None] - b[:, None], -jnp.inf))
        att = jnp.einsum('bthk,bshk,btshk->bhts', qc, kc, decay)
        o = jnp.einsum('bhts,bshv->bthv', att, vc) + jnp.einsum('bthk,bhkv->bthv', qc * jnp.exp(b), s)
        bl = b[:, -1]
        s = s * jnp.exp(bl)[..., None] + jnp.einsum('bshk,bshv->bhkv', kc * jnp.exp(bl[:, None] - b), vc)
        return s, o

    s, o = lax.scan(step, s0, tuple(_to_chunks(t, c) for t in seqs))
    return _from_chunks(o), s


def _mlstm_core(seqs, state):
    T = seqs[0].shape[1]
    c = math.gcd(T, CHUNK)
    mask = jnp.tril(jnp.ones((c, c), dtype=bool))[None, :, :, None]

    def step(carry, inp):
        cm, nm, mm = carry
        qc, kc, vc, ic, fc = inp
        b = jnp.cumsum(fc, axis=1)
        dmat = jnp.where(mask, b[:, :, None] - b[:, None] + ic[:, None], -jnp.inf)
        inter = b + mm[:, None]
        m_t = jnp.maximum(inter, jnp.max(dmat, axis=2))
        wts = jnp.exp(dmat - m_t[:, :, None])
        sc = jnp.exp(inter - m_t)
        qk = jnp.einsum('bthk,bshk->btsh', qc, kc) * wts
        num = jnp.einsum('btsh,bshv->bthv', qk, vc) + sc[..., None] * jnp.einsum('bthk,bhkv->bthv', qc, cm)
        den = jnp.sum(qk, axis=2) + sc * jnp.einsum('bthk,bhk->bth', qc, nm)
        hc = num / jnp.maximum(jnp.abs(den), jnp.exp(-m_t))[..., None]
        m_new = m_t[:, -1]
        wl = jnp.exp(b[:, -1:] - b + ic - m_new[:, None])
        sl = jnp.exp(b[:, -1] + mm - m_new)
        cm = sl[..., None, None] * cm + jnp.einsum('bsh,bshk,bshv->bhkv', wl, kc, vc)
        nm = sl[..., None] * nm + jnp.einsum('bsh,bshk->bhk', wl, kc)
        return (cm, nm, m_new), hc

    st, o = lax.scan(step, state, tuple(_to_chunks(t, c) for t in seqs))
    return _from_chunks(o), st


def _run_chunked(core, seqs, state, lead):
    if lead:
        o1, state = core(tuple(t[:, :lead] for t in seqs), state)
        o2, state = core(tuple(t[:, lead:] for t in seqs), state)
        return jnp.concatenate([o1, o2], axis=1), state
    return core(seqs, state)


def _rwkv_scan(seqs, s0):
    def step(s, inp):
        rt, lwt, kt, vt, kkt, at = inp
        sa = jnp.einsum('bhvk,bhk->bhv', s, kkt)
        s = (s * jnp.exp(lwt)[:, :, None, :] - sa[..., None] * (kkt * at)[:, :, None, :]
             + vt[..., None] * kt[:, :, None, :])
        return s, jnp.einsum('bhvk,bhk->bhv', s, rt)

    s, ys = lax.scan(step, s0, tuple(jnp.moveaxis(t, 1, 0) for t in seqs))
    return jnp.moveaxis(ys, 0, 1), s


def _mixer(h, st, lead, lb, p):
    f32 = jnp.float32
    s_hg, s_rw, shift, s_gl, c_ml, n_ml, m_ml, conv_buf = st
    B, T, _ = h.shape
    z = h @ p['w_in']
    zh, zr, zg, zm, zgate = _split(z, IN_SIZES)

    hq, hf, hi, hg = _split(zh.astype(f32), HG_SIZES)
    log_f = jnp.logaddexp(jnp.log(lb), jnp.log1p(-lb) + jax.nn.log_sigmoid(hf))
    k_hg = (1.0 - lb) * jax.nn.sigmoid(-hf)
    o_hg, s_hg = _run_chunked(_gla_core, (_heads(hq), _heads(k_hg), _heads(hi), _heads(log_f)),
                              s_hg.astype(f32), lead)
    o_hg = _head_rms(o_hg, p['hgrn_norm']) * jax.nn.silu(hg)

    zr32 = zr.astype(f32)
    prev = jnp.concatenate([shift.astype(f32)[:, None], zr32[:, :-1]], axis=1)
    new_shift = zr32[:, -1]
    zs = zr32 + (prev - zr32) * p['rwkv_mu']
    r, k, v, wd, ad, gd = _split(zs, RW_SIZES)
    w = -jax.nn.softplus(-(p['rwkv_w0'] + jnp.tanh(wd) @ p['rwkv_w_up'])) - 0.5
    log_w = -jnp.exp(w)
    a = jax.nn.sigmoid(p['rwkv_a0'] + ad @ p['rwkv_a_up'])
    g = jax.nn.sigmoid(gd) @ p['rwkv_g_up']
    kk = _heads(k * p['rwkv_k_k'])
    kk = kk / jnp.maximum(jnp.linalg.norm(kk, axis=-1, keepdims=True), 1e-12)
    k = k * (1.0 + (a - 1.0) * p['rwkv_k_a'])
    rh, kh, vh = _heads(r), _heads(k), _heads(v)
    y_rw, s_rw = _rwkv_scan((rh, _heads(log_w), kh, vh, kk, _heads(a)), s_rw.astype(f32))
    bonus = jnp.sum(rh * kh * _heads(p['rwkv_r_k']), axis=-1, keepdims=True) * vh
    o_rw = (_head_ln(y_rw, p['rwkv_ln_w'], p['rwkv_ln_b']) + bonus.reshape(B, T, -1)) * g

    gq, gk, gv, ga, gg = _split(zg.astype(f32), GL_SIZES)
    log_a = jax.nn.log_sigmoid(ga @ p['gla_gate_up'] + p['gla_gate_b']) / GLA_GATE_TAU
    o_gl, s_gl = _run_chunked(_gla_core, (_heads(gq) * GLA_DK ** -0.5, _heads(gk), _heads(gv), _heads(log_a)),
                              s_gl.astype(f32), lead)
    o_gl = _head_rms(o_gl, p['gla_norm']) * jax.nn.silu(gg)

    mqk, mv, mi, mf, mo = _split(zm.astype(f32), ML_SIZES)
    full = jnp.concatenate([conv_buf.astype(f32), mqk], axis=1)
    new_buf = full[:, T:]
    conv = p['mlstm_conv_b'] + full[:, 0:T] * p['mlstm_conv_w'][0]
    for j in range(1, CONV_W):
        conv = conv + full[:, j:j + T] * p['mlstm_conv_w'][j]
    mq, mk = jnp.split(jax.nn.silu(conv), 2, axis=-1)
    ig = mi + p['mlstm_i_b']
    lf = jax.nn.log_sigmoid(mf + p['mlstm_f_b'])
    o_ml, (c_ml, n_ml, m_ml) = _run_chunked(
        _mlstm_core, (_heads(mq), _heads(mk) * MLSTM_DK ** -0.5, _heads(mv), ig, lf),
        (c_ml.astype(f32), n_ml.astype(f32), m_ml.astype(f32)), lead)
    o_ml = _head_rms(o_ml, p['mlstm_norm']) * jax.nn.sigmoid(mo)

    outs = jnp.stack([o_hg, o_rw, o_gl, o_ml], axis=2).astype(h.dtype)
    proj = jnp.einsum('btnc,ncd->btnd', outs, p['w_branch'])
    gates = jax.nn.sigmoid(zgate).reshape(B, T, N_BRANCH, D_MODEL)
    mix = jnp.sum(proj * gates, axis=2) @ p['w_out']
    return mix, (s_hg, s_rw, new_shift, s_gl, c_ml, n_ml, m_ml, new_buf)


def _trunk(x, init, lead, lb_all, P):
    new = [[] for _ in init]
    for l in range(DEPTH):
        p = {name: arr[l] for name, arr in P.items()}
        mix, st = _mixer(_rms(x, p['norm_mix']), tuple(s[l] for s in init), lead, lb_all[l], p)
        x = x + mix.astype(x.dtype)
        hn = _rms(x, p['norm_mlp'])
        x = x + jnp.square(jax.nn.relu(hn @ p['w_up'])) @ p['w_down']
        for lst, s in zip(new, st):
            lst.append(s)
    return x, [jnp.stack(lst).astype(x.dtype) for lst in new]


def setup_inputs(seed: int = 0) -> dict:
    key = jax.random.key(seed)
    ks = iter(jax.random.split(key, 64))
    f32 = jnp.float32
    L, H = DEPTH, N_HEADS

    def nrm(shape, scale):
        return jax.random.normal(next(ks), shape, f32) * scale

    return {
        'x_prompt': nrm((BATCH, SEQ, D_MODEL), 1.0),
        'x_sample': nrm((DEC_BATCH, DEC_SEQ, D_MODEL), 1.0),
        'state_hgrn': nrm((L, DEC_BATCH, H, HGRN_DK, HEAD_V), 0.3),
        'state_rwkv': nrm((L, DEC_BATCH, H, RWKV_N, RWKV_N), 0.3),
        'state_rwkv_shift': nrm((L, DEC_BATCH, RW_W), 1.0),
        'state_gla': nrm((L, DEC_BATCH, H, GLA_DK, HEAD_V), 0.3),
        'state_mlstm_c': nrm((L, DEC_BATCH, H, MLSTM_DK, HEAD_V), 0.3),
        'state_mlstm_n': nrm((L, DEC_BATCH, H, MLSTM_DK), 0.3),
        'state_mlstm_m': 1.0 + nrm((L, DEC_BATCH, H), 0.5),
        'state_mlstm_conv': nrm((L, DEC_BATCH, CONV_W - 1, 2 * H * MLSTM_DK), 1.0),
        'meta_tokens': nrm((N_META, D_MODEL), 1.0),
        'norm_mix': 1.0 + nrm((L, D_MODEL), 0.05),
        'norm_mlp': 1.0 + nrm((L, D_MODEL), 0.05),
        'norm_final': 1.0 + nrm((D_MODEL,), 0.05),
        'w_in': nrm((L, D_MODEL, IN_COLS), D_MODEL ** -0.5),
        'hgrn_lb': nrm((L, H * HGRN_DK), 0.5),
        'hgrn_norm': 1.0 + nrm((L, BRANCH_W), 0.05),
        'rwkv_mu': jax.random.uniform(next(ks), (L, RW_W), f32),
        'rwkv_w0': nrm((L, BRANCH_W), 0.5) - 0.5,
        'rwkv_w_up': nrm((L, RWKV_W_LORA, BRANCH_W), 0.5 * RWKV_W_LORA ** -0.5),
        'rwkv_a0': nrm((L, BRANCH_W), 0.5),
        'rwkv_a_up': nrm((L, RWKV_A_LORA, BRANCH_W), 0.5 * RWKV_A_LORA ** -0.5),
        'rwkv_g_up': nrm((L, RWKV_G_LORA, BRANCH_W), RWKV_G_LORA ** -0.5),
        'rwkv_k_k': 0.85 + nrm((L, BRANCH_W), 0.05),
        'rwkv_k_a': 1.0 + nrm((L, BRANCH_W), 0.05),
        'rwkv_r_k': nrm((L, BRANCH_W), 0.1),
        'rwkv_ln_w': 1.0 + nrm((L, BRANCH_W), 0.05),
        'rwkv_ln_b': nrm((L, BRANCH_W), 0.02),
        'gla_gate_up': nrm((L, GLA_GATE_RANK, H * GLA_DK), GLA_GATE_RANK ** -0.5),
        'gla_gate_b': 1.0 + nrm((L, H * GLA_DK), 1.0),
        'gla_norm': 1.0 + nrm((L, BRANCH_W), 0.05),
        'mlstm_conv_w': nrm((L, CONV_W, 2 * H * MLSTM_DK), CONV_W ** -0.5),
        'mlstm_conv_b': nrm((L, 2 * H * MLSTM_DK), 0.02),
        'mlstm_i_b': nrm((L, H), 0.1) - 1.0,
        'mlstm_f_b': 3.0 + nrm((L, H), 0.5),
        'mlstm_norm': 1.0 + nrm((L, BRANCH_W), 0.05),
        'w_branch': nrm((L, N_BRANCH, BRANCH_W, D_MODEL), BRANCH_W ** -0.5),
        'w_out': nrm((L, D_MODEL, D_MODEL), D_MODEL ** -0.5),
        'w_up': nrm((L, D_MODEL, D_FF), D_MODEL ** -0.5),
        'w_down': nrm((L, D_FF, D_MODEL), 0.5 * D_FF ** -0.5),
    }


def reference(x_prompt, x_sample, state_hgrn, state_rwkv, state_rwkv_shift, state_gla,
              state_mlstm_c, state_mlstm_n, state_mlstm_m, state_mlstm_conv,
              meta_tokens, norm_mix, norm_mlp, norm_final, w_in, hgrn_lb, hgrn_norm,
              rwkv_mu, rwkv_w0, rwkv_w_up, rwkv_a0, rwkv_a_up, rwkv_g_up, rwkv_k_k, rwkv_k_a,
              rwkv_r_k, rwkv_ln_w, rwkv_ln_b, gla_gate_up, gla_gate_b, gla_norm,
              mlstm_conv_w, mlstm_conv_b, mlstm_i_b, mlstm_f_b, mlstm_norm,
              w_branch, w_out, w_up, w_down):
    P = {
        'norm_mix': norm_mix, 'norm_mlp': norm_mlp, 'w_in': w_in, 'hgrn_norm': hgrn_norm,
        'rwkv_mu': rwkv_mu, 'rwkv_w0': rwkv_w0, 'rwkv_w_up': rwkv_w_up, 'rwkv_a0': rwkv_a0,
        'rwkv_a_up': rwkv_a_up, 'rwkv_g_up': rwkv_g_up, 'rwkv_k_k': rwkv_k_k, 'rwkv_k_a': rwkv_k_a,
        'rwkv_r_k': rwkv_r_k, 'rwkv_ln_w': rwkv_ln_w, 'rwkv_ln_b': rwkv_ln_b,
        'gla_gate_up': gla_gate_up, 'gla_gate_b': gla_gate_b, 'gla_norm': gla_norm,
        'mlstm_conv_w': mlstm_conv_w, 'mlstm_conv_b': mlstm_conv_b, 'mlstm_i_b': mlstm_i_b,
        'mlstm_f_b': mlstm_f_b, 'mlstm_norm': mlstm_norm,
        'w_branch': w_branch, 'w_out': w_out, 'w_up': w_up, 'w_down': w_down,
    }
    lb_cs = jnp.cumsum(jax.nn.softmax(hgrn_lb.astype(jnp.float32), axis=0), axis=0)
    lb_all = lb_cs - lb_cs[:1]

    sample_states = (state_hgrn, state_rwkv, state_rwkv_shift, state_gla,
                     state_mlstm_c, state_mlstm_n, state_mlstm_m, state_mlstm_conv)

    B = x_prompt.shape[0]
    meta = jnp.broadcast_to(meta_tokens.astype(x_prompt.dtype)[None], (B, N_META, D_MODEL))
    xp = jnp.concatenate([meta, x_prompt], axis=1)
    init_p = [jnp.zeros((DEPTH, B) + s.shape[2:], x_prompt.dtype) for s in sample_states]
    hp, st_p = _trunk(xp, init_p, N_META, lb_all, P)
    y_prompt = _rms(hp[:, N_META:], norm_final)

    hs, st_s = _trunk(x_sample, list(sample_states), 0, lb_all, P)
    y_sample = _rms(hs, norm_final)

    hgrn_p, rwkv_p, shift_p, gla_p, mc_p, mn_p, mm_p, conv_p = st_p
    hgrn_s, rwkv_s, shift_s, gla_s, mc_s, mn_s, mm_s, conv_s = st_s
    return (y_prompt, y_sample,
            hgrn_p, rwkv_p, shift_p, gla_p, mc_p, mn_p, mm_p, conv_p,
            hgrn_s, rwkv_s, shift_s, gla_s, mc_s, mn_s, mm_s, conv_s)
```

```python
import functools

import numpy as np
import jax
import jax.numpy as jnp
from jax import lax
from jax.experimental import pallas as pl
from jax.experimental.pallas import tpu as pltpu

F32 = jnp.float32
BF16 = jnp.bfloat16

D_MODEL = 1024
N_META = 16
N_HEADS = 4
BRANCH_W = 256
HEAD_V = 64
GLA_DK = 32
MLSTM_DK = 32
GLA_GATE_TAU = 16.0
RWKV_LN_EPS = 64e-5
CONV_W = 4
D_FF = 4 * D_MODEL
EPS = 1e-6

Z_COLS = 8192
SEC = 1024
SUB = 16
SAMPLE_PAD = 8
ROW_PAD = 512
VMEM_LIMIT = 48 * 1024 * 1024


def _cparams(*sem):
    return pltpu.CompilerParams(dimension_semantics=sem, vmem_limit_bytes=VMEM_LIMIT)


def _pick_tile(n, max_tile, mult=8):
    best = None
    t = mult
    while t <= min(n, max_tile):
        if n % t == 0:
            best = t
        t += mult
    assert best is not None, (n, max_tile, mult)
    return best


def _split2(x):
    hi = x.astype(BF16)
    lo = (x - hi.astype(F32)).astype(BF16)
    return hi, lo


def _split3(x):
    hi = x.astype(BF16)
    r1 = x - hi.astype(F32)
    mid = r1.astype(BF16)
    lo = (r1 - mid.astype(F32)).astype(BF16)
    return hi, mid, lo


def _dot(a, b):
    return jnp.dot(a, b, preferred_element_type=F32)


def _dot_nt(a, b):
    return lax.dot_general(a, b, (((1,), (1,)), ((), ())), preferred_element_type=F32)


def _dot_tn(a, b):
    return lax.dot_general(a, b, (((0,), (0,)), ((), ())), preferred_element_type=F32)


def _seg_dot(x, m_bf16, parts):
    ps = _split2(x) if parts == 2 else _split3(x)
    acc = _dot(ps[0], m_bf16)
    for p in ps[1:]:
        acc = acc + _dot(p, m_bf16)
    return acc


def _seg_dot_l(m_bf16, x, parts):
    ps = _split2(x) if parts == 2 else _split3(x)
    acc = _dot(m_bf16, ps[0])
    for p in ps[1:]:
        acc = acc + _dot(m_bf16, p)
    return acc


def _log_sigmoid(x):
    return jnp.minimum(x, 0.0) - jnp.log1p(jnp.exp(-jnp.abs(x)))


def _silu(x):
    return x * jax.nn.sigmoid(x)


def _tri(n, upper=False):
    r = lax.broadcasted_iota(jnp.int32, (n, n), 0)
    c = lax.broadcasted_iota(jnp.int32, (n, n), 1)
    return (r <= c) if upper else (r >= c)


def _in_proj_kernel(x_ref, g_ref, w_ref, z_ref, h_scr):
    @pl.when(pl.program_id(1) == 0)
    def _():
        x = x_ref[...]
        y = x * lax.rsqrt(jnp.mean(x * x, axis=-1, keepdims=True) + EPS) * g_ref[...]
        h_scr[...] = y.astype(BF16)

    z_ref[...] = _dot(h_scr[...], w_ref[...])


def _in_proj(x, g, w, tm):
    n = x.shape[0]
    tn = 1024
    return pl.pallas_call(
        _in_proj_kernel,
        grid=(n // tm, Z_COLS // tn),
        in_specs=[
            pl.BlockSpec((tm, D_MODEL), lambda i, j: (i, 0)),
            pl.BlockSpec((1, D_MODEL), lambda i, j: (0, 0)),
            pl.BlockSpec((D_MODEL, tn), lambda i, j: (0, j)),
        ],
        out_specs=pl.BlockSpec((tm, tn), lambda i, j: (i, j)),
        out_shape=jax.ShapeDtypeStruct((n, Z_COLS), F32),
        scratch_shapes=[pltpu.VMEM((tm, D_MODEL), BF16)],
        compiler_params=_cparams("parallel", "arbitrary"),
    )(x, g, w)


def _gla_kernel(z_ref, par_ref, gup_ref, okv_ref, ovv_ref, mask_ref, st_in_ref, o_prev_ref,
                o_ref, st_out_ref, st_scr, k_scr, b_scr, *, mode, c, nsub, tv, dk):
    del o_prev_ref
    step = pl.program_id(1)
    voff = 512 if mode == "hgrn" else 256

    @pl.when(step == 0)
    def _():
        st = st_in_ref[0]
        st_scr[...] = jnp.concatenate([st] * N_HEADS, axis=1) * mask_ref[...]

    rowi = lax.broadcasted_iota(jnp.int32, (c, 1), 0)
    valid = rowi < tv
    tril = _tri(c).astype(BF16)

    def sub_block(j, carry):
        r0 = pl.multiple_of(j * c, c)
        rows = pl.ds(r0, c)
        if mode == "hgrn":
            q = z_ref[rows, 0:256]
            hf = z_ref[rows, 256:512]
            g = z_ref[rows, 768:1024]
            a = par_ref[0:1, :]
            cc = par_ref[1:2, :] + _log_sigmoid(hf)
            loga = jnp.maximum(a, cc) + jnp.log1p(jnp.exp(-jnp.abs(a - cc)))
            k = par_ref[2:3, :] * jax.nn.sigmoid(-hf)
        else:
            q = z_ref[rows, 0:128] * (GLA_DK ** -0.5)
            k = z_ref[rows, 128:256]
            ga = z_ref[rows, 512:640]
            g = z_ref[rows, 640:896]
            gl = _dot(ga.astype(BF16), gup_ref[...]) + par_ref[0:1, 0:128]
            loga = _log_sigmoid(gl) * (1.0 / GLA_GATE_TAU)
        v = z_ref[rows, voff:voff + 256]
        if tv < c:
            loga = jnp.where(valid, loga, 0.0)
            k = jnp.where(valid, k, 0.0)
            v = jnp.where(valid, v, 0.0)
        b = _seg_dot_l(tril, loga, 3)
        k_scr[...] = k
        b_scr[...] = b

        o_inter = _dot_nt((q * jnp.exp(b)).astype(BF16), st_scr[...].astype(BF16))

        def diag(s, acc):
            ks = k_scr[pl.ds(s, 1), :]
            bs = b_scr[pl.ds(s, 1), :]
            vs = z_ref[pl.ds(r0 + s, 1), voff:voff + 256]
            e = jnp.exp(jnp.where(rowi >= s, b - bs, -jnp.inf))
            p = q * e * ks
            return acc + _dot(p.astype(BF16), okv_ref[...]) * vs

        o = lax.fori_loop(0, tv, diag, o_inter)

        bl = b[c - 1:c, :]
        kdec = k * jnp.exp(bl - b)
        upd = _dot_tn(v.astype(BF16), kdec.astype(BF16))
        st_scr[...] = st_scr[...] * jnp.exp(bl) + upd * mask_ref[...]

        ms = _seg_dot(o * o, ovv_ref[...], 2) * (1.0 / HEAD_V)
        o_ref[rows, :] = o * lax.rsqrt(ms + EPS) * par_ref[3:4, :] * _silu(g)
        return carry

    lax.fori_loop(0, nsub, sub_block, 0)

    @pl.when(step == pl.num_programs(1) - 1)
    def _():
        st = st_scr[...]
        acc = st[:, 0:dk]
        for h in range(1, N_HEADS):
            acc = acc + st[:, h * dk:(h + 1) * dk]
        st_out_ref[0] = acc


def _gla_call(z, sec, par, gup, okv, ovv, mask, st_in, o_prev, *, mode, dk, nrows, base_blk, nb, nsteps, c, nsub, tv):
    rblk = c * nsub
    hk = N_HEADS * dk
    kern = functools.partial(_gla_kernel, mode=mode, c=c, nsub=nsub, tv=tv, dk=dk)
    const = lambda b, s: (0, 0)
    return pl.pallas_call(
        kern,
        grid=(nb, nsteps),
        in_specs=[
            pl.BlockSpec((rblk, SEC), lambda b, s: (base_blk + b * nsteps + s, sec)),
            pl.BlockSpec((8, 256), const),
            pl.BlockSpec((128, 128), const),
            pl.BlockSpec((hk, 256), const),
            pl.BlockSpec((256, 256), const),
            pl.BlockSpec((256, hk), const),
            pl.BlockSpec((1, 256, dk), lambda b, s: (b, 0, 0)),
            pl.BlockSpec(memory_space=pl.ANY),
        ],
        input_output_aliases={7: 0},
        out_specs=[
            pl.BlockSpec((rblk, 256), lambda b, s: (base_blk + b * nsteps + s, 0)),
            pl.BlockSpec((1, 256, dk), lambda b, s: (b, 0, 0)),
        ],
        out_shape=[
            jax.ShapeDtypeStruct((nrows, 256), F32),
            jax.ShapeDtypeStruct((nb, 256, dk), F32),
        ],
        scratch_shapes=[
            pltpu.VMEM((256, hk), F32),
            pltpu.VMEM((c, hk), F32),
            pltpu.VMEM((c, hk), F32),
        ],
        compiler_params=_cparams("parallel", "arbitrary"),
    )(z, par, gup, okv, ovv, mask, st_in, o_prev)


def _rwkv_prep_kernel(z_ref, mu_ref, par_ref, wup_ref, aup_ref, gup_ref, ovv_ref, shift_ref,
                      r_ref, w_ref, k_ref, v_ref, kk_ref, ka_ref, g_ref, bonus_ref, nshift_ref,
                      carry_scr, *, tv_last):
    step = pl.program_id(1)

    @pl.when(step == 0)
    def _():
        carry_scr[...] = shift_ref[0]

    zr = z_ref[...]
    nrow = zr.shape[0]
    rowi = lax.broadcasted_iota(jnp.int32, (nrow, 1), 0)
    prev = jnp.where(rowi == 0, carry_scr[...], pltpu.roll(zr, 1, axis=0))
    carry_scr[...] = zr[tv_last - 1:tv_last, :]
    zs = zr + (prev - zr) * mu_ref[...]
    r = zs[:, 0:256]
    k = zs[:, 256:512]
    v = zs[:, 512:768]
    lo = zs[:, 768:1024]
    w0, a0, k_k, k_a, r_k = (par_ref[i:i + 1, :] for i in range(5))
    wl = w0 + _dot(jnp.tanh(lo).astype(BF16), wup_ref[...])
    wexp = -(jnp.maximum(-wl, 0.0) + jnp.log1p(jnp.exp(-jnp.abs(wl)))) - 0.5
    wdec = jnp.exp(-jnp.exp(wexp))
    a = jax.nn.sigmoid(a0 + _dot(lo.astype(BF16), aup_ref[...]))
    g = _dot(jax.nn.sigmoid(lo).astype(BF16), gup_ref[...])
    kkp = k * k_k
    nrm = jnp.sqrt(_seg_dot(kkp * kkp, ovv_ref[...], 3))
    kk = kkp / jnp.maximum(nrm, 1e-12)
    k2 = k * (1.0 + (a - 1.0) * k_a)
    bonus = _seg_dot(r * k2 * r_k, ovv_ref[...], 3) * v
    r_ref[...] = r
    w_ref[...] = wdec
    k_ref[...] = k2
    v_ref[...] = v
    kk_ref[...] = kk
    ka_ref[...] = kk * a
    g_ref[...] = g
    bonus_ref[...] = bonus

    @pl.when(step == pl.num_programs(1) - 1)
    def _():
        nshift_ref[0] = carry_scr[...]


def _rwkv_prep_call(z, mu, par, wup, aup, gup, ovv, shift, *, nrows, base_blk, nb, nsteps, rblk, tv_last):
    kern = functools.partial(_rwkv_prep_kernel, tv_last=tv_last)
    const = lambda b, s: (0, 0)
    row_spec = pl.BlockSpec((rblk, 256), lambda b, s: (base_blk + b * nsteps + s, 0))
    return pl.pallas_call(
        kern,
        grid=(nb, nsteps),
        in_specs=[
            pl.BlockSpec((rblk, SEC), lambda b, s: (base_blk + b * nsteps + s, 1)),
            pl.BlockSpec((1, SEC), const),
            pl.BlockSpec((8, 256), const),
            pl.BlockSpec((256, 256), const),
            pl.BlockSpec((256, 256), const),
            pl.BlockSpec((256, 256), const),
            pl.BlockSpec((256, 256), const),
            pl.BlockSpec((1, 1, SEC), lambda b, s: (b, 0, 0)),
        ],
        out_specs=[row_spec] * 8 + [pl.BlockSpec((1, 1, SEC), lambda b, s: (b, 0, 0))],
        out_shape=[jax.ShapeDtypeStruct((nrows, 256), F32)] * 8 + [jax.ShapeDtypeStruct((nb, 1, SEC), F32)],
        scratch_shapes=[pltpu.VMEM((1, SEC), F32)],
        compiler_params=_cparams("parallel", "arbitrary"),
    )(z, mu, par, wup, aup, gup, ovv, shift)


def _rwkv_scan_kernel(tiles_ref, v_ref, g_ref, bonus_ref, par_ref, e_ref, ovv_ref, st_in_ref, o_prev_ref,
                      o_ref, st_out_ref, st_scr, y_scr, *, nsub, tsub, rblk):
    del o_prev_ref
    step = pl.program_id(1)

    @pl.when(step == 0)
    def _():
        st_scr[...] = st_in_ref[0]

    if tsub * nsub < rblk:
        y_scr[...] = jnp.zeros_like(y_scr)

    def sub_block(j, hstate):
        def token(t, hs):
            e = e_ref[t]
            tile = lambda q: _dot(tiles_ref[0, j, q], e)
            row = j * SUB + t
            sa = jnp.sum(hs * tile(3), axis=0, keepdims=True)
            hs = hs * tile(0) - tile(1) * sa + tile(2) * v_ref[pl.ds(row, 1), :]
            y_scr[pl.ds(row, 1), :] = jnp.sum(hs * tile(4), axis=0, keepdims=True)
            return hs

        return lax.fori_loop(0, tsub, token, hstate)

    st_scr[...] = lax.fori_loop(0, nsub, sub_block, st_scr[...])

    y = y_scr[...]
    mu = _seg_dot(y, ovv_ref[...], 3) * (1.0 / HEAD_V)
    d = y - mu
    var = _seg_dot(d * d, ovv_ref[...], 3) * (1.0 / HEAD_V)
    ln = d * lax.rsqrt(var + RWKV_LN_EPS) * par_ref[0:1, :] + par_ref[1:2, :]
    o_ref[...] = (ln + bonus_ref[...]) * g_ref[...]

    @pl.when(step == pl.num_programs(1) - 1)
    def _():
        st_out_ref[0] = st_scr[...]


def _rwkv_scan_call(tiles, v, g, bonus, par, e, ovv, st_in, o_prev, *, nrows, base_blk, nb, nsteps, nsub, tsub, rblk):
    kern = functools.partial(_rwkv_scan_kernel, nsub=nsub, tsub=tsub, rblk=rblk)
    const = lambda b, s: (0, 0)
    row_spec = pl.BlockSpec((rblk, 256), lambda b, s: (base_blk + b * nsteps + s, 0))
    return pl.pallas_call(
        kern,
        grid=(nb, nsteps),
        in_specs=[
            pl.BlockSpec((1, nsub, 5, 64, 128), lambda b, s: (b, s, 0, 0, 0)),
            row_spec, row_spec, row_spec,
            pl.BlockSpec((8, 256), const),
            pl.BlockSpec((SUB, 128, 256), lambda b, s: (0, 0, 0)),
            pl.BlockSpec((256, 256), const),
            pl.BlockSpec((1, 64, 256), lambda b, s: (b, 0, 0)),
            pl.BlockSpec(memory_space=pl.ANY),
        ],
        input_output_aliases={8: 0},
        out_specs=[row_spec, pl.BlockSpec((1, 64, 256), lambda b, s: (b, 0, 0))],
        out_shape=[jax.ShapeDtypeStruct((nrows, 256), F32), jax.ShapeDtypeStruct((nb, 64, 256), F32)],
        scratch_shapes=[pltpu.VMEM((64, 256), F32), pltpu.VMEM((rblk, 256), F32)],
        compiler_params=_cparams("parallel", "arbitrary"),
    )(tiles, v, g, bonus, par, e, ovv, st_in, o_prev)


def _mlstm_kernel(z_ref, gt_ref, par_ref, bcol_ref, cw_ref, cm_in, nm_in, mm_in, cv_in, o_prev_ref,
                  o_ref, cm_out, nm_out, mm_out, cv_out,
                  xbuf, cm_scr, nm_scr, mm_scr, *, c, tv):
    del o_prev_ref
    step = pl.program_id(1)

    @pl.when(step == 0)
    def _():
        cm_scr[...] = cm_in[0]
        nm_scr[...] = nm_in[0]
        mm_scr[...] = mm_in[0]
        xbuf[0:8, :] = cv_in[0]

    mqk = z_ref[:, 0:256]
    mv = z_ref[:, 256:512]
    gi = z_ref[:, 512:640]
    mo = z_ref[:, 640:896]
    xbuf[8:8 + c, :] = mqk
    conv = par_ref[0:1, :] + xbuf[5:5 + c, :] * cw_ref[0:1, :]
    for j in range(1, CONV_W):
        conv = conv + xbuf[5 + j:5 + j + c, :] * cw_ref[j:j + 1, :]
    xbuf[5:8, :] = xbuf[8 + tv - 3:8 + tv, :]
    act = _silu(conv)
    q = act[:, 0:128]
    k = act[:, 128:256] * (MLSTM_DK ** -0.5)

    rowi = lax.broadcasted_iota(jnp.int32, (c, 1), 0)
    coli = lax.broadcasted_iota(jnp.int32, (1, c), 1)
    gcol = gi + par_ref[1:2, 0:128]
    lfc = _log_sigmoid(gcol)
    grow = gt_ref[0, 0] + bcol_ref[...]
    lfr = _log_sigmoid(grow)
    if tv < c:
        gcol = jnp.where(rowi < tv, gcol, -jnp.inf)
        lfc = jnp.where(rowi < tv, lfc, 0.0)
        grow = jnp.where(coli < tv, grow, -jnp.inf)
        lfr = jnp.where(coli < tv, lfr, 0.0)
    b_col = _seg_dot_l(_tri(c).astype(BF16), lfc, 3)
    b_row = _seg_dot(lfr, _tri(c, upper=True).astype(BF16), 3)
    causal = _tri(c)

    for h in range(N_HEADS):
        bc = b_col[:, 4 + h:5 + h]
        br = b_row[4 + h:5 + h, :]
        ir = grow[h:h + 1, :]
        ic = gcol[:, h:h + 1]
        dmat = jnp.where(causal, bc - br + ir, -jnp.inf)
        mprev = mm_scr[0:1, h:h + 1]
        inter = bc + mprev
        m_t = jnp.maximum(inter, jnp.max(dmat, axis=1, keepdims=True))
        wts = jnp.exp(dmat - m_t)
        sc = jnp.exp(inter - m_t)
        qh = q[:, h * MLSTM_DK:(h + 1) * MLSTM_DK]
        kh = k[:, h * MLSTM_DK:(h + 1) * MLSTM_DK]
        vh = mv[:, h * HEAD_V:(h + 1) * HEAD_V]
        qk = _dot_nt(qh.astype(BF16), kh.astype(BF16)) * wts
        cmh = cm_scr[h * MLSTM_DK:(h + 1) * MLSTM_DK, :]
        nh = nm_scr[0:1, h * MLSTM_DK:(h + 1) * MLSTM_DK]
        num = _dot(qk.astype(BF16), vh.astype(BF16)) + sc * _dot(qh.astype(BF16), cmh.astype(BF16))
        den = jnp.sum(qk, axis=1, keepdims=True) + sc * jnp.sum(qh * nh, axis=1, keepdims=True)
        hc = num / jnp.maximum(jnp.abs(den), jnp.exp(-m_t))
        m_new = m_t[tv - 1:tv, :]
        bl = bc[tv - 1:tv, :]
        wl = jnp.exp(bl - bc + ic - m_new)
        sl = jnp.exp(bl + mprev - m_new)
        kw = kh * wl
        cm_scr[h * MLSTM_DK:(h + 1) * MLSTM_DK, :] = sl * cmh + _dot_tn(kw.astype(BF16), vh.astype(BF16))
        nm_scr[0:1, h * MLSTM_DK:(h + 1) * MLSTM_DK] = sl * nh + jnp.sum(kw, axis=0, keepdims=True)
        mm_scr[0:1, h:h + 1] = m_new
        ms = jnp.mean(hc * hc, axis=1, keepdims=True)
        lanes = slice(h * HEAD_V, (h + 1) * HEAD_V)
        o_ref[:, lanes] = hc * lax.rsqrt(ms + EPS) * par_ref[2:3, lanes] * jax.nn.sigmoid(mo[:, lanes])

    @pl.when(step == pl.num_programs(1) - 1)
    def _():
        cm_out[0] = cm_scr[...]
        nm_out[0] = nm_scr[...]
        mm_out[0] = mm_scr[...]
        cv_out[0] = xbuf[0:8, :]


def _mlstm_call(z, gt, par, bcol, cw, cm, nm, mm, cv, o_prev, *, nrows, base_blk, nb, nsteps, c, tv):
    kern = functools.partial(_mlstm_kernel, c=c, tv=tv)
    const = lambda b, s: (0, 0)
    st3 = lambda shp: pl.BlockSpec((1,) + shp, lambda b, s: (b, 0, 0))
    return pl.pallas_call(
        kern,
        grid=(nb, nsteps),
        in_specs=[
            pl.BlockSpec((c, SEC), lambda b, s: (base_blk + b * nsteps + s, 3)),
            pl.BlockSpec((1, 1, 8, c), lambda b, s: (b, s, 0, 0)),
            pl.BlockSpec((8, 256), const),
            pl.BlockSpec((8, 1), const),
            pl.BlockSpec((8, 256), const),
            st3((128, 64)), st3((1, 128)), st3((1, 128)), st3((8, 256)),
            pl.BlockSpec(memory_space=pl.ANY),
        ],
        input_output_aliases={9: 0},
        out_specs=[
            pl.BlockSpec((c, 256), lambda b, s: (base_blk + b * nsteps + s, 0)),
            st3((128, 64)), st3((1, 128)), st3((1, 128)), st3((8, 256)),
        ],
        out_shape=[
            jax.ShapeDtypeStruct((nrows, 256), F32),
            jax.ShapeDtypeStruct((nb, 128, 64), F32),
            jax.ShapeDtypeStruct((nb, 1, 128), F32),
            jax.ShapeDtypeStruct((nb, 1, 128), F32),
            jax.ShapeDtypeStruct((nb, 8, 256), F32),
        ],
        scratch_shapes=[
            pltpu.VMEM((8 + c, 256), F32),
            pltpu.VMEM((128, 64), F32),
            pltpu.VMEM((1, 128), F32),
            pltpu.VMEM((1, 128), F32),
        ],
        compiler_params=_cparams("parallel", "arbitrary"),
    )(z, gt, par, bcol, cw, cm, nm, mm, cv, o_prev)


def _merge_kernel(o0_ref, o1_ref, o2_ref, o3_ref, zg_ref, x_ref, wb_ref, wout_ref, out_ref, acc_scr):
    n = pl.program_id(1)

    @pl.when(n == 0)
    def _():
        acc_scr[...] = jnp.zeros_like(acc_scr)

    for idx, o_ref in enumerate((o0_ref, o1_ref, o2_ref, o3_ref)):
        @pl.when(n == idx)
        def _(o_ref=o_ref):
            acc_scr[...] += _dot(o_ref[...].astype(BF16), wb_ref[0]) * jax.nn.sigmoid(zg_ref[...])

    @pl.when(n == 3)
    def _():
        out_ref[...] = x_ref[...] + _dot(acc_scr[...].astype(BF16), wout_ref[...])


def _merge_call(outs, z, x, wb, wout, tm):
    n = x.shape[0]
    o_spec = pl.BlockSpec((tm, 256), lambda i, j: (i, 0))
    return pl.pallas_call(
        _merge_kernel,
        grid=(n // tm, 4),
        in_specs=[
            o_spec, o_spec, o_spec, o_spec,
            pl.BlockSpec((tm, SEC), lambda i, j: (i, 4 + j)),
            pl.BlockSpec((tm, D_MODEL), lambda i, j: (i, 0)),
            pl.BlockSpec((1, 256, D_MODEL), lambda i, j: (j, 0, 0)),
            pl.BlockSpec((D_MODEL, D_MODEL), lambda i, j: (0, 0)),
        ],
        out_specs=pl.BlockSpec((tm, D_MODEL), lambda i, j: (i, 0)),
        out_shape=jax.ShapeDtypeStruct((n, D_MODEL), F32),
        scratch_shapes=[pltpu.VMEM((tm, D_MODEL), F32)],
        compiler_params=_cparams("parallel", "arbitrary"),
    )(*outs, z, x, wb, wout)


def _mlp_kernel(x_ref, g_ref, gf_ref, wup_ref, wdn_ref, out_ref, h_scr, acc_scr, *, final):
    j = pl.program_id(1)

    @pl.when(j == 0)
    def _():
        x = x_ref[...]
        y = x * lax.rsqrt(jnp.mean(x * x, axis=-1, keepdims=True) + EPS) * g_ref[...]
        h_scr[...] = y.astype(BF16)
        acc_scr[...] = jnp.zeros_like(acc_scr)

    u = jnp.maximum(_dot(h_scr[...], wup_ref[...]), 0.0)
    acc_scr[...] += _dot((u * u).astype(BF16), wdn_ref[...])

    @pl.when(j == pl.num_programs(1) - 1)
    def _():
        y = x_ref[...] + acc_scr[...]
        if final:
            y = y * lax.rsqrt(jnp.mean(y * y, axis=-1, keepdims=True) + EPS) * gf_ref[...]
        out_ref[...] = y


def _mlp_call(x, g, gf, wup, wdn, tm, final):
    n = x.shape[0]
    tf = 1024
    return pl.pallas_call(
        functools.partial(_mlp_kernel, final=final),
        grid=(n // tm, D_FF // tf),
        in_specs=[
            pl.BlockSpec((tm, D_MODEL), lambda i, j: (i, 0)),
            pl.BlockSpec((1, D_MODEL), lambda i, j: (0, 0)),
            pl.BlockSpec((1, D_MODEL), lambda i, j: (0, 0)),
            pl.BlockSpec((D_MODEL, tf), lambda i, j: (0, j)),
            pl.BlockSpec((tf, D_MODEL), lambda i, j: (j, 0)),
        ],
        out_specs=pl.BlockSpec((tm, D_MODEL), lambda i, j: (i, 0)),
        out_shape=jax.ShapeDtypeStruct((n, D_MODEL), F32),
        scratch_shapes=[pltpu.VMEM((tm, D_MODEL), BF16), pltpu.VMEM((tm, D_MODEL), F32)],
        compiler_params=_cparams("parallel", "arbitrary"),
    )(x, g, gf, wup, wdn)


def _seg_ones(rows_per_head, cols_per_head):
    r = np.arange(N_HEADS * rows_per_head)[:, None] // rows_per_head
    c = np.arange(N_HEADS * cols_per_head)[None, :] // cols_per_head
    return (r == c).astype(np.float32)


def _token_selectors():
    e = np.zeros((SUB, 128, 256), np.float32)
    for t in range(SUB):
        for p in range(2):
            for h in range(N_HEADS):
                e[t, p * 64 + h * SUB + t, h * HEAD_V:(h + 1) * HEAD_V] = 1.0
    return e


def _rows(*vecs, width=256, nrows=8):
    out = jnp.zeros((nrows, width), F32)
    for i, v in enumerate(vecs):
        v = jnp.asarray(v, F32).reshape(-1)
        out = out.at[i, :v.shape[0]].set(v)
    return out


def _layout_w_in(w):
    d = w.shape[0]
    zeros = lambda n: jnp.zeros((d, n), w.dtype)
    gl0, ml0, gt0 = 2048, 2832, 3608
    parts = [
        w[:, 0:2048],
        w[:, gl0:gl0 + 512], w[:, gl0 + 512:gl0 + 528], zeros(112), w[:, gl0 + 528:gl0 + 784], zeros(128),
        w[:, ml0:ml0 + 512], w[:, ml0 + 512:ml0 + 520], zeros(120), w[:, ml0 + 520:ml0 + 776], zeros(128),
        w[:, gt0:gt0 + 4096],
    ]
    out = jnp.concatenate(parts, axis=1)
    assert out.shape[1] == Z_COLS
    return out.astype(BF16)


def _tiles(x, nb, t):
    nblk = t // SUB
    xt = x.reshape(nb, nblk, SUB, N_HEADS, 64).transpose(0, 1, 4, 3, 2).reshape(nb, nblk, 64, 64)
    hi = xt.astype(BF16)
    lo = (xt - hi.astype(F32)).astype(BF16)
    return jnp.concatenate([hi, lo], axis=-1)


class _Group:
    def __init__(self, nb, trow, tv, row0):
        self.nb, self.trow, self.tv, self.row0 = nb, trow, tv, row0
        if tv == trow:
            nblk = trow // SUB
            self.c = SUB
            self.nsub = max(d for d in range(1, min(nblk, 64) + 1) if nblk % d == 0)
            self.mc = _pick_tile(trow, 64)
        else:
            self.c = trow
            self.nsub = 1
            self.mc = trow
        self.rblk = self.c * self.nsub
        self.nsteps = trow // self.rblk
        self.base_blk = row0 // self.rblk
        assert row0 % self.rblk == 0
        self.msteps = trow // self.mc
        self.mbase = row0 // self.mc
        assert row0 % self.mc == 0

    def rows(self, a):
        return a[self.row0:self.row0 + self.nb * self.trow].reshape(self.nb, self.trow, a.shape[-1])


def kernel(x_prompt, x_sample, state_hgrn, state_rwkv, state_rwkv_shift, state_gla, state_mlstm_c, state_mlstm_n, state_mlstm_m, state_mlstm_conv, meta_tokens, norm_mix, norm_mlp, norm_final, w_in, hgrn_lb, hgrn_norm, rwkv_mu, rwkv_w0, rwkv_w_up, rwkv_a0, rwkv_a_up, rwkv_g_up, rwkv_k_k, rwkv_k_a, rwkv_r_k, rwkv_ln_w, rwkv_ln_b, gla_gate_up, gla_gate_b, gla_norm, mlstm_conv_w, mlstm_conv_b, mlstm_i_b, mlstm_f_b, mlstm_norm, w_branch, w_out, w_up, w_down):
    depth = w_in.shape[0]
    bp, seq, _ = x_prompt.shape
    bs, dseq, _ = x_sample.shape
    tp = N_META + seq
    assert tp % SUB == 0 and dseq <= SAMPLE_PAD and dseq >= CONV_W - 1
    dt = x_prompt.dtype

    meta = jnp.broadcast_to(meta_tokens.astype(dt)[None], (bp, N_META, D_MODEL))
    xp = jnp.concatenate([meta, x_prompt], axis=1).reshape(bp * tp, D_MODEL)
    xs = jnp.pad(x_sample, ((0, 0), (0, SAMPLE_PAD - dseq), (0, 0))).reshape(bs * SAMPLE_PAD, D_MODEL)
    n_real = bp * tp + bs * SAMPLE_PAD
    n_rows = -(-n_real // ROW_PAD) * ROW_PAD
    x = jnp.concatenate([xp, xs, jnp.zeros((n_rows - n_real, D_MODEL), dt)], axis=0).astype(F32)
    tm = _pick_tile(n_rows, 1024, 128)

    gp = _Group(bp, tp, tp, 0)
    gs = _Group(bs, SAMPLE_PAD, dseq, bp * tp)

    ones_vv = jnp.asarray(_seg_ones(64, 64), BF16)
    ones_gl = jnp.asarray(_seg_ones(GLA_DK, 64), BF16)
    mask_hg = jnp.asarray(_seg_ones(64, 64), F32)
    mask_gl = jnp.asarray(_seg_ones(64, GLA_DK), F32)
    e_sel = jnp.asarray(_token_selectors(), BF16)

    lb_cs = jnp.cumsum(jax.nn.softmax(hgrn_lb.astype(F32), axis=0), axis=0)
    lb_all = lb_cs - lb_cs[:1]

    def init_states(states, nb, zero):
        s_hg, s_rw, shift, s_gl, c_ml, n_ml, m_ml, conv = states
        if zero:
            z = lambda *shp: jnp.zeros((depth, nb) + shp, F32)
            return dict(hg=z(256, 64), rw=z(64, 256), shift=z(1, SEC), gl=z(256, GLA_DK), cm=z(128, 64),
                        nm=z(1, 128), mm=z(1, 128), cv=z(8, 256))
        f = lambda a: a.astype(F32)
        return dict(
            hg=f(s_hg).transpose(0, 1, 2, 4, 3).reshape(depth, nb, 256, 64),
            rw=f(s_rw).transpose(0, 1, 4, 2, 3).reshape(depth, nb, 64, 256),
            shift=f(shift).reshape(depth, nb, 1, SEC),
            gl=f(s_gl).transpose(0, 1, 2, 4, 3).reshape(depth, nb, 256, GLA_DK),
            cm=f(c_ml).reshape(depth, nb, 128, 64),
            nm=f(n_ml).reshape(depth, nb, 1, 128),
            mm=jnp.pad(f(m_ml), ((0, 0), (0, 0), (0, 124))).reshape(depth, nb, 1, 128),
            cv=jnp.pad(f(conv), ((0, 0), (0, 0), (5, 0), (0, 0))),
        )

    sample_states = (state_hgrn, state_rwkv, state_rwkv_shift, state_gla,
                     state_mlstm_c, state_mlstm_n, state_mlstm_m, state_mlstm_conv)
    st_p = init_states(sample_states, bp, True)
    st_s = init_states(sample_states, bs, False)
    new_p = {k: [] for k in st_p}
    new_s = {k: [] for k in st_s}

    for l in range(depth):
        wz = _layout_w_in(w_in[l])
        z = _in_proj(x, norm_mix[l].reshape(1, D_MODEL).astype(F32), wz, tm)

        lb = lb_all[l]
        par_hg = _rows(jnp.log(lb), jnp.log1p(-lb), 1.0 - lb, hgrn_norm[l])
        par_gl = _rows(gla_gate_b[l], jnp.zeros((1,)), jnp.zeros((1,)), gla_norm[l])
        gup_gl = jnp.zeros((128, 128), F32).at[:gla_gate_up.shape[1], :].set(gla_gate_up[l]).astype(BF16)
        gup_dummy = jnp.zeros((128, 128), BF16)
        par_rw = _rows(rwkv_w0[l], rwkv_a0[l], rwkv_k_k[l], rwkv_k_a[l], rwkv_r_k[l])
        par_ln = _rows(rwkv_ln_w[l], rwkv_ln_b[l])
        nw, na = rwkv_w_up.shape[1], rwkv_a_up.shape[1]
        wup_p = jnp.zeros((256, 256), F32).at[0:nw].set(rwkv_w_up[l]).astype(BF16)
        aup_p = jnp.zeros((256, 256), F32).at[nw:nw + na].set(rwkv_a_up[l]).astype(BF16)
        gup_p = jnp.zeros((256, 256), F32).at[nw + na:].set(rwkv_g_up[l]).astype(BF16)
        mu = rwkv_mu[l].reshape(1, SEC).astype(F32)
        gate_b = jnp.concatenate([mlstm_i_b[l], mlstm_f_b[l]]).astype(F32)
        par_ml = _rows(mlstm_conv_b[l], gate_b, mlstm_norm[l])
        bcol_ml = gate_b.reshape(8, 1)
        cw_ml = _rows(*[mlstm_conv_w[l, j] for j in range(CONV_W)])

        o_hg = o_rw = o_gl = o_ml = None
        for g, st, new in ((gp, st_p, new_p), (gs, st_s, new_s)):
            common = dict(nrows=n_rows, nb=g.nb)
            prev = lambda o: jnp.zeros((n_rows, 256), F32) if o is None else o
            o_hg, s_hg = _gla_call(z, 0, par_hg, gup_dummy, ones_vv, ones_vv, mask_hg, st["hg"][l], prev(o_hg),
                                   mode="hgrn", dk=64, base_blk=g.base_blk, nsteps=g.nsteps, c=g.c, nsub=g.nsub,
                                   tv=min(g.tv, g.c), **common)
            o_gl, s_gl = _gla_call(z, 2, par_gl, gup_gl, ones_gl, ones_vv, mask_gl, st["gl"][l], prev(o_gl),
                                   mode="gla", dk=GLA_DK, base_blk=g.base_blk, nsteps=g.nsteps, c=g.c, nsub=g.nsub,
                                   tv=min(g.tv, g.c), **common)
            tv_last = g.rblk if g.tv == g.trow else g.tv
            (r_, w_, k_, v_, kk_, ka_, g_, bonus_, nshift) = _rwkv_prep_call(
                z, mu, par_rw, wup_p, aup_p, gup_p, ones_vv, st["shift"][l], base_blk=g.base_blk, nsteps=g.nsteps,
                rblk=g.rblk, tv_last=tv_last, **common)
            tpad = g.trow if g.tv == g.trow else SUB

            def seqs(a):
                a = g.rows(a)
                return a if tpad == g.trow else jnp.pad(a, ((0, 0), (0, tpad - g.trow), (0, 0)))

            tiles = jnp.stack([_tiles(seqs(a), g.nb, tpad) for a in (w_, ka_, k_, kk_, r_)], axis=2)
            o_rw, s_rw = _rwkv_scan_call(tiles, v_, g_, bonus_, par_ln, e_sel, ones_vv, st["rw"][l], prev(o_rw),
                                         base_blk=g.base_blk, nsteps=g.nsteps, nsub=g.nsub,
                                         tsub=min(g.tv, SUB), rblk=g.rblk, **common)
            gates = g.rows(z[:, 3 * SEC + 512:3 * SEC + 520])
            gt = gates.reshape(g.nb, g.msteps, g.mc, 8).transpose(0, 1, 3, 2)
            o_ml, s_cm, s_nm, s_mm, s_cv = _mlstm_call(z, gt, par_ml, bcol_ml, cw_ml, st["cm"][l], st["nm"][l],
                                                        st["mm"][l], st["cv"][l], prev(o_ml), base_blk=g.mbase,
                                                        nsteps=g.msteps, c=g.mc, tv=min(g.tv, g.mc), **common)
            for key, val in (("hg", s_hg), ("rw", s_rw), ("shift", nshift), ("gl", s_gl), ("cm", s_cm),
                             ("nm", s_nm), ("mm", s_mm), ("cv", s_cv)):
                new[key].append(val)

        x = _merge_call((o_hg, o_rw, o_gl, o_ml), z, x, w_branch[l].astype(BF16), w_out[l].astype(BF16), tm)
        x = _mlp_call(x, norm_mlp[l].reshape(1, D_MODEL).astype(F32), norm_final.reshape(1, D_MODEL).astype(F32),
                      w_up[l].astype(BF16), w_down[l].astype(BF16), tm, final=(l == depth - 1))

    y_prompt = x[:bp * tp].reshape(bp, tp, D_MODEL)[:, N_META:].astype(dt)
    y_sample = x[gs.row0:gs.row0 + bs * SAMPLE_PAD].reshape(bs, SAMPLE_PAD, D_MODEL)[:, :dseq].astype(dt)

    def finish(new, nb):
        st = {k: jnp.stack(v) for k, v in new.items()}
        return (
            st["hg"].reshape(depth, nb, N_HEADS, 64, 64).transpose(0, 1, 2, 4, 3),
            st["rw"].reshape(depth, nb, 64, N_HEADS, 64).transpose(0, 1, 3, 4, 2),
            st["shift"].reshape(depth, nb, SEC),
            st["gl"].reshape(depth, nb, N_HEADS, 64, GLA_DK).transpose(0, 1, 2, 4, 3),
            st["cm"].reshape(depth, nb, N_HEADS, MLSTM_DK, 64),
            st["nm"].reshape(depth, nb, N_HEADS, MLSTM_DK),
            st["mm"].reshape(depth, nb, 128)[:, :, :N_HEADS],
            st["cv"][:, :, 5:8, :],
        )

    outs_p = tuple(a.astype(dt) for a in finish(new_p, bp))
    outs_s = tuple(a.astype(dt) for a in finish(new_s, bs))
    return (y_prompt, y_sample) + outs_p + outs_s
```

```python
import functools

import numpy as np
import jax
import jax.numpy as jnp
from jax import lax
from jax.experimental import pallas as pl
from jax.experimental.pallas import tpu as pltpu

F32 = jnp.float32
BF16 = jnp.bfloat16

D_MODEL = 1024
N_META = 16
N_HEADS = 4
HEAD_V = 64
GLA_DK = 32
MLSTM_DK = 32
GLA_GATE_TAU = 16.0
RWKV_LN_EPS = 64e-5
CONV_W = 4
D_FF = 4 * D_MODEL
EPS = 1e-6

SEC = 1024
Z_COLS = 4 * SEC
SUB = 16
SAMPLE_PAD = 8
ROW_PAD = 512
MLSTM_CHUNK = 128
VMEM_LIMIT = 48 * 1024 * 1024


def _cparams(*sem):
    return pltpu.CompilerParams(dimension_semantics=sem, vmem_limit_bytes=VMEM_LIMIT)


def _pick_tile(n, max_tile, mult=8):
    best = None
    t = mult
    while t <= min(n, max_tile):
        if n % t == 0:
            best = t
        t += mult
    assert best is not None, (n, max_tile, mult)
    return best


def _split2(x):
    hi = x.astype(BF16)
    lo = (x - hi.astype(F32)).astype(BF16)
    return hi, lo


def _split3(x):
    hi = x.astype(BF16)
    r1 = x - hi.astype(F32)
    mid = r1.astype(BF16)
    lo = (r1 - mid.astype(F32)).astype(BF16)
    return hi, mid, lo


def _dot(a, b):
    return jnp.dot(a, b, preferred_element_type=F32)


def _dot_nt(a, b):
    return lax.dot_general(a, b, (((1,), (1,)), ((), ())), preferred_element_type=F32)


def _dot_tn(a, b):
    return lax.dot_general(a, b, (((0,), (0,)), ((), ())), preferred_element_type=F32)


def _seg_dot(x, m_bf16, parts):
    ps = _split2(x) if parts == 2 else _split3(x)
    acc = _dot(ps[0], m_bf16)
    for p in ps[1:]:
        acc = acc + _dot(p, m_bf16)
    return acc


def _seg_dot_l(m_bf16, x, parts):
    ps = _split2(x) if parts == 2 else _split3(x)
    acc = _dot(m_bf16, ps[0])
    for p in ps[1:]:
        acc = acc + _dot(m_bf16, p)
    return acc


def _log_sigmoid(x):
    return jnp.minimum(x, 0.0) - jnp.log1p(jnp.exp(-jnp.abs(x)))


def _silu(x):
    return x * jax.nn.sigmoid(x)


def _tri(n, upper=False):
    r = lax.broadcasted_iota(jnp.int32, (n, n), 0)
    c = lax.broadcasted_iota(jnp.int32, (n, n), 1)
    return (r <= c) if upper else (r >= c)


def _cumsum_rows(x, rowi):
    d = 1
    while d < x.shape[0]:
        x = x + jnp.where(rowi >= d, pltpu.roll(x, d, axis=0), 0.0)
        d *= 2
    return x


def _rms_bf16(x, g):
    return (x * lax.rsqrt(jnp.mean(x * x, axis=-1, keepdims=True) + EPS) * g).astype(BF16)


def _in_proj_kernel(x_ref, g_ref, w_ref, z_ref, h_scr):
    @pl.when(pl.program_id(1) == 0)
    def _():
        h_scr[...] = _rms_bf16(x_ref[...], g_ref[...])

    z_ref[...] = _dot(h_scr[...], w_ref[...])


def _in_proj(x, g, w, tm):
    n = x.shape[0]
    tn = 1024
    return pl.pallas_call(
        _in_proj_kernel,
        grid=(n // tm, Z_COLS // tn),
        in_specs=[
            pl.BlockSpec((tm, D_MODEL), lambda i, j: (i, 0)),
            pl.BlockSpec((1, D_MODEL), lambda i, j: (0, 0)),
            pl.BlockSpec((D_MODEL, tn), lambda i, j: (0, j)),
        ],
        out_specs=pl.BlockSpec((tm, tn), lambda i, j: (i, j)),
        out_shape=jax.ShapeDtypeStruct((n, Z_COLS), F32),
        scratch_shapes=[pltpu.VMEM((tm, D_MODEL), BF16)],
        compiler_params=_cparams("parallel", "arbitrary"),
    )(x, g, w)


def _gla_kernel(z_ref, par_ref, gup_ref, okv_ref, ovv_ref, mask_ref, st_in_ref, o_prev_ref,
                o_ref, st_out_ref, *st_scrs, mode, c, nsub, grp, tv, dk, nseq, trow):
    del o_prev_ref
    voff = 512 if mode == "hgrn" else 256
    rowi = lax.broadcasted_iota(jnp.int32, (c, 1), 0)
    valid = rowi < tv

    def sub_block(st_scr, r0):
        rows = pl.ds(r0, c)
        if mode == "hgrn":
            q = z_ref[rows, 0:256]
            hf = z_ref[rows, 256:512]
            g = z_ref[rows, 768:1024]
            a = par_ref[0:1, :]
            cc = par_ref[1:2, :] + _log_sigmoid(hf)
            loga = jnp.maximum(a, cc) + jnp.log1p(jnp.exp(-jnp.abs(a - cc)))
            k = par_ref[2:3, :] * jax.nn.sigmoid(-hf)
        else:
            q = z_ref[rows, 0:128] * (GLA_DK ** -0.5)
            k = z_ref[rows, 128:256]
            ga = z_ref[rows, 512:640]
            g = z_ref[rows, 640:896]
            gl = _dot(ga.astype(BF16), gup_ref[...]) + par_ref[0:1, 0:128]
            loga = _log_sigmoid(gl) * (1.0 / GLA_GATE_TAU)
        v = z_ref[rows, voff:voff + 256]
        if tv < c:
            loga = jnp.where(valid, loga, 0.0)
            k = jnp.where(valid, k, 0.0)
            v = jnp.where(valid, v, 0.0)
        b = _cumsum_rows(loga, rowi)

        o = _dot_nt((q * jnp.exp(b)).astype(BF16), st_scr[...].astype(BF16))

        lo_rows = [8 * (s // 8) for s in range(tv)]
        pieces = []
        for s in range(tv):
            rs = slice(lo_rows[s], c)
            e = jnp.exp(jnp.where(rowi[rs] >= s, b[rs] - b[s:s + 1, :], -jnp.inf))
            pieces.append(q[rs] * e * k[s:s + 1, :])
        sm = _dot(jnp.concatenate(pieces, axis=0).astype(BF16), okv_ref[...])
        acc = {}
        off = 0
        for s in range(tv):
            n = c - lo_rows[s]
            contrib = sm[off:off + n] * v[s:s + 1, :]
            off += n
            acc[lo_rows[s]] = contrib if lo_rows[s] not in acc else acc[lo_rows[s]] + contrib
        for lo, a_ in acc.items():
            o = o + (a_ if lo == 0 else jnp.concatenate([jnp.zeros((lo, 256), F32), a_], axis=0))

        bl = b[c - 1:c, :]
        kdec = k * jnp.exp(bl - b)
        upd = _dot_tn(v.astype(BF16), kdec.astype(BF16))
        st_scr[...] = st_scr[...] * jnp.exp(bl) + upd * mask_ref[...]

        ms = _seg_dot(o * o, ovv_ref[...], 2) * (1.0 / HEAD_V)
        o_ref[rows, :] = o * lax.rsqrt(ms + EPS) * par_ref[3:4, :] * _silu(g)

    def sequence(st_scr, seq):
        st = st_in_ref[seq]
        st_scr[...] = jnp.concatenate([st] * N_HEADS, axis=1) * mask_ref[...]

        def group(gi, carry):
            for u in range(grp):
                sub_block(st_scr, pl.multiple_of(seq * trow + (gi * grp + u) * c, 8))
            return carry

        lax.fori_loop(0, nsub // grp, group, 0)
        st = st_scr[...]
        acc = st[:, 0:dk]
        for h in range(1, N_HEADS):
            acc = acc + st[:, h * dk:(h + 1) * dk]
        st_out_ref[seq] = acc

    lanes = len(st_scrs)

    def seq_group(si, carry):
        for u in range(lanes):
            sequence(st_scrs[u], si * lanes + u)
        return carry

    if nseq == lanes:
        seq_group(0, 0)
    else:
        lax.fori_loop(0, nseq // lanes, seq_group, 0)


def _gla_call(z, sec, par, gup, okv, ovv, mask, st_in, o_prev, *, mode, dk, nrows, g):
    hk = N_HEADS * dk
    lanes = 2 if g.nseq % 2 == 0 else 1
    grp = max(d for d in (4, 3, 2, 1) if g.nsub % d == 0)
    kern = functools.partial(_gla_kernel, mode=mode, c=g.c, nsub=g.nsub, grp=grp, tv=g.tvs, dk=dk,
                             nseq=g.nseq, trow=g.trow)
    const = lambda b: (0, 0)
    return pl.pallas_call(
        kern,
        grid=(g.nsteps,),
        in_specs=[
            pl.BlockSpec((g.rows, SEC), lambda b: (g.base_blk + b, sec)),
            pl.BlockSpec((8, 256), const),
            pl.BlockSpec((128, 128), const),
            pl.BlockSpec((hk, 256), const),
            pl.BlockSpec((256, 256), const),
            pl.BlockSpec((256, hk), const),
            pl.BlockSpec((g.nseq, 256, dk), lambda b: (b, 0, 0)),
            pl.BlockSpec(memory_space=pl.ANY),
        ],
        input_output_aliases={7: 0},
        out_specs=[
            pl.BlockSpec((g.rows, 256), lambda b: (g.base_blk + b, 0)),
            pl.BlockSpec((g.nseq, 256, dk), lambda b: (b, 0, 0)),
        ],
        out_shape=[
            jax.ShapeDtypeStruct((nrows, 256), F32),
            jax.ShapeDtypeStruct((g.nb, 256, dk), F32),
        ],
        scratch_shapes=[pltpu.VMEM((256, hk), F32)] * lanes,
        compiler_params=_cparams("parallel"),
    )(z, par, gup, okv, ovv, mask, st_in, o_prev)


def _rwkv_prep_kernel(z_ref, mu_ref, par_ref, wup_ref, aup_ref, gup_ref, ovv_ref, shift_ref,
                      r_ref, w_ref, k_ref, v_ref, kk_ref, ka_ref, g_ref, bonus_ref, nshift_ref,
                      prev_scr, *, full, nseq, trow, tv):
    step = pl.program_id(1)
    zr = z_ref[...]
    nrow = zr.shape[0]
    rolled = pltpu.roll(zr, 1, axis=0)
    if full:
        @pl.when(step == 0)
        def _():
            prev_scr[0:1, :] = shift_ref[0]

        rowi = lax.broadcasted_iota(jnp.int32, (nrow, 1), 0)
        prev = jnp.where(rowi == 0, prev_scr[0:1, :], rolled)
        prev_scr[0:1, :] = zr[nrow - 1:nrow, :]

        @pl.when(step == pl.num_programs(1) - 1)
        def _():
            nshift_ref[0] = zr[nrow - 1:nrow, :]
    else:
        prev_scr[...] = rolled
        for i in range(nseq):
            prev_scr[i * trow:i * trow + 1, :] = shift_ref[i]
            nshift_ref[i] = zr[i * trow + tv - 1:i * trow + tv, :]
        prev = prev_scr[...]
    zs = zr + (prev - zr) * mu_ref[...]
    r = zs[:, 0:256]
    k = zs[:, 256:512]
    v = zs[:, 512:768]
    lo = zs[:, 768:1024]
    w0, a0, k_k, k_a, r_k = (par_ref[i:i + 1, :] for i in range(5))
    wl = w0 + _dot(jnp.tanh(lo).astype(BF16), wup_ref[...])
    wexp = -(jnp.maximum(-wl, 0.0) + jnp.log1p(jnp.exp(-jnp.abs(wl)))) - 0.5
    wdec = jnp.exp(-jnp.exp(wexp))
    a = jax.nn.sigmoid(a0 + _dot(lo.astype(BF16), aup_ref[...]))
    g = _dot(jax.nn.sigmoid(lo).astype(BF16), gup_ref[...])
    kkp = k * k_k
    nrm = jnp.sqrt(_seg_dot(kkp * kkp, ovv_ref[...], 3))
    kk = kkp / jnp.maximum(nrm, 1e-12)
    k2 = k * (1.0 + (a - 1.0) * k_a)
    bonus = _seg_dot(r * k2 * r_k, ovv_ref[...], 3) * v
    r_ref[...] = r
    w_ref[...] = wdec
    k_ref[...] = k2
    v_ref[...] = v
    kk_ref[...] = kk
    ka_ref[...] = kk * a
    g_ref[...] = g
    bonus_ref[...] = bonus


def _rwkv_prep_call(z, mu, par, wup, aup, gup, ovv, shift, *, nrows, g):
    if g.full:
        assert g.nseq == 1
        rblk = g.srows
        nsteps = g.trow // rblk
        scr_rows = 8
    else:
        rblk = g.rows
        nsteps = 1
        scr_rows = rblk
    base = g.row0 // rblk
    kern = functools.partial(_rwkv_prep_kernel, full=g.full, nseq=g.nseq, trow=g.trow, tv=g.tv)
    const = lambda b, s: (0, 0)
    row_spec = pl.BlockSpec((rblk, 256), lambda b, s: (base + b * nsteps + s, 0))
    return pl.pallas_call(
        kern,
        grid=(g.nsteps, nsteps),
        in_specs=[
            pl.BlockSpec((rblk, SEC), lambda b, s: (base + b * nsteps + s, 1)),
            pl.BlockSpec((1, SEC), const),
            pl.BlockSpec((8, 256), const),
            pl.BlockSpec((256, 256), const),
            pl.BlockSpec((256, 256), const),
            pl.BlockSpec((256, 256), const),
            pl.BlockSpec((256, 256), const),
            pl.BlockSpec((g.nseq, 1, SEC), lambda b, s: (b, 0, 0)),
        ],
        out_specs=[row_spec] * 8 + [pl.BlockSpec((g.nseq, 1, SEC), lambda b, s: (b, 0, 0))],
        out_shape=[jax.ShapeDtypeStruct((nrows, 256), F32)] * 8 + [jax.ShapeDtypeStruct((g.nb, 1, SEC), F32)],
        scratch_shapes=[pltpu.VMEM((scr_rows, SEC), F32)],
        compiler_params=_cparams("parallel", "arbitrary"),
    )(z, mu, par, wup, aup, gup, ovv, shift)


def _rwkv_scan_kernel(w4_ref, ka4_ref, k4_ref, kk4_ref, r4_ref, v_ref, g_ref, bonus_ref, par_ref, e_ref, eye_ref,
                      ovv_ref, st_in_ref, o_prev_ref,
                      o_ref, st_out_ref, st_scr, y_scr, *, full, c, nsub, tsub, nseq, trow):
    del o_prev_ref
    step = pl.program_id(1)
    q_refs = (w4_ref, ka4_ref, k4_ref, kk4_ref, r4_ref)
    if tsub < c:
        y_scr[...] = jnp.zeros_like(y_scr)

    def sub_block(row0, hs):
        r4 = pl.multiple_of(row0 * N_HEADS, 8)
        lhs = []
        for ref in q_refs:
            hi, lo = _split2(ref[pl.ds(r4, N_HEADS * c), :])
            th = _dot_nt(eye_ref[...], hi)
            tl = _dot_nt(eye_ref[...], lo)
            lhs.append(jnp.concatenate([th, tl], axis=1).astype(BF16))
        for t in range(tsub):
            e = e_ref[t]
            tile = lambda q: _dot(lhs[q], e)
            row = row0 + t
            sa = jnp.sum(hs * tile(3), axis=0, keepdims=True)
            hs = hs * tile(0) - tile(1) * sa + tile(2) * v_ref[pl.ds(row, 1), :]
            y_scr[pl.ds(row, 1), :] = jnp.sum(hs * tile(4), axis=0, keepdims=True)
        return hs

    if full:
        @pl.when(step == 0)
        def _():
            st_scr[...] = st_in_ref[0]

        st_scr[...] = lax.fori_loop(0, nsub, lambda j, hs: sub_block(pl.multiple_of(j * c, 8), hs), st_scr[...])

        @pl.when(step == pl.num_programs(1) - 1)
        def _():
            st_out_ref[0] = st_scr[...]
    else:
        def sequence(seq, carry):
            st_out_ref[seq] = sub_block(pl.multiple_of(seq * trow, 8), st_in_ref[seq])
            return carry

        lax.fori_loop(0, nseq, sequence, 0, unroll=2)

    y = y_scr[...]
    mu = _seg_dot(y, ovv_ref[...], 3) * (1.0 / HEAD_V)
    d = y - mu
    var = _seg_dot(d * d, ovv_ref[...], 3) * (1.0 / HEAD_V)
    ln = d * lax.rsqrt(var + RWKV_LN_EPS) * par_ref[0:1, :] + par_ref[1:2, :]
    o_ref[...] = (ln + bonus_ref[...]) * g_ref[...]


def _rwkv_scan_call(cols, v, g_, bonus, par, e, eye, ovv, st_in, o_prev, *, nrows, g):
    if g.full:
        assert g.nseq == 1
        rblk = g.srows
        nsteps = g.trow // rblk
        nsub = rblk // g.c
    else:
        rblk = g.rows
        nsteps = 1
        nsub = 1
    base = g.row0 // rblk
    kern = functools.partial(_rwkv_scan_kernel, full=g.full, c=g.c, nsub=nsub, tsub=g.tvs, nseq=g.nseq,
                             trow=g.trow)
    const = lambda b, s: (0, 0)
    row_spec = pl.BlockSpec((rblk, 256), lambda b, s: (base + b * nsteps + s, 0))
    col_spec = pl.BlockSpec((N_HEADS * rblk, 64), lambda b, s: (base + b * nsteps + s, 0))
    ke = e.shape[1]
    return pl.pallas_call(
        kern,
        grid=(g.nsteps, nsteps),
        in_specs=[col_spec] * 5 + [
            row_spec, row_spec, row_spec,
            pl.BlockSpec((8, 256), const),
            pl.BlockSpec((g.c, ke, 256), lambda b, s: (0, 0, 0)),
            pl.BlockSpec((64, 64), const),
            pl.BlockSpec((256, 256), const),
            pl.BlockSpec((g.nseq, 64, 256), lambda b, s: (b, 0, 0)),
            pl.BlockSpec(memory_space=pl.ANY),
        ],
        input_output_aliases={13: 0},
        out_specs=[row_spec, pl.BlockSpec((g.nseq, 64, 256), lambda b, s: (b, 0, 0))],
        out_shape=[jax.ShapeDtypeStruct((nrows, 256), F32), jax.ShapeDtypeStruct((g.nb, 64, 256), F32)],
        scratch_shapes=[pltpu.VMEM((64, 256), F32), pltpu.VMEM((rblk, 256), F32)],
        compiler_params=_cparams("parallel", "arbitrary"),
    )(*cols, v, g_, bonus, par, e, eye, ovv, st_in, o_prev)


def _mlstm_kernel(z_ref, par_ref, bcol_ref, cw_ref, sel_ref, cm_in, nm_in, mm_in, cv_in, o_prev_ref,
                  o_ref, cm_out, nm_out, mm_out, cv_out,
                  *scratch, chunks, nseq, trow):
    del o_prev_ref
    lanes = len(scratch) // 4

    def chunk(scr, row0, c, tv):
        xbuf, cm_scr, nm_scr, mm_scr = scr
        rows = pl.ds(row0, c)
        mqk = z_ref[rows, 0:256]
        mv = z_ref[rows, 256:512]
        gi = z_ref[rows, 512:640]
        mo = z_ref[rows, 640:896]
        xbuf[8:8 + c, :] = mqk
        conv = par_ref[0:1, :] + xbuf[5:5 + c, :] * cw_ref[0:1, :]
        for j in range(1, CONV_W):
            conv = conv + xbuf[5 + j:5 + j + c, :] * cw_ref[j:j + 1, :]
        xbuf[5:8, :] = xbuf[8 + tv - 3:8 + tv, :]
        act = _silu(conv)
        q = act[:, 0:128]
        k = act[:, 128:256] * (MLSTM_DK ** -0.5)

        rowi = lax.broadcasted_iota(jnp.int32, (c, 1), 0)
        coli = lax.broadcasted_iota(jnp.int32, (1, c), 1)
        gcol = gi + par_ref[1:2, 0:128]
        lfc = _log_sigmoid(gcol)
        graw = None
        for part in _split3(gi):
            t_ = _dot_nt(sel_ref[...], part)
            graw = t_ if graw is None else graw + t_
        grow = graw + bcol_ref[...]
        lfr = _log_sigmoid(grow)
        if tv < c:
            gcol = jnp.where(rowi < tv, gcol, -jnp.inf)
            lfc = jnp.where(rowi < tv, lfc, 0.0)
            grow = jnp.where(coli < tv, grow, -jnp.inf)
            lfr = jnp.where(coli < tv, lfr, 0.0)
        b_col = _seg_dot_l(_tri(c).astype(BF16), lfc, 3)
        b_row = _seg_dot(lfr, _tri(c, upper=True).astype(BF16), 3)
        causal = _tri(c)
        mm_old = mm_scr[...]
        nm_old = nm_scr[...]
        cm_old = cm_scr[...]
        lane = lax.broadcasted_iota(jnp.int32, (1, 128), 1)
        mm_new = mm_old
        cm_parts, nm_parts, o_parts = [], [], []

        for h in range(N_HEADS):
            bc = b_col[:, 4 + h:5 + h]
            br = b_row[4 + h:5 + h, :]
            ir = grow[h:h + 1, :]
            ic = gcol[:, h:h + 1]
            dmat = jnp.where(causal, bc - br + ir, -jnp.inf)
            mprev = mm_old[0:1, h:h + 1]
            inter = bc + mprev
            m_t = jnp.maximum(inter, jnp.max(dmat, axis=1, keepdims=True))
            wts = jnp.exp(dmat - m_t)
            sc = jnp.exp(inter - m_t)
            qh = q[:, h * MLSTM_DK:(h + 1) * MLSTM_DK]
            kh = k[:, h * MLSTM_DK:(h + 1) * MLSTM_DK]
            vh = mv[:, h * HEAD_V:(h + 1) * HEAD_V]
            qk = _dot_nt(qh.astype(BF16), kh.astype(BF16)) * wts
            cmh = cm_old[h * MLSTM_DK:(h + 1) * MLSTM_DK, :]
            nh = nm_old[0:1, h * MLSTM_DK:(h + 1) * MLSTM_DK]
            num = _dot(qk.astype(BF16), vh.astype(BF16)) + sc * _dot(qh.astype(BF16), cmh.astype(BF16))
            den = jnp.sum(qk, axis=1, keepdims=True) + sc * jnp.sum(qh * nh, axis=1, keepdims=True)
            hc = num / jnp.maximum(jnp.abs(den), jnp.exp(-m_t))
            m_new = m_t[tv - 1:tv, :]
            bl = bc[tv - 1:tv, :]
            wl = jnp.exp(bl - bc + ic - m_new)
            sl = jnp.exp(bl + mprev - m_new)
            kw = kh * wl
            cm_parts.append(sl * cmh + _dot_tn(kw.astype(BF16), vh.astype(BF16)))
            nm_parts.append(sl * nh + jnp.sum(kw, axis=0, keepdims=True))
            mm_new = jnp.where(lane == h, m_new, mm_new)
            ms = jnp.mean(hc * hc, axis=1, keepdims=True)
            o_parts.append(hc * lax.rsqrt(ms + EPS))

        cm_scr[...] = jnp.concatenate(cm_parts, axis=0)
        nm_scr[...] = jnp.concatenate(nm_parts, axis=1)
        mm_scr[...] = mm_new
        o_ref[rows, :] = jnp.concatenate(o_parts, axis=1) * par_ref[2:3, :] * jax.nn.sigmoid(mo)

    def sequence(scr, seq):
        xbuf, cm_scr, nm_scr, mm_scr = scr
        cm_scr[...] = cm_in[seq]
        nm_scr[...] = nm_in[seq]
        mm_scr[...] = mm_in[seq]
        xbuf[0:8, :] = cv_in[seq]
        base = seq * trow
        for start, c, tv, count in chunks:
            if count == 1:
                first = base + start
                chunk(scr, first if isinstance(first, int) else pl.multiple_of(first, 8), c, tv)
            else:
                def body(ci, carry2, start=start, c=c, tv=tv):
                    chunk(scr, pl.multiple_of(base + start + ci * c, 8), c, tv)
                    return carry2

                lax.fori_loop(0, count, body, 0)
        cm_out[seq] = cm_scr[...]
        nm_out[seq] = nm_scr[...]
        mm_out[seq] = mm_scr[...]
        cv_out[seq] = xbuf[0:8, :]

    def seq_group(si, carry):
        for u in range(lanes):
            sequence(scratch[4 * u:4 * u + 4], si * lanes + u)
        return carry

    if nseq == lanes:
        seq_group(0, 0)
    else:
        lax.fori_loop(0, nseq // lanes, seq_group, 0)


def _mlstm_call(z, par, bcol, cw, sel, cm, nm, mm, cv, o_prev, *, nrows, g):
    if g.full:
        lead = N_META if g.trow > N_META else g.trow
        chunks = [(0, lead, lead, 1)]
        rest = g.trow - lead
        if rest:
            cmain = max(d for d in (MLSTM_CHUNK, 64, 32, 16, 8) if rest % d == 0)
            chunks.append((lead, cmain, cmain, rest // cmain))
    else:
        chunks = [(0, g.trow, g.tv, 1)]
    cmax = max(c for _, c, _, _ in chunks)
    kern = functools.partial(_mlstm_kernel, chunks=tuple(chunks), nseq=g.nseq, trow=g.trow)
    const = lambda b: (0, 0)
    st3 = lambda shp: pl.BlockSpec((g.nseq,) + shp, lambda b: (b, 0, 0))
    return pl.pallas_call(
        kern,
        grid=(g.nsteps,),
        in_specs=[
            pl.BlockSpec((g.rows, SEC), lambda b: (g.base_blk + b, 3)),
            pl.BlockSpec((8, 256), const),
            pl.BlockSpec((8, 1), const),
            pl.BlockSpec((8, 256), const),
            pl.BlockSpec((8, 128), const),
            st3((128, 64)), st3((1, 128)), st3((1, 128)), st3((8, 256)),
            pl.BlockSpec(memory_space=pl.ANY),
        ],
        input_output_aliases={9: 0},
        out_specs=[
            pl.BlockSpec((g.rows, 256), lambda b: (g.base_blk + b, 0)),
            st3((128, 64)), st3((1, 128)), st3((1, 128)), st3((8, 256)),
        ],
        out_shape=[
            jax.ShapeDtypeStruct((nrows, 256), F32),
            jax.ShapeDtypeStruct((g.nb, 128, 64), F32),
            jax.ShapeDtypeStruct((g.nb, 1, 128), F32),
            jax.ShapeDtypeStruct((g.nb, 1, 128), F32),
            jax.ShapeDtypeStruct((g.nb, 8, 256), F32),
        ],
        scratch_shapes=[
            pltpu.VMEM((8 + cmax, 256), F32),
            pltpu.VMEM((128, 64), F32),
            pltpu.VMEM((1, 128), F32),
            pltpu.VMEM((1, 128), F32),
        ] * (2 if g.nseq % 2 == 0 else 1),
        compiler_params=_cparams("parallel"),
    )(z, par, bcol, cw, sel, cm, nm, mm, cv, o_prev)


def _merge_kernel(o0_ref, o1_ref, o2_ref, o3_ref, x_ref, g_ref, wg_ref, wb_ref, wout_ref, out_ref, h_scr, acc_scr):
    n = pl.program_id(1)

    @pl.when(n == 0)
    def _():
        h_scr[...] = _rms_bf16(x_ref[...], g_ref[...])
        acc_scr[...] = jnp.zeros_like(acc_scr)

    gate = jax.nn.sigmoid(_dot(h_scr[...], wg_ref[...]))
    for idx, o_ref in enumerate((o0_ref, o1_ref, o2_ref, o3_ref)):
        @pl.when(n == idx)
        def _(o_ref=o_ref):
            acc_scr[...] += _dot(o_ref[...].astype(BF16), wb_ref[0]) * gate

    @pl.when(n == 3)
    def _():
        out_ref[...] = x_ref[...] + _dot(acc_scr[...].astype(BF16), wout_ref[...])


def _merge_call(outs, x, g, wg, wb, wout, tm):
    n = x.shape[0]
    o_spec = pl.BlockSpec((tm, 256), lambda i, j: (i, 0))
    return pl.pallas_call(
        _merge_kernel,
        grid=(n // tm, 4),
        in_specs=[
            o_spec, o_spec, o_spec, o_spec,
            pl.BlockSpec((tm, D_MODEL), lambda i, j: (i, 0)),
            pl.BlockSpec((1, D_MODEL), lambda i, j: (0, 0)),
            pl.BlockSpec((D_MODEL, D_MODEL), lambda i, j: (0, j)),
            pl.BlockSpec((1, 256, D_MODEL), lambda i, j: (j, 0, 0)),
            pl.BlockSpec((D_MODEL, D_MODEL), lambda i, j: (0, 0)),
        ],
        out_specs=pl.BlockSpec((tm, D_MODEL), lambda i, j: (i, 0)),
        out_shape=jax.ShapeDtypeStruct((n, D_MODEL), F32),
        scratch_shapes=[pltpu.VMEM((tm, D_MODEL), BF16), pltpu.VMEM((tm, D_MODEL), F32)],
        compiler_params=_cparams("parallel", "arbitrary"),
    )(*outs, x, g, wg, wb, wout)


def _mlp_kernel(x_ref, g_ref, gf_ref, wup_ref, wdn_ref, out_ref, h_scr, acc_scr, *, final):
    j = pl.program_id(1)

    @pl.when(j == 0)
    def _():
        h_scr[...] = _rms_bf16(x_ref[...], g_ref[...])
        acc_scr[...] = jnp.zeros_like(acc_scr)

    u = jnp.maximum(_dot(h_scr[...], wup_ref[...]), 0.0)
    acc_scr[...] += _dot((u * u).astype(BF16), wdn_ref[...])

    @pl.when(j == pl.num_programs(1) - 1)
    def _():
        y = x_ref[...] + acc_scr[...]
        if final:
            y = y * lax.rsqrt(jnp.mean(y * y, axis=-1, keepdims=True) + EPS) * gf_ref[...]
        out_ref[...] = y


def _mlp_call(x, g, gf, wup, wdn, tm, final):
    n = x.shape[0]
    tf = 1024
    return pl.pallas_call(
        functools.partial(_mlp_kernel, final=final),
        grid=(n // tm, D_FF // tf),
        in_specs=[
            pl.BlockSpec((tm, D_MODEL), lambda i, j: (i, 0)),
            pl.BlockSpec((1, D_MODEL), lambda i, j: (0, 0)),
            pl.BlockSpec((1, D_MODEL), lambda i, j: (0, 0)),
            pl.BlockSpec((D_MODEL, tf), lambda i, j: (0, j)),
            pl.BlockSpec((tf, D_MODEL), lambda i, j: (j, 0)),
        ],
        out_specs=pl.BlockSpec((tm, D_MODEL), lambda i, j: (i, 0)),
        out_shape=jax.ShapeDtypeStruct((n, D_MODEL), F32),
        scratch_shapes=[pltpu.VMEM((tm, D_MODEL), BF16), pltpu.VMEM((tm, D_MODEL), F32)],
        compiler_params=_cparams("parallel", "arbitrary"),
    )(x, g, gf, wup, wdn)


def _seg_ones(rows_per_head, cols_per_head):
    r = np.arange(N_HEADS * rows_per_head)[:, None] // rows_per_head
    c = np.arange(N_HEADS * cols_per_head)[None, :] // cols_per_head
    return (r == c).astype(np.float32)


def _token_selectors(c):
    e = np.zeros((c, 2 * N_HEADS * c, 256), np.float32)
    for t in range(c):
        for p in range(2):
            for h in range(N_HEADS):
                e[t, p * N_HEADS * c + t * N_HEADS + h, h * HEAD_V:(h + 1) * HEAD_V] = 1.0
    return e


def _rows(*vecs, width=256, nrows=8):
    out = jnp.zeros((nrows, width), F32)
    for i, v in enumerate(vecs):
        v = jnp.asarray(v, F32).reshape(-1)
        out = out.at[i, :v.shape[0]].set(v)
    return out


def _layout_w_in(w):
    d = w.shape[0]
    zeros = lambda n: jnp.zeros((d, n), w.dtype)
    gl0, ml0, gt0 = 2048, 2832, 3608
    parts = [
        w[:, 0:2048],
        w[:, gl0:gl0 + 512], w[:, gl0 + 512:gl0 + 528], zeros(112), w[:, gl0 + 528:gl0 + 784], zeros(128),
        w[:, ml0:ml0 + 512], w[:, ml0 + 512:ml0 + 520], zeros(120), w[:, ml0 + 520:ml0 + 776], zeros(128),
    ]
    wz = jnp.concatenate(parts, axis=1)
    assert wz.shape[1] == Z_COLS
    return wz.astype(BF16), w[:, gt0:gt0 + 4 * D_MODEL].astype(BF16)


class _Group:
    def __init__(self, nb, trow, tv, row0, max_seq):
        self.nb, self.trow, self.tv, self.row0 = nb, trow, tv, row0
        self.full = tv == trow
        self.c = SUB if self.full else trow
        self.tvs = min(tv, self.c)
        self.nsub = trow // self.c
        self.nseq = max(d for d in range(1, max_seq + 1) if nb % d == 0 and row0 % (d * trow) == 0)
        self.rows = self.nseq * trow
        self.nsteps = nb // self.nseq
        self.base_blk = row0 // self.rows
        nblk = trow // self.c
        self.srows = self.c * max(d for d in range(1, min(nblk, 64) + 1) if nblk % d == 0)


def kernel(x_prompt, x_sample, state_hgrn, state_rwkv, state_rwkv_shift, state_gla, state_mlstm_c, state_mlstm_n, state_mlstm_m, state_mlstm_conv, meta_tokens, norm_mix, norm_mlp, norm_final, w_in, hgrn_lb, hgrn_norm, rwkv_mu, rwkv_w0, rwkv_w_up, rwkv_a0, rwkv_a_up, rwkv_g_up, rwkv_k_k, rwkv_k_a, rwkv_r_k, rwkv_ln_w, rwkv_ln_b, gla_gate_up, gla_gate_b, gla_norm, mlstm_conv_w, mlstm_conv_b, mlstm_i_b, mlstm_f_b, mlstm_norm, w_branch, w_out, w_up, w_down):
    depth = w_in.shape[0]
    bp, seq, _ = x_prompt.shape
    bs, dseq, _ = x_sample.shape
    tp = N_META + seq
    assert tp % SUB == 0 and CONV_W - 1 <= dseq <= SAMPLE_PAD
    dt = x_prompt.dtype

    meta = jnp.broadcast_to(meta_tokens.astype(dt)[None], (bp, N_META, D_MODEL))
    xp = jnp.concatenate([meta, x_prompt], axis=1).reshape(bp * tp, D_MODEL)
    xs = jnp.pad(x_sample, ((0, 0), (0, SAMPLE_PAD - dseq), (0, 0))).reshape(bs * SAMPLE_PAD, D_MODEL)
    n_real = bp * tp + bs * SAMPLE_PAD
    n_rows = -(-n_real // ROW_PAD) * ROW_PAD
    x = jnp.concatenate([xp, xs, jnp.zeros((n_rows - n_real, D_MODEL), dt)], axis=0).astype(F32)
    tm = _pick_tile(n_rows, 1024, 128)

    gp = _Group(bp, tp, tp, 0, 1)
    gs = _Group(bs, SAMPLE_PAD, dseq, bp * tp, 16)

    ones_vv = jnp.asarray(_seg_ones(64, 64), BF16)
    ones_gl = jnp.asarray(_seg_ones(GLA_DK, 64), BF16)
    mask_hg = jnp.asarray(_seg_ones(64, 64), F32)
    mask_gl = jnp.asarray(_seg_ones(64, GLA_DK), F32)
    eye64 = jnp.asarray(np.eye(64, dtype=np.float32), BF16)
    sel8 = jnp.asarray(np.eye(8, 128, dtype=np.float32), BF16)
    e_sel = {g.c: jnp.asarray(_token_selectors(g.c), BF16) for g in (gp, gs)}

    lb_cs = jnp.cumsum(jax.nn.softmax(hgrn_lb.astype(F32), axis=0), axis=0)
    lb_all = lb_cs - lb_cs[:1]

    def init_states(states, nb, zero):
        s_hg, s_rw, shift, s_gl, c_ml, n_ml, m_ml, conv = states
        if zero:
            z = lambda *shp: jnp.zeros((depth, nb) + shp, F32)
            return dict(hg=z(256, 64), rw=z(64, 256), shift=z(1, SEC), gl=z(256, GLA_DK), cm=z(128, 64),
                        nm=z(1, 128), mm=z(1, 128), cv=z(8, 256))
        f = lambda a: a.astype(F32)
        return dict(
            hg=f(s_hg).transpose(0, 1, 2, 4, 3).reshape(depth, nb, 256, 64),
            rw=f(s_rw).transpose(0, 1, 4, 2, 3).reshape(depth, nb, 64, 256),
            shift=f(shift).reshape(depth, nb, 1, SEC),
            gl=f(s_gl).transpose(0, 1, 2, 4, 3).reshape(depth, nb, 256, GLA_DK),
            cm=f(c_ml).reshape(depth, nb, 128, 64),
            nm=f(n_ml).reshape(depth, nb, 1, 128),
            mm=jnp.pad(f(m_ml), ((0, 0), (0, 0), (0, 124))).reshape(depth, nb, 1, 128),
            cv=jnp.pad(f(conv), ((0, 0), (0, 0), (5, 0), (0, 0))),
        )

    sample_states = (state_hgrn, state_rwkv, state_rwkv_shift, state_gla,
                     state_mlstm_c, state_mlstm_n, state_mlstm_m, state_mlstm_conv)
    st_p = init_states(sample_states, bp, True)
    st_s = init_states(sample_states, bs, False)
    new_p = {k: [] for k in st_p}
    new_s = {k: [] for k in st_s}

    for l in range(depth):
        wz, wg = _layout_w_in(w_in[l])
        g_mix = norm_mix[l].reshape(1, D_MODEL).astype(F32)
        z = _in_proj(x, g_mix, wz, tm)

        lb = lb_all[l]
        par_hg = _rows(jnp.log(lb), jnp.log1p(-lb), 1.0 - lb, hgrn_norm[l])
        par_gl = _rows(gla_gate_b[l], jnp.zeros((1,)), jnp.zeros((1,)), gla_norm[l])
        gup_gl = jnp.zeros((128, 128), F32).at[:gla_gate_up.shape[1], :].set(gla_gate_up[l]).astype(BF16)
        gup_dummy = jnp.zeros((128, 128), BF16)
        par_rw = _rows(rwkv_w0[l], rwkv_a0[l], rwkv_k_k[l], rwkv_k_a[l], rwkv_r_k[l])
        par_ln = _rows(rwkv_ln_w[l], rwkv_ln_b[l])
        nw, na = rwkv_w_up.shape[1], rwkv_a_up.shape[1]
        wup_p = jnp.zeros((256, 256), F32).at[0:nw].set(rwkv_w_up[l]).astype(BF16)
        aup_p = jnp.zeros((256, 256), F32).at[nw:nw + na].set(rwkv_a_up[l]).astype(BF16)
        gup_p = jnp.zeros((256, 256), F32).at[nw + na:].set(rwkv_g_up[l]).astype(BF16)
        mu = rwkv_mu[l].reshape(1, SEC).astype(F32)
        gate_b = jnp.concatenate([mlstm_i_b[l], mlstm_f_b[l]]).astype(F32)
        par_ml = _rows(mlstm_conv_b[l], gate_b, mlstm_norm[l])
        bcol_ml = gate_b.reshape(8, 1)
        cw_ml = _rows(*[mlstm_conv_w[l, j] for j in range(CONV_W)])

        o_hg = o_rw = o_gl = o_ml = None
        for g, st, new in ((gp, st_p, new_p), (gs, st_s, new_s)):
            prev = lambda o: jnp.zeros((n_rows, 256), F32) if o is None else o
            o_hg, s_hg = _gla_call(z, 0, par_hg, gup_dummy, ones_vv, ones_vv, mask_hg, st["hg"][l], prev(o_hg),
                                   mode="hgrn", dk=64, nrows=n_rows, g=g)
            o_gl, s_gl = _gla_call(z, 2, par_gl, gup_gl, ones_gl, ones_vv, mask_gl, st["gl"][l], prev(o_gl),
                                   mode="gla", dk=GLA_DK, nrows=n_rows, g=g)
            (r_, w_, k_, v_, kk_, ka_, g_, bonus_, nshift) = _rwkv_prep_call(
                z, mu, par_rw, wup_p, aup_p, gup_p, ones_vv, st["shift"][l], nrows=n_rows, g=g)
            cols = [a.reshape(n_rows * N_HEADS, 64) for a in (w_, ka_, k_, kk_, r_)]
            o_rw, s_rw = _rwkv_scan_call(cols, v_, g_, bonus_, par_ln, e_sel[g.c], eye64, ones_vv, st["rw"][l],
                                         prev(o_rw), nrows=n_rows, g=g)
            o_ml, s_cm, s_nm, s_mm, s_cv = _mlstm_call(z, par_ml, bcol_ml, cw_ml, sel8, st["cm"][l], st["nm"][l],
                                                        st["mm"][l], st["cv"][l], prev(o_ml), nrows=n_rows, g=g)
            for key, val in (("hg", s_hg), ("rw", s_rw), ("shift", nshift), ("gl", s_gl), ("cm", s_cm),
                             ("nm", s_nm), ("mm", s_mm), ("cv", s_cv)):
                new[key].append(val)

        x = _merge_call((o_hg, o_rw, o_gl, o_ml), x, g_mix, wg, w_branch[l].astype(BF16), w_out[l].astype(BF16), tm)
        x = _mlp_call(x, norm_mlp[l].reshape(1, D_MODEL).astype(F32), norm_final.reshape(1, D_MODEL).astype(F32),
                      w_up[l].astype(BF16), w_down[l].astype(BF16), tm, final=(l == depth - 1))

    y_prompt = x[:bp * tp].reshape(bp, tp, D_MODEL)[:, N_META:].astype(dt)
    y_sample = x[gs.row0:gs.row0 + bs * SAMPLE_PAD].reshape(bs, SAMPLE_PAD, D_MODEL)[:, :dseq].astype(dt)

    def finish(new, nb):
        st = {k: jnp.stack(v) for k, v in new.items()}
        return (
            st["hg"].reshape(depth, nb, N_HEADS, 64, 64).transpose(0, 1, 2, 4, 3),
            st["rw"].reshape(depth, nb, 64, N_HEADS, 64).transpose(0, 1, 3, 4, 2),
            st["shift"].reshape(depth, nb, SEC),
            st["gl"].reshape(depth, nb, N_HEADS, 64, GLA_DK).transpose(0, 1, 2, 4, 3),
            st["cm"].reshape(depth, nb, N_HEADS, MLSTM_DK, 64),
            st["nm"].reshape(depth, nb, N_HEADS, MLSTM_DK),
            st["mm"].reshape(depth, nb, 128)[:, :, :N_HEADS],
            st["cv"][:, :, 5:8, :],
        )

    outs_p = tuple(a.astype(dt) for a in finish(new_p, bp))
    outs_s = tuple(a.astype(dt) for a in finish(new_s, bs))
    return (y_prompt, y_sample) + outs_p + outs_s
```

```python
import functools

import numpy as np
import jax
import jax.numpy as jnp
from jax import lax
from jax.experimental import pallas as pl
from jax.experimental.pallas import tpu as pltpu

F32 = jnp.float32
BF16 = jnp.bfloat16

D_MODEL = 1024
N_META = 16
N_HEADS = 4
HEAD_V = 64
GLA_DK = 32
MLSTM_DK = 32
GLA_GATE_TAU = 16.0
RWKV_LN_EPS = 64e-5
CONV_W = 4
D_FF = 4 * D_MODEL
EPS = 1e-6

SEC = 1024
Z_COLS = 4 * SEC
SUB = 16
SAMPLE_PAD = 8
ROW_PAD = 512
MLSTM_CHUNK = 128
VMEM_LIMIT = 48 * 1024 * 1024


def _cparams(*sem):
    return pltpu.CompilerParams(dimension_semantics=sem, vmem_limit_bytes=VMEM_LIMIT)


def _pick_tile(n, max_tile, mult=8):
    best = None
    t = mult
    while t <= min(n, max_tile):
        if n % t == 0:
            best = t
        t += mult
    assert best is not None, (n, max_tile, mult)
    return best


def _split2(x):
    hi = x.astype(BF16)
    lo = (x - hi.astype(F32)).astype(BF16)
    return hi, lo


def _split3(x):
    hi = x.astype(BF16)
    r1 = x - hi.astype(F32)
    mid = r1.astype(BF16)
    lo = (r1 - mid.astype(F32)).astype(BF16)
    return hi, mid, lo


def _dot(a, b):
    return jnp.dot(a, b, preferred_element_type=F32)


def _dot_nt(a, b):
    return lax.dot_general(a, b, (((1,), (1,)), ((), ())), preferred_element_type=F32)


def _dot_tn(a, b):
    return lax.dot_general(a, b, (((0,), (0,)), ((), ())), preferred_element_type=F32)


def _seg_dot(x, m_bf16, parts):
    ps = _split2(x) if parts == 2 else _split3(x)
    acc = _dot(ps[0], m_bf16)
    for p in ps[1:]:
        acc = acc + _dot(p, m_bf16)
    return acc


def _seg_dot_l(m_bf16, x, parts):
    ps = _split2(x) if parts == 2 else _split3(x)
    acc = _dot(m_bf16, ps[0])
    for p in ps[1:]:
        acc = acc + _dot(m_bf16, p)
    return acc


def _log_sigmoid(x):
    return jnp.minimum(x, 0.0) - jnp.log1p(jnp.exp(-jnp.abs(x)))


def _silu(x):
    return x * jax.nn.sigmoid(x)


def _tri(n, upper=False):
    r = lax.broadcasted_iota(jnp.int32, (n, n), 0)
    c = lax.broadcasted_iota(jnp.int32, (n, n), 1)
    return (r <= c) if upper else (r >= c)


def _cumsum_rows(x, rowi):
    d = 1
    while d < x.shape[0]:
        x = x + jnp.where(rowi >= d, pltpu.roll(x, d, axis=0), 0.0)
        d *= 2
    return x


def _rms_bf16(x, g):
    return (x * lax.rsqrt(jnp.mean(x * x, axis=-1, keepdims=True) + EPS) * g).astype(BF16)


def _in_proj_kernel(x_ref, g_ref, w_ref, z_ref, h_scr):
    @pl.when(pl.program_id(1) == 0)
    def _():
        h_scr[...] = _rms_bf16(x_ref[...], g_ref[...])

    z_ref[...] = _dot(h_scr[...], w_ref[...])


def _in_proj(x, g, w, tm):
    n = x.shape[0]
    tn = 1024
    return pl.pallas_call(
        _in_proj_kernel,
        grid=(n // tm, Z_COLS // tn),
        in_specs=[
            pl.BlockSpec((tm, D_MODEL), lambda i, j: (i, 0)),
            pl.BlockSpec((1, D_MODEL), lambda i, j: (0, 0)),
            pl.BlockSpec((D_MODEL, tn), lambda i, j: (0, j)),
        ],
        out_specs=pl.BlockSpec((tm, tn), lambda i, j: (i, j)),
        out_shape=jax.ShapeDtypeStruct((n, Z_COLS), F32),
        scratch_shapes=[pltpu.VMEM((tm, D_MODEL), BF16)],
        compiler_params=_cparams("parallel", "arbitrary"),
    )(x, g, w)


def _gla_kernel(z_ref, par_ref, gup_ref, okv_ref, ovv_ref, mask_ref, st_in_ref, o_prev_ref,
                o_ref, st_out_ref, *st_scrs, mode, c, nsub, grp, tv, dk, nseq, trow):
    del o_prev_ref
    voff = 512 if mode == "hgrn" else 256
    rowi = lax.broadcasted_iota(jnp.int32, (c, 1), 0)
    valid = rowi < tv

    def sub_block(st_scr, r0):
        rows = pl.ds(r0, c)
        if mode == "hgrn":
            q = z_ref[rows, 0:256]
            hf = z_ref[rows, 256:512]
            g = z_ref[rows, 768:1024]
            a = par_ref[0:1, :]
            cc = par_ref[1:2, :] + _log_sigmoid(hf)
            loga = jnp.maximum(a, cc) + jnp.log1p(jnp.exp(-jnp.abs(a - cc)))
            k = par_ref[2:3, :] * jax.nn.sigmoid(-hf)
        else:
            q = z_ref[rows, 0:128] * (GLA_DK ** -0.5)
            k = z_ref[rows, 128:256]
            ga = z_ref[rows, 512:640]
            g = z_ref[rows, 640:896]
            gl = _dot(ga.astype(BF16), gup_ref[...]) + par_ref[0:1, 0:128]
            loga = _log_sigmoid(gl) * (1.0 / GLA_GATE_TAU)
        v = z_ref[rows, voff:voff + 256]
        if tv < c:
            loga = jnp.where(valid, loga, 0.0)
            k = jnp.where(valid, k, 0.0)
            v = jnp.where(valid, v, 0.0)
        b = _cumsum_rows(loga, rowi)

        o = _dot_nt((q * jnp.exp(b)).astype(BF16), st_scr[...].astype(BF16))

        lo_rows = [8 * (s // 8) for s in range(tv)]
        pieces = []
        for s in range(tv):
            rs = slice(lo_rows[s], c)
            e = jnp.exp(jnp.where(rowi[rs] >= s, b[rs] - b[s:s + 1, :], -jnp.inf))
            pieces.append(q[rs] * e * k[s:s + 1, :])
        sm = _dot(jnp.concatenate(pieces, axis=0).astype(BF16), okv_ref[...])
        acc = {}
        off = 0
        for s in range(tv):
            n = c - lo_rows[s]
            contrib = sm[off:off + n] * v[s:s + 1, :]
            off += n
            acc[lo_rows[s]] = contrib if lo_rows[s] not in acc else acc[lo_rows[s]] + contrib
        for lo, a_ in acc.items():
            o = o + (a_ if lo == 0 else jnp.concatenate([jnp.zeros((lo, 256), F32), a_], axis=0))

        bl = b[c - 1:c, :]
        kdec = k * jnp.exp(bl - b)
        upd = _dot_tn(v.astype(BF16), kdec.astype(BF16))
        st_scr[...] = st_scr[...] * jnp.exp(bl) + upd * mask_ref[...]

        ms = _seg_dot(o * o, ovv_ref[...], 2) * (1.0 / HEAD_V)
        o_ref[rows, :] = o * lax.rsqrt(ms + EPS) * par_ref[3:4, :] * _silu(g)

    def sequence(st_scr, seq):
        st = st_in_ref[seq]
        st_scr[...] = jnp.concatenate([st] * N_HEADS, axis=1) * mask_ref[...]

        def group(gi, carry):
            for u in range(grp):
                sub_block(st_scr, pl.multiple_of(seq * trow + (gi * grp + u) * c, 8))
            return carry

        lax.fori_loop(0, nsub // grp, group, 0)
        st = st_scr[...]
        acc = st[:, 0:dk]
        for h in range(1, N_HEADS):
            acc = acc + st[:, h * dk:(h + 1) * dk]
        st_out_ref[seq] = acc

    lanes = len(st_scrs)

    def seq_group(si, carry):
        for u in range(lanes):
            sequence(st_scrs[u], si * lanes + u)
        return carry

    if nseq == lanes:
        seq_group(0, 0)
    else:
        lax.fori_loop(0, nseq // lanes, seq_group, 0)


def _gla_call(z, sec, par, gup, okv, ovv, mask, st_in, o_prev, *, mode, dk, nrows, g):
    hk = N_HEADS * dk
    lanes = 2 if g.nseq % 2 == 0 else 1
    grp = max(d for d in (4, 3, 2, 1) if g.nsub % d == 0)
    kern = functools.partial(_gla_kernel, mode=mode, c=g.c, nsub=g.nsub, grp=grp, tv=g.tvs, dk=dk,
                             nseq=g.nseq, trow=g.trow)
    const = lambda b: (0, 0)
    return pl.pallas_call(
        kern,
        grid=(g.nsteps,),
        in_specs=[
            pl.BlockSpec((g.rows, SEC), lambda b: (g.base_blk + b, sec)),
            pl.BlockSpec((8, 256), const),
            pl.BlockSpec((128, 128), const),
            pl.BlockSpec((hk, 256), const),
            pl.BlockSpec((256, 256), const),
            pl.BlockSpec((256, hk), const),
            pl.BlockSpec((g.nseq, 256, dk), lambda b: (b, 0, 0)),
            pl.BlockSpec(memory_space=pl.ANY),
        ],
        input_output_aliases={7: 0},
        out_specs=[
            pl.BlockSpec((g.rows, 256), lambda b: (g.base_blk + b, 0)),
            pl.BlockSpec((g.nseq, 256, dk), lambda b: (b, 0, 0)),
        ],
        out_shape=[
            jax.ShapeDtypeStruct((nrows, 256), F32),
            jax.ShapeDtypeStruct((g.nb, 256, dk), F32),
        ],
        scratch_shapes=[pltpu.VMEM((256, hk), F32)] * lanes,
        compiler_params=_cparams("parallel"),
    )(z, par, gup, okv, ovv, mask, st_in, o_prev)


def _rwkv_kernel(z_ref, mu_ref, par_ref, wup_ref, aup_ref, gup_ref, ovv_ref, hmask_ref, ehead_ref, mask_ref,
                 shift_ref, st_in_ref, o_prev_ref,
                 o_ref, nshift_ref, st_out_ref,
                 prev_scr, r_scr, lw_scr, k_scr, v_scr, kk_scr, ka_scr, g_scr, bonus_scr, y_scr, *st_scrs,
                 full, c, nsub, tv, nseq, trow):
    del o_prev_ref
    step = pl.program_id(1)
    zr = z_ref[...]
    nrow = zr.shape[0]
    rolled = pltpu.roll(zr, 1, axis=0)
    if full:
        @pl.when(step == 0)
        def _():
            prev_scr[0:1, :] = shift_ref[0]

        rowi = lax.broadcasted_iota(jnp.int32, (nrow, 1), 0)
        prev = jnp.where(rowi == 0, prev_scr[0:1, :], rolled)
        prev_scr[0:1, :] = zr[nrow - 1:nrow, :]

        @pl.when(step == pl.num_programs(1) - 1)
        def _():
            nshift_ref[0] = zr[nrow - 1:nrow, :]
    else:
        prev_scr[...] = rolled
        for i in range(nseq):
            prev_scr[i * trow:i * trow + 1, :] = shift_ref[i]
            nshift_ref[i] = zr[i * trow + tv - 1:i * trow + tv, :]
        prev = prev_scr[...]
    zs = zr + (prev - zr) * mu_ref[...]
    r = zs[:, 0:256]
    k = zs[:, 256:512]
    v = zs[:, 512:768]
    lo = zs[:, 768:1024]
    w0, a0, k_k, k_a, r_k = (par_ref[i:i + 1, :] for i in range(5))
    wl = w0 + _dot(jnp.tanh(lo).astype(BF16), wup_ref[...])
    wexp = -(jnp.maximum(-wl, 0.0) + jnp.log1p(jnp.exp(-jnp.abs(wl)))) - 0.5
    lw = -jnp.exp(wexp)
    a = jax.nn.sigmoid(a0 + _dot(lo.astype(BF16), aup_ref[...]))
    g_scr[...] = _dot(jax.nn.sigmoid(lo).astype(BF16), gup_ref[...])
    kkp = k * k_k
    nrm = jnp.sqrt(_seg_dot(kkp * kkp, ovv_ref[...], 3))
    kk = kkp / jnp.maximum(nrm, 1e-12)
    k2 = k * (1.0 + (a - 1.0) * k_a)
    bonus_scr[...] = _seg_dot(r * k2 * r_k, ovv_ref[...], 3) * v
    ka = kk * a
    if tv < trow:
        okrow = lax.rem(lax.broadcasted_iota(jnp.int32, (nrow, 1), 0), trow) < tv
        lw, k2, v, kk, ka = (jnp.where(okrow, t_, 0.0) for t_ in (lw, k2, v, kk, ka))
    r_scr[...] = r
    lw_scr[...] = lw
    k_scr[...] = k2
    v_scr[...] = v
    kk_scr[...] = kk
    ka_scr[...] = ka

    rowc = lax.broadcasted_iota(jnp.int32, (c, 1), 0)
    src2 = lax.broadcasted_iota(jnp.int32, (1, 2 * N_HEADS * c), 1) & (c - 1)
    src = src2[:, 0:N_HEADS * c]
    strict = rowc > src
    incl = rowc >= src

    def sub_block(st_scr, r0):
        rows = pl.ds(r0, c)
        lw_, r_, k_, v_, kk_, ka_ = (s_[rows, :] for s_ in (lw_scr, r_scr, k_scr, v_scr, kk_scr, ka_scr))
        gam = _cumsum_rows(lw_, rowc)
        gl = gam[c - 1:c, :]
        ginv = jnp.exp(-gam)
        gend = jnp.exp(gl - gam)
        hm = hmask_ref[...]
        tile4 = lambda t_: jnp.concatenate([t_] * N_HEADS, axis=0) * hm
        lhs = jnp.concatenate([kk_ * jnp.exp(gam - lw_), r_ * jnp.exp(gam)], axis=0)
        rhs = jnp.concatenate([tile4(ka_ * ginv), tile4(k_ * ginv)], axis=0)
        lh, ll = _split2(lhs)
        rh, rl = _split2(rhs)
        g2 = _dot_nt(jnp.concatenate([lh, ll], axis=0), rh)
        gm = g2[0:2 * c] + g2[2 * c:4 * c] + _dot_nt(lh, rl)
        nc = N_HEADS * c
        m_ab = jnp.where(strict, gm[0:c, 0:nc], 0.0)
        m_bk = jnp.where(strict, gm[0:c, nc:2 * nc], 0.0)
        m_ra = jnp.where(incl, gm[c:2 * c, 0:nc], 0.0)
        m_rk = jnp.where(incl, gm[c:2 * c, nc:2 * nc], 0.0)
        x0 = _dot_nt(lh, st_scr[...].astype(BF16))
        vbig = tile4(v_).astype(BF16)
        u = x0[0:c] + _dot(m_bk.astype(BF16), vbig)
        m2 = jnp.concatenate(_split2(m_ab), axis=1)
        nsolve = min(tv, c - 1)
        cols = _dot(jnp.concatenate([jnp.where(src2 == s, m2, jnp.zeros_like(m2)) for s in range(nsolve)], axis=0),
                    ehead_ref[...])
        for s in range(nsolve):
            u = u - cols[s * c:(s + 1) * c] * u[s:s + 1, :]
        ubig = tile4(u).astype(BF16)
        y_scr[rows, :] = x0[c:2 * c] + _dot(jnp.concatenate([-m_ra, m_rk], axis=1).astype(BF16),
                                            jnp.concatenate([ubig, vbig], axis=0))
        upd = _dot_tn(jnp.concatenate([-u, v_], axis=0).astype(BF16),
                      jnp.concatenate([ka_ * gend, k_ * gend], axis=0).astype(BF16))
        st_scr[...] = st_scr[...] * jnp.exp(gl) + upd * mask_ref[...]

    def load_state(st_scr, seq):
        st_scr[...] = jnp.concatenate([st_in_ref[seq]] * N_HEADS, axis=1) * mask_ref[...]

    def store_state(st_scr, seq):
        st = st_scr[...]
        acc = st[:, 0:64]
        for h in range(1, N_HEADS):
            acc = acc + st[:, h * 64:(h + 1) * 64]
        st_out_ref[seq] = acc

    if full:
        @pl.when(step == 0)
        def _():
            load_state(st_scrs[0], 0)

        def blocks(j, carry):
            sub_block(st_scrs[0], pl.multiple_of(j * c, 8))
            return carry

        lax.fori_loop(0, nsub, blocks, 0, unroll=2)

        @pl.when(step == pl.num_programs(1) - 1)
        def _():
            store_state(st_scrs[0], 0)
    else:
        lanes = len(st_scrs)

        def seq_group(si, carry):
            for u_ in range(lanes):
                seq = si * lanes + u_
                load_state(st_scrs[u_], seq)
                sub_block(st_scrs[u_], pl.multiple_of(seq * trow, 8))
                store_state(st_scrs[u_], seq)
            return carry

        lax.fori_loop(0, nseq // lanes, seq_group, 0)

    y = y_scr[...]
    mean = _seg_dot(y, ovv_ref[...], 3) * (1.0 / HEAD_V)
    d = y - mean
    var = _seg_dot(d * d, ovv_ref[...], 3) * (1.0 / HEAD_V)
    ln = d * lax.rsqrt(var + RWKV_LN_EPS) * par_ref[5:6, :] + par_ref[6:7, :]
    o_ref[...] = (ln + bonus_scr[...]) * g_scr[...]


def _rwkv_call(z, mu, par, wup, aup, gup, ovv, hmask, ehead, mask, shift, st_in, o_prev, *, nrows, g):
    if g.full:
        assert g.nseq == 1
        rblk = g.srows
        nsteps = g.trow // rblk
        prev_rows = 8
        lanes = 1
    else:
        assert g.nsub == 1
        rblk = g.rows
        nsteps = 1
        prev_rows = rblk
        lanes = 2 if g.nseq % 2 == 0 else 1
    base = g.row0 // rblk
    kern = functools.partial(_rwkv_kernel, full=g.full, c=g.c, nsub=rblk // g.c, tv=g.tv, nseq=g.nseq, trow=g.trow)
    const = lambda b, s: (0, 0)
    return pl.pallas_call(
        kern,
        grid=(g.nsteps, nsteps),
        in_specs=[
            pl.BlockSpec((rblk, SEC), lambda b, s: (base + b * nsteps + s, 1)),
            pl.BlockSpec((1, SEC), const),
            pl.BlockSpec((8, 256), const),
            pl.BlockSpec((256, 256), const),
            pl.BlockSpec((256, 256), const),
            pl.BlockSpec((256, 256), const),
            pl.BlockSpec((256, 256), const),
            pl.BlockSpec((N_HEADS * g.c, 256), const),
            pl.BlockSpec((2 * N_HEADS * g.c, 256), const),
            pl.BlockSpec((256, 256), const),
            pl.BlockSpec((g.nseq, 1, SEC), lambda b, s: (b, 0, 0)),
            pl.BlockSpec((g.nseq, 256, 64), lambda b, s: (b, 0, 0)),
            pl.BlockSpec(memory_space=pl.ANY),
        ],
        input_output_aliases={12: 0},
        out_specs=[
            pl.BlockSpec((rblk, 256), lambda b, s: (base + b * nsteps + s, 0)),
            pl.BlockSpec((g.nseq, 1, SEC), lambda b, s: (b, 0, 0)),
            pl.BlockSpec((g.nseq, 256, 64), lambda b, s: (b, 0, 0)),
        ],
        out_shape=[
            jax.ShapeDtypeStruct((nrows, 256), F32),
            jax.ShapeDtypeStruct((g.nb, 1, SEC), F32),
            jax.ShapeDtypeStruct((g.nb, 256, 64), F32),
        ],
        scratch_shapes=[pltpu.VMEM((prev_rows, SEC), F32)] + [pltpu.VMEM((rblk, 256), F32)] * 9
        + [pltpu.VMEM((256, 256), F32)] * lanes,
        compiler_params=_cparams("parallel", "arbitrary"),
    )(z, mu, par, wup, aup, gup, ovv, hmask, ehead, mask, shift, st_in, o_prev)


def _mlstm_kernel(z_ref, par_ref, bcol_ref, cw_ref, sel_ref, cm_in, nm_in, mm_in, cv_in, o_prev_ref,
                  o_ref, cm_out, nm_out, mm_out, cv_out,
                  *scratch, chunks, nseq, trow):
    del o_prev_ref
    lanes = len(scratch) // 4

    def chunk(scr, row0, c, tv):
        xbuf, cm_scr, nm_scr, mm_scr = scr
        rows = pl.ds(row0, c)
        mqk = z_ref[rows, 0:256]
        mv = z_ref[rows, 256:512]
        gi = z_ref[rows, 512:640]
        mo = z_ref[rows, 640:896]
        xbuf[8:8 + c, :] = mqk
        conv = par_ref[0:1, :] + xbuf[5:5 + c, :] * cw_ref[0:1, :]
        for j in range(1, CONV_W):
            conv = conv + xbuf[5 + j:5 + j + c, :] * cw_ref[j:j + 1, :]
        xbuf[5:8, :] = xbuf[8 + tv - 3:8 + tv, :]
        act = _silu(conv)
        q = act[:, 0:128]
        k = act[:, 128:256] * (MLSTM_DK ** -0.5)

        rowi = lax.broadcasted_iota(jnp.int32, (c, 1), 0)
        coli = lax.broadcasted_iota(jnp.int32, (1, c), 1)
        gcol = gi + par_ref[1:2, 0:128]
        lfc = _log_sigmoid(gcol)
        graw = None
        for part in _split3(gi):
            t_ = _dot_nt(sel_ref[...], part)
            graw = t_ if graw is None else graw + t_
        grow = graw + bcol_ref[...]
        lfr = _log_sigmoid(grow)
        if tv < c:
            gcol = jnp.where(rowi < tv, gcol, -jnp.inf)
            lfc = jnp.where(rowi < tv, lfc, 0.0)
            grow = jnp.where(coli < tv, grow, -jnp.inf)
            lfr = jnp.where(coli < tv, lfr, 0.0)
        b_col = _seg_dot_l(_tri(c).astype(BF16), lfc, 3)
        b_row = _seg_dot(lfr, _tri(c, upper=True).astype(BF16), 3)
        causal = _tri(c)
        mm_old = mm_scr[...]
        nm_old = nm_scr[...]
        cm_old = cm_scr[...]
        lane = lax.broadcasted_iota(jnp.int32, (1, 128), 1)
        mm_new = mm_old
        cm_parts, nm_parts, o_parts = [], [], []

        for h in range(N_HEADS):
            bc = b_col[:, 4 + h:5 + h]
            br = b_row[4 + h:5 + h, :]
            ir = grow[h:h + 1, :]
            ic = gcol[:, h:h + 1]
            dmat = jnp.where(causal, bc - br + ir, -jnp.inf)
            mprev = mm_old[0:1, h:h + 1]
            inter = bc + mprev
            m_t = jnp.maximum(inter, jnp.max(dmat, axis=1, keepdims=True))
            wts = jnp.exp(dmat - m_t)
            sc = jnp.exp(inter - m_t)
            qh = q[:, h * MLSTM_DK:(h + 1) * MLSTM_DK]
            kh = k[:, h * MLSTM_DK:(h + 1) * MLSTM_DK]
            vh = mv[:, h * HEAD_V:(h + 1) * HEAD_V]
            qk = _dot_nt(qh.astype(BF16), kh.astype(BF16)) * wts
            cmh = cm_old[h * MLSTM_DK:(h + 1) * MLSTM_DK, :]
            nh = nm_old[0:1, h * MLSTM_DK:(h + 1) * MLSTM_DK]
            num = _dot(qk.astype(BF16), vh.astype(BF16)) + sc * _dot(qh.astype(BF16), cmh.astype(BF16))
            den = jnp.sum(qk, axis=1, keepdims=True) + sc * jnp.sum(qh * nh, axis=1, keepdims=True)
            hc = num / jnp.maximum(jnp.abs(den), jnp.exp(-m_t))
            m_new = m_t[tv - 1:tv, :]
            bl = bc[tv - 1:tv, :]
            wl = jnp.exp(bl - bc + ic - m_new)
            sl = jnp.exp(bl + mprev - m_new)
            kw = kh * wl
            cm_parts.append(sl * cmh + _dot_tn(kw.astype(BF16), vh.astype(BF16)))
            nm_parts.append(sl * nh + jnp.sum(kw, axis=0, keepdims=True))
            mm_new = jnp.where(lane == h, m_new, mm_new)
            ms = jnp.mean(hc * hc, axis=1, keepdims=True)
            o_parts.append(hc * lax.rsqrt(ms + EPS))

        cm_scr[...] = jnp.concatenate(cm_parts, axis=0)
        nm_scr[...] = jnp.concatenate(nm_parts, axis=1)
        mm_scr[...] = mm_new
        o_ref[rows, :] = jnp.concatenate(o_parts, axis=1) * par_ref[2:3, :] * jax.nn.sigmoid(mo)

    def sequence(scr, seq):
        xbuf, cm_scr, nm_scr, mm_scr = scr
        cm_scr[...] = cm_in[seq]
        nm_scr[...] = nm_in[seq]
        mm_scr[...] = mm_in[seq]
        xbuf[0:8, :] = cv_in[seq]
        base = seq * trow
        for start, c, tv, count in chunks:
            if count == 1:
                first = base + start
                chunk(scr, first if isinstance(first, int) else pl.multiple_of(first, 8), c, tv)
            else:
                def body(ci, carry2, start=start, c=c, tv=tv):
                    chunk(scr, pl.multiple_of(base + start + ci * c, 8), c, tv)
                    return carry2

                lax.fori_loop(0, count, body, 0)
        cm_out[seq] = cm_scr[...]
        nm_out[seq] = nm_scr[...]
        mm_out[seq] = mm_scr[...]
        cv_out[seq] = xbuf[0:8, :]

    def seq_group(si, carry):
        for u in range(lanes):
            sequence(scratch[4 * u:4 * u + 4], si * lanes + u)
        return carry

    if nseq == lanes:
        seq_group(0, 0)
    else:
        lax.fori_loop(0, nseq // lanes, seq_group, 0)


def _mlstm_call(z, par, bcol, cw, sel, cm, nm, mm, cv, o_prev, *, nrows, g):
    if g.full:
        lead = N_META if g.trow > N_META else g.trow
        chunks = [(0, lead, lead, 1)]
        rest = g.trow - lead
        if rest:
            cmain = max(d for d in (MLSTM_CHUNK, 64, 32, 16, 8) if rest % d == 0)
            chunks.append((lead, cmain, cmain, rest // cmain))
    else:
        chunks = [(0, g.trow, g.tv, 1)]
    cmax = max(c for _, c, _, _ in chunks)
    kern = functools.partial(_mlstm_kernel, chunks=tuple(chunks), nseq=g.nseq, trow=g.trow)
    const = lambda b: (0, 0)
    st3 = lambda shp: pl.BlockSpec((g.nseq,) + shp, lambda b: (b, 0, 0))
    return pl.pallas_call(
        kern,
        grid=(g.nsteps,),
        in_specs=[
            pl.BlockSpec((g.rows, SEC), lambda b: (g.base_blk + b, 3)),
            pl.BlockSpec((8, 256), const),
            pl.BlockSpec((8, 1), const),
            pl.BlockSpec((8, 256), const),
            pl.BlockSpec((8, 128), const),
            st3((128, 64)), st3((1, 128)), st3((1, 128)), st3((8, 256)),
            pl.BlockSpec(memory_space=pl.ANY),
        ],
        input_output_aliases={9: 0},
        out_specs=[
            pl.BlockSpec((g.rows, 256), lambda b: (g.base_blk + b, 0)),
            st3((128, 64)), st3((1, 128)), st3((1, 128)), st3((8, 256)),
        ],
        out_shape=[
            jax.ShapeDtypeStruct((nrows, 256), F32),
            jax.ShapeDtypeStruct((g.nb, 128, 64), F32),
            jax.ShapeDtypeStruct((g.nb, 1, 128), F32),
            jax.ShapeDtypeStruct((g.nb, 1, 128), F32),
            jax.ShapeDtypeStruct((g.nb, 8, 256), F32),
        ],
        scratch_shapes=[
            pltpu.VMEM((8 + cmax, 256), F32),
            pltpu.VMEM((128, 64), F32),
            pltpu.VMEM((1, 128), F32),
            pltpu.VMEM((1, 128), F32),
        ] * (2 if g.nseq % 2 == 0 else 1),
        compiler_params=_cparams("parallel"),
    )(z, par, bcol, cw, sel, cm, nm, mm, cv, o_prev)


def _merge_kernel(o0_ref, o1_ref, o2_ref, o3_ref, x_ref, g_ref, wg_ref, wb_ref, wout_ref, out_ref, h_scr, acc_scr):
    n = pl.program_id(1)

    @pl.when(n == 0)
    def _():
        h_scr[...] = _rms_bf16(x_ref[...], g_ref[...])
        acc_scr[...] = jnp.zeros_like(acc_scr)

    gate = jax.nn.sigmoid(_dot(h_scr[...], wg_ref[...]))
    for idx, o_ref in enumerate((o0_ref, o1_ref, o2_ref, o3_ref)):
        @pl.when(n == idx)
        def _(o_ref=o_ref):
            acc_scr[...] += _dot(o_ref[...].astype(BF16), wb_ref[0]) * gate

    @pl.when(n == 3)
    def _():
        out_ref[...] = x_ref[...] + _dot(acc_scr[...].astype(BF16), wout_ref[...])


def _merge_call(outs, x, g, wg, wb, wout, tm):
    n = x.shape[0]
    o_spec = pl.BlockSpec((tm, 256), lambda i, j: (i, 0))
    return pl.pallas_call(
        _merge_kernel,
        grid=(n // tm, 4),
        in_specs=[
            o_spec, o_spec, o_spec, o_spec,
            pl.BlockSpec((tm, D_MODEL), lambda i, j: (i, 0)),
            pl.BlockSpec((1, D_MODEL), lambda i, j: (0, 0)),
            pl.BlockSpec((D_MODEL, D_MODEL), lambda i, j: (0, j)),
            pl.BlockSpec((1, 256, D_MODEL), lambda i, j: (j, 0, 0)),
            pl.BlockSpec((D_MODEL, D_MODEL), lambda i, j: (0, 0)),
        ],
        out_specs=pl.BlockSpec((tm, D_MODEL), lambda i, j: (i, 0)),
        out_shape=jax.ShapeDtypeStruct((n, D_MODEL), F32),
        scratch_shapes=[pltpu.VMEM((tm, D_MODEL), BF16), pltpu.VMEM((tm, D_MODEL), F32)],
        compiler_params=_cparams("parallel", "arbitrary"),
    )(*outs, x, g, wg, wb, wout)


def _mlp_kernel(x_ref, g_ref, gf_ref, wup_ref, wdn_ref, out_ref, h_scr, acc_scr, *, final):
    j = pl.program_id(1)

    @pl.when(j == 0)
    def _():
        h_scr[...] = _rms_bf16(x_ref[...], g_ref[...])
        acc_scr[...] = jnp.zeros_like(acc_scr)

    u = jnp.maximum(_dot(h_scr[...], wup_ref[...]), 0.0)
    acc_scr[...] += _dot((u * u).astype(BF16), wdn_ref[...])

    @pl.when(j == pl.num_programs(1) - 1)
    def _():
        y = x_ref[...] + acc_scr[...]
        if final:
            y = y * lax.rsqrt(jnp.mean(y * y, axis=-1, keepdims=True) + EPS) * gf_ref[...]
        out_ref[...] = y


def _mlp_call(x, g, gf, wup, wdn, tm, final):
    n = x.shape[0]
    tf = 1024
    return pl.pallas_call(
        functools.partial(_mlp_kernel, final=final),
        grid=(n // tm, D_FF // tf),
        in_specs=[
            pl.BlockSpec((tm, D_MODEL), lambda i, j: (i, 0)),
            pl.BlockSpec((1, D_MODEL), lambda i, j: (0, 0)),
            pl.BlockSpec((1, D_MODEL), lambda i, j: (0, 0)),
            pl.BlockSpec((D_MODEL, tf), lambda i, j: (0, j)),
            pl.BlockSpec((tf, D_MODEL), lambda i, j: (j, 0)),
        ],
        out_specs=pl.BlockSpec((tm, D_MODEL), lambda i, j: (i, 0)),
        out_shape=jax.ShapeDtypeStruct((n, D_MODEL), F32),
        scratch_shapes=[pltpu.VMEM((tm, D_MODEL), BF16), pltpu.VMEM((tm, D_MODEL), F32)],
        compiler_params=_cparams("parallel", "arbitrary"),
    )(x, g, gf, wup, wdn)


def _seg_ones(rows_per_head, cols_per_head):
    r = np.arange(N_HEADS * rows_per_head)[:, None] // rows_per_head
    c = np.arange(N_HEADS * cols_per_head)[None, :] // cols_per_head
    return (r == c).astype(np.float32)


def _rows(*vecs, width=256, nrows=8):
    out = jnp.zeros((nrows, width), F32)
    for i, v in enumerate(vecs):
        v = jnp.asarray(v, F32).reshape(-1)
        out = out.at[i, :v.shape[0]].set(v)
    return out


def _layout_w_in(w):
    d = w.shape[0]
    zeros = lambda n: jnp.zeros((d, n), w.dtype)
    gl0, ml0, gt0 = 2048, 2832, 3608
    parts = [
        w[:, 0:2048],
        w[:, gl0:gl0 + 512], w[:, gl0 + 512:gl0 + 528], zeros(112), w[:, gl0 + 528:gl0 + 784], zeros(128),
        w[:, ml0:ml0 + 512], w[:, ml0 + 512:ml0 + 520], zeros(120), w[:, ml0 + 520:ml0 + 776], zeros(128),
    ]
    wz = jnp.concatenate(parts, axis=1)
    assert wz.shape[1] == Z_COLS
    return wz.astype(BF16), w[:, gt0:gt0 + 4 * D_MODEL].astype(BF16)


class _Group:
    def __init__(self, nb, trow, tv, row0, max_seq):
        self.nb, self.trow, self.tv, self.row0 = nb, trow, tv, row0
        self.full = tv == trow
        self.c = SUB if self.full else trow
        self.tvs = min(tv, self.c)
        self.nsub = trow // self.c
        self.nseq = max(d for d in range(1, max_seq + 1) if nb % d == 0 and row0 % (d * trow) == 0)
        self.rows = self.nseq * trow
        self.nsteps = nb // self.nseq
        self.base_blk = row0 // self.rows
        nblk = trow // self.c
        self.srows = self.c * max(d for d in range(1, min(nblk, 64) + 1) if nblk % d == 0)


def kernel(x_prompt, x_sample, state_hgrn, state_rwkv, state_rwkv_shift, state_gla, state_mlstm_c, state_mlstm_n, state_mlstm_m, state_mlstm_conv, meta_tokens, norm_mix, norm_mlp, norm_final, w_in, hgrn_lb, hgrn_norm, rwkv_mu, rwkv_w0, rwkv_w_up, rwkv_a0, rwkv_a_up, rwkv_g_up, rwkv_k_k, rwkv_k_a, rwkv_r_k, rwkv_ln_w, rwkv_ln_b, gla_gate_up, gla_gate_b, gla_norm, mlstm_conv_w, mlstm_conv_b, mlstm_i_b, mlstm_f_b, mlstm_norm, w_branch, w_out, w_up, w_down):
    depth = w_in.shape[0]
    bp, seq, _ = x_prompt.shape
    bs, dseq, _ = x_sample.shape
    tp = N_META + seq
    assert tp % SUB == 0 and CONV_W - 1 <= dseq <= SAMPLE_PAD
    dt = x_prompt.dtype

    meta = jnp.broadcast_to(meta_tokens.astype(dt)[None], (bp, N_META, D_MODEL))
    xp = jnp.concatenate([meta, x_prompt], axis=1).reshape(bp * tp, D_MODEL)
    xs = jnp.pad(x_sample, ((0, 0), (0, SAMPLE_PAD - dseq), (0, 0))).reshape(bs * SAMPLE_PAD, D_MODEL)
    n_real = bp * tp + bs * SAMPLE_PAD
    n_rows = -(-n_real // ROW_PAD) * ROW_PAD
    x = jnp.concatenate([xp, xs, jnp.zeros((n_rows - n_real, D_MODEL), dt)], axis=0).astype(F32)
    tm = _pick_tile(n_rows, 1024, 128)

    gp = _Group(bp, tp, tp, 0, 1)
    gs = _Group(bs, SAMPLE_PAD, dseq, bp * tp, 16)

    ones_vv = jnp.asarray(_seg_ones(64, 64), BF16)
    ones_gl = jnp.asarray(_seg_ones(GLA_DK, 64), BF16)
    mask_hg = jnp.asarray(_seg_ones(64, 64), F32)
    mask_gl = jnp.asarray(_seg_ones(64, GLA_DK), F32)
    sel8 = jnp.asarray(np.eye(8, 128, dtype=np.float32), BF16)
    hmask = {g.c: jnp.asarray(np.kron(np.eye(N_HEADS), np.ones((g.c, 64))), F32) for g in (gp, gs)}
    ehead = {c_: jnp.concatenate([m_, m_], axis=0).astype(BF16) for c_, m_ in hmask.items()}

    lb_cs = jnp.cumsum(jax.nn.softmax(hgrn_lb.astype(F32), axis=0), axis=0)
    lb_all = lb_cs - lb_cs[:1]

    def init_states(states, nb, zero):
        s_hg, s_rw, shift, s_gl, c_ml, n_ml, m_ml, conv = states
        if zero:
            z = lambda *shp: jnp.zeros((depth, nb) + shp, F32)
            return dict(hg=z(256, 64), rw=z(256, 64), shift=z(1, SEC), gl=z(256, GLA_DK), cm=z(128, 64),
                        nm=z(1, 128), mm=z(1, 128), cv=z(8, 256))
        f = lambda a: a.astype(F32)
        return dict(
            hg=f(s_hg).transpose(0, 1, 2, 4, 3).reshape(depth, nb, 256, 64),
            rw=f(s_rw).reshape(depth, nb, 256, 64),
            shift=f(shift).reshape(depth, nb, 1, SEC),
            gl=f(s_gl).transpose(0, 1, 2, 4, 3).reshape(depth, nb, 256, GLA_DK),
            cm=f(c_ml).reshape(depth, nb, 128, 64),
            nm=f(n_ml).reshape(depth, nb, 1, 128),
            mm=jnp.pad(f(m_ml), ((0, 0), (0, 0), (0, 124))).reshape(depth, nb, 1, 128),
            cv=jnp.pad(f(conv), ((0, 0), (0, 0), (5, 0), (0, 0))),
        )

    sample_states = (state_hgrn, state_rwkv, state_rwkv_shift, state_gla,
                     state_mlstm_c, state_mlstm_n, state_mlstm_m, state_mlstm_conv)
    st_p = init_states(sample_states, bp, True)
    st_s = init_states(sample_states, bs, False)
    new_p = {k: [] for k in st_p}
    new_s = {k: [] for k in st_s}

    for l in range(depth):
        wz, wg = _layout_w_in(w_in[l])
        g_mix = norm_mix[l].reshape(1, D_MODEL).astype(F32)
        z = _in_proj(x, g_mix, wz, tm)

        lb = lb_all[l]
        par_hg = _rows(jnp.log(lb), jnp.log1p(-lb), 1.0 - lb, hgrn_norm[l])
        par_gl = _rows(gla_gate_b[l], jnp.zeros((1,)), jnp.zeros((1,)), gla_norm[l])
        gup_gl = jnp.zeros((128, 128), F32).at[:gla_gate_up.shape[1], :].set(gla_gate_up[l]).astype(BF16)
        gup_dummy = jnp.zeros((128, 128), BF16)
        par_rw = _rows(rwkv_w0[l], rwkv_a0[l], rwkv_k_k[l], rwkv_k_a[l], rwkv_r_k[l], rwkv_ln_w[l], rwkv_ln_b[l])
        nw, na = rwkv_w_up.shape[1], rwkv_a_up.shape[1]
        wup_p = jnp.zeros((256, 256), F32).at[0:nw].set(rwkv_w_up[l]).astype(BF16)
        aup_p = jnp.zeros((256, 256), F32).at[nw:nw + na].set(rwkv_a_up[l]).astype(BF16)
        gup_p = jnp.zeros((256, 256), F32).at[nw + na:].set(rwkv_g_up[l]).astype(BF16)
        mu = rwkv_mu[l].reshape(1, SEC).astype(F32)
        gate_b = jnp.concatenate([mlstm_i_b[l], mlstm_f_b[l]]).astype(F32)
        par_ml = _rows(mlstm_conv_b[l], gate_b, mlstm_norm[l])
        bcol_ml = gate_b.reshape(8, 1)
        cw_ml = _rows(*[mlstm_conv_w[l, j] for j in range(CONV_W)])

        o_hg = o_rw = o_gl = o_ml = None
        for g, st, new in ((gp, st_p, new_p), (gs, st_s, new_s)):
            prev = lambda o: jnp.zeros((n_rows, 256), F32) if o is None else o
            o_hg, s_hg = _gla_call(z, 0, par_hg, gup_dummy, ones_vv, ones_vv, mask_hg, st["hg"][l], prev(o_hg),
                                   mode="hgrn", dk=64, nrows=n_rows, g=g)
            o_gl, s_gl = _gla_call(z, 2, par_gl, gup_gl, ones_gl, ones_vv, mask_gl, st["gl"][l], prev(o_gl),
                                   mode="gla", dk=GLA_DK, nrows=n_rows, g=g)
            o_rw, nshift, s_rw = _rwkv_call(z, mu, par_rw, wup_p, aup_p, gup_p, ones_vv, hmask[g.c], ehead[g.c],
                                            mask_hg, st["shift"][l], st["rw"][l], prev(o_rw), nrows=n_rows, g=g)
            o_ml, s_cm, s_nm, s_mm, s_cv = _mlstm_call(z, par_ml, bcol_ml, cw_ml, sel8, st["cm"][l], st["nm"][l],
                                                        st["mm"][l], st["cv"][l], prev(o_ml), nrows=n_rows, g=g)
            for key, val in (("hg", s_hg), ("rw", s_rw), ("shift", nshift), ("gl", s_gl), ("cm", s_cm),
                             ("nm", s_nm), ("mm", s_mm), ("cv", s_cv)):
                new[key].append(val)

        x = _merge_call((o_hg, o_rw, o_gl, o_ml), x, g_mix, wg, w_branch[l].astype(BF16), w_out[l].astype(BF16), tm)
        x = _mlp_call(x, norm_mlp[l].reshape(1, D_MODEL).astype(F32), norm_final.reshape(1, D_MODEL).astype(F32),
                      w_up[l].astype(BF16), w_down[l].astype(BF16), tm, final=(l == depth - 1))

    y_prompt = x[:bp * tp].reshape(bp, tp, D_MODEL)[:, N_META:].astype(dt)
    y_sample = x[gs.row0:gs.row0 + bs * SAMPLE_PAD].reshape(bs, SAMPLE_PAD, D_MODEL)[:, :dseq].astype(dt)

    def finish(new, nb):
        st = {k: jnp.stack(v) for k, v in new.items()}
        return (
            st["hg"].reshape(depth, nb, N_HEADS, 64, 64).transpose(0, 1, 2, 4, 3),
            st["rw"].reshape(depth, nb, N_HEADS, 64, 64),
            st["shift"].reshape(depth, nb, SEC),
            st["gl"].reshape(depth, nb, N_HEADS, 64, GLA_DK).transpose(0, 1, 2, 4, 3),
            st["cm"].reshape(depth, nb, N_HEADS, MLSTM_DK, 64),
            st["nm"].reshape(depth, nb, N_HEADS, MLSTM_DK),
            st["mm"].reshape(depth, nb, 128)[:, :, :N_HEADS],
            st["cv"][:, :, 5:8, :],
        )

    outs_p = tuple(a.astype(dt) for a in finish(new_p, bp))
    outs_s = tuple(a.astype(dt) for a in finish(new_s, bs))
    return (y_prompt, y_sample) + outs_p + outs_s
```

```python
import functools

import numpy as np
import jax
import jax.numpy as jnp
from jax import lax
from jax.experimental import pallas as pl
from jax.experimental.pallas import tpu as pltpu

F32 = jnp.float32
BF16 = jnp.bfloat16

D_MODEL = 1024
N_META = 16
N_HEADS = 4
HEAD_V = 64
GLA_DK = 32
MLSTM_DK = 32
GLA_GATE_TAU = 16.0
RWKV_LN_EPS = 64e-5
CONV_W = 4
D_FF = 4 * D_MODEL
EPS = 1e-6

SEC = 1024
Z_COLS = 4 * SEC
SUB = 16
SAMPLE_PAD = 8
ROW_PAD = 512
MLSTM_CHUNK = 128
VMEM_LIMIT = 48 * 1024 * 1024


def _cparams(*sem):
    return pltpu.CompilerParams(dimension_semantics=sem, vmem_limit_bytes=VMEM_LIMIT)


def _pick_tile(n, max_tile, mult=8):
    best = None
    t = mult
    while t <= min(n, max_tile):
        if n % t == 0:
            best = t
        t += mult
    assert best is not None, (n, max_tile, mult)
    return best


def _split2(x):
    hi = x.astype(BF16)
    lo = (x - hi.astype(F32)).astype(BF16)
    return hi, lo


def _split3(x):
    hi = x.astype(BF16)
    r1 = x - hi.astype(F32)
    mid = r1.astype(BF16)
    lo = (r1 - mid.astype(F32)).astype(BF16)
    return hi, mid, lo


def _dot(a, b):
    return jnp.dot(a, b, preferred_element_type=F32)


def _dot_nt(a, b):
    return lax.dot_general(a, b, (((1,), (1,)), ((), ())), preferred_element_type=F32)


def _dot_tn(a, b):
    return lax.dot_general(a, b, (((0,), (0,)), ((), ())), preferred_element_type=F32)


def _seg_dot(x, m_bf16, parts):
    ps = _split2(x) if parts == 2 else _split3(x)
    acc = _dot(ps[0], m_bf16)
    for p in ps[1:]:
        acc = acc + _dot(p, m_bf16)
    return acc


def _seg_dot_l(m_bf16, x, parts):
    ps = _split2(x) if parts == 2 else _split3(x)
    acc = _dot(m_bf16, ps[0])
    for p in ps[1:]:
        acc = acc + _dot(m_bf16, p)
    return acc


def _log_sigmoid(x):
    return jnp.minimum(x, 0.0) - jnp.log1p(jnp.exp(-jnp.abs(x)))


def _silu(x):
    return x * jax.nn.sigmoid(x)


def _tri(n, upper=False):
    r = lax.broadcasted_iota(jnp.int32, (n, n), 0)
    c = lax.broadcasted_iota(jnp.int32, (n, n), 1)
    return (r <= c) if upper else (r >= c)


def _cumsum_rows(x, rowi):
    d = 1
    while d < x.shape[0]:
        x = x + jnp.where(rowi >= d, pltpu.roll(x, d, axis=0), 0.0)
        d *= 2
    return x


def _rms_bf16(x, g):
    return (x * lax.rsqrt(jnp.mean(x * x, axis=-1, keepdims=True) + EPS) * g).astype(BF16)


def _in_proj_kernel(x_ref, g_ref, w_ref, z_ref, h_scr):
    @pl.when(pl.program_id(1) == 0)
    def _():
        h_scr[...] = _rms_bf16(x_ref[...], g_ref[...])

    z_ref[...] = _dot(h_scr[...], w_ref[...])


def _in_proj(x, g, w, tm):
    n = x.shape[0]
    tn = 1024
    return pl.pallas_call(
        _in_proj_kernel,
        grid=(n // tm, Z_COLS // tn),
        in_specs=[
            pl.BlockSpec((tm, D_MODEL), lambda i, j: (i, 0)),
            pl.BlockSpec((1, D_MODEL), lambda i, j: (0, 0)),
            pl.BlockSpec((D_MODEL, tn), lambda i, j: (0, j)),
        ],
        out_specs=pl.BlockSpec((tm, tn), lambda i, j: (i, j)),
        out_shape=jax.ShapeDtypeStruct((n, Z_COLS), F32),
        scratch_shapes=[pltpu.VMEM((tm, D_MODEL), BF16)],
        compiler_params=_cparams("parallel", "arbitrary"),
    )(x, g, w)


def _gla_kernel(z_ref, par_ref, gup_ref, okv_ref, ovv_ref, mask_ref, st_in_ref, o_prev_ref,
                o_ref, st_out_ref, *st_scrs, mode, c, nsub, tv, dk, nseq, rps, nst, pc):
    del o_prev_ref
    voff = 512 if mode == "hgrn" else 256
    rowi = lax.broadcasted_iota(jnp.int32, (c, 1), 0)
    valid = rowi < tv

    def sub_block(st_scr, r0):
        rows = pl.ds(r0, c)
        if mode == "hgrn":
            q = z_ref[rows, 0:256]
            hf = z_ref[rows, 256:512]
            g = z_ref[rows, 768:1024]
            a = par_ref[0:1, :]
            cc = par_ref[1:2, :] + _log_sigmoid(hf)
            loga = jnp.maximum(a, cc) + jnp.log1p(jnp.exp(-jnp.abs(a - cc)))
            k = par_ref[2:3, :] * jax.nn.sigmoid(-hf)
        else:
            q = z_ref[rows, 0:128] * (GLA_DK ** -0.5)
            k = z_ref[rows, 128:256]
            ga = z_ref[rows, 512:640]
            g = z_ref[rows, 640:896]
            gl = _dot(ga.astype(BF16), gup_ref[...]) + par_ref[0:1, 0:128]
            loga = _log_sigmoid(gl) * (1.0 / GLA_GATE_TAU)
        v = z_ref[rows, voff:voff + 256]
        if tv < c:
            loga = jnp.where(valid, loga, 0.0)
            k = jnp.where(valid, k, 0.0)
            v = jnp.where(valid, v, 0.0)
        b = _cumsum_rows(loga, rowi)

        o = _dot_nt((q * jnp.exp(b)).astype(BF16), st_scr[...].astype(BF16))

        lo_rows = [8 * (s // 8) for s in range(tv)]
        pieces = []
        for s in range(tv):
            rs = slice(lo_rows[s], c)
            e = jnp.exp(jnp.where(rowi[rs] >= s, b[rs] - b[s:s + 1, :], -jnp.inf))
            pieces.append(q[rs] * e * k[s:s + 1, :])
        return rows, lo_rows, k, v, b, o, jnp.concatenate(pieces, axis=0)

    def sub_block_finish(st_scr, part, sm):
        rows, lo_rows, k, v, b, o, _ = part
        acc = {}
        off = 0
        for s in range(tv):
            n = c - lo_rows[s]
            contrib = sm[off:off + n] * v[s:s + 1, :]
            off += n
            acc[lo_rows[s]] = contrib if lo_rows[s] not in acc else acc[lo_rows[s]] + contrib
        for lo, a_ in acc.items():
            o = o + (a_ if lo == 0 else jnp.concatenate([jnp.zeros((lo, 256), F32), a_], axis=0))

        bl = b[c - 1:c, :]
        kdec = k * jnp.exp(bl - b)
        upd = _dot_tn(v.astype(BF16), kdec.astype(BF16))
        st_scr[...] = st_scr[...] * jnp.exp(bl) + upd * mask_ref[...]
        o_ref[rows, :] = o

    def sub_blocks(scrs, row0s):
        parts = [sub_block(scr, r0) for scr, r0 in zip(scrs, row0s)]
        sm = _dot(jnp.concatenate([p[-1] for p in parts], axis=0).astype(BF16), okv_ref[...])
        n = parts[0][-1].shape[0]
        for i, (scr, part) in enumerate(zip(scrs, parts)):
            sub_block_finish(scr, part, sm[i * n:(i + 1) * n])

    def load_state(st_scr, seq):
        st_scr[...] = jnp.concatenate([st_in_ref[seq]] * N_HEADS, axis=1) * mask_ref[...]

    def store_state(st_scr, seq):
        st = st_scr[...]
        acc = st[:, 0:dk]
        for h in range(1, N_HEADS):
            acc = acc + st[:, h * dk:(h + 1) * dk]
        st_out_ref[seq] = acc

    _run_lanes(st_scrs, nseq, rps, nsub, c, nst, load_state, store_state, sub_blocks)

    goff = 768 if mode == "hgrn" else 640
    nchunk = (nseq * rps) // pc

    def finish_chunk(ci, carry):
        rows = pl.ds(ci * pc if isinstance(ci, int) else pl.multiple_of(ci * pc, 8), pc)
        o = o_ref[rows, :]
        ms = _seg_dot(o * o, ovv_ref[...], 2) * (1.0 / HEAD_V)
        o_ref[rows, :] = o * lax.rsqrt(ms + EPS) * par_ref[3:4, :] * _silu(z_ref[rows, goff:goff + 256])
        return carry

    if nchunk == 1:
        finish_chunk(0, 0)
    else:
        lax.fori_loop(0, nchunk, finish_chunk, 0)


def _run_lanes(scrs, nseq, rps, nsub, c, nst, load_state, store_state, sub_blocks):
    lanes = len(scrs)
    assert nseq == lanes or nst == 1
    step = pl.program_id(1)

    def seq_group(sg, carry):
        seqs = [sg * lanes + u for u in range(lanes)]
        for scr, seq in zip(scrs, seqs):
            if nst == 1:
                load_state(scr, seq)
            else:
                pl.when(step == 0)(functools.partial(load_state, scr, seq))

        def blocks(j, carry2):
            sub_blocks(scrs, [pl.multiple_of(seq * rps + j * c, 8) for seq in seqs])
            return carry2

        if nsub == 1:
            blocks(0, 0)
        else:
            lax.fori_loop(0, nsub, blocks, 0, unroll=2)
        for scr, seq in zip(scrs, seqs):
            if nst == 1:
                store_state(scr, seq)
            else:
                pl.when(step == nst - 1)(functools.partial(store_state, scr, seq))
        return carry

    if nseq == lanes:
        seq_group(0, 0)
    else:
        lax.fori_loop(0, nseq // lanes, seq_group, 0)


def _gla_call(z, sec, par, gup, okv, ovv, mask, st_in, o_prev, *, mode, dk, nrows, g):
    hk = N_HEADS * dk
    kern = functools.partial(_gla_kernel, mode=mode, c=g.c, nsub=g.nsub, tv=g.tvs, dk=dk,
                             nseq=g.nseq, rps=g.rps, nst=g.nst, pc=g.pc)
    const = lambda b, s: (0, 0)
    return pl.pallas_call(
        kern,
        grid=(g.ngrid, g.nst),
        in_specs=[
            pl.BlockSpec((g.rows, SEC), lambda b, s: (g.base_blk + b * g.nst + s, sec)),
            pl.BlockSpec((8, 256), const),
            pl.BlockSpec((128, 128), const),
            pl.BlockSpec((hk, 256), const),
            pl.BlockSpec((256, 256), const),
            pl.BlockSpec((256, hk), const),
            pl.BlockSpec((g.nseq, 256, dk), lambda b, s: (b, 0, 0)),
            pl.BlockSpec(memory_space=pl.ANY),
        ],
        input_output_aliases={7: 0},
        out_specs=[
            pl.BlockSpec((g.rows, 256), lambda b, s: (g.base_blk + b * g.nst + s, 0)),
            pl.BlockSpec((g.nseq, 256, dk), lambda b, s: (b, 0, 0)),
        ],
        out_shape=[
            jax.ShapeDtypeStruct((nrows, 256), F32),
            jax.ShapeDtypeStruct((g.nb, 256, dk), F32),
        ],
        scratch_shapes=[pltpu.VMEM((256, hk), F32)] * g.lanes,
        compiler_params=_cparams("parallel", "arbitrary"),
    )(z, par, gup, okv, ovv, mask, st_in, o_prev)


def _rwkv_kernel(z_ref, mu_ref, par_ref, wup_ref, aup_ref, gup_ref, ovv_ref, hmask_ref, ehead_ref, mask_ref,
                 shift_ref, st_in_ref, o_prev_ref,
                 o_ref, nshift_ref, st_out_ref,
                 prev_scr, fix_scr, r_scr, lw_scr, k_scr, v_scr, kk_scr, ka_scr, g_scr, bonus_scr, y_scr, *st_scrs,
                 c, nsub, tv, nseq, rps, nst, pc):
    del o_prev_ref
    step = pl.program_id(1)

    def shifted_chunk(ci):
        r0 = ci * pc if isinstance(ci, int) else pl.multiple_of(ci * pc, 8)
        zr = z_ref[pl.ds(r0, pc), :]
        rolled = pltpu.roll(zr, 1, axis=0)
        if pc <= rps:
            cps = rps // pc
            seq = ci // cps
            rowi = lax.broadcasted_iota(jnp.int32, (pc, 1), 0)
            before = z_ref[pl.ds(pl.multiple_of(jnp.maximum(r0 - 8, 0), 8), 8), :][7:8, :]
            first = jnp.where(ci % cps == 0, prev_scr[pl.ds(seq, 1), :], before)
            prev = jnp.where(rowi == 0, first, rolled)
            prev_scr[pl.ds(seq, 1), :] = zr[pc - 1:pc, :]
            nshift_ref[seq] = zr[pc - 1:pc, :]
        else:
            fix_scr[...] = rolled
            for i in range(nseq):
                fix_scr[i * rps:i * rps + 1, :] = shift_ref[i]
                nshift_ref[i] = zr[i * rps + tv - 1:i * rps + tv, :]
            prev = fix_scr[...]
        return r0, zr, prev

    if pc <= rps:
        @pl.when(step == 0)
        def _():
            for i in range(nseq):
                prev_scr[i:i + 1, :] = shift_ref[i]

    def prep_chunk(ci, carry):
        r0, zr, prev = shifted_chunk(ci)
        rows = pl.ds(r0, pc)
        _rwkv_prep_rows(zr, prev, rows)
        return carry

    def _rwkv_prep_rows(zr, prev, rows):
        nrow = zr.shape[0]
        zs = zr + (prev - zr) * mu_ref[...]
        r = zs[:, 0:256]
        k = zs[:, 256:512]
        v = zs[:, 512:768]
        lo = zs[:, 768:1024]
        w0, a0, k_k, k_a, r_k = (par_ref[i:i + 1, :] for i in range(5))
        wl = w0 + _dot(jnp.tanh(lo).astype(BF16), wup_ref[...])
        wexp = -(jnp.maximum(-wl, 0.0) + jnp.log1p(jnp.exp(-jnp.abs(wl)))) - 0.5
        lw = -jnp.exp(wexp)
        a = jax.nn.sigmoid(a0 + _dot(lo.astype(BF16), aup_ref[...]))
        g_scr[rows, :] = _dot(jax.nn.sigmoid(lo).astype(BF16), gup_ref[...])
        kkp = k * k_k
        nrm = jnp.sqrt(_seg_dot(kkp * kkp, ovv_ref[...], 3))
        kk = kkp / jnp.maximum(nrm, 1e-12)
        k2 = k * (1.0 + (a - 1.0) * k_a)
        bonus_scr[rows, :] = _seg_dot(r * k2 * r_k, ovv_ref[...], 3) * v
        ka = kk * a
        if tv < rps:
            okrow = lax.rem(lax.broadcasted_iota(jnp.int32, (nrow, 1), 0), rps) < tv
            lw, k2, v, kk, ka = (jnp.where(okrow, t_, 0.0) for t_ in (lw, k2, v, kk, ka))
        r_scr[rows, :] = r
        lw_scr[rows, :] = lw
        k_scr[rows, :] = k2
        v_scr[rows, :] = v
        kk_scr[rows, :] = kk
        ka_scr[rows, :] = ka

    nchunk = (nseq * rps) // pc
    if nchunk == 1:
        prep_chunk(0, 0)
    else:
        lax.fori_loop(0, nchunk, prep_chunk, 0)

    rowc = lax.broadcasted_iota(jnp.int32, (c, 1), 0)
    src2 = lax.broadcasted_iota(jnp.int32, (1, 2 * N_HEADS * c), 1) & (c - 1)
    src = src2[:, 0:N_HEADS * c]
    strict = rowc > src
    incl = rowc >= src

    def sub_block(st_scr, r0):
        rows = pl.ds(r0, c)
        lw_, r_, k_, v_, kk_, ka_ = (s_[rows, :] for s_ in (lw_scr, r_scr, k_scr, v_scr, kk_scr, ka_scr))
        gam = _cumsum_rows(lw_, rowc)
        gl = gam[c - 1:c, :]
        ginv = jnp.exp(-gam)
        gend = jnp.exp(gl - gam)
        hm = hmask_ref[...]
        tile4 = lambda t_: jnp.concatenate([t_] * N_HEADS, axis=0) * hm
        lhs = jnp.concatenate([kk_ * jnp.exp(gam - lw_), r_ * jnp.exp(gam)], axis=0)
        rhs = jnp.concatenate([tile4(ka_ * ginv), tile4(k_ * ginv)], axis=0)
        lh, ll = _split2(lhs)
        rh, rl = _split2(rhs)
        g2 = _dot_nt(jnp.concatenate([lh, ll], axis=0), rh)
        gm = g2[0:2 * c] + g2[2 * c:4 * c] + _dot_nt(lh, rl)
        nc = N_HEADS * c
        m_ab = jnp.where(strict, gm[0:c, 0:nc], 0.0)
        m_bk = jnp.where(strict, gm[0:c, nc:2 * nc], 0.0)
        m_ra = jnp.where(incl, gm[c:2 * c, 0:nc], 0.0)
        m_rk = jnp.where(incl, gm[c:2 * c, nc:2 * nc], 0.0)
        x0 = _dot_nt(lh, st_scr[...].astype(BF16))
        vbig = tile4(v_).astype(BF16)
        u = x0[0:c] + _dot(m_bk.astype(BF16), vbig)
        m2 = jnp.concatenate(_split2(m_ab), axis=1)
        nsolve = min(tv, c - 1)
        cols = _dot(jnp.concatenate([jnp.where(src2 == s, m2, jnp.zeros_like(m2)) for s in range(nsolve)], axis=0),
                    ehead_ref[...])
        for s in range(nsolve):
            u = u - cols[s * c:(s + 1) * c] * u[s:s + 1, :]
        ubig = tile4(u).astype(BF16)
        y_scr[rows, :] = x0[c:2 * c] + _dot(jnp.concatenate([-m_ra, m_rk], axis=1).astype(BF16),
                                            jnp.concatenate([ubig, vbig], axis=0))
        upd = _dot_tn(jnp.concatenate([-u, v_], axis=0).astype(BF16),
                      jnp.concatenate([ka_ * gend, k_ * gend], axis=0).astype(BF16))
        st_scr[...] = st_scr[...] * jnp.exp(gl) + upd * mask_ref[...]

    def load_state(st_scr, seq):
        st_scr[...] = jnp.concatenate([st_in_ref[seq]] * N_HEADS, axis=1) * mask_ref[...]

    def store_state(st_scr, seq):
        st = st_scr[...]
        acc = st[:, 0:64]
        for h in range(1, N_HEADS):
            acc = acc + st[:, h * 64:(h + 1) * 64]
        st_out_ref[seq] = acc

    _run_lanes(st_scrs, nseq, rps, nsub, c, nst, load_state, store_state,
               lambda scrs, row0s: [sub_block(scr, r0) for scr, r0 in zip(scrs, row0s)])

    def finish_chunk(ci, carry):
        rows = pl.ds(ci * pc if isinstance(ci, int) else pl.multiple_of(ci * pc, 8), pc)
        y = y_scr[rows, :]
        mean = _seg_dot(y, ovv_ref[...], 3) * (1.0 / HEAD_V)
        d = y - mean
        var = _seg_dot(d * d, ovv_ref[...], 3) * (1.0 / HEAD_V)
        ln = d * lax.rsqrt(var + RWKV_LN_EPS) * par_ref[5:6, :] + par_ref[6:7, :]
        o_ref[rows, :] = (ln + bonus_scr[rows, :]) * g_scr[rows, :]
        return carry

    if nchunk == 1:
        finish_chunk(0, 0)
    else:
        lax.fori_loop(0, nchunk, finish_chunk, 0)


def _rwkv_call(z, mu, par, wup, aup, gup, ovv, hmask, ehead, mask, shift, st_in, o_prev, *, nrows, g):
    rblk = g.rows
    nsteps = g.nst
    base = g.base_blk
    pc = g.pc
    kern = functools.partial(_rwkv_kernel, c=g.c, nsub=g.nsub, tv=g.tv, nseq=g.nseq, rps=g.rps, nst=g.nst, pc=pc)
    const = lambda b, s: (0, 0)
    return pl.pallas_call(
        kern,
        grid=(g.ngrid, nsteps),
        in_specs=[
            pl.BlockSpec((rblk, SEC), lambda b, s: (base + b * nsteps + s, 1)),
            pl.BlockSpec((1, SEC), const),
            pl.BlockSpec((8, 256), const),
            pl.BlockSpec((256, 256), const),
            pl.BlockSpec((256, 256), const),
            pl.BlockSpec((256, 256), const),
            pl.BlockSpec((256, 256), const),
            pl.BlockSpec((N_HEADS * g.c, 256), const),
            pl.BlockSpec((2 * N_HEADS * g.c, 256), const),
            pl.BlockSpec((256, 256), const),
            pl.BlockSpec((g.nseq, 1, SEC), lambda b, s: (b, 0, 0)),
            pl.BlockSpec((g.nseq, 256, 64), lambda b, s: (b, 0, 0)),
            pl.BlockSpec(memory_space=pl.ANY),
        ],
        input_output_aliases={12: 0},
        out_specs=[
            pl.BlockSpec((rblk, 256), lambda b, s: (base + b * nsteps + s, 0)),
            pl.BlockSpec((g.nseq, 1, SEC), lambda b, s: (b, 0, 0)),
            pl.BlockSpec((g.nseq, 256, 64), lambda b, s: (b, 0, 0)),
        ],
        out_shape=[
            jax.ShapeDtypeStruct((nrows, 256), F32),
            jax.ShapeDtypeStruct((g.nb, 1, SEC), F32),
            jax.ShapeDtypeStruct((g.nb, 256, 64), F32),
        ],
        scratch_shapes=[pltpu.VMEM((max(8, g.nseq), SEC), F32), pltpu.VMEM((pc if pc > g.rps else 8, SEC), F32)]
        + [pltpu.VMEM((rblk, 256), F32)] * 9 + [pltpu.VMEM((256, 256), F32)] * g.lanes,
        compiler_params=_cparams("parallel", "arbitrary"),
    )(z, mu, par, wup, aup, gup, ovv, hmask, ehead, mask, shift, st_in, o_prev)


def _mlstm_kernel(z_ref, par_ref, bcol_ref, cw_ref, sel_ref, cm_in, nm_in, mm_in, cv_in, o_prev_ref,
                  o_ref, cm_out, nm_out, mm_out, cv_out,
                  *scratch, chunks, nseq, rps, nst):
    del o_prev_ref
    lanes = len(scratch) // 4

    def chunk(scr, row0, c, tv):
        xbuf, cm_scr, nm_scr, mm_scr = scr
        rows = pl.ds(row0, c)
        mqk = z_ref[rows, 0:256]
        mv = z_ref[rows, 256:512]
        gi = z_ref[rows, 512:640]
        mo = z_ref[rows, 640:896]
        xbuf[8:8 + c, :] = mqk
        conv = par_ref[0:1, :] + xbuf[5:5 + c, :] * cw_ref[0:1, :]
        for j in range(1, CONV_W):
            conv = conv + xbuf[5 + j:5 + j + c, :] * cw_ref[j:j + 1, :]
        xbuf[5:8, :] = xbuf[8 + tv - 3:8 + tv, :]
        act = _silu(conv)
        q = act[:, 0:128]
        k = act[:, 128:256] * (MLSTM_DK ** -0.5)

        rowi = lax.broadcasted_iota(jnp.int32, (c, 1), 0)
        coli = lax.broadcasted_iota(jnp.int32, (1, c), 1)
        gcol = gi + par_ref[1:2, 0:128]
        lfc = _log_sigmoid(gcol)
        graw = None
        for part in _split3(gi):
            t_ = _dot_nt(sel_ref[...], part)
            graw = t_ if graw is None else graw + t_
        grow = graw + bcol_ref[...]
        lfr = _log_sigmoid(grow)
        if tv < c:
            gcol = jnp.where(rowi < tv, gcol, -jnp.inf)
            lfc = jnp.where(rowi < tv, lfc, 0.0)
            grow = jnp.where(coli < tv, grow, -jnp.inf)
            lfr = jnp.where(coli < tv, lfr, 0.0)
        b_col = _seg_dot_l(_tri(c).astype(BF16), lfc, 3)
        b_row = _seg_dot(lfr, _tri(c, upper=True).astype(BF16), 3)
        causal = _tri(c)
        mm_old = mm_scr[...]
        nm_old = nm_scr[...]
        cm_old = cm_scr[...]
        lane = lax.broadcasted_iota(jnp.int32, (1, 128), 1)
        mm_new = mm_old
        cm_parts, nm_parts, o_parts = [], [], []

        for h in range(N_HEADS):
            bc = b_col[:, 4 + h:5 + h]
            br = b_row[4 + h:5 + h, :]
            ir = grow[h:h + 1, :]
            ic = gcol[:, h:h + 1]
            dmat = jnp.where(causal, bc - br + ir, -jnp.inf)
            mprev = mm_old[0:1, h:h + 1]
            inter = bc + mprev
            m_t = jnp.maximum(inter, jnp.max(dmat, axis=1, keepdims=True))
            wts = jnp.exp(dmat - m_t)
            sc = jnp.exp(inter - m_t)
            qh = q[:, h * MLSTM_DK:(h + 1) * MLSTM_DK]
            kh = k[:, h * MLSTM_DK:(h + 1) * MLSTM_DK]
            vh = mv[:, h * HEAD_V:(h + 1) * HEAD_V]
            qk = _dot_nt(qh.astype(BF16), kh.astype(BF16)) * wts
            cmh = cm_old[h * MLSTM_DK:(h + 1) * MLSTM_DK, :]
            nh = nm_old[0:1, h * MLSTM_DK:(h + 1) * MLSTM_DK]
            num = _dot(qk.astype(BF16), vh.astype(BF16)) + sc * _dot(qh.astype(BF16), cmh.astype(BF16))
            den = jnp.sum(qk, axis=1, keepdims=True) + sc * jnp.sum(qh * nh, axis=1, keepdims=True)
            hc = num / jnp.maximum(jnp.abs(den), jnp.exp(-m_t))
            m_new = m_t[tv - 1:tv, :]
            bl = bc[tv - 1:tv, :]
            wl = jnp.exp(bl - bc + ic - m_new)
            sl = jnp.exp(bl + mprev - m_new)
            kw = kh * wl
            cm_parts.append(sl * cmh + _dot_tn(kw.astype(BF16), vh.astype(BF16)))
            nm_parts.append(sl * nh + jnp.sum(kw, axis=0, keepdims=True))
            mm_new = jnp.where(lane == h, m_new, mm_new)
            ms = jnp.mean(hc * hc, axis=1, keepdims=True)
            o_parts.append(hc * lax.rsqrt(ms + EPS))

        cm_scr[...] = jnp.concatenate(cm_parts, axis=0)
        nm_scr[...] = jnp.concatenate(nm_parts, axis=1)
        mm_scr[...] = mm_new
        o_ref[rows, :] = jnp.concatenate(o_parts, axis=1) * par_ref[2:3, :] * jax.nn.sigmoid(mo)

    def load_state(scr, seq):
        xbuf, cm_scr, nm_scr, mm_scr = scr
        cm_scr[...] = cm_in[seq]
        nm_scr[...] = nm_in[seq]
        mm_scr[...] = mm_in[seq]
        xbuf[0:8, :] = cv_in[seq]

    def store_state(scr, seq):
        xbuf, cm_scr, nm_scr, mm_scr = scr
        cm_out[seq] = cm_scr[...]
        nm_out[seq] = nm_scr[...]
        mm_out[seq] = mm_scr[...]
        cv_out[seq] = xbuf[0:8, :]

    assert nseq == lanes or nst == 1
    step = pl.program_id(1)

    def seq_group(sg, carry):
        scrs = [scratch[4 * u:4 * u + 4] for u in range(lanes)]
        seqs = [sg * lanes + u for u in range(lanes)]
        for scr, seq in zip(scrs, seqs):
            if nst == 1:
                load_state(scr, seq)
            else:
                pl.when(step == 0)(functools.partial(load_state, scr, seq))
        for scr, seq in zip(scrs, seqs):
            for start, c, tv, count in chunks:
                def body(ci, carry2, scr=scr, seq=seq, start=start, c=c, tv=tv):
                    first = seq * rps + start + ci * c
                    chunk(scr, first if isinstance(first, int) else pl.multiple_of(first, 8), c, tv)
                    return carry2

                if count == 1:
                    body(0, 0)
                else:
                    lax.fori_loop(0, count, body, 0)
        for scr, seq in zip(scrs, seqs):
            if nst == 1:
                store_state(scr, seq)
            else:
                pl.when(step == nst - 1)(functools.partial(store_state, scr, seq))
        return carry

    if nseq == lanes:
        seq_group(0, 0)
    else:
        lax.fori_loop(0, nseq // lanes, seq_group, 0)


def _mlstm_call(z, par, bcol, cw, sel, cm, nm, mm, cv, o_prev, *, nrows, g):
    if g.full:
        cmain = max(d for d in (MLSTM_CHUNK, 64, 32, 16, 8) if d <= g.rps)
        lead = g.rps % cmain
        chunks = ([(0, lead, lead, 1)] if lead else []) + [(lead, cmain, cmain, g.rps // cmain)]
        assert lead % 8 == 0
    else:
        chunks = [(0, g.rps, g.tv, 1)]
    cmax = max(c for _, c, _, _ in chunks)
    kern = functools.partial(_mlstm_kernel, chunks=tuple(chunks), nseq=g.nseq, rps=g.rps, nst=g.nst)
    const = lambda b, s: (0, 0)
    st3 = lambda shp: pl.BlockSpec((g.nseq,) + shp, lambda b, s: (b, 0, 0))
    return pl.pallas_call(
        kern,
        grid=(g.ngrid, g.nst),
        in_specs=[
            pl.BlockSpec((g.rows, SEC), lambda b, s: (g.base_blk + b * g.nst + s, 3)),
            pl.BlockSpec((8, 256), const),
            pl.BlockSpec((8, 1), const),
            pl.BlockSpec((8, 256), const),
            pl.BlockSpec((8, 128), const),
            st3((128, 64)), st3((1, 128)), st3((1, 128)), st3((8, 256)),
            pl.BlockSpec(memory_space=pl.ANY),
        ],
        input_output_aliases={9: 0},
        out_specs=[
            pl.BlockSpec((g.rows, 256), lambda b, s: (g.base_blk + b * g.nst + s, 0)),
            st3((128, 64)), st3((1, 128)), st3((1, 128)), st3((8, 256)),
        ],
        out_shape=[
            jax.ShapeDtypeStruct((nrows, 256), F32),
            jax.ShapeDtypeStruct((g.nb, 128, 64), F32),
            jax.ShapeDtypeStruct((g.nb, 1, 128), F32),
            jax.ShapeDtypeStruct((g.nb, 1, 128), F32),
            jax.ShapeDtypeStruct((g.nb, 8, 256), F32),
        ],
        scratch_shapes=[
            pltpu.VMEM((8 + cmax, 256), F32),
            pltpu.VMEM((128, 64), F32),
            pltpu.VMEM((1, 128), F32),
            pltpu.VMEM((1, 128), F32),
        ] * g.lanes,
        compiler_params=_cparams("parallel", "arbitrary"),
    )(z, par, bcol, cw, sel, cm, nm, mm, cv, o_prev)


def _merge_kernel(o0_ref, o1_ref, o2_ref, o3_ref, x_ref, g_ref, wg_ref, wb_ref, wout_ref, out_ref, h_scr, acc_scr):
    n = pl.program_id(1)

    @pl.when(n == 0)
    def _():
        h_scr[...] = _rms_bf16(x_ref[...], g_ref[...])
        acc_scr[...] = jnp.zeros_like(acc_scr)

    gate = jax.nn.sigmoid(_dot(h_scr[...], wg_ref[...]))
    for idx, o_ref in enumerate((o0_ref, o1_ref, o2_ref, o3_ref)):
        @pl.when(n == idx)
        def _(o_ref=o_ref):
            acc_scr[...] += _dot(o_ref[...].astype(BF16), wb_ref[0]) * gate

    @pl.when(n == 3)
    def _():
        out_ref[...] = x_ref[...] + _dot(acc_scr[...].astype(BF16), wout_ref[...])


def _merge_call(outs, x, g, wg, wb, wout, tm):
    n = x.shape[0]
    o_spec = pl.BlockSpec((tm, 256), lambda i, j: (i, 0))
    return pl.pallas_call(
        _merge_kernel,
        grid=(n // tm, 4),
        in_specs=[
            o_spec, o_spec, o_spec, o_spec,
            pl.BlockSpec((tm, D_MODEL), lambda i, j: (i, 0)),
            pl.BlockSpec((1, D_MODEL), lambda i, j: (0, 0)),
            pl.BlockSpec((D_MODEL, D_MODEL), lambda i, j: (0, j)),
            pl.BlockSpec((1, 256, D_MODEL), lambda i, j: (j, 0, 0)),
            pl.BlockSpec((D_MODEL, D_MODEL), lambda i, j: (0, 0)),
        ],
        out_specs=pl.BlockSpec((tm, D_MODEL), lambda i, j: (i, 0)),
        out_shape=jax.ShapeDtypeStruct((n, D_MODEL), F32),
        scratch_shapes=[pltpu.VMEM((tm, D_MODEL), BF16), pltpu.VMEM((tm, D_MODEL), F32)],
        compiler_params=_cparams("parallel", "arbitrary"),
    )(*outs, x, g, wg, wb, wout)


def _mlp_kernel(x_ref, g_ref, gf_ref, wup_ref, wdn_ref, out_ref, h_scr, acc_scr, *, final):
    j = pl.program_id(1)

    @pl.when(j == 0)
    def _():
        h_scr[...] = _rms_bf16(x_ref[...], g_ref[...])
        acc_scr[...] = jnp.zeros_like(acc_scr)

    u = jnp.maximum(_dot(h_scr[...], wup_ref[...]), 0.0)
    acc_scr[...] += _dot((u * u).astype(BF16), wdn_ref[...])

    @pl.when(j == pl.num_programs(1) - 1)
    def _():
        y = x_ref[...] + acc_scr[...]
        if final:
            y = y * lax.rsqrt(jnp.mean(y * y, axis=-1, keepdims=True) + EPS) * gf_ref[...]
        out_ref[...] = y


def _mlp_call(x, g, gf, wup, wdn, tm, final):
    n = x.shape[0]
    tf = 1024
    return pl.pallas_call(
        functools.partial(_mlp_kernel, final=final),
        grid=(n // tm, D_FF // tf),
        in_specs=[
            pl.BlockSpec((tm, D_MODEL), lambda i, j: (i, 0)),
            pl.BlockSpec((1, D_MODEL), lambda i, j: (0, 0)),
            pl.BlockSpec((1, D_MODEL), lambda i, j: (0, 0)),
            pl.BlockSpec((D_MODEL, tf), lambda i, j: (0, j)),
            pl.BlockSpec((tf, D_MODEL), lambda i, j: (j, 0)),
        ],
        out_specs=pl.BlockSpec((tm, D_MODEL), lambda i, j: (i, 0)),
        out_shape=jax.ShapeDtypeStruct((n, D_MODEL), F32),
        scratch_shapes=[pltpu.VMEM((tm, D_MODEL), BF16), pltpu.VMEM((tm, D_MODEL), F32)],
        compiler_params=_cparams("parallel", "arbitrary"),
    )(x, g, gf, wup, wdn)


def _seg_ones(rows_per_head, cols_per_head):
    r = np.arange(N_HEADS * rows_per_head)[:, None] // rows_per_head
    c = np.arange(N_HEADS * cols_per_head)[None, :] // cols_per_head
    return (r == c).astype(np.float32)


def _rows(*vecs, width=256, nrows=8):
    out = jnp.zeros((nrows, width), F32)
    for i, v in enumerate(vecs):
        v = jnp.asarray(v, F32).reshape(-1)
        out = out.at[i, :v.shape[0]].set(v)
    return out


def _layout_w_in(w):
    d = w.shape[0]
    zeros = lambda n: jnp.zeros((d, n), w.dtype)
    gl0, ml0, gt0 = 2048, 2832, 3608
    parts = [
        w[:, 0:2048],
        w[:, gl0:gl0 + 512], w[:, gl0 + 512:gl0 + 528], zeros(112), w[:, gl0 + 528:gl0 + 784], zeros(128),
        w[:, ml0:ml0 + 512], w[:, ml0 + 512:ml0 + 520], zeros(120), w[:, ml0 + 520:ml0 + 776], zeros(128),
    ]
    wz = jnp.concatenate(parts, axis=1)
    assert wz.shape[1] == Z_COLS
    return wz.astype(BF16), w[:, gt0:gt0 + 4 * D_MODEL].astype(BF16)


class _Group:
    def __init__(self, nb, trow, tv, row0, max_seq):
        self.nb, self.trow, self.tv, self.row0 = nb, trow, tv, row0
        self.full = tv == trow
        self.c = SUB if self.full else trow
        self.tvs = min(tv, self.c)
        if self.full:
            nblk = trow // self.c
            self.rps = self.c * max(d for d in range(1, min(nblk, 64) + 1) if nblk % d == 0)
        else:
            self.rps = trow
        self.nst = trow // self.rps
        self.nsub = self.rps // self.c
        self.nseq = max(d for d in range(1, max_seq + 1) if nb % d == 0 and row0 % (d * self.rps) == 0)
        self.lanes = 2 if self.nseq % 2 == 0 else 1
        assert self.nseq == self.lanes or self.nst == 1
        self.rows = self.nseq * self.rps
        self.ngrid = nb // self.nseq
        self.base_blk = row0 // self.rows
        self.pc = _pick_tile(self.rps, 384) if self.rps >= 64 else self.rows

    def to_rows(self, a):
        d = a.shape[-1]
        a = a.reshape(self.ngrid, self.nseq, self.nst, self.rps, d).transpose(0, 2, 1, 3, 4)
        return a.reshape(self.nb * self.trow, d)

    def from_rows(self, a):
        d = a.shape[-1]
        a = a.reshape(self.ngrid, self.nst, self.nseq, self.rps, d).transpose(0, 2, 1, 3, 4)
        return a.reshape(self.nb, self.trow, d)


def kernel(x_prompt, x_sample, state_hgrn, state_rwkv, state_rwkv_shift, state_gla, state_mlstm_c, state_mlstm_n, state_mlstm_m, state_mlstm_conv, meta_tokens, norm_mix, norm_mlp, norm_final, w_in, hgrn_lb, hgrn_norm, rwkv_mu, rwkv_w0, rwkv_w_up, rwkv_a0, rwkv_a_up, rwkv_g_up, rwkv_k_k, rwkv_k_a, rwkv_r_k, rwkv_ln_w, rwkv_ln_b, gla_gate_up, gla_gate_b, gla_norm, mlstm_conv_w, mlstm_conv_b, mlstm_i_b, mlstm_f_b, mlstm_norm, w_branch, w_out, w_up, w_down):
    depth = w_in.shape[0]
    bp, seq, _ = x_prompt.shape
    bs, dseq, _ = x_sample.shape
    tp = N_META + seq
    assert tp % SUB == 0 and CONV_W - 1 <= dseq <= SAMPLE_PAD
    dt = x_prompt.dtype

    gp = _Group(bp, tp, tp, 0, 2)
    gs = _Group(bs, SAMPLE_PAD, dseq, bp * tp, 16)
    meta = jnp.broadcast_to(meta_tokens.astype(dt)[None], (bp, N_META, D_MODEL))
    xp = gp.to_rows(jnp.concatenate([meta, x_prompt], axis=1))
    xs = gs.to_rows(jnp.pad(x_sample, ((0, 0), (0, SAMPLE_PAD - dseq), (0, 0))))
    n_real = bp * tp + bs * SAMPLE_PAD
    n_rows = -(-n_real // ROW_PAD) * ROW_PAD
    x = jnp.concatenate([xp, xs, jnp.zeros((n_rows - n_real, D_MODEL), dt)], axis=0).astype(F32)
    tm = _pick_tile(n_rows, 1024, 128)

    ones_vv = jnp.asarray(_seg_ones(64, 64), BF16)
    ones_gl = jnp.asarray(_seg_ones(GLA_DK, 64), BF16)
    mask_hg = jnp.asarray(_seg_ones(64, 64), F32)
    mask_gl = jnp.asarray(_seg_ones(64, GLA_DK), F32)
    sel8 = jnp.asarray(np.eye(8, 128, dtype=np.float32), BF16)
    hmask = {g.c: jnp.asarray(np.kron(np.eye(N_HEADS), np.ones((g.c, 64))), F32) for g in (gp, gs)}
    ehead = {c_: jnp.concatenate([m_, m_], axis=0).astype(BF16) for c_, m_ in hmask.items()}

    lb_cs = jnp.cumsum(jax.nn.softmax(hgrn_lb.astype(F32), axis=0), axis=0)
    lb_all = lb_cs - lb_cs[:1]

    def init_states(states, nb, zero):
        s_hg, s_rw, shift, s_gl, c_ml, n_ml, m_ml, conv = states
        if zero:
            z = lambda *shp: jnp.zeros((depth, nb) + shp, F32)
            return dict(hg=z(256, 64), rw=z(256, 64), shift=z(1, SEC), gl=z(256, GLA_DK), cm=z(128, 64),
                        nm=z(1, 128), mm=z(1, 128), cv=z(8, 256))
        f = lambda a: a.astype(F32)
        return dict(
            hg=f(s_hg).transpose(0, 1, 2, 4, 3).reshape(depth, nb, 256, 64),
            rw=f(s_rw).reshape(depth, nb, 256, 64),
            shift=f(shift).reshape(depth, nb, 1, SEC),
            gl=f(s_gl).transpose(0, 1, 2, 4, 3).reshape(depth, nb, 256, GLA_DK),
            cm=f(c_ml).reshape(depth, nb, 128, 64),
            nm=f(n_ml).reshape(depth, nb, 1, 128),
            mm=jnp.pad(f(m_ml), ((0, 0), (0, 0), (0, 124))).reshape(depth, nb, 1, 128),
            cv=jnp.pad(f(conv), ((0, 0), (0, 0), (5, 0), (0, 0))),
        )

    sample_states = (state_hgrn, state_rwkv, state_rwkv_shift, state_gla,
                     state_mlstm_c, state_mlstm_n, state_mlstm_m, state_mlstm_conv)
    st_p = init_states(sample_states, bp, True)
    st_s = init_states(sample_states, bs, False)
    new_p = {k: [] for k in st_p}
    new_s = {k: [] for k in st_s}

    for l in range(depth):
        wz, wg = _layout_w_in(w_in[l])
        g_mix = norm_mix[l].reshape(1, D_MODEL).astype(F32)
        z = _in_proj(x, g_mix, wz, tm)

        lb = lb_all[l]
        par_hg = _rows(jnp.log(lb), jnp.log1p(-lb), 1.0 - lb, hgrn_norm[l])
        par_gl = _rows(gla_gate_b[l], jnp.zeros((1,)), jnp.zeros((1,)), gla_norm[l])
        gup_gl = jnp.zeros((128, 128), F32).at[:gla_gate_up.shape[1], :].set(gla_gate_up[l]).astype(BF16)
        gup_dummy = jnp.zeros((128, 128), BF16)
        par_rw = _rows(rwkv_w0[l], rwkv_a0[l], rwkv_k_k[l], rwkv_k_a[l], rwkv_r_k[l], rwkv_ln_w[l], rwkv_ln_b[l])
        nw, na = rwkv_w_up.shape[1], rwkv_a_up.shape[1]
        wup_p = jnp.zeros((256, 256), F32).at[0:nw].set(rwkv_w_up[l]).astype(BF16)
        aup_p = jnp.zeros((256, 256), F32).at[nw:nw + na].set(rwkv_a_up[l]).astype(BF16)
        gup_p = jnp.zeros((256, 256), F32).at[nw + na:].set(rwkv_g_up[l]).astype(BF16)
        mu = rwkv_mu[l].reshape(1, SEC).astype(F32)
        gate_b = jnp.concatenate([mlstm_i_b[l], mlstm_f_b[l]]).astype(F32)
        par_ml = _rows(mlstm_conv_b[l], gate_b, mlstm_norm[l])
        bcol_ml = gate_b.reshape(8, 1)
        cw_ml = _rows(*[mlstm_conv_w[l, j] for j in range(CONV_W)])

        o_hg = o_rw = o_gl = o_ml = None
        for g, st, new in ((gp, st_p, new_p), (gs, st_s, new_s)):
            prev = lambda o: jnp.zeros((n_rows, 256), F32) if o is None else o
            o_hg, s_hg = _gla_call(z, 0, par_hg, gup_dummy, ones_vv, ones_vv, mask_hg, st["hg"][l], prev(o_hg),
                                   mode="hgrn", dk=64, nrows=n_rows, g=g)
            o_gl, s_gl = _gla_call(z, 2, par_gl, gup_gl, ones_gl, ones_vv, mask_gl, st["gl"][l], prev(o_gl),
                                   mode="gla", dk=GLA_DK, nrows=n_rows, g=g)
            o_rw, nshift, s_rw = _rwkv_call(z, mu, par_rw, wup_p, aup_p, gup_p, ones_vv, hmask[g.c], ehead[g.c],
                                            mask_hg, st["shift"][l], st["rw"][l], prev(o_rw), nrows=n_rows, g=g)
            o_ml, s_cm, s_nm, s_mm, s_cv = _mlstm_call(z, par_ml, bcol_ml, cw_ml, sel8, st["cm"][l], st["nm"][l],
                                                        st["mm"][l], st["cv"][l], prev(o_ml), nrows=n_rows, g=g)
            for key, val in (("hg", s_hg), ("rw", s_rw), ("shift", nshift), ("gl", s_gl), ("cm", s_cm),
                             ("nm", s_nm), ("mm", s_mm), ("cv", s_cv)):
                new[key].append(val)

        x = _merge_call((o_hg, o_rw, o_gl, o_ml), x, g_mix, wg, w_branch[l].astype(BF16), w_out[l].astype(BF16), tm)
        x = _mlp_call(x, norm_mlp[l].reshape(1, D_MODEL).astype(F32), norm_final.reshape(1, D_MODEL).astype(F32),
                      w_up[l].astype(BF16), w_down[l].astype(BF16), tm, final=(l == depth - 1))

    y_prompt = gp.from_rows(x[:bp * tp])[:, N_META:].astype(dt)
    y_sample = gs.from_rows(x[gs.row0:gs.row0 + bs * SAMPLE_PAD])[:, :dseq].astype(dt)

    def finish(new, nb):
        st = {k: jnp.stack(v) for k, v in new.items()}
        return (
            st["hg"].reshape(depth, nb, N_HEADS, 64, 64).transpose(0, 1, 2, 4, 3),
            st["rw"].reshape(depth, nb, N_HEADS, 64, 64),
            st["shift"].reshape(depth, nb, SEC),
            st["gl"].reshape(depth, nb, N_HEADS, 64, GLA_DK).transpose(0, 1, 2, 4, 3),
            st["cm"].reshape(depth, nb, N_HEADS, MLSTM_DK, 64),
            st["nm"].reshape(depth, nb, N_HEADS, MLSTM_DK),
            st["mm"].reshape(depth, nb, 128)[:, :, :N_HEADS],
            st["cv"][:, :, 5:8, :],
        )

    outs_p = tuple(a.astype(dt) for a in finish(new_p, bp))
    outs_s = tuple(a.astype(dt) for a in finish(new_s, bs))
    return (y_prompt, y_sample) + outs_p + outs_s
```

```python
import functools

import numpy as np
import jax
import jax.numpy as jnp
from jax import lax
from jax.experimental import pallas as pl
from jax.experimental.pallas import tpu as pltpu

F32 = jnp.float32
BF16 = jnp.bfloat16

D_MODEL = 1024
N_META = 16
N_HEADS = 4
HEAD_V = 64
GLA_DK = 32
MLSTM_DK = 32
GLA_GATE_TAU = 16.0
RWKV_LN_EPS = 64e-5
CONV_W = 4
D_FF = 4 * D_MODEL
EPS = 1e-6

SEC = 1024
Z_COLS = 4 * SEC
SUB = 16
SAMPLE_PAD = 8
ROW_PAD = 512
MLSTM_CHUNK = 128
VMEM_LIMIT = 48 * 1024 * 1024


def _cparams(*sem):
    return pltpu.CompilerParams(dimension_semantics=sem, vmem_limit_bytes=VMEM_LIMIT)


def _pick_tile(n, max_tile, mult=8):
    best = None
    t = mult
    while t <= min(n, max_tile):
        if n % t == 0:
            best = t
        t += mult
    assert best is not None, (n, max_tile, mult)
    return best


def _split2(x):
    hi = x.astype(BF16)
    lo = (x - hi.astype(F32)).astype(BF16)
    return hi, lo


def _split3(x):
    hi = x.astype(BF16)
    r1 = x - hi.astype(F32)
    mid = r1.astype(BF16)
    lo = (r1 - mid.astype(F32)).astype(BF16)
    return hi, mid, lo


def _dot(a, b):
    return jnp.dot(a, b, preferred_element_type=F32)


def _dot_nt(a, b):
    return lax.dot_general(a, b, (((1,), (1,)), ((), ())), preferred_element_type=F32)


def _dot_tn(a, b):
    return lax.dot_general(a, b, (((0,), (0,)), ((), ())), preferred_element_type=F32)


def _seg_dot(x, m_bf16, parts):
    ps = _split2(x) if parts == 2 else _split3(x)
    acc = _dot(ps[0], m_bf16)
    for p in ps[1:]:
        acc = acc + _dot(p, m_bf16)
    return acc


def _seg_dot_l(m_bf16, x, parts):
    ps = _split2(x) if parts == 2 else _split3(x)
    acc = _dot(m_bf16, ps[0])
    for p in ps[1:]:
        acc = acc + _dot(m_bf16, p)
    return acc


def _log_sigmoid(x):
    return jnp.minimum(x, 0.0) - jnp.log1p(jnp.exp(-jnp.abs(x)))


def _silu(x):
    return x * jax.nn.sigmoid(x)


def _tri(n, upper=False):
    r = lax.broadcasted_iota(jnp.int32, (n, n), 0)
    c = lax.broadcasted_iota(jnp.int32, (n, n), 1)
    return (r <= c) if upper else (r >= c)


def _cumsum_rows(x, rowi):
    d = 1
    while d < x.shape[0]:
        x = x + jnp.where(rowi >= d, pltpu.roll(x, d, axis=0), 0.0)
        d *= 2
    return x


def _rms_bf16(x, g):
    return (x * lax.rsqrt(jnp.mean(x * x, axis=-1, keepdims=True) + EPS) * g).astype(BF16)


def _in_proj_kernel(x_ref, g_ref, w_ref, z_ref, h_scr):
    @pl.when(pl.program_id(1) == 0)
    def _():
        h_scr[...] = _rms_bf16(x_ref[...], g_ref[...])

    z_ref[...] = _dot(h_scr[...], w_ref[...])


def _in_proj(x, g, w, tm):
    n = x.shape[0]
    tn = 1024
    return pl.pallas_call(
        _in_proj_kernel,
        grid=(n // tm, Z_COLS // tn),
        in_specs=[
            pl.BlockSpec((tm, D_MODEL), lambda i, j: (i, 0)),
            pl.BlockSpec((1, D_MODEL), lambda i, j: (0, 0)),
            pl.BlockSpec((D_MODEL, tn), lambda i, j: (0, j)),
        ],
        out_specs=pl.BlockSpec((tm, tn), lambda i, j: (i, j)),
        out_shape=jax.ShapeDtypeStruct((n, Z_COLS), F32),
        scratch_shapes=[pltpu.VMEM((tm, D_MODEL), BF16)],
        compiler_params=_cparams("parallel", "arbitrary"),
    )(x, g, w)


def _gla_kernel(z_ref, par_ref, gup_ref, okv_ref, ovv_ref, mask_ref, st_in_ref, o_prev_ref,
                o_ref, st_out_ref, *st_scrs, mode, c, nsub, tv, dk, nseq, rps, nst, pc):
    del o_prev_ref
    voff = 512 if mode == "hgrn" else 256
    rowi = lax.broadcasted_iota(jnp.int32, (c, 1), 0)
    valid = rowi < tv

    def sub_block(st_scr, r0):
        rows = pl.ds(r0, c)
        if mode == "hgrn":
            q = z_ref[rows, 0:256]
            hf = z_ref[rows, 256:512]
            g = z_ref[rows, 768:1024]
            a = par_ref[0:1, :]
            cc = par_ref[1:2, :] + _log_sigmoid(hf)
            loga = jnp.maximum(a, cc) + jnp.log1p(jnp.exp(-jnp.abs(a - cc)))
            k = par_ref[2:3, :] * jax.nn.sigmoid(-hf)
        else:
            q = z_ref[rows, 0:128] * (GLA_DK ** -0.5)
            k = z_ref[rows, 128:256]
            ga = z_ref[rows, 512:640]
            g = z_ref[rows, 640:896]
            gl = _dot(ga.astype(BF16), gup_ref[...]) + par_ref[0:1, 0:128]
            loga = _log_sigmoid(gl) * (1.0 / GLA_GATE_TAU)
        v = z_ref[rows, voff:voff + 256]
        if tv < c:
            loga = jnp.where(valid, loga, 0.0)
            k = jnp.where(valid, k, 0.0)
            v = jnp.where(valid, v, 0.0)
        b = _cumsum_rows(loga, rowi)

        o = _dot_nt((q * jnp.exp(b)).astype(BF16), st_scr[...].astype(BF16))

        lo_rows = [8 * (s // 8) for s in range(tv)]
        pieces = []
        for s in range(tv):
            rs = slice(lo_rows[s], c)
            e = jnp.exp(jnp.where(rowi[rs] >= s, b[rs] - b[s:s + 1, :], -jnp.inf))
            pieces.append(q[rs] * e * k[s:s + 1, :])
        return rows, lo_rows, k, v, b, o, jnp.concatenate(pieces, axis=0)

    def sub_block_finish(st_scr, part, sm):
        rows, lo_rows, k, v, b, o, _ = part
        acc = {}
        off = 0
        for s in range(tv):
            n = c - lo_rows[s]
            contrib = sm[off:off + n] * v[s:s + 1, :]
            off += n
            acc[lo_rows[s]] = contrib if lo_rows[s] not in acc else acc[lo_rows[s]] + contrib
        for lo, a_ in acc.items():
            o = o + (a_ if lo == 0 else jnp.concatenate([jnp.zeros((lo, 256), F32), a_], axis=0))

        bl = b[c - 1:c, :]
        kdec = k * jnp.exp(bl - b)
        upd = _dot_tn(v.astype(BF16), kdec.astype(BF16))
        st_scr[...] = st_scr[...] * jnp.exp(bl) + upd * mask_ref[...]
        o_ref[rows, :] = o

    def sub_blocks(scrs, row0s):
        parts = [sub_block(scr, r0) for scr, r0 in zip(scrs, row0s)]
        sm = _dot(jnp.concatenate([p[-1] for p in parts], axis=0).astype(BF16), okv_ref[...])
        n = parts[0][-1].shape[0]
        for i, (scr, part) in enumerate(zip(scrs, parts)):
            sub_block_finish(scr, part, sm[i * n:(i + 1) * n])

    def load_state(st_scr, seq):
        st_scr[...] = jnp.concatenate([st_in_ref[seq]] * N_HEADS, axis=1) * mask_ref[...]

    def store_state(st_scr, seq):
        st = st_scr[...]
        acc = st[:, 0:dk]
        for h in range(1, N_HEADS):
            acc = acc + st[:, h * dk:(h + 1) * dk]
        st_out_ref[seq] = acc

    _run_lanes(st_scrs, nseq, rps, nsub, c, nst, load_state, store_state, sub_blocks)

    goff = 768 if mode == "hgrn" else 640
    nchunk = (nseq * rps) // pc

    def finish_chunk(ci, carry):
        rows = pl.ds(ci * pc if isinstance(ci, int) else pl.multiple_of(ci * pc, 8), pc)
        o = o_ref[rows, :]
        ms = _seg_dot(o * o, ovv_ref[...], 2) * (1.0 / HEAD_V)
        o_ref[rows, :] = o * lax.rsqrt(ms + EPS) * par_ref[3:4, :] * _silu(z_ref[rows, goff:goff + 256])
        return carry

    if nchunk == 1:
        finish_chunk(0, 0)
    else:
        lax.fori_loop(0, nchunk, finish_chunk, 0)


def _run_lanes(scrs, nseq, rps, nsub, c, nst, load_state, store_state, sub_blocks):
    lanes = len(scrs)
    assert nseq == lanes or nst == 1
    step = pl.program_id(1)

    def seq_group(sg, carry):
        seqs = [sg * lanes + u for u in range(lanes)]
        for scr, seq in zip(scrs, seqs):
            if nst == 1:
                load_state(scr, seq)
            else:
                pl.when(step == 0)(functools.partial(load_state, scr, seq))

        def blocks(j, carry2):
            sub_blocks(scrs, [pl.multiple_of(seq * rps + j * c, 8) for seq in seqs])
            return carry2

        if nsub == 1:
            blocks(0, 0)
        else:
            lax.fori_loop(0, nsub, blocks, 0, unroll=2)
        for scr, seq in zip(scrs, seqs):
            if nst == 1:
                store_state(scr, seq)
            else:
                pl.when(step == nst - 1)(functools.partial(store_state, scr, seq))
        return carry

    if nseq == lanes:
        seq_group(0, 0)
    else:
        lax.fori_loop(0, nseq // lanes, seq_group, 0)


def _gla_call(z, sec, par, gup, okv, ovv, mask, st_in, o_prev, *, mode, dk, nrows, g):
    hk = N_HEADS * dk
    kern = functools.partial(_gla_kernel, mode=mode, c=g.c, nsub=g.nsub, tv=g.tvs, dk=dk,
                             nseq=g.nseq, rps=g.rps, nst=g.nst, pc=g.pc)
    const = lambda b, s: (0, 0)
    return pl.pallas_call(
        kern,
        grid=(g.ngrid, g.nst),
        in_specs=[
            pl.BlockSpec((g.rows, SEC), lambda b, s: (g.base_blk + b * g.nst + s, sec)),
            pl.BlockSpec((8, 256), const),
            pl.BlockSpec((128, 128), const),
            pl.BlockSpec((hk, 256), const),
            pl.BlockSpec((256, 256), const),
            pl.BlockSpec((256, hk), const),
            pl.BlockSpec((g.nseq, 256, dk), lambda b, s: (b, 0, 0)),
            pl.BlockSpec(memory_space=pl.ANY),
        ],
        input_output_aliases={7: 0},
        out_specs=[
            pl.BlockSpec((g.rows, 256), lambda b, s: (g.base_blk + b * g.nst + s, 0)),
            pl.BlockSpec((g.nseq, 256, dk), lambda b, s: (b, 0, 0)),
        ],
        out_shape=[
            jax.ShapeDtypeStruct((nrows, 256), F32),
            jax.ShapeDtypeStruct((g.nb, 256, dk), F32),
        ],
        scratch_shapes=[pltpu.VMEM((256, hk), F32)] * g.lanes,
        compiler_params=_cparams("parallel", "arbitrary"),
    )(z, par, gup, okv, ovv, mask, st_in, o_prev)


def _rwkv_kernel(z_ref, mu_ref, par_ref, wup_ref, aup_ref, gup_ref, ovv_ref, hmask_ref, ehead_ref, mask_ref,
                 shift_ref, st_in_ref, o_prev_ref,
                 o_ref, nshift_ref, st_out_ref,
                 prev_scr, fix_scr, r_scr, lw_scr, k_scr, v_scr, kk_scr, ka_scr, g_scr, bonus_scr, y_scr, *st_scrs,
                 c, nsub, tv, nseq, rps, nst, pc):
    del o_prev_ref
    step = pl.program_id(1)

    def shifted_chunk(ci):
        r0 = ci * pc if isinstance(ci, int) else pl.multiple_of(ci * pc, 8)
        zr = z_ref[pl.ds(r0, pc), :]
        rolled = pltpu.roll(zr, 1, axis=0)
        if pc <= rps:
            cps = rps // pc
            seq = ci // cps
            rowi = lax.broadcasted_iota(jnp.int32, (pc, 1), 0)
            before = z_ref[pl.ds(pl.multiple_of(jnp.maximum(r0 - 8, 0), 8), 8), :][7:8, :]
            first = jnp.where(ci % cps == 0, prev_scr[pl.ds(seq, 1), :], before)
            prev = jnp.where(rowi == 0, first, rolled)
            prev_scr[pl.ds(seq, 1), :] = zr[pc - 1:pc, :]
            nshift_ref[seq] = zr[pc - 1:pc, :]
        else:
            fix_scr[...] = rolled
            for i in range(nseq):
                fix_scr[i * rps:i * rps + 1, :] = shift_ref[i]
                nshift_ref[i] = zr[i * rps + tv - 1:i * rps + tv, :]
            prev = fix_scr[...]
        return r0, zr, prev

    if pc <= rps:
        @pl.when(step == 0)
        def _():
            for i in range(nseq):
                prev_scr[i:i + 1, :] = shift_ref[i]

    def prep_chunk(ci, carry):
        r0, zr, prev = shifted_chunk(ci)
        rows = pl.ds(r0, pc)
        _rwkv_prep_rows(zr, prev, rows)
        return carry

    def _rwkv_prep_rows(zr, prev, rows):
        nrow = zr.shape[0]
        zs = zr + (prev - zr) * mu_ref[...]
        r = zs[:, 0:256]
        k = zs[:, 256:512]
        v = zs[:, 512:768]
        lo = zs[:, 768:1024]
        w0, a0, k_k, k_a, r_k = (par_ref[i:i + 1, :] for i in range(5))
        wl = w0 + _dot(jnp.tanh(lo).astype(BF16), wup_ref[...])
        wexp = -(jnp.maximum(-wl, 0.0) + jnp.log1p(jnp.exp(-jnp.abs(wl)))) - 0.5
        lw = -jnp.exp(wexp)
        a = jax.nn.sigmoid(a0 + _dot(lo.astype(BF16), aup_ref[...]))
        g_scr[rows, :] = _dot(jax.nn.sigmoid(lo).astype(BF16), gup_ref[...])
        kkp = k * k_k
        nrm = jnp.sqrt(_seg_dot(kkp * kkp, ovv_ref[...], 3))
        kk = kkp / jnp.maximum(nrm, 1e-12)
        k2 = k * (1.0 + (a - 1.0) * k_a)
        bonus_scr[rows, :] = _seg_dot(r * k2 * r_k, ovv_ref[...], 3) * v
        ka = kk * a
        if tv < rps:
            okrow = lax.rem(lax.broadcasted_iota(jnp.int32, (nrow, 1), 0), rps) < tv
            lw, k2, v, kk, ka = (jnp.where(okrow, t_, 0.0) for t_ in (lw, k2, v, kk, ka))
        r_scr[rows, :] = r
        lw_scr[rows, :] = lw
        k_scr[rows, :] = k2
        v_scr[rows, :] = v
        kk_scr[rows, :] = kk
        ka_scr[rows, :] = ka

    nchunk = (nseq * rps) // pc
    if nchunk == 1:
        prep_chunk(0, 0)
    else:
        lax.fori_loop(0, nchunk, prep_chunk, 0)

    rowc = lax.broadcasted_iota(jnp.int32, (c, 1), 0)
    src2 = lax.broadcasted_iota(jnp.int32, (1, 2 * N_HEADS * c), 1) & (c - 1)
    src = src2[:, 0:N_HEADS * c]
    strict = rowc > src
    incl = rowc >= src

    def sub_blocks(st_scrs_, row0s):
        hm = hmask_ref[...]
        tile4 = lambda t_: jnp.concatenate([t_] * N_HEADS, axis=0) * hm
        nc = N_HEADS * c
        nsolve = min(tv, c - 1)
        lanes_ = range(len(row0s))
        cx = []
        for r0 in row0s:
            rows = pl.ds(r0, c)
            lw_, r_, k_, v_, kk_, ka_ = (s_[rows, :] for s_ in (lw_scr, r_scr, k_scr, v_scr, kk_scr, ka_scr))
            gam = _cumsum_rows(lw_, rowc)
            gl = gam[c - 1:c, :]
            ginv = jnp.exp(-gam)
            gend = jnp.exp(gl - gam)
            lhs = jnp.concatenate([kk_ * jnp.exp(gam - lw_), r_ * jnp.exp(gam)], axis=0)
            rhs = jnp.concatenate([tile4(ka_ * ginv), tile4(k_ * ginv)], axis=0)
            lh, ll = _split2(lhs)
            rh, rl = _split2(rhs)
            cx.append(dict(rows=rows, v=v_, gl=gl, lh=lh, ll=ll, rh=rh, rl=rl, vbig=tile4(v_).astype(BF16),
                           end=jnp.concatenate([ka_ * gend, k_ * gend], axis=0).astype(BF16)))
        g2 = [_dot_nt(jnp.concatenate([x["lh"], x["ll"]], axis=0), x["rh"]) for x in cx]
        g3 = [_dot_nt(x["lh"], x["rl"]) for x in cx]
        x0 = [_dot_nt(x["lh"], scr[...].astype(BF16)) for x, scr in zip(cx, st_scrs_)]
        m_ab, m_ra, m_rk, bkv = [], [], [], []
        for i in lanes_:
            gm = g2[i][0:2 * c] + g2[i][2 * c:4 * c] + g3[i]
            m_ab.append(jnp.where(strict, gm[0:c, 0:nc], 0.0))
            m_ra.append(jnp.where(incl, gm[c:2 * c, 0:nc], 0.0))
            m_rk.append(jnp.where(incl, gm[c:2 * c, nc:2 * nc], 0.0))
            bkv.append(_dot(jnp.where(strict, gm[0:c, nc:2 * nc], 0.0).astype(BF16), cx[i]["vbig"]))
        m2 = [jnp.concatenate(_split2(m), axis=1) for m in m_ab]
        cols = _dot(jnp.concatenate([jnp.where(src2 == s, m, jnp.zeros_like(m)) for m in m2 for s in range(nsolve)],
                                    axis=0), ehead_ref[...])
        u = [x0[i][0:c] + bkv[i] for i in lanes_]
        for s in range(nsolve):
            for i in lanes_:
                o_ = (i * nsolve + s) * c
                u[i] = u[i] - cols[o_:o_ + c] * u[i][s:s + 1, :]
        ys = [_dot(jnp.concatenate([-m_ra[i], m_rk[i]], axis=1).astype(BF16),
                   jnp.concatenate([tile4(u[i]).astype(BF16), cx[i]["vbig"]], axis=0)) for i in lanes_]
        upd = [_dot_tn(jnp.concatenate([-u[i], cx[i]["v"]], axis=0).astype(BF16), cx[i]["end"]) for i in lanes_]
        for i, scr in enumerate(st_scrs_):
            y_scr[cx[i]["rows"], :] = x0[i][c:2 * c] + ys[i]
            scr[...] = scr[...] * jnp.exp(cx[i]["gl"]) + upd[i] * mask_ref[...]

    def load_state(st_scr, seq):
        st_scr[...] = jnp.concatenate([st_in_ref[seq]] * N_HEADS, axis=1) * mask_ref[...]

    def store_state(st_scr, seq):
        st = st_scr[...]
        acc = st[:, 0:64]
        for h in range(1, N_HEADS):
            acc = acc + st[:, h * 64:(h + 1) * 64]
        st_out_ref[seq] = acc

    _run_lanes(st_scrs, nseq, rps, nsub, c, nst, load_state, store_state, sub_blocks)

    def finish_chunk(ci, carry):
        rows = pl.ds(ci * pc if isinstance(ci, int) else pl.multiple_of(ci * pc, 8), pc)
        y = y_scr[rows, :]
        mean = _seg_dot(y, ovv_ref[...], 3) * (1.0 / HEAD_V)
        d = y - mean
        var = _seg_dot(d * d, ovv_ref[...], 3) * (1.0 / HEAD_V)
        ln = d * lax.rsqrt(var + RWKV_LN_EPS) * par_ref[5:6, :] + par_ref[6:7, :]
        o_ref[rows, :] = (ln + bonus_scr[rows, :]) * g_scr[rows, :]
        return carry

    if nchunk == 1:
        finish_chunk(0, 0)
    else:
        lax.fori_loop(0, nchunk, finish_chunk, 0)


def _rwkv_call(z, mu, par, wup, aup, gup, ovv, hmask, ehead, mask, shift, st_in, o_prev, *, nrows, g):
    rblk = g.rows
    nsteps = g.nst
    base = g.base_blk
    pc = g.pc
    kern = functools.partial(_rwkv_kernel, c=g.c, nsub=g.nsub, tv=g.tv, nseq=g.nseq, rps=g.rps, nst=g.nst, pc=pc)
    const = lambda b, s: (0, 0)
    return pl.pallas_call(
        kern,
        grid=(g.ngrid, nsteps),
        in_specs=[
            pl.BlockSpec((rblk, SEC), lambda b, s: (base + b * nsteps + s, 1)),
            pl.BlockSpec((1, SEC), const),
            pl.BlockSpec((8, 256), const),
            pl.BlockSpec((256, 256), const),
            pl.BlockSpec((256, 256), const),
            pl.BlockSpec((256, 256), const),
            pl.BlockSpec((256, 256), const),
            pl.BlockSpec((N_HEADS * g.c, 256), const),
            pl.BlockSpec((2 * N_HEADS * g.c, 256), const),
            pl.BlockSpec((256, 256), const),
            pl.BlockSpec((g.nseq, 1, SEC), lambda b, s: (b, 0, 0)),
            pl.BlockSpec((g.nseq, 256, 64), lambda b, s: (b, 0, 0)),
            pl.BlockSpec(memory_space=pl.ANY),
        ],
        input_output_aliases={12: 0},
        out_specs=[
            pl.BlockSpec((rblk, 256), lambda b, s: (base + b * nsteps + s, 0)),
            pl.BlockSpec((g.nseq, 1, SEC), lambda b, s: (b, 0, 0)),
            pl.BlockSpec((g.nseq, 256, 64), lambda b, s: (b, 0, 0)),
        ],
        out_shape=[
            jax.ShapeDtypeStruct((nrows, 256), F32),
            jax.ShapeDtypeStruct((g.nb, 1, SEC), F32),
            jax.ShapeDtypeStruct((g.nb, 256, 64), F32),
        ],
        scratch_shapes=[pltpu.VMEM((max(8, g.nseq), SEC), F32), pltpu.VMEM((pc if pc > g.rps else 8, SEC), F32)]
        + [pltpu.VMEM((rblk, 256), F32)] * 9 + [pltpu.VMEM((256, 256), F32)] * g.lanes,
        compiler_params=_cparams("parallel", "arbitrary"),
    )(z, mu, par, wup, aup, gup, ovv, hmask, ehead, mask, shift, st_in, o_prev)


def _mlstm_kernel(z_ref, par_ref, bcol_ref, cw_ref, sel_ref, cm_in, nm_in, mm_in, cv_in, o_prev_ref,
                  o_ref, cm_out, nm_out, mm_out, cv_out,
                  *scratch, chunks, nseq, rps, nst):
    del o_prev_ref
    lanes = len(scratch) // 4

    def chunk_gates(scr, row0, c, tv):
        xbuf, cm_scr, nm_scr, mm_scr = scr
        rows = pl.ds(row0, c)
        mqk = z_ref[rows, 0:256]
        mv = z_ref[rows, 256:512]
        gi = z_ref[rows, 512:640]
        mo = z_ref[rows, 640:896]
        xbuf[8:8 + c, :] = mqk
        conv = par_ref[0:1, :] + xbuf[5:5 + c, :] * cw_ref[0:1, :]
        for j in range(1, CONV_W):
            conv = conv + xbuf[5 + j:5 + j + c, :] * cw_ref[j:j + 1, :]
        xbuf[5:8, :] = xbuf[8 + tv - 3:8 + tv, :]
        act = _silu(conv)
        q = act[:, 0:128]
        k = act[:, 128:256] * (MLSTM_DK ** -0.5)

        rowi = lax.broadcasted_iota(jnp.int32, (c, 1), 0)
        coli = lax.broadcasted_iota(jnp.int32, (1, c), 1)
        gcol = gi + par_ref[1:2, 0:128]
        lfc = _log_sigmoid(gcol)
        graw = None
        for part in _split3(gi):
            t_ = _dot_nt(sel_ref[...], part)
            graw = t_ if graw is None else graw + t_
        grow = graw + bcol_ref[...]
        lfr = _log_sigmoid(grow)
        if tv < c:
            gcol = jnp.where(rowi < tv, gcol, -jnp.inf)
            lfc = jnp.where(rowi < tv, lfc, 0.0)
            grow = jnp.where(coli < tv, grow, -jnp.inf)
            lfr = jnp.where(coli < tv, lfr, 0.0)
        b_col = _seg_dot_l(_tri(c).astype(BF16), lfc, 3)
        b_row = _seg_dot(lfr, _tri(c, upper=True).astype(BF16), 3)
        return dict(rows=rows, q=q, k=k, mv=mv, mo=mo, gcol=gcol, grow=grow, b_col=b_col, b_row=b_row,
                    mm=mm_scr[...], nm=nm_scr[...], cm=cm_scr[...])

    def chunk(scrs, row0s, c, tv):
        ctx = [chunk_gates(scr, row0, c, tv) for scr, row0 in zip(scrs, row0s)]
        causal = _tri(c)
        lane = lax.broadcasted_iota(jnp.int32, (1, 128), 1)
        items = [(l, h) for l in range(len(ctx)) for h in range(N_HEADS)]
        ks = lambda h: slice(h * MLSTM_DK, (h + 1) * MLSTM_DK)
        vs = lambda h: slice(h * HEAD_V, (h + 1) * HEAD_V)
        qh = {(l, h): ctx[l]["q"][:, ks(h)] for l, h in items}
        kh = {(l, h): ctx[l]["k"][:, ks(h)] for l, h in items}
        vh = {(l, h): ctx[l]["mv"][:, vs(h)].astype(BF16) for l, h in items}
        cmh = {(l, h): ctx[l]["cm"][ks(h), :] for l, h in items}
        nh = {(l, h): ctx[l]["nm"][0:1, ks(h)] for l, h in items}
        qk_raw = {it: _dot_nt(qh[it].astype(BF16), kh[it].astype(BF16)) for it in items}
        q_cm = {it: _dot(qh[it].astype(BF16), cmh[it].astype(BF16)) for it in items}
        bc, ic, mprev, m_t, sc, qk = {}, {}, {}, {}, {}, {}
        for it in items:
            l, h = it
            bc[it] = ctx[l]["b_col"][:, 4 + h:5 + h]
            br = ctx[l]["b_row"][4 + h:5 + h, :]
            ir = ctx[l]["grow"][h:h + 1, :]
            ic[it] = ctx[l]["gcol"][:, h:h + 1]
            dmat = jnp.where(causal, bc[it] - br + ir, -jnp.inf)
            mprev[it] = ctx[l]["mm"][0:1, h:h + 1]
            inter = bc[it] + mprev[it]
            m_t[it] = jnp.maximum(inter, jnp.max(dmat, axis=1, keepdims=True))
            sc[it] = jnp.exp(inter - m_t[it])
            qk[it] = qk_raw[it] * jnp.exp(dmat - m_t[it])
        num = {it: _dot(qk[it].astype(BF16), vh[it]) + sc[it] * q_cm[it] for it in items}
        kw, sl, m_new, hn = {}, {}, {}, {}
        for it in items:
            den = (jnp.sum(qk[it], axis=1, keepdims=True)
                   + sc[it] * jnp.sum(qh[it] * nh[it], axis=1, keepdims=True))
            hc = num[it] / jnp.maximum(jnp.abs(den), jnp.exp(-m_t[it]))
            m_new[it] = m_t[it][tv - 1:tv, :]
            bl = bc[it][tv - 1:tv, :]
            wl = jnp.exp(bl - bc[it] + ic[it] - m_new[it])
            sl[it] = jnp.exp(bl + mprev[it] - m_new[it])
            kw[it] = kh[it] * wl
            ms = jnp.mean(hc * hc, axis=1, keepdims=True)
            hn[it] = hc * lax.rsqrt(ms + EPS)
        kv = {it: _dot_tn(kw[it].astype(BF16), vh[it]) for it in items}
        for l, (scr, cx) in enumerate(zip(scrs, ctx)):
            xbuf, cm_scr, nm_scr, mm_scr = scr
            heads = [(l, h) for h in range(N_HEADS)]
            cm_scr[...] = jnp.concatenate([sl[it] * cmh[it] + kv[it] for it in heads], axis=0)
            nm_scr[...] = jnp.concatenate([sl[it] * nh[it] + jnp.sum(kw[it], axis=0, keepdims=True)
                                           for it in heads], axis=1)
            mm_new = cx["mm"]
            for it in heads:
                mm_new = jnp.where(lane == it[1], m_new[it], mm_new)
            mm_scr[...] = mm_new
            o_ref[cx["rows"], :] = (jnp.concatenate([hn[it] for it in heads], axis=1) * par_ref[2:3, :]
                                    * jax.nn.sigmoid(cx["mo"]))

    def load_state(scr, seq):
        xbuf, cm_scr, nm_scr, mm_scr = scr
        cm_scr[...] = cm_in[seq]
        nm_scr[...] = nm_in[seq]
        mm_scr[...] = mm_in[seq]
        xbuf[0:8, :] = cv_in[seq]

    def store_state(scr, seq):
        xbuf, cm_scr, nm_scr, mm_scr = scr
        cm_out[seq] = cm_scr[...]
        nm_out[seq] = nm_scr[...]
        mm_out[seq] = mm_scr[...]
        cv_out[seq] = xbuf[0:8, :]

    assert nseq == lanes or nst == 1
    step = pl.program_id(1)

    def seq_group(sg, carry):
        scrs = [scratch[4 * u:4 * u + 4] for u in range(lanes)]
        seqs = [sg * lanes + u for u in range(lanes)]
        for scr, seq in zip(scrs, seqs):
            if nst == 1:
                load_state(scr, seq)
            else:
                pl.when(step == 0)(functools.partial(load_state, scr, seq))
        lockstep = max(c for _, c, _, _ in chunks) <= 16
        for grp in ([list(zip(scrs, seqs))] if lockstep else [[p] for p in zip(scrs, seqs)]):
            for start, c, tv, count in chunks:
                def body(ci, carry2, grp=grp, start=start, c=c, tv=tv):
                    firsts = [seq * rps + start + ci * c for _, seq in grp]
                    chunk([scr for scr, _ in grp],
                          [f if isinstance(f, int) else pl.multiple_of(f, 8) for f in firsts], c, tv)
                    return carry2

                if count == 1:
                    body(0, 0)
                else:
                    lax.fori_loop(0, count, body, 0)
        for scr, seq in zip(scrs, seqs):
            if nst == 1:
                store_state(scr, seq)
            else:
                pl.when(step == nst - 1)(functools.partial(store_state, scr, seq))
        return carry

    if nseq == lanes:
        seq_group(0, 0)
    else:
        lax.fori_loop(0, nseq // lanes, seq_group, 0)


def _mlstm_call(z, par, bcol, cw, sel, cm, nm, mm, cv, o_prev, *, nrows, g):
    if g.full:
        cmain = max(d for d in (MLSTM_CHUNK, 64, 32, 16, 8) if d <= g.rps)
        lead = g.rps % cmain
        chunks = ([(0, lead, lead, 1)] if lead else []) + [(lead, cmain, cmain, g.rps // cmain)]
        assert lead % 8 == 0
    else:
        chunks = [(0, g.rps, g.tv, 1)]
    cmax = max(c for _, c, _, _ in chunks)
    kern = functools.partial(_mlstm_kernel, chunks=tuple(chunks), nseq=g.nseq, rps=g.rps, nst=g.nst)
    const = lambda b, s: (0, 0)
    st3 = lambda shp: pl.BlockSpec((g.nseq,) + shp, lambda b, s: (b, 0, 0))
    return pl.pallas_call(
        kern,
        grid=(g.ngrid, g.nst),
        in_specs=[
            pl.BlockSpec((g.rows, SEC), lambda b, s: (g.base_blk + b * g.nst + s, 3)),
            pl.BlockSpec((8, 256), const),
            pl.BlockSpec((8, 1), const),
            pl.BlockSpec((8, 256), const),
            pl.BlockSpec((8, 128), const),
            st3((128, 64)), st3((1, 128)), st3((1, 128)), st3((8, 256)),
            pl.BlockSpec(memory_space=pl.ANY),
        ],
        input_output_aliases={9: 0},
        out_specs=[
            pl.BlockSpec((g.rows, 256), lambda b, s: (g.base_blk + b * g.nst + s, 0)),
            st3((128, 64)), st3((1, 128)), st3((1, 128)), st3((8, 256)),
        ],
        out_shape=[
            jax.ShapeDtypeStruct((nrows, 256), F32),
            jax.ShapeDtypeStruct((g.nb, 128, 64), F32),
            jax.ShapeDtypeStruct((g.nb, 1, 128), F32),
            jax.ShapeDtypeStruct((g.nb, 1, 128), F32),
            jax.ShapeDtypeStruct((g.nb, 8, 256), F32),
        ],
        scratch_shapes=[
            pltpu.VMEM((8 + cmax, 256), F32),
            pltpu.VMEM((128, 64), F32),
            pltpu.VMEM((1, 128), F32),
            pltpu.VMEM((1, 128), F32),
        ] * g.lanes,
        compiler_params=_cparams("parallel", "arbitrary"),
    )(z, par, bcol, cw, sel, cm, nm, mm, cv, o_prev)


def _merge_kernel(o0_ref, o1_ref, o2_ref, o3_ref, x_ref, g_ref, wg_ref, wb_ref, wout_ref, out_ref, h_scr, acc_scr):
    n = pl.program_id(1)

    @pl.when(n == 0)
    def _():
        h_scr[...] = _rms_bf16(x_ref[...], g_ref[...])
        acc_scr[...] = jnp.zeros_like(acc_scr)

    gate = jax.nn.sigmoid(_dot(h_scr[...], wg_ref[...]))
    for idx, o_ref in enumerate((o0_ref, o1_ref, o2_ref, o3_ref)):
        @pl.when(n == idx)
        def _(o_ref=o_ref):
            acc_scr[...] += _dot(o_ref[...].astype(BF16), wb_ref[0]) * gate

    @pl.when(n == 3)
    def _():
        out_ref[...] = x_ref[...] + _dot(acc_scr[...].astype(BF16), wout_ref[...])


def _merge_call(outs, x, g, wg, wb, wout, tm):
    n = x.shape[0]
    o_spec = pl.BlockSpec((tm, 256), lambda i, j: (i, 0))
    return pl.pallas_call(
        _merge_kernel,
        grid=(n // tm, 4),
        in_specs=[
            o_spec, o_spec, o_spec, o_spec,
            pl.BlockSpec((tm, D_MODEL), lambda i, j: (i, 0)),
            pl.BlockSpec((1, D_MODEL), lambda i, j: (0, 0)),
            pl.BlockSpec((D_MODEL, D_MODEL), lambda i, j: (0, j)),
            pl.BlockSpec((1, 256, D_MODEL), lambda i, j: (j, 0, 0)),
            pl.BlockSpec((D_MODEL, D_MODEL), lambda i, j: (0, 0)),
        ],
        out_specs=pl.BlockSpec((tm, D_MODEL), lambda i, j: (i, 0)),
        out_shape=jax.ShapeDtypeStruct((n, D_MODEL), F32),
        scratch_shapes=[pltpu.VMEM((tm, D_MODEL), BF16), pltpu.VMEM((tm, D_MODEL), F32)],
        compiler_params=_cparams("parallel", "arbitrary"),
    )(*outs, x, g, wg, wb, wout)


def _mlp_kernel(x_ref, g_ref, gf_ref, wup_ref, wdn_ref, out_ref, h_scr, acc_scr, *, final):
    j = pl.program_id(1)

    @pl.when(j == 0)
    def _():
        h_scr[...] = _rms_bf16(x_ref[...], g_ref[...])
        acc_scr[...] = jnp.zeros_like(acc_scr)

    u = jnp.maximum(_dot(h_scr[...], wup_ref[...]), 0.0)
    acc_scr[...] += _dot((u * u).astype(BF16), wdn_ref[...])

    @pl.when(j == pl.num_programs(1) - 1)
    def _():
        y = x_ref[...] + acc_scr[...]
        if final:
            y = y * lax.rsqrt(jnp.mean(y * y, axis=-1, keepdims=True) + EPS) * gf_ref[...]
        out_ref[...] = y


def _mlp_call(x, g, gf, wup, wdn, tm, final):
    n = x.shape[0]
    tf = 1024
    return pl.pallas_call(
        functools.partial(_mlp_kernel, final=final),
        grid=(n // tm, D_FF // tf),
        in_specs=[
            pl.BlockSpec((tm, D_MODEL), lambda i, j: (i, 0)),
            pl.BlockSpec((1, D_MODEL), lambda i, j: (0, 0)),
            pl.BlockSpec((1, D_MODEL), lambda i, j: (0, 0)),
            pl.BlockSpec((D_MODEL, tf), lambda i, j: (0, j)),
            pl.BlockSpec((tf, D_MODEL), lambda i, j: (j, 0)),
        ],
        out_specs=pl.BlockSpec((tm, D_MODEL), lambda i, j: (i, 0)),
        out_shape=jax.ShapeDtypeStruct((n, D_MODEL), F32),
        scratch_shapes=[pltpu.VMEM((tm, D_MODEL), BF16), pltpu.VMEM((tm, D_MODEL), F32)],
        compiler_params=_cparams("parallel", "arbitrary"),
    )(x, g, gf, wup, wdn)


def _seg_ones(rows_per_head, cols_per_head):
    r = np.arange(N_HEADS * rows_per_head)[:, None] // rows_per_head
    c = np.arange(N_HEADS * cols_per_head)[None, :] // cols_per_head
    return (r == c).astype(np.float32)


def _rows(*vecs, width=256, nrows=8):
    out = jnp.zeros((nrows, width), F32)
    for i, v in enumerate(vecs):
        v = jnp.asarray(v, F32).reshape(-1)
        out = out.at[i, :v.shape[0]].set(v)
    return out


def _layout_w_in(w):
    d = w.shape[0]
    zeros = lambda n: jnp.zeros((d, n), w.dtype)
    gl0, ml0, gt0 = 2048, 2832, 3608
    parts = [
        w[:, 0:2048],
        w[:, gl0:gl0 + 512], w[:, gl0 + 512:gl0 + 528], zeros(112), w[:, gl0 + 528:gl0 + 784], zeros(128),
        w[:, ml0:ml0 + 512], w[:, ml0 + 512:ml0 + 520], zeros(120), w[:, ml0 + 520:ml0 + 776], zeros(128),
    ]
    wz = jnp.concatenate(parts, axis=1)
    assert wz.shape[1] == Z_COLS
    return wz.astype(BF16), w[:, gt0:gt0 + 4 * D_MODEL].astype(BF16)


class _Group:
    def __init__(self, nb, trow, tv, row0, max_seq):
        self.nb, self.trow, self.tv, self.row0 = nb, trow, tv, row0
        self.full = tv == trow
        self.c = SUB if self.full else trow
        self.tvs = min(tv, self.c)
        if self.full:
            nblk = trow // self.c
            self.rps = self.c * max(d for d in range(1, min(nblk, 64) + 1) if nblk % d == 0)
        else:
            self.rps = trow
        self.nst = trow // self.rps
        self.nsub = self.rps // self.c
        self.nseq = max(d for d in range(1, max_seq + 1) if nb % d == 0 and row0 % (d * self.rps) == 0)
        self.lanes = 4 if (not self.full and self.nseq % 4 == 0) else 2 if self.nseq % 2 == 0 else 1
        assert self.nseq == self.lanes or self.nst == 1
        self.rows = self.nseq * self.rps
        self.ngrid = nb // self.nseq
        self.base_blk = row0 // self.rows
        self.pc = _pick_tile(self.rps, 384) if self.rps >= 64 else self.rows

    def to_rows(self, a):
        d = a.shape[-1]
        a = a.reshape(self.ngrid, self.nseq, self.nst, self.rps, d).transpose(0, 2, 1, 3, 4)
        return a.reshape(self.nb * self.trow, d)

    def from_rows(self, a):
        d = a.shape[-1]
        a = a.reshape(self.ngrid, self.nst, self.nseq, self.rps, d).transpose(0, 2, 1, 3, 4)
        return a.reshape(self.nb, self.trow, d)


def kernel(x_prompt, x_sample, state_hgrn, state_rwkv, state_rwkv_shift, state_gla, state_mlstm_c, state_mlstm_n, state_mlstm_m, state_mlstm_conv, meta_tokens, norm_mix, norm_mlp, norm_final, w_in, hgrn_lb, hgrn_norm, rwkv_mu, rwkv_w0, rwkv_w_up, rwkv_a0, rwkv_a_up, rwkv_g_up, rwkv_k_k, rwkv_k_a, rwkv_r_k, rwkv_ln_w, rwkv_ln_b, gla_gate_up, gla_gate_b, gla_norm, mlstm_conv_w, mlstm_conv_b, mlstm_i_b, mlstm_f_b, mlstm_norm, w_branch, w_out, w_up, w_down):
    depth = w_in.shape[0]
    bp, seq, _ = x_prompt.shape
    bs, dseq, _ = x_sample.shape
    tp = N_META + seq
    assert tp % SUB == 0 and CONV_W - 1 <= dseq <= SAMPLE_PAD
    dt = x_prompt.dtype

    gp = _Group(bp, tp, tp, 0, 2)
    gs = _Group(bs, SAMPLE_PAD, dseq, bp * tp, 16)
    meta = jnp.broadcast_to(meta_tokens.astype(dt)[None], (bp, N_META, D_MODEL))
    xp = gp.to_rows(jnp.concatenate([meta, x_prompt], axis=1))
    xs = gs.to_rows(jnp.pad(x_sample, ((0, 0), (0, SAMPLE_PAD - dseq), (0, 0))))
    n_real = bp * tp + bs * SAMPLE_PAD
    n_rows = -(-n_real // ROW_PAD) * ROW_PAD
    x = jnp.concatenate([xp, xs, jnp.zeros((n_rows - n_real, D_MODEL), dt)], axis=0).astype(F32)
    tm = _pick_tile(n_rows, 1024, 128)

    ones_vv = jnp.asarray(_seg_ones(64, 64), BF16)
    ones_gl = jnp.asarray(_seg_ones(GLA_DK, 64), BF16)
    mask_hg = jnp.asarray(_seg_ones(64, 64), F32)
    mask_gl = jnp.asarray(_seg_ones(64, GLA_DK), F32)
    sel8 = jnp.asarray(np.eye(8, 128, dtype=np.float32), BF16)
    hmask = {g.c: jnp.asarray(np.kron(np.eye(N_HEADS), np.ones((g.c, 64))), F32) for g in (gp, gs)}
    ehead = {c_: jnp.concatenate([m_, m_], axis=0).astype(BF16) for c_, m_ in hmask.items()}

    lb_cs = jnp.cumsum(jax.nn.softmax(hgrn_lb.astype(F32), axis=0), axis=0)
    lb_all = lb_cs - lb_cs[:1]

    def init_states(states, nb, zero):
        s_hg, s_rw, shift, s_gl, c_ml, n_ml, m_ml, conv = states
        if zero:
            z = lambda *shp: jnp.zeros((depth, nb) + shp, F32)
            return dict(hg=z(256, 64), rw=z(256, 64), shift=z(1, SEC), gl=z(256, GLA_DK), cm=z(128, 64),
                        nm=z(1, 128), mm=z(1, 128), cv=z(8, 256))
        f = lambda a: a.astype(F32)
        return dict(
            hg=f(s_hg).transpose(0, 1, 2, 4, 3).reshape(depth, nb, 256, 64),
            rw=f(s_rw).reshape(depth, nb, 256, 64),
            shift=f(shift).reshape(depth, nb, 1, SEC),
            gl=f(s_gl).transpose(0, 1, 2, 4, 3).reshape(depth, nb, 256, GLA_DK),
            cm=f(c_ml).reshape(depth, nb, 128, 64),
            nm=f(n_ml).reshape(depth, nb, 1, 128),
            mm=jnp.pad(f(m_ml), ((0, 0), (0, 0), (0, 124))).reshape(depth, nb, 1, 128),
            cv=jnp.pad(f(conv), ((0, 0), (0, 0), (5, 0), (0, 0))),
        )

    sample_states = (state_hgrn, state_rwkv, state_rwkv_shift, state_gla,
                     state_mlstm_c, state_mlstm_n, state_mlstm_m, state_mlstm_conv)
    st_p = init_states(sample_states, bp, True)
    st_s = init_states(sample_states, bs, False)
    new_p = {k: [] for k in st_p}
    new_s = {k: [] for k in st_s}

    for l in range(depth):
        wz, wg = _layout_w_in(w_in[l])
        g_mix = norm_mix[l].reshape(1, D_MODEL).astype(F32)
        z = _in_proj(x, g_mix, wz, tm)

        lb = lb_all[l]
        par_hg = _rows(jnp.log(lb), jnp.log1p(-lb), 1.0 - lb, hgrn_norm[l])
        par_gl = _rows(gla_gate_b[l], jnp.zeros((1,)), jnp.zeros((1,)), gla_norm[l])
        gup_gl = jnp.zeros((128, 128), F32).at[:gla_gate_up.shape[1], :].set(gla_gate_up[l]).astype(BF16)
        gup_dummy = jnp.zeros((128, 128), BF16)
        par_rw = _rows(rwkv_w0[l], rwkv_a0[l], rwkv_k_k[l], rwkv_k_a[l], rwkv_r_k[l], rwkv_ln_w[l], rwkv_ln_b[l])
        nw, na = rwkv_w_up.shape[1], rwkv_a_up.shape[1]
        wup_p = jnp.zeros((256, 256), F32).at[0:nw].set(rwkv_w_up[l]).astype(BF16)
        aup_p = jnp.zeros((256, 256), F32).at[nw:nw + na].set(rwkv_a_up[l]).astype(BF16)
        gup_p = jnp.zeros((256, 256), F32).at[nw + na:].set(rwkv_g_up[l]).astype(BF16)
        mu = rwkv_mu[l].reshape(1, SEC).astype(F32)
        gate_b = jnp.concatenate([mlstm_i_b[l], mlstm_f_b[l]]).astype(F32)
        par_ml = _rows(mlstm_conv_b[l], gate_b, mlstm_norm[l])
        bcol_ml = gate_b.reshape(8, 1)
        cw_ml = _rows(*[mlstm_conv_w[l, j] for j in range(CONV_W)])

        o_hg = o_rw = o_gl = o_ml = None
        for g, st, new in ((gp, st_p, new_p), (gs, st_s, new_s)):
            prev = lambda o: jnp.zeros((n_rows, 256), F32) if o is None else o
            o_hg, s_hg = _gla_call(z, 0, par_hg, gup_dummy, ones_vv, ones_vv, mask_hg, st["hg"][l], prev(o_hg),
                                   mode="hgrn", dk=64, nrows=n_rows, g=g)
            o_gl, s_gl = _gla_call(z, 2, par_gl, gup_gl, ones_gl, ones_vv, mask_gl, st["gl"][l], prev(o_gl),
                                   mode="gla", dk=GLA_DK, nrows=n_rows, g=g)
            o_rw, nshift, s_rw = _rwkv_call(z, mu, par_rw, wup_p, aup_p, gup_p, ones_vv, hmask[g.c], ehead[g.c],
                                            mask_hg, st["shift"][l], st["rw"][l], prev(o_rw), nrows=n_rows, g=g)
            o_ml, s_cm, s_nm, s_mm, s_cv = _mlstm_call(z, par_ml, bcol_ml, cw_ml, sel8, st["cm"][l], st["nm"][l],
                                                        st["mm"][l], st["cv"][l], prev(o_ml), nrows=n_rows, g=g)
            for key, val in (("hg", s_hg), ("rw", s_rw), ("shift", nshift), ("gl", s_gl), ("cm", s_cm),
                             ("nm", s_nm), ("mm", s_mm), ("cv", s_cv)):
                new[key].append(val)

        x = _merge_call((o_hg, o_rw, o_gl, o_ml), x, g_mix, wg, w_branch[l].astype(BF16), w_out[l].astype(BF16), tm)
        x = _mlp_call(x, norm_mlp[l].reshape(1, D_MODEL).astype(F32), norm_final.reshape(1, D_MODEL).astype(F32),
                      w_up[l].astype(BF16), w_down[l].astype(BF16), tm, final=(l == depth - 1))

    y_prompt = gp.from_rows(x[:bp * tp])[:, N_META:].astype(dt)
    y_sample = gs.from_rows(x[gs.row0:gs.row0 + bs * SAMPLE_PAD])[:, :dseq].astype(dt)

    def finish(new, nb):
        st = {k: jnp.stack(v) for k, v in new.items()}
        return (
            st["hg"].reshape(depth, nb, N_HEADS, 64, 64).transpose(0, 1, 2, 4, 3),
            st["rw"].reshape(depth, nb, N_HEADS, 64, 64),
            st["shift"].reshape(depth, nb, SEC),
            st["gl"].reshape(depth, nb, N_HEADS, 64, GLA_DK).transpose(0, 1, 2, 4, 3),
            st["cm"].reshape(depth, nb, N_HEADS, MLSTM_DK, 64),
            st["nm"].reshape(depth, nb, N_HEADS, MLSTM_DK),
            st["mm"].reshape(depth, nb, 128)[:, :, :N_HEADS],
            st["cv"][:, :, 5:8, :],
        )

    outs_p = tuple(a.astype(dt) for a in finish(new_p, bp))
    outs_s = tuple(a.astype(dt) for a in finish(new_s, bs))
    return (y_prompt, y_sample) + outs_p + outs_s
```

```python
import functools

import numpy as np
import jax
import jax.numpy as jnp
from jax import lax
from jax.experimental import pallas as pl
from jax.experimental.pallas import tpu as pltpu

F32 = jnp.float32
BF16 = jnp.bfloat16

D_MODEL = 1024
N_META = 16
N_HEADS = 4
HEAD_V = 64
GLA_DK = 32
MLSTM_DK = 32
GLA_GATE_TAU = 16.0
RWKV_LN_EPS = 64e-5
CONV_W = 4
D_FF = 4 * D_MODEL
EPS = 1e-6

SEC = 1024
Z_COLS = 4 * SEC
SUB = 16
SAMPLE_PAD = 8
ROW_PAD = 512
MLSTM_CHUNK = 128
VMEM_LIMIT = 48 * 1024 * 1024


def _cparams(*sem):
    return pltpu.CompilerParams(dimension_semantics=sem, vmem_limit_bytes=VMEM_LIMIT)


def _pick_tile(n, max_tile, mult=8):
    best = None
    t = mult
    while t <= min(n, max_tile):
        if n % t == 0:
            best = t
        t += mult
    assert best is not None, (n, max_tile, mult)
    return best


def _split2(x):
    hi = x.astype(BF16)
    lo = (x - hi.astype(F32)).astype(BF16)
    return hi, lo


def _split3(x):
    hi = x.astype(BF16)
    r1 = x - hi.astype(F32)
    mid = r1.astype(BF16)
    lo = (r1 - mid.astype(F32)).astype(BF16)
    return hi, mid, lo


def _dot(a, b):
    return jnp.dot(a, b, preferred_element_type=F32)


def _dot_nt(a, b):
    return lax.dot_general(a, b, (((1,), (1,)), ((), ())), preferred_element_type=F32)


def _dot_tn(a, b):
    return lax.dot_general(a, b, (((0,), (0,)), ((), ())), preferred_element_type=F32)


def _seg_dot(x, m_bf16, parts):
    ps = _split2(x) if parts == 2 else _split3(x)
    acc = _dot(ps[0], m_bf16)
    for p in ps[1:]:
        acc = acc + _dot(p, m_bf16)
    return acc


def _seg_dot_l(m_bf16, x, parts):
    ps = _split2(x) if parts == 2 else _split3(x)
    acc = _dot(m_bf16, ps[0])
    for p in ps[1:]:
        acc = acc + _dot(m_bf16, p)
    return acc


def _log_sigmoid(x):
    return jnp.minimum(x, 0.0) - jnp.log1p(jnp.exp(-jnp.abs(x)))


def _silu(x):
    return x * jax.nn.sigmoid(x)


def _tri(n, upper=False):
    r = lax.broadcasted_iota(jnp.int32, (n, n), 0)
    c = lax.broadcasted_iota(jnp.int32, (n, n), 1)
    return (r <= c) if upper else (r >= c)


def _cumsum_rows(x, rowi):
    d = 1
    while d < x.shape[0]:
        x = x + jnp.where(rowi >= d, pltpu.roll(x, d, axis=0), 0.0)
        d *= 2
    return x


def _rms_bf16(x, g):
    return (x * lax.rsqrt(jnp.mean(x * x, axis=-1, keepdims=True) + EPS) * g).astype(BF16)


def _in_proj_kernel(x_ref, g_ref, w_ref, z_ref):
    h = _rms_bf16(x_ref[...], g_ref[...])
    for j in range(Z_COLS // SEC):
        z_ref[:, j * SEC:(j + 1) * SEC] = _dot(h, w_ref[:, j * SEC:(j + 1) * SEC])


def _in_proj(x, g, w):
    n = x.shape[0]
    tm = _pick_tile(n, 512, 128)
    return pl.pallas_call(
        _in_proj_kernel,
        grid=(n // tm,),
        in_specs=[
            pl.BlockSpec((tm, D_MODEL), lambda i: (i, 0)),
            pl.BlockSpec((1, D_MODEL), lambda i: (0, 0)),
            pl.BlockSpec((D_MODEL, Z_COLS), lambda i: (0, 0)),
        ],
        out_specs=pl.BlockSpec((tm, Z_COLS), lambda i: (i, 0)),
        out_shape=jax.ShapeDtypeStruct((n, Z_COLS), F32),
        compiler_params=_cparams("parallel"),
    )(x, g, w)


def _gla_kernel(z_ref, par_ref, gup_ref, okv_ref, ovv_ref, mask_ref, st_in_ref, o_prev_ref,
                o_ref, st_out_ref, *st_scrs, mode, c, nsub, tv, dk, nseq, rps, nst, pc):
    del o_prev_ref
    voff = 512 if mode == "hgrn" else 256
    rowi = lax.broadcasted_iota(jnp.int32, (c, 1), 0)
    valid = rowi < tv

    def sub_block(st_scr, r0):
        rows = pl.ds(r0, c)
        if mode == "hgrn":
            q = z_ref[rows, 0:256]
            hf = z_ref[rows, 256:512]
            g = z_ref[rows, 768:1024]
            a = par_ref[0:1, :]
            cc = par_ref[1:2, :] + _log_sigmoid(hf)
            loga = jnp.maximum(a, cc) + jnp.log1p(jnp.exp(-jnp.abs(a - cc)))
            k = par_ref[2:3, :] * jax.nn.sigmoid(-hf)
        else:
            q = z_ref[rows, 0:128] * (GLA_DK ** -0.5)
            k = z_ref[rows, 128:256]
            ga = z_ref[rows, 512:640]
            g = z_ref[rows, 640:896]
            gl = _dot(ga.astype(BF16), gup_ref[...]) + par_ref[0:1, 0:128]
            loga = _log_sigmoid(gl) * (1.0 / GLA_GATE_TAU)
        v = z_ref[rows, voff:voff + 256]
        if tv < c:
            loga = jnp.where(valid, loga, 0.0)
            k = jnp.where(valid, k, 0.0)
            v = jnp.where(valid, v, 0.0)
        b = _cumsum_rows(loga, rowi)

        o = _dot_nt((q * jnp.exp(b)).astype(BF16), st_scr[...].astype(BF16))

        lo_rows = [8 * (s // 8) for s in range(tv)]
        pieces = []
        for s in range(tv):
            rs = slice(lo_rows[s], c)
            e = jnp.exp(jnp.where(rowi[rs] >= s, b[rs] - b[s:s + 1, :], -jnp.inf))
            pieces.append(q[rs] * e * k[s:s + 1, :])
        return rows, lo_rows, k, v, b, o, jnp.concatenate(pieces, axis=0)

    def sub_block_finish(st_scr, part, sm):
        rows, lo_rows, k, v, b, o, _ = part
        acc = {}
        off = 0
        for s in range(tv):
            n = c - lo_rows[s]
            contrib = sm[off:off + n] * v[s:s + 1, :]
            off += n
            acc[lo_rows[s]] = contrib if lo_rows[s] not in acc else acc[lo_rows[s]] + contrib
        for lo, a_ in acc.items():
            o = o + (a_ if lo == 0 else jnp.concatenate([jnp.zeros((lo, 256), F32), a_], axis=0))

        bl = b[c - 1:c, :]
        kdec = k * jnp.exp(bl - b)
        upd = _dot_tn(v.astype(BF16), kdec.astype(BF16))
        st_scr[...] = st_scr[...] * jnp.exp(bl) + upd * mask_ref[...]
        o_ref[rows, :] = o

    def sub_blocks(scrs, row0s):
        parts = [sub_block(scr, r0) for scr, r0 in zip(scrs, row0s)]
        sm = _dot(jnp.concatenate([p[-1] for p in parts], axis=0).astype(BF16), okv_ref[...])
        n = parts[0][-1].shape[0]
        for i, (scr, part) in enumerate(zip(scrs, parts)):
            sub_block_finish(scr, part, sm[i * n:(i + 1) * n])

    def load_state(st_scr, seq):
        st_scr[...] = jnp.concatenate([st_in_ref[seq]] * N_HEADS, axis=1) * mask_ref[...]

    def store_state(st_scr, seq):
        st = st_scr[...]
        acc = st[:, 0:dk]
        for h in range(1, N_HEADS):
            acc = acc + st[:, h * dk:(h + 1) * dk]
        st_out_ref[seq] = acc

    _run_lanes(st_scrs, nseq, rps, nsub, c, nst, load_state, store_state, sub_blocks)

    goff = 768 if mode == "hgrn" else 640
    nchunk = (nseq * rps) // pc

    def finish_chunk(ci, carry):
        rows = pl.ds(ci * pc if isinstance(ci, int) else pl.multiple_of(ci * pc, 8), pc)
        o = o_ref[rows, :]
        ms = _seg_dot(o * o, ovv_ref[...], 2) * (1.0 / HEAD_V)
        o_ref[rows, :] = o * lax.rsqrt(ms + EPS) * par_ref[3:4, :] * _silu(z_ref[rows, goff:goff + 256])
        return carry

    if nchunk == 1:
        finish_chunk(0, 0)
    else:
        lax.fori_loop(0, nchunk, finish_chunk, 0)


def _run_lanes(scrs, nseq, rps, nsub, c, nst, load_state, store_state, sub_blocks, group=0):
    lanes = len(scrs)
    assert nseq == lanes or nst == 1
    step = pl.program_id(1)

    def seq_group(sg, carry):
        seqs = [sg * lanes + u for u in range(lanes)]
        for scr, seq in zip(scrs, seqs):
            if nst == 1:
                load_state(scr, seq)
            else:
                pl.when(step == 0)(functools.partial(load_state, scr, seq))

        def starts(j):
            return [seq * rps + j * c if isinstance(seq * rps + j * c, int)
                    else pl.multiple_of(seq * rps + j * c, 8) for seq in seqs]

        def blocks(j, carry2):
            if group:
                sub_blocks(scrs, [starts(j * group + g_) for g_ in range(group)])
            else:
                sub_blocks(scrs, starts(j))
            return carry2

        if group:
            if nsub // group == 1:
                blocks(0, 0)
            elif nsub // group > 1:
                lax.fori_loop(0, nsub // group, blocks, 0, unroll=2 if group == 1 else 1)
            for j in range(nsub - nsub % group, nsub):
                sub_blocks(scrs, [starts(j)])
        elif nsub == 1:
            blocks(0, 0)
        else:
            lax.fori_loop(0, nsub, blocks, 0, unroll=2)
        for scr, seq in zip(scrs, seqs):
            if nst == 1:
                store_state(scr, seq)
            else:
                pl.when(step == nst - 1)(functools.partial(store_state, scr, seq))
        return carry

    if nseq == lanes:
        seq_group(0, 0)
    else:
        lax.fori_loop(0, nseq // lanes, seq_group, 0)


def _gla_call(z, sec, par, gup, okv, ovv, mask, st_in, o_prev, *, mode, dk, nrows, g):
    hk = N_HEADS * dk
    kern = functools.partial(_gla_kernel, mode=mode, c=g.c, nsub=g.nsub, tv=g.tvs, dk=dk,
                             nseq=g.nseq, rps=g.rps, nst=g.nst, pc=g.pc)
    const = lambda b, s: (0, 0)
    return pl.pallas_call(
        kern,
        grid=(g.ngrid, g.nst),
        in_specs=[
            pl.BlockSpec((g.rows, SEC), lambda b, s: (g.base_blk + b * g.nst + s, sec)),
            pl.BlockSpec((8, 256), const),
            pl.BlockSpec((128, 128), const),
            pl.BlockSpec((hk, 256), const),
            pl.BlockSpec((256, 256), const),
            pl.BlockSpec((256, hk), const),
            pl.BlockSpec((g.nseq, 256, dk), lambda b, s: (b, 0, 0)),
            pl.BlockSpec(memory_space=pl.ANY),
        ],
        input_output_aliases={7: 0},
        out_specs=[
            pl.BlockSpec((g.rows, 256), lambda b, s: (g.base_blk + b * g.nst + s, 0)),
            pl.BlockSpec((g.nseq, 256, dk), lambda b, s: (b, 0, 0)),
        ],
        out_shape=[
            jax.ShapeDtypeStruct((nrows, 256), F32),
            jax.ShapeDtypeStruct((g.nb, 256, dk), F32),
        ],
        scratch_shapes=[pltpu.VMEM((256, hk), F32)] * g.lanes,
        compiler_params=_cparams("parallel", "arbitrary"),
    )(z, par, gup, okv, ovv, mask, st_in, o_prev)


def _rwkv_kernel(z_ref, mu_ref, par_ref, wup_ref, aup_ref, gup_ref, ovv_ref, hmask_ref, ehead_ref, mask_ref,
                 shift_ref, st_in_ref, o_prev_ref,
                 o_ref, nshift_ref, st_out_ref,
                 prev_scr, fix_scr, r_scr, lw_scr, k_scr, v_scr, kk_scr, ka_scr, g_scr, bonus_scr, y_scr, *st_scrs,
                 c, nsub, tv, nseq, rps, nst, pc):
    del o_prev_ref
    step = pl.program_id(1)

    def shifted_chunk(ci):
        r0 = ci * pc if isinstance(ci, int) else pl.multiple_of(ci * pc, 8)
        zr = z_ref[pl.ds(r0, pc), :]
        rolled = pltpu.roll(zr, 1, axis=0)
        if pc <= rps:
            cps = rps // pc
            seq = ci // cps
            rowi = lax.broadcasted_iota(jnp.int32, (pc, 1), 0)
            before = z_ref[pl.ds(pl.multiple_of(jnp.maximum(r0 - 8, 0), 8), 8), :][7:8, :]
            first = jnp.where(ci % cps == 0, prev_scr[pl.ds(seq, 1), :], before)
            prev = jnp.where(rowi == 0, first, rolled)
            prev_scr[pl.ds(seq, 1), :] = zr[pc - 1:pc, :]
            nshift_ref[seq] = zr[pc - 1:pc, :]
        else:
            fix_scr[...] = rolled
            for i in range(nseq):
                fix_scr[i * rps:i * rps + 1, :] = shift_ref[i]
                nshift_ref[i] = zr[i * rps + tv - 1:i * rps + tv, :]
            prev = fix_scr[...]
        return r0, zr, prev

    if pc <= rps:
        @pl.when(step == 0)
        def _():
            for i in range(nseq):
                prev_scr[i:i + 1, :] = shift_ref[i]

    def prep_chunk(ci, carry):
        r0, zr, prev = shifted_chunk(ci)
        rows = pl.ds(r0, pc)
        _rwkv_prep_rows(zr, prev, rows)
        return carry

    def _rwkv_prep_rows(zr, prev, rows):
        nrow = zr.shape[0]
        zs = zr + (prev - zr) * mu_ref[...]
        r = zs[:, 0:256]
        k = zs[:, 256:512]
        v = zs[:, 512:768]
        lo = zs[:, 768:1024]
        w0, a0, k_k, k_a, r_k = (par_ref[i:i + 1, :] for i in range(5))
        wl = w0 + _dot(jnp.tanh(lo).astype(BF16), wup_ref[...])
        wexp = -(jnp.maximum(-wl, 0.0) + jnp.log1p(jnp.exp(-jnp.abs(wl)))) - 0.5
        lw = -jnp.exp(wexp)
        a = jax.nn.sigmoid(a0 + _dot(lo.astype(BF16), aup_ref[...]))
        g_scr[rows, :] = _dot(jax.nn.sigmoid(lo).astype(BF16), gup_ref[...])
        kkp = k * k_k
        nrm = jnp.sqrt(_seg_dot(kkp * kkp, ovv_ref[...], 3))
        kk = kkp / jnp.maximum(nrm, 1e-12)
        k2 = k * (1.0 + (a - 1.0) * k_a)
        bonus_scr[rows, :] = _seg_dot(r * k2 * r_k, ovv_ref[...], 3) * v
        ka = kk * a
        if tv < rps:
            okrow = lax.rem(lax.broadcasted_iota(jnp.int32, (nrow, 1), 0), rps) < tv
            lw, k2, v, kk, ka = (jnp.where(okrow, t_, 0.0) for t_ in (lw, k2, v, kk, ka))
        r_scr[rows, :] = r
        lw_scr[rows, :] = lw
        k_scr[rows, :] = k2
        v_scr[rows, :] = v
        kk_scr[rows, :] = kk
        ka_scr[rows, :] = ka

    nchunk = (nseq * rps) // pc
    if nchunk == 1:
        prep_chunk(0, 0)
    else:
        lax.fori_loop(0, nchunk, prep_chunk, 0)

    hp = 2
    ng = N_HEADS // hp
    gw = hp * 64
    rowc = lax.broadcasted_iota(jnp.int32, (c, 1), 0)
    src2 = lax.broadcasted_iota(jnp.int32, (1, 2 * hp * c), 1) & (c - 1)
    src = src2[:, 0:hp * c]
    strict = rowc > src
    incl = rowc >= src

    def sub_blocks(lane_scrs, row0_lists):
        staged = [stage_free(lane_scrs, row0s) for row0s in row0_lists]
        for st_ in staged:
            stage_state(*st_)

    def stage_free(lane_scrs, row0s):
        hm = hmask_ref[...]
        tile = lambda t_: jnp.concatenate([t_] * hp, axis=0) * hm
        nc = hp * c
        nsolve = min(tv, c - 1)
        units, cx = [], {}
        for i, r0 in enumerate(row0s):
            rows = pl.ds(r0, c)
            lw_, r_, k_, v_, kk_, ka_ = (s_[rows, :] for s_ in (lw_scr, r_scr, k_scr, v_scr, kk_scr, ka_scr))
            gam = _cumsum_rows(lw_, rowc)
            gl = gam[c - 1:c, :]
            ginv = jnp.exp(-gam)
            gend = jnp.exp(gl - gam)
            lh, ll = _split2(jnp.concatenate([kk_ * jnp.exp(gam - lw_), r_ * jnp.exp(gam)], axis=0))
            a_, kq_ = ka_ * ginv, k_ * ginv
            ae_, ke_ = ka_ * gend, k_ * gend
            for p in range(ng):
                ls = slice(p * gw, (p + 1) * gw)
                rh, rl = _split2(jnp.concatenate([tile(a_[:, ls]), tile(kq_[:, ls])], axis=0))
                units.append((i, p))
                cx[i, p] = dict(rows=rows, ls=ls, v=v_[:, ls], egl=jnp.exp(gl[:, ls]), lh=lh[:, ls], ll=ll[:, ls],
                                rh=rh, rl=rl, vbig=tile(v_[:, ls]).astype(BF16), scr=lane_scrs[i][p],
                                end=jnp.concatenate([ae_[:, ls], ke_[:, ls]], axis=0).astype(BF16))
        g2 = {u_: _dot_nt(jnp.concatenate([cx[u_]["lh"], cx[u_]["ll"]], axis=0), cx[u_]["rh"]) for u_ in units}
        g3 = {u_: _dot_nt(cx[u_]["lh"], cx[u_]["rl"]) for u_ in units}
        m_ab, m_ra, m_rk, bkv = {}, {}, {}, {}
        for u_ in units:
            gm = g2[u_][0:2 * c] + g2[u_][2 * c:4 * c] + g3[u_]
            m_ab[u_] = jnp.where(strict, gm[0:c, 0:nc], 0.0)
            m_ra[u_] = jnp.where(incl, gm[c:2 * c, 0:nc], 0.0)
            m_rk[u_] = jnp.where(incl, gm[c:2 * c, nc:2 * nc], 0.0)
            bkv[u_] = _dot(jnp.where(strict, gm[0:c, nc:2 * nc], 0.0).astype(BF16), cx[u_]["vbig"])
        m2 = {u_: jnp.concatenate(_split2(m_ab[u_]), axis=1) for u_ in units}
        cols = _dot(jnp.concatenate([jnp.where(src2 == s, m2[u_], jnp.zeros_like(m2[u_]))
                                     for u_ in units for s in range(nsolve)], axis=0), ehead_ref[...])
        return units, cx, m_ra, m_rk, bkv, cols

    def stage_state(units, cx, m_ra, m_rk, bkv, cols):
        nsolve = min(tv, c - 1)
        tile = lambda t_: jnp.concatenate([t_] * hp, axis=0) * hmask_ref[...]
        x0 = {u_: _dot_nt(cx[u_]["lh"], cx[u_]["scr"][...].astype(BF16)) for u_ in units}
        usol = {u_: x0[u_][0:c] + bkv[u_] for u_ in units}
        for s in range(nsolve):
            for n_, u_ in enumerate(units):
                o_ = (n_ * nsolve + s) * c
                usol[u_] = usol[u_] - cols[o_:o_ + c] * usol[u_][s:s + 1, :]
        ys = {u_: _dot(jnp.concatenate([-m_ra[u_], m_rk[u_]], axis=1).astype(BF16),
                       jnp.concatenate([tile(usol[u_]).astype(BF16), cx[u_]["vbig"]], axis=0)) for u_ in units}
        upd = {u_: _dot_tn(jnp.concatenate([-usol[u_], cx[u_]["v"]], axis=0).astype(BF16), cx[u_]["end"])
               for u_ in units}
        for u_ in units:
            x = cx[u_]
            y_scr[x["rows"], x["ls"]] = x0[u_][c:2 * c] + ys[u_]
            x["scr"][...] = x["scr"][...] * x["egl"] + upd[u_] * mask_ref[...]

    def load_state(scrs, seq):
        for p, scr in enumerate(scrs):
            scr[...] = jnp.concatenate([st_in_ref[seq, p * gw:(p + 1) * gw, :]] * hp, axis=1) * mask_ref[...]

    def store_state(scrs, seq):
        for p, scr in enumerate(scrs):
            st = scr[...]
            st_out_ref[seq, p * gw:(p + 1) * gw, :] = st[:, 0:64] + st[:, 64:128]

    lane_scrs = [st_scrs[i * ng:(i + 1) * ng] for i in range(len(st_scrs) // ng)]
    _run_lanes(lane_scrs, nseq, rps, nsub, c, nst, load_state, store_state, sub_blocks, group=1)

    def finish_chunk(ci, carry):
        rows = pl.ds(ci * pc if isinstance(ci, int) else pl.multiple_of(ci * pc, 8), pc)
        y = y_scr[rows, :]
        mean = _seg_dot(y, ovv_ref[...], 3) * (1.0 / HEAD_V)
        d = y - mean
        var = _seg_dot(d * d, ovv_ref[...], 3) * (1.0 / HEAD_V)
        ln = d * lax.rsqrt(var + RWKV_LN_EPS) * par_ref[5:6, :] + par_ref[6:7, :]
        o_ref[rows, :] = (ln + bonus_scr[rows, :]) * g_scr[rows, :]
        return carry

    if nchunk == 1:
        finish_chunk(0, 0)
    else:
        lax.fori_loop(0, nchunk, finish_chunk, 0)


def _rwkv_call(z, mu, par, wup, aup, gup, ovv, hmask, ehead, mask, shift, st_in, o_prev, *, nrows, g):
    rblk = g.rows
    nsteps = g.nst
    base = g.base_blk
    pc = g.pc
    kern = functools.partial(_rwkv_kernel, c=g.c, nsub=g.nsub, tv=g.tv, nseq=g.nseq, rps=g.rps, nst=g.nst, pc=pc)
    const = lambda b, s: (0, 0)
    return pl.pallas_call(
        kern,
        grid=(g.ngrid, nsteps),
        in_specs=[
            pl.BlockSpec((rblk, SEC), lambda b, s: (base + b * nsteps + s, 1)),
            pl.BlockSpec((1, SEC), const),
            pl.BlockSpec((8, 256), const),
            pl.BlockSpec((256, 256), const),
            pl.BlockSpec((256, 256), const),
            pl.BlockSpec((256, 256), const),
            pl.BlockSpec((256, 256), const),
            pl.BlockSpec((2 * g.c, 128), const),
            pl.BlockSpec((4 * g.c, 128), const),
            pl.BlockSpec((128, 128), const),
            pl.BlockSpec((g.nseq, 1, SEC), lambda b, s: (b, 0, 0)),
            pl.BlockSpec((g.nseq, 256, 64), lambda b, s: (b, 0, 0)),
            pl.BlockSpec(memory_space=pl.ANY),
        ],
        input_output_aliases={12: 0},
        out_specs=[
            pl.BlockSpec((rblk, 256), lambda b, s: (base + b * nsteps + s, 0)),
            pl.BlockSpec((g.nseq, 1, SEC), lambda b, s: (b, 0, 0)),
            pl.BlockSpec((g.nseq, 256, 64), lambda b, s: (b, 0, 0)),
        ],
        out_shape=[
            jax.ShapeDtypeStruct((nrows, 256), F32),
            jax.ShapeDtypeStruct((g.nb, 1, SEC), F32),
            jax.ShapeDtypeStruct((g.nb, 256, 64), F32),
        ],
        scratch_shapes=[pltpu.VMEM((max(8, g.nseq), SEC), F32), pltpu.VMEM((pc if pc > g.rps else 8, SEC), F32)]
        + [pltpu.VMEM((rblk, 256), F32)] * 9 + [pltpu.VMEM((128, 128), F32)] * (2 * g.lanes),
        compiler_params=_cparams("parallel", "arbitrary"),
    )(z, mu, par, wup, aup, gup, ovv, hmask, ehead, mask, shift, st_in, o_prev)


def _mlstm_kernel(z_ref, par_ref, bcol_ref, cw_ref, sel_ref, cm_in, nm_in, mm_in, cv_in, o_prev_ref,
                  o_ref, cm_out, nm_out, mm_out, cv_out,
                  *scratch, chunks, nseq, rps, nst):
    del o_prev_ref
    lanes = len(scratch) // 4

    def chunk_gates(scr, row0, c, tv):
        xbuf, cm_scr, nm_scr, mm_scr = scr
        rows = pl.ds(row0, c)
        mqk = z_ref[rows, 0:256]
        mv = z_ref[rows, 256:512]
        gi = z_ref[rows, 512:640]
        mo = z_ref[rows, 640:896]
        xbuf[8:8 + c, :] = mqk
        conv = par_ref[0:1, :] + xbuf[5:5 + c, :] * cw_ref[0:1, :]
        for j in range(1, CONV_W):
            conv = conv + xbuf[5 + j:5 + j + c, :] * cw_ref[j:j + 1, :]
        xbuf[5:8, :] = xbuf[8 + tv - 3:8 + tv, :]
        act = _silu(conv)
        q = act[:, 0:128]
        k = act[:, 128:256] * (MLSTM_DK ** -0.5)

        rowi = lax.broadcasted_iota(jnp.int32, (c, 1), 0)
        coli = lax.broadcasted_iota(jnp.int32, (1, c), 1)
        gcol = gi + par_ref[1:2, 0:128]
        lfc = _log_sigmoid(gcol)
        graw = None
        for part in _split3(gi):
            t_ = _dot_nt(sel_ref[...], part)
            graw = t_ if graw is None else graw + t_
        grow = graw + bcol_ref[...]
        lfr = _log_sigmoid(grow)
        if tv < c:
            gcol = jnp.where(rowi < tv, gcol, -jnp.inf)
            lfc = jnp.where(rowi < tv, lfc, 0.0)
            grow = jnp.where(coli < tv, grow, -jnp.inf)
            lfr = jnp.where(coli < tv, lfr, 0.0)
        b_col = _seg_dot_l(_tri(c).astype(BF16), lfc, 3)
        b_row = _seg_dot(lfr, _tri(c, upper=True).astype(BF16), 3)
        return dict(rows=rows, q=q, k=k, mv=mv, mo=mo, gcol=gcol, grow=grow, b_col=b_col, b_row=b_row,
                    mm=mm_scr[...], nm=nm_scr[...], cm=cm_scr[...])

    def chunk(scrs, row0s, c, tv):
        ctx = [chunk_gates(scr, row0, c, tv) for scr, row0 in zip(scrs, row0s)]
        causal = _tri(c)
        lane = lax.broadcasted_iota(jnp.int32, (1, 128), 1)
        items = [(l, h) for l in range(len(ctx)) for h in range(N_HEADS)]
        ks = lambda h: slice(h * MLSTM_DK, (h + 1) * MLSTM_DK)
        vs = lambda h: slice(h * HEAD_V, (h + 1) * HEAD_V)
        qh = {(l, h): ctx[l]["q"][:, ks(h)] for l, h in items}
        kh = {(l, h): ctx[l]["k"][:, ks(h)] for l, h in items}
        vh = {(l, h): ctx[l]["mv"][:, vs(h)].astype(BF16) for l, h in items}
        cmh = {(l, h): ctx[l]["cm"][ks(h), :] for l, h in items}
        nh = {(l, h): ctx[l]["nm"][0:1, ks(h)] for l, h in items}
        qk_raw = {it: _dot_nt(qh[it].astype(BF16), kh[it].astype(BF16)) for it in items}
        q_cm = {it: _dot(qh[it].astype(BF16), cmh[it].astype(BF16)) for it in items}
        bc, ic, mprev, m_t, sc, qk = {}, {}, {}, {}, {}, {}
        for it in items:
            l, h = it
            bc[it] = ctx[l]["b_col"][:, 4 + h:5 + h]
            br = ctx[l]["b_row"][4 + h:5 + h, :]
            ir = ctx[l]["grow"][h:h + 1, :]
            ic[it] = ctx[l]["gcol"][:, h:h + 1]
            dmat = jnp.where(causal, bc[it] - br + ir, -jnp.inf)
            mprev[it] = ctx[l]["mm"][0:1, h:h + 1]
            inter = bc[it] + mprev[it]
            m_t[it] = jnp.maximum(inter, jnp.max(dmat, axis=1, keepdims=True))
            sc[it] = jnp.exp(inter - m_t[it])
            qk[it] = qk_raw[it] * jnp.exp(dmat - m_t[it])
        num = {it: _dot(qk[it].astype(BF16), vh[it]) + sc[it] * q_cm[it] for it in items}
        kw, sl, m_new, hn = {}, {}, {}, {}
        for it in items:
            den = (jnp.sum(qk[it], axis=1, keepdims=True)
                   + sc[it] * jnp.sum(qh[it] * nh[it], axis=1, keepdims=True))
            hc = num[it] / jnp.maximum(jnp.abs(den), jnp.exp(-m_t[it]))
            m_new[it] = m_t[it][tv - 1:tv, :]
            bl = bc[it][tv - 1:tv, :]
            wl = jnp.exp(bl - bc[it] + ic[it] - m_new[it])
            sl[it] = jnp.exp(bl + mprev[it] - m_new[it])
            kw[it] = kh[it] * wl
            ms = jnp.mean(hc * hc, axis=1, keepdims=True)
            hn[it] = hc * lax.rsqrt(ms + EPS)
        kv = {it: _dot_tn(kw[it].astype(BF16), vh[it]) for it in items}
        for l, (scr, cx) in enumerate(zip(scrs, ctx)):
            xbuf, cm_scr, nm_scr, mm_scr = scr
            heads = [(l, h) for h in range(N_HEADS)]
            cm_scr[...] = jnp.concatenate([sl[it] * cmh[it] + kv[it] for it in heads], axis=0)
            nm_scr[...] = jnp.concatenate([sl[it] * nh[it] + jnp.sum(kw[it], axis=0, keepdims=True)
                                           for it in heads], axis=1)
            mm_new = cx["mm"]
            for it in heads:
                mm_new = jnp.where(lane == it[1], m_new[it], mm_new)
            mm_scr[...] = mm_new
            o_ref[cx["rows"], :] = (jnp.concatenate([hn[it] for it in heads], axis=1) * par_ref[2:3, :]
                                    * jax.nn.sigmoid(cx["mo"]))

    def load_state(scr, seq):
        xbuf, cm_scr, nm_scr, mm_scr = scr
        cm_scr[...] = cm_in[seq]
        nm_scr[...] = nm_in[seq]
        mm_scr[...] = mm_in[seq]
        xbuf[0:8, :] = cv_in[seq]

    def store_state(scr, seq):
        xbuf, cm_scr, nm_scr, mm_scr = scr
        cm_out[seq] = cm_scr[...]
        nm_out[seq] = nm_scr[...]
        mm_out[seq] = mm_scr[...]
        cv_out[seq] = xbuf[0:8, :]

    assert nseq == lanes or nst == 1
    step = pl.program_id(1)

    def seq_group(sg, carry):
        scrs = [scratch[4 * u:4 * u + 4] for u in range(lanes)]
        seqs = [sg * lanes + u for u in range(lanes)]
        for scr, seq in zip(scrs, seqs):
            if nst == 1:
                load_state(scr, seq)
            else:
                pl.when(step == 0)(functools.partial(load_state, scr, seq))
        lockstep = max(c for _, c, _, _ in chunks) <= 16
        for grp in ([list(zip(scrs, seqs))] if lockstep else [[p] for p in zip(scrs, seqs)]):
            for start, c, tv, count in chunks:
                def body(ci, carry2, grp=grp, start=start, c=c, tv=tv):
                    firsts = [seq * rps + start + ci * c for _, seq in grp]
                    chunk([scr for scr, _ in grp],
                          [f if isinstance(f, int) else pl.multiple_of(f, 8) for f in firsts], c, tv)
                    return carry2

                if count == 1:
                    body(0, 0)
                else:
                    lax.fori_loop(0, count, body, 0)
        for scr, seq in zip(scrs, seqs):
            if nst == 1:
                store_state(scr, seq)
            else:
                pl.when(step == nst - 1)(functools.partial(store_state, scr, seq))
        return carry

    if nseq == lanes:
        seq_group(0, 0)
    else:
        lax.fori_loop(0, nseq // lanes, seq_group, 0)


def _mlstm_call(z, par, bcol, cw, sel, cm, nm, mm, cv, o_prev, *, nrows, g):
    if g.full:
        cmain = max(d for d in (MLSTM_CHUNK, 64, 32, 16, 8) if d <= g.rps)
        lead = g.rps % cmain
        chunks = ([(0, lead, lead, 1)] if lead else []) + [(lead, cmain, cmain, g.rps // cmain)]
        assert lead % 8 == 0
    else:
        chunks = [(0, g.rps, g.tv, 1)]
    cmax = max(c for _, c, _, _ in chunks)
    kern = functools.partial(_mlstm_kernel, chunks=tuple(chunks), nseq=g.nseq, rps=g.rps, nst=g.nst)
    const = lambda b, s: (0, 0)
    st3 = lambda shp: pl.BlockSpec((g.nseq,) + shp, lambda b, s: (b, 0, 0))
    return pl.pallas_call(
        kern,
        grid=(g.ngrid, g.nst),
        in_specs=[
            pl.BlockSpec((g.rows, SEC), lambda b, s: (g.base_blk + b * g.nst + s, 3)),
            pl.BlockSpec((8, 256), const),
            pl.BlockSpec((8, 1), const),
            pl.BlockSpec((8, 256), const),
            pl.BlockSpec((8, 128), const),
            st3((128, 64)), st3((1, 128)), st3((1, 128)), st3((8, 256)),
            pl.BlockSpec(memory_space=pl.ANY),
        ],
        input_output_aliases={9: 0},
        out_specs=[
            pl.BlockSpec((g.rows, 256), lambda b, s: (g.base_blk + b * g.nst + s, 0)),
            st3((128, 64)), st3((1, 128)), st3((1, 128)), st3((8, 256)),
        ],
        out_shape=[
            jax.ShapeDtypeStruct((nrows, 256), F32),
            jax.ShapeDtypeStruct((g.nb, 128, 64), F32),
            jax.ShapeDtypeStruct((g.nb, 1, 128), F32),
            jax.ShapeDtypeStruct((g.nb, 1, 128), F32),
            jax.ShapeDtypeStruct((g.nb, 8, 256), F32),
        ],
        scratch_shapes=[
            pltpu.VMEM((8 + cmax, 256), F32),
            pltpu.VMEM((128, 64), F32),
            pltpu.VMEM((1, 128), F32),
            pltpu.VMEM((1, 128), F32),
        ] * g.lanes,
        compiler_params=_cparams("parallel", "arbitrary"),
    )(z, par, bcol, cw, sel, cm, nm, mm, cv, o_prev)


def _merge_kernel(o0_ref, o1_ref, o2_ref, o3_ref, x_ref, g_ref, wg_ref, wb_ref, wout_ref, out_ref, h_scr, acc_scr):
    n = pl.program_id(1)

    @pl.when(n == 0)
    def _():
        h_scr[...] = _rms_bf16(x_ref[...], g_ref[...])
        acc_scr[...] = jnp.zeros_like(acc_scr)

    gate = jax.nn.sigmoid(_dot(h_scr[...], wg_ref[...]))
    for idx, o_ref in enumerate((o0_ref, o1_ref, o2_ref, o3_ref)):
        @pl.when(n == idx)
        def _(o_ref=o_ref):
            acc_scr[...] += _dot(o_ref[...].astype(BF16), wb_ref[0]) * gate

    @pl.when(n == 3)
    def _():
        out_ref[...] = x_ref[...] + _dot(acc_scr[...].astype(BF16), wout_ref[...])


def _merge_call(outs, x, g, wg, wb, wout, tm):
    n = x.shape[0]
    o_spec = pl.BlockSpec((tm, 256), lambda i, j: (i, 0))
    return pl.pallas_call(
        _merge_kernel,
        grid=(n // tm, 4),
        in_specs=[
            o_spec, o_spec, o_spec, o_spec,
            pl.BlockSpec((tm, D_MODEL), lambda i, j: (i, 0)),
            pl.BlockSpec((1, D_MODEL), lambda i, j: (0, 0)),
            pl.BlockSpec((D_MODEL, D_MODEL), lambda i, j: (0, j)),
            pl.BlockSpec((1, 256, D_MODEL), lambda i, j: (j, 0, 0)),
            pl.BlockSpec((D_MODEL, D_MODEL), lambda i, j: (0, 0)),
        ],
        out_specs=pl.BlockSpec((tm, D_MODEL), lambda i, j: (i, 0)),
        out_shape=jax.ShapeDtypeStruct((n, D_MODEL), F32),
        scratch_shapes=[pltpu.VMEM((tm, D_MODEL), BF16), pltpu.VMEM((tm, D_MODEL), F32)],
        compiler_params=_cparams("parallel", "arbitrary"),
    )(*outs, x, g, wg, wb, wout)


def _mlp_kernel(x_ref, g_ref, gf_ref, wup_ref, wdn_ref, out_ref, h_scr, acc_scr, *, final):
    j = pl.program_id(1)

    @pl.when(j == 0)
    def _():
        h_scr[...] = _rms_bf16(x_ref[...], g_ref[...])
        acc_scr[...] = jnp.zeros_like(acc_scr)

    u = jnp.maximum(_dot(h_scr[...], wup_ref[...]), 0.0)
    acc_scr[...] += _dot((u * u).astype(BF16), wdn_ref[...])

    @pl.when(j == pl.num_programs(1) - 1)
    def _():
        y = x_ref[...] + acc_scr[...]
        if final:
            y = y * lax.rsqrt(jnp.mean(y * y, axis=-1, keepdims=True) + EPS) * gf_ref[...]
        out_ref[...] = y


def _mlp_call(x, g, gf, wup, wdn, tm, final):
    n = x.shape[0]
    tf = 1024
    return pl.pallas_call(
        functools.partial(_mlp_kernel, final=final),
        grid=(n // tm, D_FF // tf),
        in_specs=[
            pl.BlockSpec((tm, D_MODEL), lambda i, j: (i, 0)),
            pl.BlockSpec((1, D_MODEL), lambda i, j: (0, 0)),
            pl.BlockSpec((1, D_MODEL), lambda i, j: (0, 0)),
            pl.BlockSpec((D_MODEL, tf), lambda i, j: (0, j)),
            pl.BlockSpec((tf, D_MODEL), lambda i, j: (j, 0)),
        ],
        out_specs=pl.BlockSpec((tm, D_MODEL), lambda i, j: (i, 0)),
        out_shape=jax.ShapeDtypeStruct((n, D_MODEL), F32),
        scratch_shapes=[pltpu.VMEM((tm, D_MODEL), BF16), pltpu.VMEM((tm, D_MODEL), F32)],
        compiler_params=_cparams("parallel", "arbitrary"),
    )(x, g, gf, wup, wdn)


def _seg_ones(rows_per_head, cols_per_head):
    r = np.arange(N_HEADS * rows_per_head)[:, None] // rows_per_head
    c = np.arange(N_HEADS * cols_per_head)[None, :] // cols_per_head
    return (r == c).astype(np.float32)


def _rows(*vecs, width=256, nrows=8):
    rows = []
    for v in vecs:
        v = jnp.asarray(v, F32).reshape(-1)
        rows.append(jnp.pad(v, (0, width - v.shape[0])))
    rows += [jnp.zeros((width,), F32)] * (nrows - len(rows))
    return jnp.stack(rows)


def _layout_w_in(w):
    d = w.shape[0]
    zeros = lambda n: jnp.zeros((d, n), w.dtype)
    gl0, ml0, gt0 = 2048, 2832, 3608
    parts = [
        w[:, 0:2048],
        w[:, gl0:gl0 + 512], w[:, gl0 + 512:gl0 + 528], zeros(112), w[:, gl0 + 528:gl0 + 784], zeros(128),
        w[:, ml0:ml0 + 512], w[:, ml0 + 512:ml0 + 520], zeros(120), w[:, ml0 + 520:ml0 + 776], zeros(128),
    ]
    wz = jnp.concatenate(parts, axis=1)
    assert wz.shape[1] == Z_COLS
    return wz.astype(BF16), w[:, gt0:gt0 + 4 * D_MODEL].astype(BF16)


class _Group:
    def __init__(self, nb, trow, tv, row0, max_seq):
        self.nb, self.trow, self.tv, self.row0 = nb, trow, tv, row0
        self.full = tv == trow
        self.c = SUB if self.full else trow
        self.tvs = min(tv, self.c)
        if self.full:
            nblk = trow // self.c
            self.rps = self.c * max(d for d in range(1, min(nblk, 64) + 1) if nblk % d == 0)
        else:
            self.rps = trow
        self.nst = trow // self.rps
        self.nsub = self.rps // self.c
        self.nseq = max(d for d in range(1, max_seq + 1) if nb % d == 0 and row0 % (d * self.rps) == 0)
        self.lanes = 4 if (not self.full and self.nseq % 4 == 0) else 2 if self.nseq % 2 == 0 else 1
        assert self.nseq == self.lanes or self.nst == 1
        self.rows = self.nseq * self.rps
        self.ngrid = nb // self.nseq
        self.base_blk = row0 // self.rows
        self.pc = _pick_tile(self.rps, 384) if self.rps >= 64 else self.rows

    def to_rows(self, a):
        d = a.shape[-1]
        a = a.reshape(self.ngrid, self.nseq, self.nst, self.rps, d).transpose(0, 2, 1, 3, 4)
        return a.reshape(self.nb * self.trow, d)

    def from_rows(self, a):
        d = a.shape[-1]
        a = a.reshape(self.ngrid, self.nst, self.nseq, self.rps, d).transpose(0, 2, 1, 3, 4)
        return a.reshape(self.nb, self.trow, d)


def kernel(x_prompt, x_sample, state_hgrn, state_rwkv, state_rwkv_shift, state_gla, state_mlstm_c, state_mlstm_n, state_mlstm_m, state_mlstm_conv, meta_tokens, norm_mix, norm_mlp, norm_final, w_in, hgrn_lb, hgrn_norm, rwkv_mu, rwkv_w0, rwkv_w_up, rwkv_a0, rwkv_a_up, rwkv_g_up, rwkv_k_k, rwkv_k_a, rwkv_r_k, rwkv_ln_w, rwkv_ln_b, gla_gate_up, gla_gate_b, gla_norm, mlstm_conv_w, mlstm_conv_b, mlstm_i_b, mlstm_f_b, mlstm_norm, w_branch, w_out, w_up, w_down):
    depth = w_in.shape[0]
    bp, seq, _ = x_prompt.shape
    bs, dseq, _ = x_sample.shape
    tp = N_META + seq
    assert tp % SUB == 0 and CONV_W - 1 <= dseq <= SAMPLE_PAD
    dt = x_prompt.dtype

    gp = _Group(bp, tp, tp, 0, 2)
    gs = _Group(bs, SAMPLE_PAD, dseq, bp * tp, 16)
    meta = jnp.broadcast_to(meta_tokens.astype(dt)[None], (bp, N_META, D_MODEL))
    xp = gp.to_rows(jnp.concatenate([meta, x_prompt], axis=1))
    xs = gs.to_rows(jnp.pad(x_sample, ((0, 0), (0, SAMPLE_PAD - dseq), (0, 0))))
    n_real = bp * tp + bs * SAMPLE_PAD
    n_rows = -(-n_real // ROW_PAD) * ROW_PAD
    x = jnp.concatenate([xp, xs, jnp.zeros((n_rows - n_real, D_MODEL), dt)], axis=0).astype(F32)
    tm = _pick_tile(n_rows, 1024, 128)

    ones_vv = jnp.asarray(_seg_ones(64, 64), BF16)
    ones_gl = jnp.asarray(_seg_ones(GLA_DK, 64), BF16)
    mask_hg = jnp.asarray(_seg_ones(64, 64), F32)
    mask_gl = jnp.asarray(_seg_ones(64, GLA_DK), F32)
    sel8 = jnp.asarray(np.eye(8, 128, dtype=np.float32), BF16)
    hmask = {g.c: jnp.asarray(np.kron(np.eye(2), np.ones((g.c, 64))), F32) for g in (gp, gs)}
    ehead = {c_: jnp.concatenate([m_, m_], axis=0).astype(BF16) for c_, m_ in hmask.items()}
    mask_pair = jnp.asarray(np.kron(np.eye(2), np.ones((64, 64))), F32)

    lb_cs = jnp.cumsum(jax.nn.softmax(hgrn_lb.astype(F32), axis=0), axis=0)
    lb_all = lb_cs - lb_cs[:1]

    def init_states(states, nb, zero):
        s_hg, s_rw, shift, s_gl, c_ml, n_ml, m_ml, conv = states
        if zero:
            z = lambda *shp: jnp.zeros((depth, nb) + shp, F32)
            return dict(hg=z(256, 64), rw=z(256, 64), shift=z(1, SEC), gl=z(256, GLA_DK), cm=z(128, 64),
                        nm=z(1, 128), mm=z(1, 128), cv=z(8, 256))
        f = lambda a: a.astype(F32)
        return dict(
            hg=f(s_hg).transpose(0, 1, 2, 4, 3).reshape(depth, nb, 256, 64),
            rw=f(s_rw).reshape(depth, nb, 256, 64),
            shift=f(shift).reshape(depth, nb, 1, SEC),
            gl=f(s_gl).transpose(0, 1, 2, 4, 3).reshape(depth, nb, 256, GLA_DK),
            cm=f(c_ml).reshape(depth, nb, 128, 64),
            nm=f(n_ml).reshape(depth, nb, 1, 128),
            mm=jnp.pad(f(m_ml), ((0, 0), (0, 0), (0, 124))).reshape(depth, nb, 1, 128),
            cv=jnp.pad(f(conv), ((0, 0), (0, 0), (5, 0), (0, 0))),
        )

    sample_states = (state_hgrn, state_rwkv, state_rwkv_shift, state_gla,
                     state_mlstm_c, state_mlstm_n, state_mlstm_m, state_mlstm_conv)
    st_p = init_states(sample_states, bp, True)
    st_s = init_states(sample_states, bs, False)
    new_p = {k: [] for k in st_p}
    new_s = {k: [] for k in st_s}

    for l in range(depth):
        wz, wg = _layout_w_in(w_in[l])
        g_mix = norm_mix[l].reshape(1, D_MODEL).astype(F32)
        z = _in_proj(x, g_mix, wz)

        lb = lb_all[l]
        par_hg = _rows(jnp.log(lb), jnp.log1p(-lb), 1.0 - lb, hgrn_norm[l])
        par_gl = _rows(gla_gate_b[l], jnp.zeros((1,)), jnp.zeros((1,)), gla_norm[l])
        gup_gl = jnp.pad(gla_gate_up[l], ((0, 128 - gla_gate_up.shape[1]), (0, 0))).astype(BF16)
        gup_dummy = jnp.zeros((128, 128), BF16)
        par_rw = _rows(rwkv_w0[l], rwkv_a0[l], rwkv_k_k[l], rwkv_k_a[l], rwkv_r_k[l], rwkv_ln_w[l], rwkv_ln_b[l])
        nw, na = rwkv_w_up.shape[1], rwkv_a_up.shape[1]
        wup_p = jnp.pad(rwkv_w_up[l], ((0, 256 - nw), (0, 0))).astype(BF16)
        aup_p = jnp.pad(rwkv_a_up[l], ((nw, 256 - nw - na), (0, 0))).astype(BF16)
        gup_p = jnp.pad(rwkv_g_up[l], ((nw + na, 0), (0, 0))).astype(BF16)
        mu = rwkv_mu[l].reshape(1, SEC).astype(F32)
        gate_b = jnp.concatenate([mlstm_i_b[l], mlstm_f_b[l]]).astype(F32)
        par_ml = _rows(mlstm_conv_b[l], gate_b, mlstm_norm[l])
        bcol_ml = gate_b.reshape(8, 1)
        cw_ml = _rows(*[mlstm_conv_w[l, j] for j in range(CONV_W)])

        o_hg = o_rw = o_gl = o_ml = None
        for g, st, new in ((gp, st_p, new_p), (gs, st_s, new_s)):
            prev = lambda o: jnp.zeros((n_rows, 256), F32) if o is None else o
            o_hg, s_hg = _gla_call(z, 0, par_hg, gup_dummy, ones_vv, ones_vv, mask_hg, st["hg"][l], prev(o_hg),
                                   mode="hgrn", dk=64, nrows=n_rows, g=g)
            o_gl, s_gl = _gla_call(z, 2, par_gl, gup_gl, ones_gl, ones_vv, mask_gl, st["gl"][l], prev(o_gl),
                                   mode="gla", dk=GLA_DK, nrows=n_rows, g=g)
            o_rw, nshift, s_rw = _rwkv_call(z, mu, par_rw, wup_p, aup_p, gup_p, ones_vv, hmask[g.c], ehead[g.c],
                                            mask_pair, st["shift"][l], st["rw"][l], prev(o_rw), nrows=n_rows, g=g)
            o_ml, s_cm, s_nm, s_mm, s_cv = _mlstm_call(z, par_ml, bcol_ml, cw_ml, sel8, st["cm"][l], st["nm"][l],
                                                        st["mm"][l], st["cv"][l], prev(o_ml), nrows=n_rows, g=g)
            for key, val in (("hg", s_hg), ("rw", s_rw), ("shift", nshift), ("gl", s_gl), ("cm", s_cm),
                             ("nm", s_nm), ("mm", s_mm), ("cv", s_cv)):
                new[key].append(val)

        x = _merge_call((o_hg, o_rw, o_gl, o_ml), x, g_mix, wg, w_branch[l].astype(BF16), w_out[l].astype(BF16), tm)
        x = _mlp_call(x, norm_mlp[l].reshape(1, D_MODEL).astype(F32), norm_final.reshape(1, D_MODEL).astype(F32),
                      w_up[l].astype(BF16), w_down[l].astype(BF16), tm, final=(l == depth - 1))

    y_prompt = gp.from_rows(x[:bp * tp])[:, N_META:].astype(dt)
    y_sample = gs.from_rows(x[gs.row0:gs.row0 + bs * SAMPLE_PAD])[:, :dseq].astype(dt)

    def finish(new, nb):
        st = {k: jnp.stack(v) for k, v in new.items()}
        return (
            st["hg"].reshape(depth, nb, N_HEADS, 64, 64).transpose(0, 1, 2, 4, 3),
            st["rw"].reshape(depth, nb, N_HEADS, 64, 64),
            st["shift"].reshape(depth, nb, SEC),
            st["gl"].reshape(depth, nb, N_HEADS, 64, GLA_DK).transpose(0, 1, 2, 4, 3),
            st["cm"].reshape(depth, nb, N_HEADS, MLSTM_DK, 64),
            st["nm"].reshape(depth, nb, N_HEADS, MLSTM_DK),
            st["mm"].reshape(depth, nb, 128)[:, :, :N_HEADS],
            st["cv"][:, :, 5:8, :],
        )

    outs_p = tuple(a.astype(dt) for a in finish(new_p, bp))
    outs_s = tuple(a.astype(dt) for a in finish(new_s, bs))
    return (y_prompt, y_sample) + outs_p + outs_s
```

```python
import functools

import numpy as np
import jax
import jax.numpy as jnp
from jax import lax
from jax.experimental import pallas as pl
from jax.experimental.pallas import tpu as pltpu

F32 = jnp.float32
BF16 = jnp.bfloat16

D_MODEL = 1024
N_META = 16
N_HEADS = 4
HEAD_V = 64
GLA_DK = 32
MLSTM_DK = 32
GLA_GATE_TAU = 16.0
RWKV_LN_EPS = 64e-5
CONV_W = 4
D_FF = 4 * D_MODEL
EPS = 1e-6

SEC = 1024
Z_COLS = 4 * SEC
SUB = 16
SAMPLE_PAD = 8
MLSTM_CHUNK = 128
VMEM_LIMIT = 48 * 1024 * 1024


def _cparams(*sem):
    return pltpu.CompilerParams(dimension_semantics=sem, vmem_limit_bytes=VMEM_LIMIT)


def _pick_tile(n, max_tile, mult=8):
    best = None
    t = mult
    while t <= min(n, max_tile):
        if n % t == 0:
            best = t
        t += mult
    assert best is not None, (n, max_tile, mult)
    return best


def _split2(x):
    hi = x.astype(BF16)
    lo = (x - hi.astype(F32)).astype(BF16)
    return hi, lo


def _split3(x):
    hi = x.astype(BF16)
    r1 = x - hi.astype(F32)
    mid = r1.astype(BF16)
    lo = (r1 - mid.astype(F32)).astype(BF16)
    return hi, mid, lo


def _dot(a, b):
    return jnp.dot(a, b, preferred_element_type=F32)


def _dot_nt(a, b):
    return lax.dot_general(a, b, (((1,), (1,)), ((), ())), preferred_element_type=F32)


def _dot_tn(a, b):
    return lax.dot_general(a, b, (((0,), (0,)), ((), ())), preferred_element_type=F32)


def _seg_dot(x, m_bf16, parts):
    ps = _split2(x) if parts == 2 else _split3(x)
    acc = _dot(ps[0], m_bf16)
    for p in ps[1:]:
        acc = acc + _dot(p, m_bf16)
    return acc


def _seg_dot_l(m_bf16, x, parts):
    ps = _split2(x) if parts == 2 else _split3(x)
    acc = _dot(m_bf16, ps[0])
    for p in ps[1:]:
        acc = acc + _dot(m_bf16, p)
    return acc


def _log_sigmoid(x):
    return jnp.minimum(x, 0.0) - jnp.log1p(jnp.exp(-jnp.abs(x)))


def _silu(x):
    return x * jax.nn.sigmoid(x)


def _tri(n, upper=False):
    r = lax.broadcasted_iota(jnp.int32, (n, n), 0)
    c = lax.broadcasted_iota(jnp.int32, (n, n), 1)
    return (r <= c) if upper else (r >= c)


def _cumsum_rows(x, rowi):
    d = 1
    while d < x.shape[0]:
        x = x + jnp.where(rowi >= d, pltpu.roll(x, d, axis=0), 0.0)
        d *= 2
    return x


def _rms_bf16(x, g):
    return (x * lax.rsqrt(jnp.mean(x * x, axis=-1, keepdims=True) + EPS) * g).astype(BF16)


def _in_proj_kernel(x_ref, g_ref, w_ref, z_ref):
    h = _rms_bf16(x_ref[...], g_ref[...])
    for j in range(Z_COLS // SEC):
        z_ref[:, j * SEC:(j + 1) * SEC] = _dot(h, w_ref[:, j * SEC:(j + 1) * SEC])


def _in_proj(x, g, w, tm, out_tile):
    n = x.shape[0]
    return pl.pallas_call(
        _in_proj_kernel,
        grid=(n // tm,),
        in_specs=[
            pl.BlockSpec((tm, D_MODEL), lambda i: (i, 0)),
            pl.BlockSpec((1, D_MODEL), lambda i: (0, 0)),
            pl.BlockSpec((D_MODEL, Z_COLS), lambda i: (0, 0)),
        ],
        out_specs=pl.BlockSpec((tm, Z_COLS), lambda i: (out_tile(i), 0)),
        out_shape=jax.ShapeDtypeStruct((n, Z_COLS), F32),
        compiler_params=_cparams("parallel"),
    )(x, g, w)


def _gla_kernel(z_ref, par_ref, gup_ref, okv_ref, ovv_ref, mask_ref, st_in_ref, o_prev_ref,
                o_ref, st_out_ref, *st_scrs, mode, c, nsub, tv, dk, nseq, rps, nst, pc):
    del o_prev_ref
    voff = 512 if mode == "hgrn" else 256
    rowi = lax.broadcasted_iota(jnp.int32, (c, 1), 0)
    valid = rowi < tv

    def sub_block(st_scr, r0):
        rows = pl.ds(r0, c)
        if mode == "hgrn":
            q = z_ref[rows, 0:256]
            hf = z_ref[rows, 256:512]
            g = z_ref[rows, 768:1024]
            a = par_ref[0:1, :]
            cc = par_ref[1:2, :] + _log_sigmoid(hf)
            loga = jnp.maximum(a, cc) + jnp.log1p(jnp.exp(-jnp.abs(a - cc)))
            k = par_ref[2:3, :] * jax.nn.sigmoid(-hf)
        else:
            q = z_ref[rows, 0:128] * (GLA_DK ** -0.5)
            k = z_ref[rows, 128:256]
            ga = z_ref[rows, 512:640]
            g = z_ref[rows, 640:896]
            gl = _dot(ga.astype(BF16), gup_ref[...]) + par_ref[0:1, 0:128]
            loga = _log_sigmoid(gl) * (1.0 / GLA_GATE_TAU)
        v = z_ref[rows, voff:voff + 256]
        if tv < c:
            loga = jnp.where(valid, loga, 0.0)
            k = jnp.where(valid, k, 0.0)
            v = jnp.where(valid, v, 0.0)
        b = _cumsum_rows(loga, rowi)

        o = _dot_nt((q * jnp.exp(b)).astype(BF16), st_scr[...].astype(BF16))

        lo_rows = [8 * (s // 8) for s in range(tv)]
        pieces = []
        for s in range(tv):
            rs = slice(lo_rows[s], c)
            e = jnp.exp(jnp.where(rowi[rs] >= s, b[rs] - b[s:s + 1, :], -jnp.inf))
            pieces.append(q[rs] * e * k[s:s + 1, :])
        return rows, lo_rows, k, v, b, o, jnp.concatenate(pieces, axis=0)

    def sub_block_finish(st_scr, part, sm):
        rows, lo_rows, k, v, b, o, _ = part
        acc = {}
        off = 0
        for s in range(tv):
            n = c - lo_rows[s]
            contrib = sm[off:off + n] * v[s:s + 1, :]
            off += n
            acc[lo_rows[s]] = contrib if lo_rows[s] not in acc else acc[lo_rows[s]] + contrib
        for lo, a_ in acc.items():
            o = o + (a_ if lo == 0 else jnp.concatenate([jnp.zeros((lo, 256), F32), a_], axis=0))

        bl = b[c - 1:c, :]
        kdec = k * jnp.exp(bl - b)
        upd = _dot_tn(v.astype(BF16), kdec.astype(BF16))
        st_scr[...] = st_scr[...] * jnp.exp(bl) + upd * mask_ref[...]
        o_ref[rows, :] = o

    def sub_blocks(scrs, row0s):
        parts = [sub_block(scr, r0) for scr, r0 in zip(scrs, row0s)]
        sm = _dot(jnp.concatenate([p[-1] for p in parts], axis=0).astype(BF16), okv_ref[...])
        n = parts[0][-1].shape[0]
        for i, (scr, part) in enumerate(zip(scrs, parts)):
            sub_block_finish(scr, part, sm[i * n:(i + 1) * n])

    def load_state(st_scr, seq):
        st_scr[...] = jnp.concatenate([st_in_ref[seq]] * N_HEADS, axis=1) * mask_ref[...]

    def store_state(st_scr, seq):
        st = st_scr[...]
        acc = st[:, 0:dk]
        for h in range(1, N_HEADS):
            acc = acc + st[:, h * dk:(h + 1) * dk]
        st_out_ref[seq] = acc

    _run_lanes(st_scrs, nseq, rps, nsub, c, nst, load_state, store_state, sub_blocks)

    goff = 768 if mode == "hgrn" else 640
    nchunk = (nseq * rps) // pc

    def finish_chunk(ci, carry):
        rows = pl.ds(ci * pc if isinstance(ci, int) else pl.multiple_of(ci * pc, 8), pc)
        o = o_ref[rows, :]
        ms = _seg_dot(o * o, ovv_ref[...], 2) * (1.0 / HEAD_V)
        o_ref[rows, :] = o * lax.rsqrt(ms + EPS) * par_ref[3:4, :] * _silu(z_ref[rows, goff:goff + 256])
        return carry

    if nchunk == 1:
        finish_chunk(0, 0)
    else:
        lax.fori_loop(0, nchunk, finish_chunk, 0)


def _run_lanes(scrs, nseq, rps, nsub, c, nst, load_state, store_state, sub_blocks, group=0):
    lanes = len(scrs)
    assert nseq == lanes or nst == 1
    step = pl.program_id(1)

    def seq_group(sg, carry):
        seqs = [sg * lanes + u for u in range(lanes)]
        for scr, seq in zip(scrs, seqs):
            if nst == 1:
                load_state(scr, seq)
            else:
                pl.when(step == 0)(functools.partial(load_state, scr, seq))

        def starts(j):
            return [seq * rps + j * c if isinstance(seq * rps + j * c, int)
                    else pl.multiple_of(seq * rps + j * c, 8) for seq in seqs]

        def blocks(j, carry2):
            if group:
                sub_blocks(scrs, [starts(j * group + g_) for g_ in range(group)])
            else:
                sub_blocks(scrs, starts(j))
            return carry2

        if group:
            if nsub // group == 1:
                blocks(0, 0)
            elif nsub // group > 1:
                lax.fori_loop(0, nsub // group, blocks, 0, unroll=2 if group == 1 else 1)
            for j in range(nsub - nsub % group, nsub):
                sub_blocks(scrs, [starts(j)])
        elif nsub == 1:
            blocks(0, 0)
        else:
            lax.fori_loop(0, nsub, blocks, 0, unroll=2)
        for scr, seq in zip(scrs, seqs):
            if nst == 1:
                store_state(scr, seq)
            else:
                pl.when(step == nst - 1)(functools.partial(store_state, scr, seq))
        return carry

    if nseq == lanes:
        seq_group(0, 0)
    else:
        lax.fori_loop(0, nseq // lanes, seq_group, 0)


def _gla_call(z, sec, par, gup, okv, ovv, mask, st_in, o_prev, *, mode, dk, nrows, g):
    hk = N_HEADS * dk
    kern = functools.partial(_gla_kernel, mode=mode, c=g.c, nsub=g.nsub, tv=g.tvs, dk=dk,
                             nseq=g.nseq, rps=g.rps, nst=g.nst, pc=g.pc)
    const = lambda b, s: (0, 0)
    return pl.pallas_call(
        kern,
        grid=(g.ngrid, g.nst),
        in_specs=[
            pl.BlockSpec((g.rows, SEC), lambda b, s: (g.base_blk + b * g.nst + s, sec)),
            pl.BlockSpec((8, 256), const),
            pl.BlockSpec((128, 128), const),
            pl.BlockSpec((hk, 256), const),
            pl.BlockSpec((256, 256), const),
            pl.BlockSpec((256, hk), const),
            pl.BlockSpec((g.nseq, 256, dk), lambda b, s: (b, 0, 0)),
            pl.BlockSpec(memory_space=pl.ANY),
        ],
        input_output_aliases={7: 0},
        out_specs=[
            pl.BlockSpec((g.rows, 256), lambda b, s: (g.base_blk + b * g.nst + s, 0)),
            pl.BlockSpec((g.nseq, 256, dk), lambda b, s: (b, 0, 0)),
        ],
        out_shape=[
            jax.ShapeDtypeStruct((nrows, 256), F32),
            jax.ShapeDtypeStruct((g.nb, 256, dk), F32),
        ],
        scratch_shapes=[pltpu.VMEM((256, hk), F32)] * g.lanes,
        compiler_params=_cparams("parallel", "arbitrary"),
    )(z, par, gup, okv, ovv, mask, st_in, o_prev)


def _rwkv_kernel(z_ref, mu_ref, par_ref, wup_ref, aup_ref, gup_ref, ovv_ref, hmask_ref, ehead_ref, mask_ref,
                 shift_ref, st_in_ref, o_prev_ref,
                 o_ref, nshift_ref, st_out_ref,
                 prev_scr, fix_scr, r_scr, lw_scr, k_scr, v_scr, kk_scr, ka_scr, g_scr, bonus_scr, y_scr, *st_scrs,
                 c, nsub, tv, nseq, rps, nst, pc):
    del o_prev_ref
    step = pl.program_id(1)

    def shifted_chunk(ci):
        r0 = ci * pc if isinstance(ci, int) else pl.multiple_of(ci * pc, 8)
        zr = z_ref[pl.ds(r0, pc), :]
        rolled = pltpu.roll(zr, 1, axis=0)
        if pc <= rps:
            cps = rps // pc
            seq = ci // cps
            rowi = lax.broadcasted_iota(jnp.int32, (pc, 1), 0)
            before = z_ref[pl.ds(pl.multiple_of(jnp.maximum(r0 - 8, 0), 8), 8), :][7:8, :]
            first = jnp.where(ci % cps == 0, prev_scr[pl.ds(seq, 1), :], before)
            prev = jnp.where(rowi == 0, first, rolled)
            prev_scr[pl.ds(seq, 1), :] = zr[pc - 1:pc, :]
            nshift_ref[seq] = zr[pc - 1:pc, :]
        else:
            fix_scr[...] = rolled
            for i in range(nseq):
                fix_scr[i * rps:i * rps + 1, :] = shift_ref[i]
                nshift_ref[i] = zr[i * rps + tv - 1:i * rps + tv, :]
            prev = fix_scr[...]
        return r0, zr, prev

    if pc <= rps:
        @pl.when(step == 0)
        def _():
            for i in range(nseq):
                prev_scr[i:i + 1, :] = shift_ref[i]

    def prep_chunk(ci, carry):
        r0, zr, prev = shifted_chunk(ci)
        rows = pl.ds(r0, pc)
        _rwkv_prep_rows(zr, prev, rows)
        return carry

    def _rwkv_prep_rows(zr, prev, rows):
        nrow = zr.shape[0]
        zs = zr + (prev - zr) * mu_ref[...]
        r = zs[:, 0:256]
        k = zs[:, 256:512]
        v = zs[:, 512:768]
        lo = zs[:, 768:1024]
        w0, a0, k_k, k_a, r_k = (par_ref[i:i + 1, :] for i in range(5))
        wl = w0 + _dot(jnp.tanh(lo).astype(BF16), wup_ref[...])
        wexp = -(jnp.maximum(-wl, 0.0) + jnp.log1p(jnp.exp(-jnp.abs(wl)))) - 0.5
        lw = -jnp.exp(wexp)
        a = jax.nn.sigmoid(a0 + _dot(lo.astype(BF16), aup_ref[...]))
        g_scr[rows, :] = _dot(jax.nn.sigmoid(lo).astype(BF16), gup_ref[...])
        kkp = k * k_k
        nrm = jnp.sqrt(_seg_dot(kkp * kkp, ovv_ref[...], 3))
        kk = kkp / jnp.maximum(nrm, 1e-12)
        k2 = k * (1.0 + (a - 1.0) * k_a)
        bonus_scr[rows, :] = _seg_dot(r * k2 * r_k, ovv_ref[...], 3) * v
        ka = kk * a
        if tv < rps:
            okrow = lax.rem(lax.broadcasted_iota(jnp.int32, (nrow, 1), 0), rps) < tv
            lw, k2, v, kk, ka = (jnp.where(okrow, t_, 0.0) for t_ in (lw, k2, v, kk, ka))
        r_scr[rows, :] = r
        lw_scr[rows, :] = lw
        k_scr[rows, :] = k2
        v_scr[rows, :] = v
        kk_scr[rows, :] = kk
        ka_scr[rows, :] = ka

    nchunk = (nseq * rps) // pc
    if nchunk == 1:
        prep_chunk(0, 0)
    else:
        lax.fori_loop(0, nchunk, prep_chunk, 0)

    hp = 2
    ng = N_HEADS // hp
    gw = hp * 64
    rowc = lax.broadcasted_iota(jnp.int32, (c, 1), 0)
    src2 = lax.broadcasted_iota(jnp.int32, (1, 2 * hp * c), 1) & (c - 1)
    src = src2[:, 0:hp * c]
    strict = rowc > src
    incl = rowc >= src

    def sub_blocks(lane_scrs, row0_lists):
        staged = [stage_free(lane_scrs, row0s) for row0s in row0_lists]
        for st_ in staged:
            stage_state(*st_)

    def stage_free(lane_scrs, row0s):
        hm = hmask_ref[...]
        tile = lambda t_: jnp.concatenate([t_] * hp, axis=0) * hm
        nc = hp * c
        nsolve = min(tv, c - 1)
        units, cx = [], {}
        for i, r0 in enumerate(row0s):
            rows = pl.ds(r0, c)
            lw_, r_, k_, v_, kk_, ka_ = (s_[rows, :] for s_ in (lw_scr, r_scr, k_scr, v_scr, kk_scr, ka_scr))
            gam = _cumsum_rows(lw_, rowc)
            gl = gam[c - 1:c, :]
            ginv = jnp.exp(-gam)
            gend = jnp.exp(gl - gam)
            lh, ll = _split2(jnp.concatenate([kk_ * jnp.exp(gam - lw_), r_ * jnp.exp(gam)], axis=0))
            a_, kq_ = ka_ * ginv, k_ * ginv
            ae_, ke_ = ka_ * gend, k_ * gend
            for p in range(ng):
                ls = slice(p * gw, (p + 1) * gw)
                rh, rl = _split2(jnp.concatenate([tile(a_[:, ls]), tile(kq_[:, ls])], axis=0))
                units.append((i, p))
                cx[i, p] = dict(rows=rows, ls=ls, v=v_[:, ls], egl=jnp.exp(gl[:, ls]), lh=lh[:, ls], ll=ll[:, ls],
                                rh=rh, rl=rl, vbig=tile(v_[:, ls]).astype(BF16), scr=lane_scrs[i][p],
                                end=jnp.concatenate([ae_[:, ls], ke_[:, ls]], axis=0).astype(BF16))
        g2 = {u_: _dot_nt(jnp.concatenate([cx[u_]["lh"], cx[u_]["ll"]], axis=0), cx[u_]["rh"]) for u_ in units}
        g3 = {u_: _dot_nt(cx[u_]["lh"], cx[u_]["rl"]) for u_ in units}
        m_ab, m_ra, m_rk, bkv = {}, {}, {}, {}
        for u_ in units:
            gm = g2[u_][0:2 * c] + g2[u_][2 * c:4 * c] + g3[u_]
            m_ab[u_] = jnp.where(strict, gm[0:c, 0:nc], 0.0)
            m_ra[u_] = jnp.where(incl, gm[c:2 * c, 0:nc], 0.0)
            m_rk[u_] = jnp.where(incl, gm[c:2 * c, nc:2 * nc], 0.0)
            bkv[u_] = _dot(jnp.where(strict, gm[0:c, nc:2 * nc], 0.0).astype(BF16), cx[u_]["vbig"])
        m2 = {u_: jnp.concatenate(_split2(m_ab[u_]), axis=1) for u_ in units}
        cols = _dot(jnp.concatenate([jnp.where(src2 == s, m2[u_], jnp.zeros_like(m2[u_]))
                                     for u_ in units for s in range(nsolve)], axis=0), ehead_ref[...])
        return units, cx, m_ra, m_rk, bkv, cols

    def stage_state(units, cx, m_ra, m_rk, bkv, cols):
        nsolve = min(tv, c - 1)
        tile = lambda t_: jnp.concatenate([t_] * hp, axis=0) * hmask_ref[...]
        x0 = {u_: _dot_nt(cx[u_]["lh"], cx[u_]["scr"][...].astype(BF16)) for u_ in units}
        usol = {u_: x0[u_][0:c] + bkv[u_] for u_ in units}
        for s in range(nsolve):
            for n_, u_ in enumerate(units):
                o_ = (n_ * nsolve + s) * c
                usol[u_] = usol[u_] - cols[o_:o_ + c] * usol[u_][s:s + 1, :]
        ys = {u_: _dot(jnp.concatenate([-m_ra[u_], m_rk[u_]], axis=1).astype(BF16),
                       jnp.concatenate([tile(usol[u_]).astype(BF16), cx[u_]["vbig"]], axis=0)) for u_ in units}
        upd = {u_: _dot_tn(jnp.concatenate([-usol[u_], cx[u_]["v"]], axis=0).astype(BF16), cx[u_]["end"])
               for u_ in units}
        for u_ in units:
            x = cx[u_]
            y_scr[x["rows"], x["ls"]] = x0[u_][c:2 * c] + ys[u_]
            x["scr"][...] = x["scr"][...] * x["egl"] + upd[u_] * mask_ref[...]

    def load_state(scrs, seq):
        for p, scr in enumerate(scrs):
            scr[...] = jnp.concatenate([st_in_ref[seq, p * gw:(p + 1) * gw, :]] * hp, axis=1) * mask_ref[...]

    def store_state(scrs, seq):
        for p, scr in enumerate(scrs):
            st = scr[...]
            st_out_ref[seq, p * gw:(p + 1) * gw, :] = st[:, 0:64] + st[:, 64:128]

    lane_scrs = [st_scrs[i * ng:(i + 1) * ng] for i in range(len(st_scrs) // ng)]
    _run_lanes(lane_scrs, nseq, rps, nsub, c, nst, load_state, store_state, sub_blocks, group=1)

    def finish_chunk(ci, carry):
        rows = pl.ds(ci * pc if isinstance(ci, int) else pl.multiple_of(ci * pc, 8), pc)
        y = y_scr[rows, :]
        mean = _seg_dot(y, ovv_ref[...], 3) * (1.0 / HEAD_V)
        d = y - mean
        var = _seg_dot(d * d, ovv_ref[...], 3) * (1.0 / HEAD_V)
        ln = d * lax.rsqrt(var + RWKV_LN_EPS) * par_ref[5:6, :] + par_ref[6:7, :]
        o_ref[rows, :] = (ln + bonus_scr[rows, :]) * g_scr[rows, :]
        return carry

    if nchunk == 1:
        finish_chunk(0, 0)
    else:
        lax.fori_loop(0, nchunk, finish_chunk, 0)


def _rwkv_call(z, mu, par, wup, aup, gup, ovv, hmask, ehead, mask, shift, st_in, o_prev, *, nrows, g):
    rblk = g.rows
    nsteps = g.nst
    base = g.base_blk
    pc = g.pc
    kern = functools.partial(_rwkv_kernel, c=g.c, nsub=g.nsub, tv=g.tv, nseq=g.nseq, rps=g.rps, nst=g.nst, pc=pc)
    const = lambda b, s: (0, 0)
    return pl.pallas_call(
        kern,
        grid=(g.ngrid, nsteps),
        in_specs=[
            pl.BlockSpec((rblk, SEC), lambda b, s: (base + b * nsteps + s, 1)),
            pl.BlockSpec((1, SEC), const),
            pl.BlockSpec((8, 256), const),
            pl.BlockSpec((256, 256), const),
            pl.BlockSpec((256, 256), const),
            pl.BlockSpec((256, 256), const),
            pl.BlockSpec((256, 256), const),
            pl.BlockSpec((2 * g.c, 128), const),
            pl.BlockSpec((4 * g.c, 128), const),
            pl.BlockSpec((128, 128), const),
            pl.BlockSpec((g.nseq, 1, SEC), lambda b, s: (b, 0, 0)),
            pl.BlockSpec((g.nseq, 256, 64), lambda b, s: (b, 0, 0)),
            pl.BlockSpec(memory_space=pl.ANY),
        ],
        input_output_aliases={12: 0},
        out_specs=[
            pl.BlockSpec((rblk, 256), lambda b, s: (base + b * nsteps + s, 0)),
            pl.BlockSpec((g.nseq, 1, SEC), lambda b, s: (b, 0, 0)),
            pl.BlockSpec((g.nseq, 256, 64), lambda b, s: (b, 0, 0)),
        ],
        out_shape=[
            jax.ShapeDtypeStruct((nrows, 256), F32),
            jax.ShapeDtypeStruct((g.nb, 1, SEC), F32),
            jax.ShapeDtypeStruct((g.nb, 256, 64), F32),
        ],
        scratch_shapes=[pltpu.VMEM((max(8, g.nseq), SEC), F32), pltpu.VMEM((pc if pc > g.rps else 8, SEC), F32)]
        + [pltpu.VMEM((rblk, 256), F32)] * 9 + [pltpu.VMEM((128, 128), F32)] * (2 * g.lanes),
        compiler_params=_cparams("parallel", "arbitrary"),
    )(z, mu, par, wup, aup, gup, ovv, hmask, ehead, mask, shift, st_in, o_prev)


def _mlstm_kernel(z_ref, par_ref, bcol_ref, cw_ref, sel_ref, ovv_ref, cm_in, nm_in, mm_in, cv_in, o_prev_ref,
                  o_ref, cm_out, nm_out, mm_out, cv_out,
                  *scratch, chunks, nseq, rps, nst, pc):
    del o_prev_ref
    lanes = len(scratch) // 4

    def chunk_gates(scr, row0, c, tv):
        xbuf, cm_scr, nm_scr, mm_scr = scr
        rows = pl.ds(row0, c)
        mqk = z_ref[rows, 0:256]
        mv = z_ref[rows, 256:512]
        gi = z_ref[rows, 512:640]
        mo = z_ref[rows, 640:896]
        xbuf[8:8 + c, :] = mqk
        conv = par_ref[0:1, :] + xbuf[5:5 + c, :] * cw_ref[0:1, :]
        for j in range(1, CONV_W):
            conv = conv + xbuf[5 + j:5 + j + c, :] * cw_ref[j:j + 1, :]
        xbuf[5:8, :] = xbuf[8 + tv - 3:8 + tv, :]
        act = _silu(conv)
        q = act[:, 0:128]
        k = act[:, 128:256] * (MLSTM_DK ** -0.5)

        rowi = lax.broadcasted_iota(jnp.int32, (c, 1), 0)
        coli = lax.broadcasted_iota(jnp.int32, (1, c), 1)
        gcol = gi + par_ref[1:2, 0:128]
        lfc = _log_sigmoid(gcol)
        graw = None
        for part in _split3(gi):
            t_ = _dot_nt(sel_ref[...], part)
            graw = t_ if graw is None else graw + t_
        grow = graw + bcol_ref[...]
        lfr = _log_sigmoid(grow)
        if tv < c:
            gcol = jnp.where(rowi < tv, gcol, -jnp.inf)
            lfc = jnp.where(rowi < tv, lfc, 0.0)
            grow = jnp.where(coli < tv, grow, -jnp.inf)
            lfr = jnp.where(coli < tv, lfr, 0.0)
        b_col = _seg_dot_l(_tri(c).astype(BF16), lfc, 3)
        b_row = _seg_dot(lfr, _tri(c, upper=True).astype(BF16), 3)
        return dict(rows=rows, q=q, k=k, mv=mv, mo=mo, gcol=gcol, grow=grow, b_col=b_col, b_row=b_row,
                    mm=mm_scr[...], nm=nm_scr[...], cm=cm_scr[...])

    def chunk(scrs, row0s, c, tv):
        ctx = [chunk_gates(scr, row0, c, tv) for scr, row0 in zip(scrs, row0s)]
        causal = _tri(c)
        lane = lax.broadcasted_iota(jnp.int32, (1, 128), 1)
        items = [(l, h) for l in range(len(ctx)) for h in range(N_HEADS)]
        ks = lambda h: slice(h * MLSTM_DK, (h + 1) * MLSTM_DK)
        vs = lambda h: slice(h * HEAD_V, (h + 1) * HEAD_V)
        qh = {(l, h): ctx[l]["q"][:, ks(h)] for l, h in items}
        kh = {(l, h): ctx[l]["k"][:, ks(h)] for l, h in items}
        vh = {(l, h): ctx[l]["mv"][:, vs(h)].astype(BF16) for l, h in items}
        cmh = {(l, h): ctx[l]["cm"][ks(h), :] for l, h in items}
        nh = {(l, h): ctx[l]["nm"][0:1, ks(h)] for l, h in items}
        qk_raw = {it: _dot_nt(qh[it].astype(BF16), kh[it].astype(BF16)) for it in items}
        q_cm = {it: _dot(qh[it].astype(BF16), cmh[it].astype(BF16)) for it in items}
        bc, ic, mprev, m_t, sc, qk = {}, {}, {}, {}, {}, {}
        for it in items:
            l, h = it
            bc[it] = ctx[l]["b_col"][:, 4 + h:5 + h]
            br = ctx[l]["b_row"][4 + h:5 + h, :]
            ir = ctx[l]["grow"][h:h + 1, :]
            ic[it] = ctx[l]["gcol"][:, h:h + 1]
            dmat = jnp.where(causal, bc[it] - br + ir, -jnp.inf)
            mprev[it] = ctx[l]["mm"][0:1, h:h + 1]
            inter = bc[it] + mprev[it]
            m_t[it] = jnp.maximum(inter, jnp.max(dmat, axis=1, keepdims=True))
            sc[it] = jnp.exp(inter - m_t[it])
            qk[it] = qk_raw[it] * jnp.exp(dmat - m_t[it])
        num = {it: _dot(qk[it].astype(BF16), vh[it]) + sc[it] * q_cm[it] for it in items}
        qk_sum = _seg_dot(jnp.concatenate([qk[it] for it in items], axis=0), jnp.ones((c, HEAD_V), BF16), 2)
        qn_sum = _seg_dot(jnp.concatenate([qh[it] * nh[it] for it in items], axis=0),
                          jnp.ones((MLSTM_DK, HEAD_V), BF16), 2)
        kw, sl, m_new, hn = {}, {}, {}, {}
        for n_, it in enumerate(items):
            den = qk_sum[n_ * c:(n_ + 1) * c] + sc[it] * qn_sum[n_ * c:(n_ + 1) * c]
            hc = num[it] / jnp.maximum(jnp.abs(den), jnp.exp(-m_t[it]))
            m_new[it] = m_t[it][tv - 1:tv, :]
            bl = bc[it][tv - 1:tv, :]
            wl = jnp.exp(bl - bc[it] + ic[it] - m_new[it])
            sl[it] = jnp.exp(bl + mprev[it] - m_new[it])
            kw[it] = kh[it] * wl
            hn[it] = hc
        kv = {it: _dot_tn(kw[it].astype(BF16), vh[it]) for it in items}
        for l, (scr, cx) in enumerate(zip(scrs, ctx)):
            xbuf, cm_scr, nm_scr, mm_scr = scr
            heads = [(l, h) for h in range(N_HEADS)]
            cm_scr[...] = jnp.concatenate([sl[it] * cmh[it] + kv[it] for it in heads], axis=0)
            nm_scr[...] = jnp.concatenate([sl[it] * nh[it] + jnp.sum(kw[it], axis=0, keepdims=True)
                                           for it in heads], axis=1)
            mm_new = cx["mm"]
            for it in heads:
                mm_new = jnp.where(lane == it[1], m_new[it], mm_new)
            mm_scr[...] = mm_new
            o_ref[cx["rows"], :] = jnp.concatenate([hn[it] for it in heads], axis=1)

    def load_state(scr, seq):
        xbuf, cm_scr, nm_scr, mm_scr = scr
        cm_scr[...] = cm_in[seq]
        nm_scr[...] = nm_in[seq]
        mm_scr[...] = mm_in[seq]
        xbuf[0:8, :] = cv_in[seq]

    def store_state(scr, seq):
        xbuf, cm_scr, nm_scr, mm_scr = scr
        cm_out[seq] = cm_scr[...]
        nm_out[seq] = nm_scr[...]
        mm_out[seq] = mm_scr[...]
        cv_out[seq] = xbuf[0:8, :]

    assert nseq == lanes or nst == 1
    step = pl.program_id(1)

    def seq_group(sg, carry):
        scrs = [scratch[4 * u:4 * u + 4] for u in range(lanes)]
        seqs = [sg * lanes + u for u in range(lanes)]
        for scr, seq in zip(scrs, seqs):
            if nst == 1:
                load_state(scr, seq)
            else:
                pl.when(step == 0)(functools.partial(load_state, scr, seq))
        lockstep = max(c for _, c, _, _ in chunks) <= 16
        for grp in ([list(zip(scrs, seqs))] if lockstep else [[p] for p in zip(scrs, seqs)]):
            for start, c, tv, count in chunks:
                def body(ci, carry2, grp=grp, start=start, c=c, tv=tv):
                    firsts = [seq * rps + start + ci * c for _, seq in grp]
                    chunk([scr for scr, _ in grp],
                          [f if isinstance(f, int) else pl.multiple_of(f, 8) for f in firsts], c, tv)
                    return carry2

                if count == 1:
                    body(0, 0)
                else:
                    lax.fori_loop(0, count, body, 0)
        for scr, seq in zip(scrs, seqs):
            if nst == 1:
                store_state(scr, seq)
            else:
                pl.when(step == nst - 1)(functools.partial(store_state, scr, seq))
        return carry

    if nseq == lanes:
        seq_group(0, 0)
    else:
        lax.fori_loop(0, nseq // lanes, seq_group, 0)

    nchunk = (nseq * rps) // pc

    def finish_chunk(ci, carry):
        rows = pl.ds(ci * pc if isinstance(ci, int) else pl.multiple_of(ci * pc, 8), pc)
        o = o_ref[rows, :]
        ms = _seg_dot(o * o, ovv_ref[...], 2) * (1.0 / HEAD_V)
        o_ref[rows, :] = o * lax.rsqrt(ms + EPS) * par_ref[2:3, :] * jax.nn.sigmoid(z_ref[rows, 640:896])
        return carry

    if nchunk == 1:
        finish_chunk(0, 0)
    else:
        lax.fori_loop(0, nchunk, finish_chunk, 0)


def _mlstm_call(z, par, bcol, cw, sel, ovv, cm, nm, mm, cv, o_prev, *, nrows, g):
    if g.full:
        cmain = max(d for d in (MLSTM_CHUNK, 64, 32, 16, 8) if d <= g.rps)
        lead = g.rps % cmain
        chunks = ([(0, lead, lead, 1)] if lead else []) + [(lead, cmain, cmain, g.rps // cmain)]
        assert lead % 8 == 0
    else:
        chunks = [(0, g.rps, g.tv, 1)]
    cmax = max(c for _, c, _, _ in chunks)
    kern = functools.partial(_mlstm_kernel, chunks=tuple(chunks), nseq=g.nseq, rps=g.rps, nst=g.nst, pc=g.pc)
    const = lambda b, s: (0, 0)
    st3 = lambda shp: pl.BlockSpec((g.nseq,) + shp, lambda b, s: (b, 0, 0))
    return pl.pallas_call(
        kern,
        grid=(g.ngrid, g.nst),
        in_specs=[
            pl.BlockSpec((g.rows, SEC), lambda b, s: (g.base_blk + b * g.nst + s, 3)),
            pl.BlockSpec((8, 256), const),
            pl.BlockSpec((8, 1), const),
            pl.BlockSpec((8, 256), const),
            pl.BlockSpec((8, 128), const),
            pl.BlockSpec((256, 256), const),
            st3((128, 64)), st3((1, 128)), st3((1, 128)), st3((8, 256)),
            pl.BlockSpec(memory_space=pl.ANY),
        ],
        input_output_aliases={10: 0},
        out_specs=[
            pl.BlockSpec((g.rows, 256), lambda b, s: (g.base_blk + b * g.nst + s, 0)),
            st3((128, 64)), st3((1, 128)), st3((1, 128)), st3((8, 256)),
        ],
        out_shape=[
            jax.ShapeDtypeStruct((nrows, 256), F32),
            jax.ShapeDtypeStruct((g.nb, 128, 64), F32),
            jax.ShapeDtypeStruct((g.nb, 1, 128), F32),
            jax.ShapeDtypeStruct((g.nb, 1, 128), F32),
            jax.ShapeDtypeStruct((g.nb, 8, 256), F32),
        ],
        scratch_shapes=[
            pltpu.VMEM((8 + cmax, 256), F32),
            pltpu.VMEM((128, 64), F32),
            pltpu.VMEM((1, 128), F32),
            pltpu.VMEM((1, 128), F32),
        ] * g.lanes,
        compiler_params=_cparams("parallel", "arbitrary"),
    )(z, par, bcol, cw, sel, ovv, cm, nm, mm, cv, o_prev)


def _merge_kernel(o0_ref, o1_ref, o2_ref, o3_ref, x_ref, g_ref, wg_ref, wb_ref, wout_ref, out_ref, h_scr, acc_scr):
    n = pl.program_id(1)

    @pl.when(n == 0)
    def _():
        h_scr[...] = _rms_bf16(x_ref[...], g_ref[...])
        acc_scr[...] = jnp.zeros_like(acc_scr)

    gate = jax.nn.sigmoid(_dot(h_scr[...], wg_ref[...]))
    for idx, o_ref in enumerate((o0_ref, o1_ref, o2_ref, o3_ref)):
        @pl.when(n == idx)
        def _(o_ref=o_ref):
            acc_scr[...] += _dot(o_ref[...].astype(BF16), wb_ref[0]) * gate

    @pl.when(n == 3)
    def _():
        out_ref[...] = x_ref[...] + _dot(acc_scr[...].astype(BF16), wout_ref[...])


def _merge_call(outs, x, g, wg, wb, wout, tm, o_tile):
    n = x.shape[0]
    o_spec = pl.BlockSpec((tm, 256), lambda i, j: (o_tile(i), 0))
    return pl.pallas_call(
        _merge_kernel,
        grid=(n // tm, 4),
        in_specs=[
            o_spec, o_spec, o_spec, o_spec,
            pl.BlockSpec((tm, D_MODEL), lambda i, j: (i, 0)),
            pl.BlockSpec((1, D_MODEL), lambda i, j: (0, 0)),
            pl.BlockSpec((D_MODEL, D_MODEL), lambda i, j: (0, j)),
            pl.BlockSpec((1, 256, D_MODEL), lambda i, j: (j, 0, 0)),
            pl.BlockSpec((D_MODEL, D_MODEL), lambda i, j: (0, 0)),
        ],
        out_specs=pl.BlockSpec((tm, D_MODEL), lambda i, j: (i, 0)),
        out_shape=jax.ShapeDtypeStruct((n, D_MODEL), F32),
        scratch_shapes=[pltpu.VMEM((tm, D_MODEL), BF16), pltpu.VMEM((tm, D_MODEL), F32)],
        compiler_params=_cparams("parallel", "arbitrary"),
    )(*outs, x, g, wg, wb, wout)


def _mlp_kernel(x_ref, g_ref, gf_ref, wup_ref, wdn_ref, out_ref, h_scr, acc_scr, *, final):
    j = pl.program_id(1)

    @pl.when(j == 0)
    def _():
        h_scr[...] = _rms_bf16(x_ref[...], g_ref[...])
        acc_scr[...] = jnp.zeros_like(acc_scr)

    u = jnp.maximum(_dot(h_scr[...], wup_ref[...]), 0.0)
    acc_scr[...] += _dot((u * u).astype(BF16), wdn_ref[...])

    @pl.when(j == pl.num_programs(1) - 1)
    def _():
        y = x_ref[...] + acc_scr[...]
        if final:
            y = y * lax.rsqrt(jnp.mean(y * y, axis=-1, keepdims=True) + EPS) * gf_ref[...]
        out_ref[...] = y


def _mlp_call(x, g, gf, wup, wdn, tm, final):
    n = x.shape[0]
    tf = 1024
    return pl.pallas_call(
        functools.partial(_mlp_kernel, final=final),
        grid=(n // tm, D_FF // tf),
        in_specs=[
            pl.BlockSpec((tm, D_MODEL), lambda i, j: (i, 0)),
            pl.BlockSpec((1, D_MODEL), lambda i, j: (0, 0)),
            pl.BlockSpec((1, D_MODEL), lambda i, j: (0, 0)),
            pl.BlockSpec((D_MODEL, tf), lambda i, j: (0, j)),
            pl.BlockSpec((tf, D_MODEL), lambda i, j: (j, 0)),
        ],
        out_specs=pl.BlockSpec((tm, D_MODEL), lambda i, j: (i, 0)),
        out_shape=jax.ShapeDtypeStruct((n, D_MODEL), F32),
        scratch_shapes=[pltpu.VMEM((tm, D_MODEL), BF16), pltpu.VMEM((tm, D_MODEL), F32)],
        compiler_params=_cparams("parallel", "arbitrary"),
    )(x, g, gf, wup, wdn)


def _seg_ones(rows_per_head, cols_per_head):
    r = np.arange(N_HEADS * rows_per_head)[:, None] // rows_per_head
    c = np.arange(N_HEADS * cols_per_head)[None, :] // cols_per_head
    return (r == c).astype(np.float32)


def _rows(*vecs, width=256, nrows=8):
    rows = []
    for v in vecs:
        v = jnp.asarray(v, F32).reshape(-1)
        rows.append(jnp.pad(v, (0, width - v.shape[0])))
    rows += [jnp.zeros((width,), F32)] * (nrows - len(rows))
    return jnp.stack(rows)


def _layout_w_in(w):
    d = w.shape[0]
    zeros = lambda n: jnp.zeros((d, n), w.dtype)
    gl0, ml0, gt0 = 2048, 2832, 3608
    parts = [
        w[:, 0:2048],
        w[:, gl0:gl0 + 512], w[:, gl0 + 512:gl0 + 528], zeros(112), w[:, gl0 + 528:gl0 + 784], zeros(128),
        w[:, ml0:ml0 + 512], w[:, ml0 + 512:ml0 + 520], zeros(120), w[:, ml0 + 520:ml0 + 776], zeros(128),
    ]
    wz = jnp.concatenate(parts, axis=1)
    assert wz.shape[1] == Z_COLS
    return wz.astype(BF16), w[:, gt0:gt0 + 4 * D_MODEL].astype(BF16)


class _Group:
    def __init__(self, nb, trow, tv, row0, max_seq):
        self.nb, self.trow, self.tv, self.row0 = nb, trow, tv, row0
        self.full = tv == trow
        self.c = SUB if self.full else trow
        self.tvs = min(tv, self.c)
        if self.full:
            nblk = trow // self.c
            self.rps = self.c * max(d for d in range(1, min(nblk, 64) + 1) if nblk % d == 0)
        else:
            self.rps = trow
        self.nst = trow // self.rps
        self.nsub = self.rps // self.c
        self.nseq = max(d for d in range(1, max_seq + 1) if nb % d == 0 and row0 % (d * self.rps) == 0)
        self.lanes = 4 if (not self.full and self.nseq % 4 == 0) else 2 if self.nseq % 2 == 0 else 1
        assert self.nseq == self.lanes or self.nst == 1
        self.rows = self.nseq * self.rps
        self.ngrid = nb // self.nseq
        self.base_blk = row0 // self.rows
        self.pc = _pick_tile(self.rps, 384) if self.rps >= 64 else self.rows


def kernel(x_prompt, x_sample, state_hgrn, state_rwkv, state_rwkv_shift, state_gla, state_mlstm_c, state_mlstm_n, state_mlstm_m, state_mlstm_conv, meta_tokens, norm_mix, norm_mlp, norm_final, w_in, hgrn_lb, hgrn_norm, rwkv_mu, rwkv_w0, rwkv_w_up, rwkv_a0, rwkv_a_up, rwkv_g_up, rwkv_k_k, rwkv_k_a, rwkv_r_k, rwkv_ln_w, rwkv_ln_b, gla_gate_up, gla_gate_b, gla_norm, mlstm_conv_w, mlstm_conv_b, mlstm_i_b, mlstm_f_b, mlstm_norm, w_branch, w_out, w_up, w_down):
    depth = w_in.shape[0]
    bp, seq, _ = x_prompt.shape
    bs, dseq, _ = x_sample.shape
    tp = N_META + seq
    assert tp % SUB == 0 and CONV_W - 1 <= dseq <= SAMPLE_PAD
    dt = x_prompt.dtype

    gp = _Group(bp, tp, tp, 0, 2)
    gs = _Group(bs, SAMPLE_PAD, dseq, bp * tp, 16)
    tm = gp.rps
    meta = jnp.broadcast_to(meta_tokens.astype(dt)[None], (bp, N_META, D_MODEL))
    xp = jnp.concatenate([meta, x_prompt], axis=1).reshape(bp * tp, D_MODEL)
    xs = jnp.pad(x_sample, ((0, 0), (0, SAMPLE_PAD - dseq), (0, 0))).reshape(bs * SAMPLE_PAD, D_MODEL)
    n_real = bp * tp + bs * SAMPLE_PAD
    n_rows = -(-n_real // tm) * tm
    x = jnp.concatenate([xp, xs, jnp.zeros((n_rows - n_real, D_MODEL), dt)], axis=0).astype(F32)

    def mixer_tile(i, split=1):
        t, part = i // split, i % split
        b, s = t // gp.nst, t % gp.nst
        p = (b // gp.nseq) * (gp.nst * gp.nseq) + s * gp.nseq + b % gp.nseq
        return jnp.where(t < bp * gp.nst, p, t) * split + part

    in_split = 2 if tm % 16 == 0 and tm >= 256 else 1

    ones_vv = jnp.asarray(_seg_ones(64, 64), BF16)
    ones_gl = jnp.asarray(_seg_ones(GLA_DK, 64), BF16)
    mask_hg = jnp.asarray(_seg_ones(64, 64), F32)
    mask_gl = jnp.asarray(_seg_ones(64, GLA_DK), F32)
    sel8 = jnp.asarray(np.eye(8, 128, dtype=np.float32), BF16)
    hmask = {g.c: jnp.asarray(np.kron(np.eye(2), np.ones((g.c, 64))), F32) for g in (gp, gs)}
    ehead = {c_: jnp.concatenate([m_, m_], axis=0).astype(BF16) for c_, m_ in hmask.items()}
    mask_pair = jnp.asarray(np.kron(np.eye(2), np.ones((64, 64))), F32)

    lb_cs = jnp.cumsum(jax.nn.softmax(hgrn_lb.astype(F32), axis=0), axis=0)
    lb_all = lb_cs - lb_cs[:1]

    def init_states(states, nb, zero):
        s_hg, s_rw, shift, s_gl, c_ml, n_ml, m_ml, conv = states
        if zero:
            z = lambda *shp: jnp.zeros((depth, nb) + shp, F32)
            return dict(hg=z(256, 64), rw=z(256, 64), shift=z(1, SEC), gl=z(256, GLA_DK), cm=z(128, 64),
                        nm=z(1, 128), mm=z(1, 128), cv=z(8, 256))
        f = lambda a: a.astype(F32)
        return dict(
            hg=f(s_hg).transpose(0, 1, 2, 4, 3).reshape(depth, nb, 256, 64),
            rw=f(s_rw).reshape(depth, nb, 256, 64),
            shift=f(shift).reshape(depth, nb, 1, SEC),
            gl=f(s_gl).transpose(0, 1, 2, 4, 3).reshape(depth, nb, 256, GLA_DK),
            cm=f(c_ml).reshape(depth, nb, 128, 64),
            nm=f(n_ml).reshape(depth, nb, 1, 128),
            mm=jnp.pad(f(m_ml), ((0, 0), (0, 0), (0, 124))).reshape(depth, nb, 1, 128),
            cv=jnp.pad(f(conv), ((0, 0), (0, 0), (5, 0), (0, 0))),
        )

    sample_states = (state_hgrn, state_rwkv, state_rwkv_shift, state_gla,
                     state_mlstm_c, state_mlstm_n, state_mlstm_m, state_mlstm_conv)
    st_p = init_states(sample_states, bp, True)
    st_s = init_states(sample_states, bs, False)
    new_p = {k: [] for k in st_p}
    new_s = {k: [] for k in st_s}

    for l in range(depth):
        wz, wg = _layout_w_in(w_in[l])
        g_mix = norm_mix[l].reshape(1, D_MODEL).astype(F32)
        z = _in_proj(x, g_mix, wz, tm // in_split, functools.partial(mixer_tile, split=in_split))

        lb = lb_all[l]
        par_hg = _rows(jnp.log(lb), jnp.log1p(-lb), 1.0 - lb, hgrn_norm[l])
        par_gl = _rows(gla_gate_b[l], jnp.zeros((1,)), jnp.zeros((1,)), gla_norm[l])
        gup_gl = jnp.pad(gla_gate_up[l], ((0, 128 - gla_gate_up.shape[1]), (0, 0))).astype(BF16)
        gup_dummy = jnp.zeros((128, 128), BF16)
        par_rw = _rows(rwkv_w0[l], rwkv_a0[l], rwkv_k_k[l], rwkv_k_a[l], rwkv_r_k[l], rwkv_ln_w[l], rwkv_ln_b[l])
        nw, na = rwkv_w_up.shape[1], rwkv_a_up.shape[1]
        wup_p = jnp.pad(rwkv_w_up[l], ((0, 256 - nw), (0, 0))).astype(BF16)
        aup_p = jnp.pad(rwkv_a_up[l], ((nw, 256 - nw - na), (0, 0))).astype(BF16)
        gup_p = jnp.pad(rwkv_g_up[l], ((nw + na, 0), (0, 0))).astype(BF16)
        mu = rwkv_mu[l].reshape(1, SEC).astype(F32)
        gate_b = jnp.concatenate([mlstm_i_b[l], mlstm_f_b[l]]).astype(F32)
        par_ml = _rows(mlstm_conv_b[l], gate_b, mlstm_norm[l])
        bcol_ml = gate_b.reshape(8, 1)
        cw_ml = _rows(*[mlstm_conv_w[l, j] for j in range(CONV_W)])

        o_hg = o_rw = o_gl = o_ml = None
        for g, st, new in ((gp, st_p, new_p), (gs, st_s, new_s)):
            prev = lambda o: jnp.zeros((n_rows, 256), F32) if o is None else o
            o_hg, s_hg = _gla_call(z, 0, par_hg, gup_dummy, ones_vv, ones_vv, mask_hg, st["hg"][l], prev(o_hg),
                                   mode="hgrn", dk=64, nrows=n_rows, g=g)
            o_gl, s_gl = _gla_call(z, 2, par_gl, gup_gl, ones_gl, ones_vv, mask_gl, st["gl"][l], prev(o_gl),
                                   mode="gla", dk=GLA_DK, nrows=n_rows, g=g)
            o_rw, nshift, s_rw = _rwkv_call(z, mu, par_rw, wup_p, aup_p, gup_p, ones_vv, hmask[g.c], ehead[g.c],
                                            mask_pair, st["shift"][l], st["rw"][l], prev(o_rw), nrows=n_rows, g=g)
            o_ml, s_cm, s_nm, s_mm, s_cv = _mlstm_call(z, par_ml, bcol_ml, cw_ml, sel8, ones_vv, st["cm"][l], st["nm"][l],
                                                        st["mm"][l], st["cv"][l], prev(o_ml), nrows=n_rows, g=g)
            for key, val in (("hg", s_hg), ("rw", s_rw), ("shift", nshift), ("gl", s_gl), ("cm", s_cm),
                             ("nm", s_nm), ("mm", s_mm), ("cv", s_cv)):
                new[key].append(val)

        x = _merge_call((o_hg, o_rw, o_gl, o_ml), x, g_mix, wg, w_branch[l].astype(BF16), w_out[l].astype(BF16), tm,
                        mixer_tile)
        x = _mlp_call(x, norm_mlp[l].reshape(1, D_MODEL).astype(F32), norm_final.reshape(1, D_MODEL).astype(F32),
                      w_up[l].astype(BF16), w_down[l].astype(BF16), tm, final=(l == depth - 1))

    y_prompt = x[:bp * tp].reshape(bp, tp, D_MODEL)[:, N_META:].astype(dt)
    y_sample = x[gs.row0:gs.row0 + bs * SAMPLE_PAD].reshape(bs, SAMPLE_PAD, D_MODEL)[:, :dseq].astype(dt)

    def finish(new, nb):
        st = {k: jnp.stack(v) for k, v in new.items()}
        return (
            st["hg"].reshape(depth, nb, N_HEADS, 64, 64).transpose(0, 1, 2, 4, 3),
            st["rw"].reshape(depth, nb, N_HEADS, 64, 64),
            st["shift"].reshape(depth, nb, SEC),
            st["gl"].reshape(depth, nb, N_HEADS, 64, GLA_DK).transpose(0, 1, 2, 4, 3),
            st["cm"].reshape(depth, nb, N_HEADS, MLSTM_DK, 64),
            st["nm"].reshape(depth, nb, N_HEADS, MLSTM_DK),
            st["mm"].reshape(depth, nb, 128)[:, :, :N_HEADS],
            st["cv"][:, :, 5:8, :],
        )

    outs_p = tuple(a.astype(dt) for a in finish(new_p, bp))
    outs_s = tuple(a.astype(dt) for a in finish(new_s, bs))
    return (y_prompt, y_sample) + outs_p + outs_s
```

```python
import functools

import numpy as np
import jax
import jax.numpy as jnp
from jax import lax
from jax.experimental import pallas as pl
from jax.experimental.pallas import tpu as pltpu

F32 = jnp.float32
BF16 = jnp.bfloat16

D_MODEL = 1024
N_META = 16
N_HEADS = 4
HEAD_V = 64
GLA_DK = 32
MLSTM_DK = 32
GLA_GATE_TAU = 16.0
RWKV_LN_EPS = 64e-5
CONV_W = 4
D_FF = 4 * D_MODEL
EPS = 1e-6

SEC = 1024
Z_COLS = 4 * SEC
SUB = 16
SAMPLE_PAD = 8
MLSTM_CHUNK = 128
VMEM_LIMIT = 56 * 1024 * 1024


def _cparams(*sem):
    return pltpu.CompilerParams(dimension_semantics=sem, vmem_limit_bytes=VMEM_LIMIT)


def _pick_tile(n, max_tile, mult=8):
    best = None
    t = mult
    while t <= min(n, max_tile):
        if n % t == 0:
            best = t
        t += mult
    assert best is not None, (n, max_tile, mult)
    return best


def _split2(x):
    hi = x.astype(BF16)
    lo = (x - hi.astype(F32)).astype(BF16)
    return hi, lo


def _split3(x):
    hi = x.astype(BF16)
    r1 = x - hi.astype(F32)
    mid = r1.astype(BF16)
    lo = (r1 - mid.astype(F32)).astype(BF16)
    return hi, mid, lo


def _dot(a, b):
    return jnp.dot(a, b, preferred_element_type=F32)


def _dot_nt(a, b):
    return lax.dot_general(a, b, (((1,), (1,)), ((), ())), preferred_element_type=F32)


def _dot_tn(a, b):
    return lax.dot_general(a, b, (((0,), (0,)), ((), ())), preferred_element_type=F32)


def _seg_dot(x, m_bf16, parts):
    ps = _split2(x) if parts == 2 else _split3(x)
    acc = _dot(ps[0], m_bf16)
    for p in ps[1:]:
        acc = acc + _dot(p, m_bf16)
    return acc


def _seg_dot_l(m_bf16, x, parts):
    ps = _split2(x) if parts == 2 else _split3(x)
    acc = _dot(m_bf16, ps[0])
    for p in ps[1:]:
        acc = acc + _dot(m_bf16, p)
    return acc


def _log_sigmoid(x):
    return jnp.minimum(x, 0.0) - jnp.log1p(jnp.exp(-jnp.abs(x)))


def _silu(x):
    return x * jax.nn.sigmoid(x)


def _tri(n, upper=False):
    r = lax.broadcasted_iota(jnp.int32, (n, n), 0)
    c = lax.broadcasted_iota(jnp.int32, (n, n), 1)
    return (r <= c) if upper else (r >= c)


def _cumsum_rows(x, rowi):
    d = 1
    while d < x.shape[0]:
        x = x + jnp.where(rowi >= d, pltpu.roll(x, d, axis=0), 0.0)
        d *= 2
    return x


def _rms_bf16(x, g):
    return (x * lax.rsqrt(jnp.mean(x * x, axis=-1, keepdims=True) + EPS) * g).astype(BF16)


def _in_proj_kernel(x_ref, g_ref, w_ref, z_ref):
    h = _rms_bf16(x_ref[...], g_ref[...])
    for j in range(Z_COLS // SEC):
        z_ref[:, j * SEC:(j + 1) * SEC] = _dot(h, w_ref[:, j * SEC:(j + 1) * SEC])


def _in_proj(x, g, w, tm, out_tile):
    n = x.shape[0]
    return pl.pallas_call(
        _in_proj_kernel,
        grid=(n // tm,),
        in_specs=[
            pl.BlockSpec((tm, D_MODEL), lambda i: (i, 0)),
            pl.BlockSpec((1, D_MODEL), lambda i: (0, 0)),
            pl.BlockSpec((D_MODEL, Z_COLS), lambda i: (0, 0)),
        ],
        out_specs=pl.BlockSpec((tm, Z_COLS), lambda i: (out_tile(i), 0)),
        out_shape=jax.ShapeDtypeStruct((n, Z_COLS), F32),
        compiler_params=_cparams("parallel"),
    )(x, g, w)


def _gla_kernel(z_ref, par_ref, gup_ref, okv_ref, ovv_ref, mask_ref, st_in_ref, o_prev_ref,
                o_ref, st_out_ref, *st_scrs, mode, c, nsub, tv, dk, nseq, rps, nst, pc):
    del o_prev_ref
    voff = 512 if mode == "hgrn" else 256
    rowi = lax.broadcasted_iota(jnp.int32, (c, 1), 0)
    valid = rowi < tv

    def sub_block(st_scr, r0):
        rows = pl.ds(r0, c)
        if mode == "hgrn":
            q = z_ref[rows, 0:256]
            hf = z_ref[rows, 256:512]
            g = z_ref[rows, 768:1024]
            a = par_ref[0:1, :]
            cc = par_ref[1:2, :] + _log_sigmoid(hf)
            loga = jnp.maximum(a, cc) + jnp.log1p(jnp.exp(-jnp.abs(a - cc)))
            k = par_ref[2:3, :] * jax.nn.sigmoid(-hf)
        else:
            q = z_ref[rows, 0:128] * (GLA_DK ** -0.5)
            k = z_ref[rows, 128:256]
            ga = z_ref[rows, 512:640]
            g = z_ref[rows, 640:896]
            gl = _dot(ga.astype(BF16), gup_ref[...]) + par_ref[0:1, 0:128]
            loga = _log_sigmoid(gl) * (1.0 / GLA_GATE_TAU)
        v = z_ref[rows, voff:voff + 256]
        if tv < c:
            loga = jnp.where(valid, loga, 0.0)
            k = jnp.where(valid, k, 0.0)
            v = jnp.where(valid, v, 0.0)
        b = _cumsum_rows(loga, rowi)

        o = _dot_nt((q * jnp.exp(b)).astype(BF16), st_scr[...].astype(BF16))

        lo_rows = [8 * (s // 8) for s in range(tv)]
        pieces = []
        for s in range(tv):
            rs = slice(lo_rows[s], c)
            e = jnp.exp(jnp.where(rowi[rs] >= s, b[rs] - b[s:s + 1, :], -jnp.inf))
            pieces.append(q[rs] * e * k[s:s + 1, :])
        return rows, lo_rows, k, v, b, o, jnp.concatenate(pieces, axis=0)

    def sub_block_finish(st_scr, part, sm):
        rows, lo_rows, k, v, b, o, _ = part
        acc = {}
        off = 0
        for s in range(tv):
            n = c - lo_rows[s]
            contrib = sm[off:off + n] * v[s:s + 1, :]
            off += n
            acc[lo_rows[s]] = contrib if lo_rows[s] not in acc else acc[lo_rows[s]] + contrib
        for lo, a_ in acc.items():
            o = o + (a_ if lo == 0 else jnp.concatenate([jnp.zeros((lo, 256), F32), a_], axis=0))

        bl = b[c - 1:c, :]
        kdec = k * jnp.exp(bl - b)
        upd = _dot_tn(v.astype(BF16), kdec.astype(BF16))
        st_scr[...] = st_scr[...] * jnp.exp(bl) + upd * mask_ref[...]
        o_ref[rows, :] = o

    def sub_blocks(scrs, row0s):
        parts = [sub_block(scr, r0) for scr, r0 in zip(scrs, row0s)]
        sm = _dot(jnp.concatenate([p[-1] for p in parts], axis=0).astype(BF16), okv_ref[...])
        n = parts[0][-1].shape[0]
        for i, (scr, part) in enumerate(zip(scrs, parts)):
            sub_block_finish(scr, part, sm[i * n:(i + 1) * n])

    def load_state(st_scr, seq):
        st_scr[...] = jnp.concatenate([st_in_ref[seq]] * N_HEADS, axis=1) * mask_ref[...]

    def store_state(st_scr, seq):
        st = st_scr[...]
        acc = st[:, 0:dk]
        for h in range(1, N_HEADS):
            acc = acc + st[:, h * dk:(h + 1) * dk]
        st_out_ref[seq] = acc

    _run_lanes(st_scrs, nseq, rps, nsub, c, nst, load_state, store_state, sub_blocks)

    goff = 768 if mode == "hgrn" else 640
    nchunk = (nseq * rps) // pc

    def finish_chunk(ci, carry):
        rows = pl.ds(ci * pc if isinstance(ci, int) else pl.multiple_of(ci * pc, 8), pc)
        o = o_ref[rows, :]
        ms = _seg_dot(o * o, ovv_ref[...], 2) * (1.0 / HEAD_V)
        o_ref[rows, :] = o * lax.rsqrt(ms + EPS) * par_ref[3:4, :] * _silu(z_ref[rows, goff:goff + 256])
        return carry

    if nchunk == 1:
        finish_chunk(0, 0)
    else:
        lax.fori_loop(0, nchunk, finish_chunk, 0)


def _run_lanes(scrs, nseq, rps, nsub, c, nst, load_state, store_state, sub_blocks, group=0):
    lanes = len(scrs)
    assert nseq == lanes or nst == 1
    step = pl.program_id(1)

    def seq_group(sg, carry):
        seqs = [sg * lanes + u for u in range(lanes)]
        for scr, seq in zip(scrs, seqs):
            if nst == 1:
                load_state(scr, seq)
            else:
                pl.when(step == 0)(functools.partial(load_state, scr, seq))

        def starts(j):
            return [seq * rps + j * c if isinstance(seq * rps + j * c, int)
                    else pl.multiple_of(seq * rps + j * c, 8) for seq in seqs]

        def blocks(j, carry2):
            if group:
                sub_blocks(scrs, [starts(j * group + g_) for g_ in range(group)])
            else:
                sub_blocks(scrs, starts(j))
            return carry2

        if group:
            if nsub // group == 1:
                blocks(0, 0)
            elif nsub // group > 1:
                lax.fori_loop(0, nsub // group, blocks, 0, unroll=2 if group == 1 else 1)
            for j in range(nsub - nsub % group, nsub):
                sub_blocks(scrs, [starts(j)])
        elif nsub == 1:
            blocks(0, 0)
        else:
            lax.fori_loop(0, nsub, blocks, 0, unroll=2)
        for scr, seq in zip(scrs, seqs):
            if nst == 1:
                store_state(scr, seq)
            else:
                pl.when(step == nst - 1)(functools.partial(store_state, scr, seq))
        return carry

    if nseq == lanes:
        seq_group(0, 0)
    else:
        lax.fori_loop(0, nseq // lanes, seq_group, 0)


def _gla_call(z, sec, par, gup, okv, ovv, mask, st_in, o_prev, *, mode, dk, nrows, g):
    hk = N_HEADS * dk
    kern = functools.partial(_gla_kernel, mode=mode, c=g.c, nsub=g.nsub, tv=g.tvs, dk=dk,
                             nseq=g.nseq, rps=g.rps, nst=g.nst, pc=g.pc)
    const = lambda b, s: (0, 0)
    return pl.pallas_call(
        kern,
        grid=(g.ngrid, g.nst),
        in_specs=[
            pl.BlockSpec((g.rows, SEC), lambda b, s: (g.base_blk + b * g.nst + s, sec)),
            pl.BlockSpec((8, 256), const),
            pl.BlockSpec((128, 128), const),
            pl.BlockSpec((hk, 256), const),
            pl.BlockSpec((256, 256), const),
            pl.BlockSpec((256, hk), const),
            pl.BlockSpec((g.nseq, 256, dk), lambda b, s: (b, 0, 0)),
            pl.BlockSpec(memory_space=pl.ANY),
        ],
        input_output_aliases={7: 0},
        out_specs=[
            pl.BlockSpec((g.rows, 256), lambda b, s: (g.base_blk + b * g.nst + s, 0)),
            pl.BlockSpec((g.nseq, 256, dk), lambda b, s: (b, 0, 0)),
        ],
        out_shape=[
            jax.ShapeDtypeStruct((nrows, 256), F32),
            jax.ShapeDtypeStruct((g.nb, 256, dk), F32),
        ],
        scratch_shapes=[pltpu.VMEM((256, hk), F32)] * g.lanes,
        compiler_params=_cparams("parallel", "arbitrary"),
    )(z, par, gup, okv, ovv, mask, st_in, o_prev)


def _rwkv_kernel(z_ref, mu_ref, par_ref, wup_ref, aup_ref, gup_ref, ovv_ref, hmask_ref, ehead_ref, mask_ref,
                 shift_ref, st_in_ref, o_prev_ref,
                 o_ref, nshift_ref, st_out_ref,
                 prev_scr, fix_scr, r_scr, lw_scr, k_scr, v_scr, kk_scr, ka_scr, g_scr, bonus_scr, y_scr, *st_scrs,
                 c, nsub, tv, nseq, rps, nst, pc):
    del o_prev_ref
    step = pl.program_id(1)

    def shifted_chunk(ci):
        r0 = ci * pc if isinstance(ci, int) else pl.multiple_of(ci * pc, 8)
        zr = z_ref[pl.ds(r0, pc), :]
        rolled = pltpu.roll(zr, 1, axis=0)
        if pc <= rps:
            cps = rps // pc
            seq = ci // cps
            rowi = lax.broadcasted_iota(jnp.int32, (pc, 1), 0)
            before = z_ref[pl.ds(pl.multiple_of(jnp.maximum(r0 - 8, 0), 8), 8), :][7:8, :]
            first = jnp.where(ci % cps == 0, prev_scr[pl.ds(seq, 1), :], before)
            prev = jnp.where(rowi == 0, first, rolled)
            prev_scr[pl.ds(seq, 1), :] = zr[pc - 1:pc, :]
            nshift_ref[seq] = zr[pc - 1:pc, :]
        else:
            fix_scr[...] = rolled
            for i in range(nseq):
                fix_scr[i * rps:i * rps + 1, :] = shift_ref[i]
                nshift_ref[i] = zr[i * rps + tv - 1:i * rps + tv, :]
            prev = fix_scr[...]
        return r0, zr, prev

    if pc <= rps:
        @pl.when(step == 0)
        def _():
            for i in range(nseq):
                prev_scr[i:i + 1, :] = shift_ref[i]

    def prep_chunk(ci, carry):
        r0, zr, prev = shifted_chunk(ci)
        rows = pl.ds(r0, pc)
        _rwkv_prep_rows(zr, prev, rows)
        return carry

    def _rwkv_prep_rows(zr, prev, rows):
        nrow = zr.shape[0]
        zs = zr + (prev - zr) * mu_ref[...]
        r = zs[:, 0:256]
        k = zs[:, 256:512]
        v = zs[:, 512:768]
        lo = zs[:, 768:1024]
        w0, a0, k_k, k_a, r_k = (par_ref[i:i + 1, :] for i in range(5))
        wl = w0 + _dot(jnp.tanh(lo).astype(BF16), wup_ref[...])
        wexp = -(jnp.maximum(-wl, 0.0) + jnp.log1p(jnp.exp(-jnp.abs(wl)))) - 0.5
        lw = -jnp.exp(wexp)
        a = jax.nn.sigmoid(a0 + _dot(lo.astype(BF16), aup_ref[...]))
        g_scr[rows, :] = _dot(jax.nn.sigmoid(lo).astype(BF16), gup_ref[...])
        kkp = k * k_k
        nrm = jnp.sqrt(_seg_dot(kkp * kkp, ovv_ref[...], 3))
        kk = kkp / jnp.maximum(nrm, 1e-12)
        k2 = k * (1.0 + (a - 1.0) * k_a)
        bonus_scr[rows, :] = _seg_dot(r * k2 * r_k, ovv_ref[...], 3) * v
        ka = kk * a
        if tv < rps:
            okrow = lax.rem(lax.broadcasted_iota(jnp.int32, (nrow, 1), 0), rps) < tv
            lw, k2, v, kk, ka = (jnp.where(okrow, t_, 0.0) for t_ in (lw, k2, v, kk, ka))
        r_scr[rows, :] = r
        lw_scr[rows, :] = lw
        k_scr[rows, :] = k2
        v_scr[rows, :] = v
        kk_scr[rows, :] = kk
        ka_scr[rows, :] = ka

    nchunk = (nseq * rps) // pc
    if nchunk == 1:
        prep_chunk(0, 0)
    else:
        lax.fori_loop(0, nchunk, prep_chunk, 0)

    hp = 2
    ng = N_HEADS // hp
    gw = hp * 64
    rowc = lax.broadcasted_iota(jnp.int32, (c, 1), 0)
    src2 = lax.broadcasted_iota(jnp.int32, (1, 2 * hp * c), 1) & (c - 1)
    src = src2[:, 0:hp * c]
    strict = rowc > src
    incl = rowc >= src

    def sub_blocks(lane_scrs, row0_lists):
        staged = [stage_free(lane_scrs, row0s) for row0s in row0_lists]
        for st_ in staged:
            stage_state(*st_)

    def stage_free(lane_scrs, row0s):
        hm = hmask_ref[...]
        tile = lambda t_: jnp.concatenate([t_] * hp, axis=0) * hm
        nc = hp * c
        nsolve = min(tv, c - 1)
        units, cx = [], {}
        for i, r0 in enumerate(row0s):
            rows = pl.ds(r0, c)
            lw_, r_, k_, v_, kk_, ka_ = (s_[rows, :] for s_ in (lw_scr, r_scr, k_scr, v_scr, kk_scr, ka_scr))
            gam = _cumsum_rows(lw_, rowc)
            gl = gam[c - 1:c, :]
            ginv = jnp.exp(-gam)
            gend = jnp.exp(gl - gam)
            lh, ll = _split2(jnp.concatenate([kk_ * jnp.exp(gam - lw_), r_ * jnp.exp(gam)], axis=0))
            a_, kq_ = ka_ * ginv, k_ * ginv
            ae_, ke_ = ka_ * gend, k_ * gend
            for p in range(ng):
                ls = slice(p * gw, (p + 1) * gw)
                rh, rl = _split2(jnp.concatenate([tile(a_[:, ls]), tile(kq_[:, ls])], axis=0))
                units.append((i, p))
                cx[i, p] = dict(rows=rows, ls=ls, v=v_[:, ls], egl=jnp.exp(gl[:, ls]), lh=lh[:, ls], ll=ll[:, ls],
                                rh=rh, rl=rl, vbig=tile(v_[:, ls]).astype(BF16), scr=lane_scrs[i][p],
                                end=jnp.concatenate([ae_[:, ls], ke_[:, ls]], axis=0).astype(BF16))
        g2 = {u_: _dot_nt(jnp.concatenate([cx[u_]["lh"], cx[u_]["ll"]], axis=0), cx[u_]["rh"]) for u_ in units}
        g3 = {u_: _dot_nt(cx[u_]["lh"], cx[u_]["rl"]) for u_ in units}
        m_ab, m_ra, m_rk, bkv = {}, {}, {}, {}
        for u_ in units:
            gm = g2[u_][0:2 * c] + g2[u_][2 * c:4 * c] + g3[u_]
            m_ab[u_] = jnp.where(strict, gm[0:c, 0:nc], 0.0)
            m_ra[u_] = jnp.where(incl, gm[c:2 * c, 0:nc], 0.0)
            m_rk[u_] = jnp.where(incl, gm[c:2 * c, nc:2 * nc], 0.0)
            bkv[u_] = _dot(jnp.where(strict, gm[0:c, nc:2 * nc], 0.0).astype(BF16), cx[u_]["vbig"])
        m2 = {u_: jnp.concatenate(_split2(m_ab[u_]), axis=1) for u_ in units}
        cols = _dot(jnp.concatenate([jnp.where(src2 == s, m2[u_], jnp.zeros_like(m2[u_]))
                                     for u_ in units for s in range(nsolve)], axis=0), ehead_ref[...])
        return units, cx, m_ra, m_rk, bkv, cols

    def stage_state(units, cx, m_ra, m_rk, bkv, cols):
        nsolve = min(tv, c - 1)
        tile = lambda t_: jnp.concatenate([t_] * hp, axis=0) * hmask_ref[...]
        x0 = {u_: _dot_nt(cx[u_]["lh"], cx[u_]["scr"][...].astype(BF16)) for u_ in units}
        usol = {u_: x0[u_][0:c] + bkv[u_] for u_ in units}
        for s in range(nsolve):
            for n_, u_ in enumerate(units):
                o_ = (n_ * nsolve + s) * c
                usol[u_] = usol[u_] - cols[o_:o_ + c] * usol[u_][s:s + 1, :]
        ys = {u_: _dot(jnp.concatenate([-m_ra[u_], m_rk[u_]], axis=1).astype(BF16),
                       jnp.concatenate([tile(usol[u_]).astype(BF16), cx[u_]["vbig"]], axis=0)) for u_ in units}
        upd = {u_: _dot_tn(jnp.concatenate([-usol[u_], cx[u_]["v"]], axis=0).astype(BF16), cx[u_]["end"])
               for u_ in units}
        for u_ in units:
            x = cx[u_]
            y_scr[x["rows"], x["ls"]] = x0[u_][c:2 * c] + ys[u_]
            x["scr"][...] = x["scr"][...] * x["egl"] + upd[u_] * mask_ref[...]

    def load_state(scrs, seq):
        for p, scr in enumerate(scrs):
            scr[...] = jnp.concatenate([st_in_ref[seq, p * gw:(p + 1) * gw, :]] * hp, axis=1) * mask_ref[...]

    def store_state(scrs, seq):
        for p, scr in enumerate(scrs):
            st = scr[...]
            st_out_ref[seq, p * gw:(p + 1) * gw, :] = st[:, 0:64] + st[:, 64:128]

    lane_scrs = [st_scrs[i * ng:(i + 1) * ng] for i in range(len(st_scrs) // ng)]
    _run_lanes(lane_scrs, nseq, rps, nsub, c, nst, load_state, store_state, sub_blocks, group=1)

    def finish_chunk(ci, carry):
        rows = pl.ds(ci * pc if isinstance(ci, int) else pl.multiple_of(ci * pc, 8), pc)
        y = y_scr[rows, :]
        mean = _seg_dot(y, ovv_ref[...], 3) * (1.0 / HEAD_V)
        d = y - mean
        var = _seg_dot(d * d, ovv_ref[...], 3) * (1.0 / HEAD_V)
        ln = d * lax.rsqrt(var + RWKV_LN_EPS) * par_ref[5:6, :] + par_ref[6:7, :]
        o_ref[rows, :] = (ln + bonus_scr[rows, :]) * g_scr[rows, :]
        return carry

    if nchunk == 1:
        finish_chunk(0, 0)
    else:
        lax.fori_loop(0, nchunk, finish_chunk, 0)


def _rwkv_call(z, mu, par, wup, aup, gup, ovv, hmask, ehead, mask, shift, st_in, o_prev, *, nrows, g):
    rblk = g.rows
    nsteps = g.nst
    base = g.base_blk
    pc = g.pc
    kern = functools.partial(_rwkv_kernel, c=g.c, nsub=g.nsub, tv=g.tv, nseq=g.nseq, rps=g.rps, nst=g.nst, pc=pc)
    const = lambda b, s: (0, 0)
    return pl.pallas_call(
        kern,
        grid=(g.ngrid, nsteps),
        in_specs=[
            pl.BlockSpec((rblk, SEC), lambda b, s: (base + b * nsteps + s, 1),
                         pipeline_mode=pl.Buffered(1) if g.lanes > 2 and g.full else None),
            pl.BlockSpec((1, SEC), const),
            pl.BlockSpec((8, 256), const),
            pl.BlockSpec((256, 256), const),
            pl.BlockSpec((256, 256), const),
            pl.BlockSpec((256, 256), const),
            pl.BlockSpec((256, 256), const),
            pl.BlockSpec((2 * g.c, 128), const),
            pl.BlockSpec((4 * g.c, 128), const),
            pl.BlockSpec((128, 128), const),
            pl.BlockSpec((g.nseq, 1, SEC), lambda b, s: (b, 0, 0)),
            pl.BlockSpec((g.nseq, 256, 64), lambda b, s: (b, 0, 0)),
            pl.BlockSpec(memory_space=pl.ANY),
        ],
        input_output_aliases={12: 0},
        out_specs=[
            pl.BlockSpec((rblk, 256), lambda b, s: (base + b * nsteps + s, 0)),
            pl.BlockSpec((g.nseq, 1, SEC), lambda b, s: (b, 0, 0)),
            pl.BlockSpec((g.nseq, 256, 64), lambda b, s: (b, 0, 0)),
        ],
        out_shape=[
            jax.ShapeDtypeStruct((nrows, 256), F32),
            jax.ShapeDtypeStruct((g.nb, 1, SEC), F32),
            jax.ShapeDtypeStruct((g.nb, 256, 64), F32),
        ],
        scratch_shapes=[pltpu.VMEM((max(8, g.nseq), SEC), F32), pltpu.VMEM((pc if pc > g.rps else 8, SEC), F32)]
        + [pltpu.VMEM((rblk, 256), F32)] * 9 + [pltpu.VMEM((128, 128), F32)] * (2 * g.lanes),
        compiler_params=_cparams("parallel", "arbitrary"),
    )(z, mu, par, wup, aup, gup, ovv, hmask, ehead, mask, shift, st_in, o_prev)


def _mlstm_kernel(z_ref, par_ref, bcol_ref, cw_ref, sel_ref, ovv_ref, cm_in, nm_in, mm_in, cv_in, o_prev_ref,
                  o_ref, cm_out, nm_out, mm_out, cv_out,
                  *scratch, chunks, nseq, rps, nst, pc):
    del o_prev_ref
    lanes = scratch[0].shape[0]

    def chunk_gates(scr, row0, c, tv):
        xbuf, cm_scr, nm_scr, mm_scr = scr
        rows = pl.ds(row0, c)
        mqk = z_ref[rows, 0:256]
        mv = z_ref[rows, 256:512]
        gi = z_ref[rows, 512:640]
        mo = z_ref[rows, 640:896]
        xbuf[8:8 + c, :] = mqk
        conv = par_ref[0:1, :] + xbuf[5:5 + c, :] * cw_ref[0:1, :]
        for j in range(1, CONV_W):
            conv = conv + xbuf[5 + j:5 + j + c, :] * cw_ref[j:j + 1, :]
        xbuf[5:8, :] = xbuf[8 + tv - 3:8 + tv, :]
        act = _silu(conv)
        q = act[:, 0:128]
        k = act[:, 128:256] * (MLSTM_DK ** -0.5)

        rowi = lax.broadcasted_iota(jnp.int32, (c, 1), 0)
        coli = lax.broadcasted_iota(jnp.int32, (1, c), 1)
        gcol = gi + par_ref[1:2, 0:128]
        lfc = _log_sigmoid(gcol)
        graw = None
        for part in _split3(gi):
            t_ = _dot_nt(sel_ref[...], part)
            graw = t_ if graw is None else graw + t_
        grow = graw + bcol_ref[...]
        lfr = _log_sigmoid(grow)
        if tv < c:
            gcol = jnp.where(rowi < tv, gcol, -jnp.inf)
            lfc = jnp.where(rowi < tv, lfc, 0.0)
            grow = jnp.where(coli < tv, grow, -jnp.inf)
            lfr = jnp.where(coli < tv, lfr, 0.0)
        b_col = _seg_dot_l(_tri(c).astype(BF16), lfc, 3)
        b_row = _seg_dot(lfr, _tri(c, upper=True).astype(BF16), 3)
        return dict(rows=rows, q=q, k=k, mv=mv, mo=mo, gcol=gcol, grow=grow, b_col=b_col, b_row=b_row,
                    mm=mm_scr[...], nm=nm_scr[...], cm=cm_scr[...])

    def chunk(scrs, row0s, c, tv):
        ctx = [chunk_gates(scr, row0, c, tv) for scr, row0 in zip(scrs, row0s)]
        causal = _tri(c)
        lane = lax.broadcasted_iota(jnp.int32, (1, 128), 1)
        items = [(l, h) for l in range(len(ctx)) for h in range(N_HEADS)]
        ks = lambda h: slice(h * MLSTM_DK, (h + 1) * MLSTM_DK)
        vs = lambda h: slice(h * HEAD_V, (h + 1) * HEAD_V)
        qh = {(l, h): ctx[l]["q"][:, ks(h)] for l, h in items}
        kh = {(l, h): ctx[l]["k"][:, ks(h)] for l, h in items}
        vh = {(l, h): ctx[l]["mv"][:, vs(h)].astype(BF16) for l, h in items}
        cmh = {(l, h): ctx[l]["cm"][ks(h), :] for l, h in items}
        nh = {(l, h): ctx[l]["nm"][0:1, ks(h)] for l, h in items}
        qk_raw = {it: _dot_nt(qh[it].astype(BF16), kh[it].astype(BF16)) for it in items}
        q_cm = {it: _dot(qh[it].astype(BF16), cmh[it].astype(BF16)) for it in items}
        bc, ic, mprev, m_t, sc, qk = {}, {}, {}, {}, {}, {}
        for it in items:
            l, h = it
            bc[it] = ctx[l]["b_col"][:, 4 + h:5 + h]
            br = ctx[l]["b_row"][4 + h:5 + h, :]
            ir = ctx[l]["grow"][h:h + 1, :]
            ic[it] = ctx[l]["gcol"][:, h:h + 1]
            dmat = jnp.where(causal, bc[it] - br + ir, -jnp.inf)
            mprev[it] = ctx[l]["mm"][0:1, h:h + 1]
            inter = bc[it] + mprev[it]
            m_t[it] = jnp.maximum(inter, jnp.max(dmat, axis=1, keepdims=True))
            sc[it] = jnp.exp(inter - m_t[it])
            qk[it] = qk_raw[it] * jnp.exp(dmat - m_t[it])
        num = {it: _dot(qk[it].astype(BF16), vh[it]) + sc[it] * q_cm[it] for it in items}
        qk_sum = _seg_dot(jnp.concatenate([qk[it] for it in items], axis=0), jnp.ones((c, HEAD_V), BF16), 2)
        qn_sum = _seg_dot(jnp.concatenate([qh[it] * nh[it] for it in items], axis=0),
                          jnp.ones((MLSTM_DK, HEAD_V), BF16), 2)
        kw, sl, m_new, hn = {}, {}, {}, {}
        for n_, it in enumerate(items):
            den = qk_sum[n_ * c:(n_ + 1) * c] + sc[it] * qn_sum[n_ * c:(n_ + 1) * c]
            hc = num[it] / jnp.maximum(jnp.abs(den), jnp.exp(-m_t[it]))
            m_new[it] = m_t[it][tv - 1:tv, :]
            bl = bc[it][tv - 1:tv, :]
            wl = jnp.exp(bl - bc[it] + ic[it] - m_new[it])
            sl[it] = jnp.exp(bl + mprev[it] - m_new[it])
            kw[it] = kh[it] * wl
            hn[it] = hc
        kv = {it: _dot_tn(kw[it].astype(BF16), vh[it]) for it in items}
        for l, (scr, cx) in enumerate(zip(scrs, ctx)):
            xbuf, cm_scr, nm_scr, mm_scr = scr
            heads = [(l, h) for h in range(N_HEADS)]
            cm_scr[...] = jnp.concatenate([sl[it] * cmh[it] + kv[it] for it in heads], axis=0)
            nm_scr[...] = jnp.concatenate([sl[it] * nh[it] + jnp.sum(kw[it], axis=0, keepdims=True)
                                           for it in heads], axis=1)
            mm_new = cx["mm"]
            for it in heads:
                mm_new = jnp.where(lane == it[1], m_new[it], mm_new)
            mm_scr[...] = mm_new
            o_ref[cx["rows"], :] = jnp.concatenate([hn[it] for it in heads], axis=1)

    def load_state(scr, seq):
        xbuf, cm_scr, nm_scr, mm_scr = scr
        cm_scr[...] = cm_in[seq]
        nm_scr[...] = nm_in[seq]
        mm_scr[...] = mm_in[seq]
        xbuf[0:8, :] = cv_in[seq]

    def store_state(scr, seq):
        xbuf, cm_scr, nm_scr, mm_scr = scr
        cm_out[seq] = cm_scr[...]
        nm_out[seq] = nm_scr[...]
        mm_out[seq] = mm_scr[...]
        cv_out[seq] = xbuf[0:8, :]

    assert nseq == lanes or nst == 1
    step = pl.program_id(1)

    def lane_scr(u):
        return tuple(s_.at[u] for s_ in scratch)

    def run_chunks(grp):
        for start, c, tv, count in chunks:
            def body(ci, carry2, start=start, c=c, tv=tv):
                firsts = [seq * rps + start + ci * c for _, seq in grp]
                chunk([scr for scr, _ in grp],
                      [f if isinstance(f, int) else pl.multiple_of(f, 8) for f in firsts], c, tv)
                return carry2

            if count == 1:
                body(0, 0)
            else:
                lax.fori_loop(0, count, body, 0)

    def seq_group(sg, carry):
        def lane(u, seq):
            scr = lane_scr(u)
            if nst == 1:
                load_state(scr, seq)
            else:
                pl.when(step == 0)(functools.partial(load_state, scr, seq))
            return scr

        def finish(scr, seq):
            if nst == 1:
                store_state(scr, seq)
            else:
                pl.when(step == nst - 1)(functools.partial(store_state, scr, seq))

        if max(c for _, c, _, _ in chunks) <= 16:
            grp = [(lane(u, sg * lanes + u), sg * lanes + u) for u in range(lanes)]
            run_chunks(grp)
            for scr, seq in grp:
                finish(scr, seq)
        else:
            def one_lane(u, carry2):
                seq = sg * lanes + u
                scr = lane(u, seq)
                run_chunks([(scr, seq)])
                finish(scr, seq)
                return carry2

            lax.fori_loop(0, lanes, one_lane, 0)
        return carry

    if nseq == lanes:
        seq_group(0, 0)
    else:
        lax.fori_loop(0, nseq // lanes, seq_group, 0)

    nchunk = (nseq * rps) // pc

    def finish_chunk(ci, carry):
        rows = pl.ds(ci * pc if isinstance(ci, int) else pl.multiple_of(ci * pc, 8), pc)
        o = o_ref[rows, :]
        ms = _seg_dot(o * o, ovv_ref[...], 2) * (1.0 / HEAD_V)
        o_ref[rows, :] = o * lax.rsqrt(ms + EPS) * par_ref[2:3, :] * jax.nn.sigmoid(z_ref[rows, 640:896])
        return carry

    if nchunk == 1:
        finish_chunk(0, 0)
    else:
        lax.fori_loop(0, nchunk, finish_chunk, 0)


def _mlstm_call(z, par, bcol, cw, sel, ovv, cm, nm, mm, cv, o_prev, *, nrows, g):
    if g.full:
        cmain = max(d for d in (MLSTM_CHUNK, 64, 32, 16, 8) if d <= g.rps)
        lead = g.rps % cmain
        chunks = ([(0, lead, lead, 1)] if lead else []) + [(lead, cmain, cmain, g.rps // cmain)]
        assert lead % 8 == 0
    else:
        chunks = [(0, g.rps, g.tv, 1)]
    cmax = max(c for _, c, _, _ in chunks)
    kern = functools.partial(_mlstm_kernel, chunks=tuple(chunks), nseq=g.nseq, rps=g.rps, nst=g.nst, pc=g.pc)
    const = lambda b, s: (0, 0)
    st3 = lambda shp: pl.BlockSpec((g.nseq,) + shp, lambda b, s: (b, 0, 0))
    return pl.pallas_call(
        kern,
        grid=(g.ngrid, g.nst),
        in_specs=[
            pl.BlockSpec((g.rows, SEC), lambda b, s: (g.base_blk + b * g.nst + s, 3)),
            pl.BlockSpec((8, 256), const),
            pl.BlockSpec((8, 1), const),
            pl.BlockSpec((8, 256), const),
            pl.BlockSpec((8, 128), const),
            pl.BlockSpec((256, 256), const),
            st3((128, 64)), st3((1, 128)), st3((1, 128)), st3((8, 256)),
            pl.BlockSpec(memory_space=pl.ANY),
        ],
        input_output_aliases={10: 0},
        out_specs=[
            pl.BlockSpec((g.rows, 256), lambda b, s: (g.base_blk + b * g.nst + s, 0)),
            st3((128, 64)), st3((1, 128)), st3((1, 128)), st3((8, 256)),
        ],
        out_shape=[
            jax.ShapeDtypeStruct((nrows, 256), F32),
            jax.ShapeDtypeStruct((g.nb, 128, 64), F32),
            jax.ShapeDtypeStruct((g.nb, 1, 128), F32),
            jax.ShapeDtypeStruct((g.nb, 1, 128), F32),
            jax.ShapeDtypeStruct((g.nb, 8, 256), F32),
        ],
        scratch_shapes=[
            pltpu.VMEM((g.lanes, 8 + cmax, 256), F32),
            pltpu.VMEM((g.lanes, 128, 64), F32),
            pltpu.VMEM((g.lanes, 1, 128), F32),
            pltpu.VMEM((g.lanes, 1, 128), F32),
        ],
        compiler_params=_cparams("parallel", "arbitrary"),
    )(z, par, bcol, cw, sel, ovv, cm, nm, mm, cv, o_prev)


def _merge_kernel(o0_ref, o1_ref, o2_ref, o3_ref, x_ref, g_ref, wg_ref, wb_ref, wout_ref, out_ref, h_scr, acc_scr):
    n = pl.program_id(1)

    @pl.when(n == 0)
    def _():
        h_scr[...] = _rms_bf16(x_ref[...], g_ref[...])
        acc_scr[...] = jnp.zeros_like(acc_scr)

    gate = jax.nn.sigmoid(_dot(h_scr[...], wg_ref[...]))
    for idx, o_ref in enumerate((o0_ref, o1_ref, o2_ref, o3_ref)):
        @pl.when(n == idx)
        def _(o_ref=o_ref):
            acc_scr[...] += _dot(o_ref[...].astype(BF16), wb_ref[0]) * gate

    @pl.when(n == 3)
    def _():
        out_ref[...] = x_ref[...] + _dot(acc_scr[...].astype(BF16), wout_ref[...])


def _merge_call(outs, x, g, wg, wb, wout, tm, o_tile):
    n = x.shape[0]
    o_spec = pl.BlockSpec((tm, 256), lambda i, j: (o_tile(i), 0))
    return pl.pallas_call(
        _merge_kernel,
        grid=(n // tm, 4),
        in_specs=[
            o_spec, o_spec, o_spec, o_spec,
            pl.BlockSpec((tm, D_MODEL), lambda i, j: (i, 0)),
            pl.BlockSpec((1, D_MODEL), lambda i, j: (0, 0)),
            pl.BlockSpec((D_MODEL, D_MODEL), lambda i, j: (0, j)),
            pl.BlockSpec((1, 256, D_MODEL), lambda i, j: (j, 0, 0)),
            pl.BlockSpec((D_MODEL, D_MODEL), lambda i, j: (0, 0)),
        ],
        out_specs=pl.BlockSpec((tm, D_MODEL), lambda i, j: (i, 0)),
        out_shape=jax.ShapeDtypeStruct((n, D_MODEL), F32),
        scratch_shapes=[pltpu.VMEM((tm, D_MODEL), BF16), pltpu.VMEM((tm, D_MODEL), F32)],
        compiler_params=_cparams("parallel", "arbitrary"),
    )(*outs, x, g, wg, wb, wout)


def _mlp_kernel(x_ref, g_ref, gf_ref, wup_ref, wdn_ref, out_ref, h_scr, acc_scr, *, final):
    j = pl.program_id(1)

    @pl.when(j == 0)
    def _():
        h_scr[...] = _rms_bf16(x_ref[...], g_ref[...])
        acc_scr[...] = jnp.zeros_like(acc_scr)

    u = jnp.maximum(_dot(h_scr[...], wup_ref[...]), 0.0)
    acc_scr[...] += _dot((u * u).astype(BF16), wdn_ref[...])

    @pl.when(j == pl.num_programs(1) - 1)
    def _():
        y = x_ref[...] + acc_scr[...]
        if final:
            y = y * lax.rsqrt(jnp.mean(y * y, axis=-1, keepdims=True) + EPS) * gf_ref[...]
        out_ref[...] = y


def _mlp_call(x, g, gf, wup, wdn, tm, final):
    n = x.shape[0]
    tf = 1024
    return pl.pallas_call(
        functools.partial(_mlp_kernel, final=final),
        grid=(n // tm, D_FF // tf),
        in_specs=[
            pl.BlockSpec((tm, D_MODEL), lambda i, j: (i, 0)),
            pl.BlockSpec((1, D_MODEL), lambda i, j: (0, 0)),
            pl.BlockSpec((1, D_MODEL), lambda i, j: (0, 0)),
            pl.BlockSpec((D_MODEL, tf), lambda i, j: (0, j)),
            pl.BlockSpec((tf, D_MODEL), lambda i, j: (j, 0)),
        ],
        out_specs=pl.BlockSpec((tm, D_MODEL), lambda i, j: (i, 0)),
        out_shape=jax.ShapeDtypeStruct((n, D_MODEL), F32),
        scratch_shapes=[pltpu.VMEM((tm, D_MODEL), BF16), pltpu.VMEM((tm, D_MODEL), F32)],
        compiler_params=_cparams("parallel", "arbitrary"),
    )(x, g, gf, wup, wdn)


def _seg_ones(rows_per_head, cols_per_head):
    r = np.arange(N_HEADS * rows_per_head)[:, None] // rows_per_head
    c = np.arange(N_HEADS * cols_per_head)[None, :] // cols_per_head
    return (r == c).astype(np.float32)


def _rows(*vecs, width=256, nrows=8):
    rows = []
    for v in vecs:
        v = jnp.asarray(v, F32).reshape(-1)
        rows.append(jnp.pad(v, (0, width - v.shape[0])))
    rows += [jnp.zeros((width,), F32)] * (nrows - len(rows))
    return jnp.stack(rows)


def _layout_w_in(w):
    d = w.shape[0]
    zeros = lambda n: jnp.zeros((d, n), w.dtype)
    gl0, ml0, gt0 = 2048, 2832, 3608
    parts = [
        w[:, 0:2048],
        w[:, gl0:gl0 + 512], w[:, gl0 + 512:gl0 + 528], zeros(112), w[:, gl0 + 528:gl0 + 784], zeros(128),
        w[:, ml0:ml0 + 512], w[:, ml0 + 512:ml0 + 520], zeros(120), w[:, ml0 + 520:ml0 + 776], zeros(128),
    ]
    wz = jnp.concatenate(parts, axis=1)
    assert wz.shape[1] == Z_COLS
    return wz.astype(BF16), w[:, gt0:gt0 + 4 * D_MODEL].astype(BF16)


class _Group:
    def __init__(self, nb, trow, tv, row0, max_seq):
        self.nb, self.trow, self.tv, self.row0 = nb, trow, tv, row0
        self.full = tv == trow
        self.c = SUB if self.full else trow
        self.tvs = min(tv, self.c)
        if self.full:
            nblk = trow // self.c
            self.rps = self.c * max(d for d in range(1, min(nblk, 64) + 1) if nblk % d == 0)
        else:
            self.rps = trow
        self.nst = trow // self.rps
        self.nsub = self.rps // self.c
        self.nseq = max(d for d in range(1, max_seq + 1) if nb % d == 0 and row0 % (d * self.rps) == 0)
        self.lanes = 4 if self.nseq % 4 == 0 else 2 if self.nseq % 2 == 0 else 1
        assert self.nseq == self.lanes or self.nst == 1
        self.rows = self.nseq * self.rps
        self.ngrid = nb // self.nseq
        self.base_blk = row0 // self.rows
        self.pc = _pick_tile(self.rps, 384) if self.rps >= 64 else self.rows


def kernel(x_prompt, x_sample, state_hgrn, state_rwkv, state_rwkv_shift, state_gla, state_mlstm_c, state_mlstm_n, state_mlstm_m, state_mlstm_conv, meta_tokens, norm_mix, norm_mlp, norm_final, w_in, hgrn_lb, hgrn_norm, rwkv_mu, rwkv_w0, rwkv_w_up, rwkv_a0, rwkv_a_up, rwkv_g_up, rwkv_k_k, rwkv_k_a, rwkv_r_k, rwkv_ln_w, rwkv_ln_b, gla_gate_up, gla_gate_b, gla_norm, mlstm_conv_w, mlstm_conv_b, mlstm_i_b, mlstm_f_b, mlstm_norm, w_branch, w_out, w_up, w_down):
    depth = w_in.shape[0]
    bp, seq, _ = x_prompt.shape
    bs, dseq, _ = x_sample.shape
    tp = N_META + seq
    assert tp % SUB == 0 and CONV_W - 1 <= dseq <= SAMPLE_PAD
    dt = x_prompt.dtype

    gp = _Group(bp, tp, tp, 0, 4)
    gs = _Group(bs, SAMPLE_PAD, dseq, bp * tp, 16)
    tm = gp.rps
    meta = jnp.broadcast_to(meta_tokens.astype(dt)[None], (bp, N_META, D_MODEL))
    xp = jnp.concatenate([meta, x_prompt], axis=1).reshape(bp * tp, D_MODEL)
    xs = jnp.pad(x_sample, ((0, 0), (0, SAMPLE_PAD - dseq), (0, 0))).reshape(bs * SAMPLE_PAD, D_MODEL)
    n_real = bp * tp + bs * SAMPLE_PAD
    n_rows = -(-n_real // tm) * tm
    x = jnp.concatenate([xp, xs, jnp.zeros((n_rows - n_real, D_MODEL), dt)], axis=0).astype(F32)

    def mixer_tile(i, split=1):
        t, part = i // split, i % split
        b, s = t // gp.nst, t % gp.nst
        p = (b // gp.nseq) * (gp.nst * gp.nseq) + s * gp.nseq + b % gp.nseq
        return jnp.where(t < bp * gp.nst, p, t) * split + part

    in_split = 2 if tm % 16 == 0 and tm >= 256 else 1

    ones_vv = jnp.asarray(_seg_ones(64, 64), BF16)
    ones_gl = jnp.asarray(_seg_ones(GLA_DK, 64), BF16)
    mask_hg = jnp.asarray(_seg_ones(64, 64), F32)
    mask_gl = jnp.asarray(_seg_ones(64, GLA_DK), F32)
    sel8 = jnp.asarray(np.eye(8, 128, dtype=np.float32), BF16)
    hmask = {g.c: jnp.asarray(np.kron(np.eye(2), np.ones((g.c, 64))), F32) for g in (gp, gs)}
    ehead = {c_: jnp.concatenate([m_, m_], axis=0).astype(BF16) for c_, m_ in hmask.items()}
    mask_pair = jnp.asarray(np.kron(np.eye(2), np.ones((64, 64))), F32)

    lb_cs = jnp.cumsum(jax.nn.softmax(hgrn_lb.astype(F32), axis=0), axis=0)
    lb_all = lb_cs - lb_cs[:1]

    def init_states(states, nb, zero):
        s_hg, s_rw, shift, s_gl, c_ml, n_ml, m_ml, conv = states
        if zero:
            z = lambda *shp: jnp.zeros((depth, nb) + shp, F32)
            return dict(hg=z(256, 64), rw=z(256, 64), shift=z(1, SEC), gl=z(256, GLA_DK), cm=z(128, 64),
                        nm=z(1, 128), mm=z(1, 128), cv=z(8, 256))
        f = lambda a: a.astype(F32)
        return dict(
            hg=f(s_hg).transpose(0, 1, 2, 4, 3).reshape(depth, nb, 256, 64),
            rw=f(s_rw).reshape(depth, nb, 256, 64),
            shift=f(shift).reshape(depth, nb, 1, SEC),
            gl=f(s_gl).transpose(0, 1, 2, 4, 3).reshape(depth, nb, 256, GLA_DK),
            cm=f(c_ml).reshape(depth, nb, 128, 64),
            nm=f(n_ml).reshape(depth, nb, 1, 128),
            mm=jnp.pad(f(m_ml), ((0, 0), (0, 0), (0, 124))).reshape(depth, nb, 1, 128),
            cv=jnp.pad(f(conv), ((0, 0), (0, 0), (5, 0), (0, 0))),
        )

    sample_states = (state_hgrn, state_rwkv, state_rwkv_shift, state_gla,
                     state_mlstm_c, state_mlstm_n, state_mlstm_m, state_mlstm_conv)
    st_p = init_states(sample_states, bp, True)
    st_s = init_states(sample_states, bs, False)
    new_p = {k: [] for k in st_p}
    new_s = {k: [] for k in st_s}

    for l in range(depth):
        wz, wg = _layout_w_in(w_in[l])
        g_mix = norm_mix[l].reshape(1, D_MODEL).astype(F32)
        z = _in_proj(x, g_mix, wz, tm // in_split, functools.partial(mixer_tile, split=in_split))

        lb = lb_all[l]
        par_hg = _rows(jnp.log(lb), jnp.log1p(-lb), 1.0 - lb, hgrn_norm[l])
        par_gl = _rows(gla_gate_b[l], jnp.zeros((1,)), jnp.zeros((1,)), gla_norm[l])
        gup_gl = jnp.pad(gla_gate_up[l], ((0, 128 - gla_gate_up.shape[1]), (0, 0))).astype(BF16)
        gup_dummy = jnp.zeros((128, 128), BF16)
        par_rw = _rows(rwkv_w0[l], rwkv_a0[l], rwkv_k_k[l], rwkv_k_a[l], rwkv_r_k[l], rwkv_ln_w[l], rwkv_ln_b[l])
        nw, na = rwkv_w_up.shape[1], rwkv_a_up.shape[1]
        wup_p = jnp.pad(rwkv_w_up[l], ((0, 256 - nw), (0, 0))).astype(BF16)
        aup_p = jnp.pad(rwkv_a_up[l], ((nw, 256 - nw - na), (0, 0))).astype(BF16)
        gup_p = jnp.pad(rwkv_g_up[l], ((nw + na, 0), (0, 0))).astype(BF16)
        mu = rwkv_mu[l].reshape(1, SEC).astype(F32)
        gate_b = jnp.concatenate([mlstm_i_b[l], mlstm_f_b[l]]).astype(F32)
        par_ml = _rows(mlstm_conv_b[l], gate_b, mlstm_norm[l])
        bcol_ml = gate_b.reshape(8, 1)
        cw_ml = _rows(*[mlstm_conv_w[l, j] for j in range(CONV_W)])

        o_hg = o_rw = o_gl = o_ml = None
        for g, st, new in ((gp, st_p, new_p), (gs, st_s, new_s)):
            prev = lambda o: jnp.zeros((n_rows, 256), F32) if o is None else o
            o_hg, s_hg = _gla_call(z, 0, par_hg, gup_dummy, ones_vv, ones_vv, mask_hg, st["hg"][l], prev(o_hg),
                                   mode="hgrn", dk=64, nrows=n_rows, g=g)
            o_gl, s_gl = _gla_call(z, 2, par_gl, gup_gl, ones_gl, ones_vv, mask_gl, st["gl"][l], prev(o_gl),
                                   mode="gla", dk=GLA_DK, nrows=n_rows, g=g)
            o_rw, nshift, s_rw = _rwkv_call(z, mu, par_rw, wup_p, aup_p, gup_p, ones_vv, hmask[g.c], ehead[g.c],
                                            mask_pair, st["shift"][l], st["rw"][l], prev(o_rw), nrows=n_rows, g=g)
            o_ml, s_cm, s_nm, s_mm, s_cv = _mlstm_call(z, par_ml, bcol_ml, cw_ml, sel8, ones_vv, st["cm"][l], st["nm"][l],
                                                        st["mm"][l], st["cv"][l], prev(o_ml), nrows=n_rows, g=g)
            for key, val in (("hg", s_hg), ("rw", s_rw), ("shift", nshift), ("gl", s_gl), ("cm", s_cm),
                             ("nm", s_nm), ("mm", s_mm), ("cv", s_cv)):
                new[key].append(val)

        x = _merge_call((o_hg, o_rw, o_gl, o_ml), x, g_mix, wg, w_branch[l].astype(BF16), w_out[l].astype(BF16), tm,
                        mixer_tile)
        x = _mlp_call(x, norm_mlp[l].reshape(1, D_MODEL).astype(F32), norm_final.reshape(1, D_MODEL).astype(F32),
                      w_up[l].astype(BF16), w_down[l].astype(BF16), tm, final=(l == depth - 1))

    y_prompt = x[:bp * tp].reshape(bp, tp, D_MODEL)[:, N_META:].astype(dt)
    y_sample = x[gs.row0:gs.row0 + bs * SAMPLE_PAD].reshape(bs, SAMPLE_PAD, D_MODEL)[:, :dseq].astype(dt)

    def finish(new, nb):
        st = {k: jnp.stack(v) for k, v in new.items()}
        return (
            st["hg"].reshape(depth, nb, N_HEADS, 64, 64).transpose(0, 1, 2, 4, 3),
            st["rw"].reshape(depth, nb, N_HEADS, 64, 64),
            st["shift"].reshape(depth, nb, SEC),
            st["gl"].reshape(depth, nb, N_HEADS, 64, GLA_DK).transpose(0, 1, 2, 4, 3),
            st["cm"].reshape(depth, nb, N_HEADS, MLSTM_DK, 64),
            st["nm"].reshape(depth, nb, N_HEADS, MLSTM_DK),
            st["mm"].reshape(depth, nb, 128)[:, :, :N_HEADS],
            st["cv"][:, :, 5:8, :],
        )

    outs_p = tuple(a.astype(dt) for a in finish(new_p, bp))
    outs_s = tuple(a.astype(dt) for a in finish(new_s, bs))
    return (y_prompt, y_sample) + outs_p + outs_s
```

```python
import functools

import numpy as np
import jax
import jax.numpy as jnp
from jax import lax
from jax.experimental import pallas as pl
from jax.experimental.pallas import tpu as pltpu

F32 = jnp.float32
BF16 = jnp.bfloat16

D_MODEL = 1024
N_META = 16
N_HEADS = 4
HEAD_V = 64
GLA_DK = 32
MLSTM_DK = 32
GLA_GATE_TAU = 16.0
RWKV_LN_EPS = 64e-5
CONV_W = 4
D_FF = 4 * D_MODEL
EPS = 1e-6

SEC = 1024
Z_COLS = 4 * SEC
SUB = 16
SAMPLE_PAD = 8
MLSTM_CHUNK = 128
VMEM_LIMIT = 56 * 1024 * 1024


def _cparams(*sem):
    return pltpu.CompilerParams(dimension_semantics=sem, vmem_limit_bytes=VMEM_LIMIT)


def _pick_tile(n, max_tile, mult=8):
    best = None
    t = mult
    while t <= min(n, max_tile):
        if n % t == 0:
            best = t
        t += mult
    assert best is not None, (n, max_tile, mult)
    return best


def _split2(x):
    hi = x.astype(BF16)
    lo = (x - hi.astype(F32)).astype(BF16)
    return hi, lo


def _split3(x):
    hi = x.astype(BF16)
    r1 = x - hi.astype(F32)
    mid = r1.astype(BF16)
    lo = (r1 - mid.astype(F32)).astype(BF16)
    return hi, mid, lo


def _dot(a, b):
    return jnp.dot(a, b, preferred_element_type=F32)


def _dot_nt(a, b):
    return lax.dot_general(a, b, (((1,), (1,)), ((), ())), preferred_element_type=F32)


def _dot_tn(a, b):
    return lax.dot_general(a, b, (((0,), (0,)), ((), ())), preferred_element_type=F32)


def _seg_dot(x, m_bf16, parts):
    ps = _split2(x) if parts == 2 else _split3(x)
    acc = _dot(ps[0], m_bf16)
    for p in ps[1:]:
        acc = acc + _dot(p, m_bf16)
    return acc


def _seg_dot_l(m_bf16, x, parts):
    ps = _split2(x) if parts == 2 else _split3(x)
    acc = _dot(m_bf16, ps[0])
    for p in ps[1:]:
        acc = acc + _dot(m_bf16, p)
    return acc


def _log_sigmoid(x):
    return jnp.minimum(x, 0.0) - jnp.log1p(jnp.exp(-jnp.abs(x)))


def _silu(x):
    return x * jax.nn.sigmoid(x)


def _tri(n, upper=False):
    r = lax.broadcasted_iota(jnp.int32, (n, n), 0)
    c = lax.broadcasted_iota(jnp.int32, (n, n), 1)
    return (r <= c) if upper else (r >= c)


def _cumsum_rows(x, rowi):
    d = 1
    while d < x.shape[0]:
        x = x + jnp.where(rowi >= d, pltpu.roll(x, d, axis=0), 0.0)
        d *= 2
    return x


def _rms_bf16(x, g):
    return (x * lax.rsqrt(jnp.mean(x * x, axis=-1, keepdims=True) + EPS) * g).astype(BF16)


def _in_proj_kernel(x_ref, g_ref, w_ref, z_ref):
    h = _rms_bf16(x_ref[...], g_ref[...])
    for j in range(Z_COLS // SEC):
        z_ref[:, j * SEC:(j + 1) * SEC] = _dot(h, w_ref[:, j * SEC:(j + 1) * SEC])


def _in_proj(x, g, w, tm, out_tile):
    n = x.shape[0]
    return pl.pallas_call(
        _in_proj_kernel,
        grid=(n // tm,),
        in_specs=[
            pl.BlockSpec((tm, D_MODEL), lambda i: (i, 0)),
            pl.BlockSpec((1, D_MODEL), lambda i: (0, 0)),
            pl.BlockSpec((D_MODEL, Z_COLS), lambda i: (0, 0)),
        ],
        out_specs=pl.BlockSpec((tm, Z_COLS), lambda i: (out_tile(i), 0)),
        out_shape=jax.ShapeDtypeStruct((n, Z_COLS), F32),
        compiler_params=_cparams("parallel"),
    )(x, g, w)


def _gla_kernel(z_ref, par_ref, gup_ref, okv_ref, ovv_ref, mask_ref, st_in_ref, o_prev_ref,
                o_ref, st_out_ref, *st_scrs, mode, c, nsub, tv, dk, nseq, rps, nst, pc):
    del o_prev_ref
    voff = 512 if mode == "hgrn" else 256
    rowi = lax.broadcasted_iota(jnp.int32, (c, 1), 0)
    valid = rowi < tv

    def sub_block(st_scr, r0):
        rows = pl.ds(r0, c)
        if mode == "hgrn":
            q = z_ref[rows, 0:256]
            hf = z_ref[rows, 256:512]
            g = z_ref[rows, 768:1024]
            a = par_ref[0:1, :]
            cc = par_ref[1:2, :] + _log_sigmoid(hf)
            loga = jnp.maximum(a, cc) + jnp.log1p(jnp.exp(-jnp.abs(a - cc)))
            k = par_ref[2:3, :] * jax.nn.sigmoid(-hf)
        else:
            q = z_ref[rows, 0:128] * (GLA_DK ** -0.5)
            k = z_ref[rows, 128:256]
            ga = z_ref[rows, 512:640]
            g = z_ref[rows, 640:896]
            gl = _dot(ga.astype(BF16), gup_ref[...]) + par_ref[0:1, 0:128]
            loga = _log_sigmoid(gl) * (1.0 / GLA_GATE_TAU)
        v = z_ref[rows, voff:voff + 256]
        if tv < c:
            loga = jnp.where(valid, loga, 0.0)
            k = jnp.where(valid, k, 0.0)
            v = jnp.where(valid, v, 0.0)
        b = _cumsum_rows(loga, rowi)

        o = _dot_nt((q * jnp.exp(b)).astype(BF16), st_scr[...].astype(BF16))

        lo_rows = [8 * (s // 8) for s in range(tv)]
        pieces = []
        for s in range(tv):
            rs = slice(lo_rows[s], c)
            e = jnp.exp(jnp.where(rowi[rs] >= s, b[rs] - b[s:s + 1, :], -jnp.inf))
            pieces.append(q[rs] * e * k[s:s + 1, :])
        return rows, lo_rows, k, v, b, o, jnp.concatenate(pieces, axis=0)

    def sub_block_finish(st_scr, part, sm):
        rows, lo_rows, k, v, b, o, _ = part
        acc = {}
        off = 0
        for s in range(tv):
            n = c - lo_rows[s]
            contrib = sm[off:off + n] * v[s:s + 1, :]
            off += n
            acc[lo_rows[s]] = contrib if lo_rows[s] not in acc else acc[lo_rows[s]] + contrib
        for lo, a_ in acc.items():
            o = o + (a_ if lo == 0 else jnp.concatenate([jnp.zeros((lo, 256), F32), a_], axis=0))

        bl = b[c - 1:c, :]
        kdec = k * jnp.exp(bl - b)
        upd = _dot_tn(v.astype(BF16), kdec.astype(BF16))
        st_scr[...] = st_scr[...] * jnp.exp(bl) + upd * mask_ref[...]
        o_ref[rows, :] = o

    def sub_blocks(scrs, row0s):
        parts = [sub_block(scr, r0) for scr, r0 in zip(scrs, row0s)]
        sm = _dot(jnp.concatenate([p[-1] for p in parts], axis=0).astype(BF16), okv_ref[...])
        n = parts[0][-1].shape[0]
        for i, (scr, part) in enumerate(zip(scrs, parts)):
            sub_block_finish(scr, part, sm[i * n:(i + 1) * n])

    def load_state(st_scr, seq):
        st_scr[...] = jnp.concatenate([st_in_ref[seq]] * N_HEADS, axis=1) * mask_ref[...]

    def store_state(st_scr, seq):
        st = st_scr[...]
        acc = st[:, 0:dk]
        for h in range(1, N_HEADS):
            acc = acc + st[:, h * dk:(h + 1) * dk]
        st_out_ref[seq] = acc

    _run_lanes(st_scrs, nseq, rps, nsub, c, nst, load_state, store_state, sub_blocks)

    goff = 768 if mode == "hgrn" else 640
    nchunk = (nseq * rps) // pc

    def finish_chunk(ci, carry):
        rows = pl.ds(ci * pc if isinstance(ci, int) else pl.multiple_of(ci * pc, 8), pc)
        o = o_ref[rows, :]
        ms = _seg_dot(o * o, ovv_ref[...], 2) * (1.0 / HEAD_V)
        o_ref[rows, :] = o * lax.rsqrt(ms + EPS) * par_ref[3:4, :] * _silu(z_ref[rows, goff:goff + 256])
        return carry

    if nchunk == 1:
        finish_chunk(0, 0)
    else:
        lax.fori_loop(0, nchunk, finish_chunk, 0)


def _run_lanes(scrs, nseq, rps, nsub, c, nst, load_state, store_state, sub_blocks, group=0):
    lanes = len(scrs)
    assert nseq == lanes or nst == 1
    step = pl.program_id(1)

    def seq_group(sg, carry):
        seqs = [sg * lanes + u for u in range(lanes)]
        for scr, seq in zip(scrs, seqs):
            if nst == 1:
                load_state(scr, seq)
            else:
                pl.when(step == 0)(functools.partial(load_state, scr, seq))

        def starts(j):
            return [seq * rps + j * c if isinstance(seq * rps + j * c, int)
                    else pl.multiple_of(seq * rps + j * c, 8) for seq in seqs]

        def blocks(j, carry2):
            if group:
                sub_blocks(scrs, [starts(j * group + g_) for g_ in range(group)])
            else:
                sub_blocks(scrs, starts(j))
            return carry2

        if group:
            if nsub // group == 1:
                blocks(0, 0)
            elif nsub // group > 1:
                lax.fori_loop(0, nsub // group, blocks, 0, unroll=2 if group == 1 else 1)
            for j in range(nsub - nsub % group, nsub):
                sub_blocks(scrs, [starts(j)])
        elif nsub == 1:
            blocks(0, 0)
        else:
            lax.fori_loop(0, nsub, blocks, 0, unroll=2)
        for scr, seq in zip(scrs, seqs):
            if nst == 1:
                store_state(scr, seq)
            else:
                pl.when(step == nst - 1)(functools.partial(store_state, scr, seq))
        return carry

    if nseq == lanes:
        seq_group(0, 0)
    else:
        lax.fori_loop(0, nseq // lanes, seq_group, 0)


def _gla_call(z, sec, par, gup, okv, ovv, mask, st_in, o_prev, *, mode, dk, nrows, g):
    hk = N_HEADS * dk
    kern = functools.partial(_gla_kernel, mode=mode, c=g.c, nsub=g.nsub, tv=g.tvs, dk=dk,
                             nseq=g.nseq, rps=g.rps, nst=g.nst, pc=g.pc)
    const = lambda b, s: (0, 0)
    return pl.pallas_call(
        kern,
        grid=(g.ngrid, g.nst),
        in_specs=[
            pl.BlockSpec((g.rows, SEC), lambda b, s: (g.base_blk + b * g.nst + s, sec)),
            pl.BlockSpec((8, 256), const),
            pl.BlockSpec((128, 128), const),
            pl.BlockSpec((hk, 256), const),
            pl.BlockSpec((256, 256), const),
            pl.BlockSpec((256, hk), const),
            pl.BlockSpec((g.nseq, 256, dk), lambda b, s: (b, 0, 0)),
            pl.BlockSpec(memory_space=pl.ANY),
        ],
        input_output_aliases={7: 0},
        out_specs=[
            pl.BlockSpec((g.rows, 256), lambda b, s: (g.base_blk + b * g.nst + s, 0)),
            pl.BlockSpec((g.nseq, 256, dk), lambda b, s: (b, 0, 0)),
        ],
        out_shape=[
            jax.ShapeDtypeStruct((nrows, 256), F32),
            jax.ShapeDtypeStruct((g.nb, 256, dk), F32),
        ],
        scratch_shapes=[pltpu.VMEM((256, hk), F32)] * g.lanes,
        compiler_params=_cparams("parallel", "arbitrary"),
    )(z, par, gup, okv, ovv, mask, st_in, o_prev)


def _rwkv_kernel(z_ref, mu_ref, par_ref, wup_ref, aup_ref, gup_ref, ovv_ref, hmask_ref, ehead_ref, mask_ref,
                 shift_ref, st_in_ref, o_prev_ref,
                 o_ref, nshift_ref, st_out_ref,
                 prev_scr, fix_scr, r_scr, lw_scr, k_scr, v_scr, kk_scr, ka_scr, g_scr, bonus_scr, y_scr, *st_scrs,
                 c, nsub, tv, nseq, rps, nst, pc):
    del o_prev_ref
    step = pl.program_id(1)

    def shifted_chunk(ci):
        r0 = ci * pc if isinstance(ci, int) else pl.multiple_of(ci * pc, 8)
        zr = z_ref[pl.ds(r0, pc), :]
        rolled = pltpu.roll(zr, 1, axis=0)
        if pc <= rps:
            cps = rps // pc
            seq = ci // cps
            rowi = lax.broadcasted_iota(jnp.int32, (pc, 1), 0)
            before = z_ref[pl.ds(pl.multiple_of(jnp.maximum(r0 - 8, 0), 8), 8), :][7:8, :]
            first = jnp.where(ci % cps == 0, prev_scr[pl.ds(seq, 1), :], before)
            prev = jnp.where(rowi == 0, first, rolled)
            prev_scr[pl.ds(seq, 1), :] = zr[pc - 1:pc, :]
            nshift_ref[seq] = zr[pc - 1:pc, :]
        else:
            fix_scr[...] = rolled
            for i in range(nseq):
                fix_scr[i * rps:i * rps + 1, :] = shift_ref[i]
                nshift_ref[i] = zr[i * rps + tv - 1:i * rps + tv, :]
            prev = fix_scr[...]
        return r0, zr, prev

    if pc <= rps:
        @pl.when(step == 0)
        def _():
            for i in range(nseq):
                prev_scr[i:i + 1, :] = shift_ref[i]

    def prep_chunk(ci, carry):
        r0, zr, prev = shifted_chunk(ci)
        rows = pl.ds(r0, pc)
        _rwkv_prep_rows(zr, prev, rows)
        return carry

    def _rwkv_prep_rows(zr, prev, rows):
        nrow = zr.shape[0]
        zs = zr + (prev - zr) * mu_ref[...]
        r = zs[:, 0:256]
        k = zs[:, 256:512]
        v = zs[:, 512:768]
        lo = zs[:, 768:1024]
        w0, a0, k_k, k_a, r_k = (par_ref[i:i + 1, :] for i in range(5))
        wl = w0 + _dot(jnp.tanh(lo).astype(BF16), wup_ref[...])
        wexp = -(jnp.maximum(-wl, 0.0) + jnp.log1p(jnp.exp(-jnp.abs(wl)))) - 0.5
        lw = -jnp.exp(wexp)
        a = jax.nn.sigmoid(a0 + _dot(lo.astype(BF16), aup_ref[...]))
        g_scr[rows, :] = _dot(jax.nn.sigmoid(lo).astype(BF16), gup_ref[...])
        kkp = k * k_k
        nrm = jnp.sqrt(_seg_dot(kkp * kkp, ovv_ref[...], 3))
        kk = kkp / jnp.maximum(nrm, 1e-12)
        k2 = k * (1.0 + (a - 1.0) * k_a)
        bonus_scr[rows, :] = _seg_dot(r * k2 * r_k, ovv_ref[...], 3) * v
        ka = kk * a
        if tv < rps:
            okrow = lax.rem(lax.broadcasted_iota(jnp.int32, (nrow, 1), 0), rps) < tv
            lw, k2, v, kk, ka = (jnp.where(okrow, t_, 0.0) for t_ in (lw, k2, v, kk, ka))
        r_scr[rows, :] = r
        lw_scr[rows, :] = lw
        k_scr[rows, :] = k2
        v_scr[rows, :] = v
        kk_scr[rows, :] = kk
        ka_scr[rows, :] = ka

    nchunk = (nseq * rps) // pc
    if nchunk == 1:
        prep_chunk(0, 0)
    else:
        lax.fori_loop(0, nchunk, prep_chunk, 0)

    hp = 2
    ng = N_HEADS // hp
    gw = hp * 64
    rowc = lax.broadcasted_iota(jnp.int32, (c, 1), 0)
    src2 = lax.broadcasted_iota(jnp.int32, (1, 2 * hp * c), 1) & (c - 1)
    src = src2[:, 0:hp * c]
    strict = rowc > src
    incl = rowc >= src

    def sub_blocks(lane_scrs, row0_lists):
        staged = [stage_free(lane_scrs, row0s) for row0s in row0_lists]
        for st_ in staged:
            stage_state(*st_)

    def stage_free(lane_scrs, row0s):
        hm = hmask_ref[...]
        tile = lambda t_: jnp.concatenate([t_] * hp, axis=0) * hm
        nc = hp * c
        nsolve = min(tv, c - 1)
        units, cx = [], {}
        for i, r0 in enumerate(row0s):
            rows = pl.ds(r0, c)
            lw_, r_, k_, v_, kk_, ka_ = (s_[rows, :] for s_ in (lw_scr, r_scr, k_scr, v_scr, kk_scr, ka_scr))
            gam = _cumsum_rows(lw_, rowc)
            gl = gam[c - 1:c, :]
            ginv = jnp.exp(-gam)
            gend = jnp.exp(gl - gam)
            lh, ll = _split2(jnp.concatenate([kk_ * jnp.exp(gam - lw_), r_ * jnp.exp(gam)], axis=0))
            a_, kq_ = ka_ * ginv, k_ * ginv
            ae_, ke_ = ka_ * gend, k_ * gend
            for p in range(ng):
                ls = slice(p * gw, (p + 1) * gw)
                rh, rl = _split2(jnp.concatenate([tile(a_[:, ls]), tile(kq_[:, ls])], axis=0))
                units.append((i, p))
                cx[i, p] = dict(rows=rows, ls=ls, v=v_[:, ls], egl=jnp.exp(gl[:, ls]), lh=lh[:, ls], ll=ll[:, ls],
                                rh=rh, rl=rl, vbig=tile(v_[:, ls]).astype(BF16), scr=lane_scrs[i][p],
                                end=jnp.concatenate([ae_[:, ls], ke_[:, ls]], axis=0).astype(BF16))
        g2 = {u_: _dot_nt(jnp.concatenate([cx[u_]["lh"], cx[u_]["ll"]], axis=0), cx[u_]["rh"]) for u_ in units}
        g3 = {u_: _dot_nt(cx[u_]["lh"], cx[u_]["rl"]) for u_ in units}
        m_ab, m_ra, m_rk, bkv = {}, {}, {}, {}
        for u_ in units:
            gm = g2[u_][0:2 * c] + g2[u_][2 * c:4 * c] + g3[u_]
            m_ab[u_] = jnp.where(strict, gm[0:c, 0:nc], 0.0)
            m_ra[u_] = jnp.where(incl, gm[c:2 * c, 0:nc], 0.0)
            m_rk[u_] = jnp.where(incl, gm[c:2 * c, nc:2 * nc], 0.0)
            bkv[u_] = _dot(jnp.where(strict, gm[0:c, nc:2 * nc], 0.0).astype(BF16), cx[u_]["vbig"])
        m2 = {u_: jnp.concatenate(_split2(m_ab[u_]), axis=1) for u_ in units}
        cols = _dot(jnp.concatenate([jnp.where(src2 == s, m2[u_], jnp.zeros_like(m2[u_]))
                                     for u_ in units for s in range(nsolve)], axis=0), ehead_ref[...])
        return units, cx, m_ra, m_rk, bkv, cols

    def stage_state(units, cx, m_ra, m_rk, bkv, cols):
        nsolve = min(tv, c - 1)
        tile = lambda t_: jnp.concatenate([t_] * hp, axis=0) * hmask_ref[...]
        x0 = {u_: _dot_nt(cx[u_]["lh"], cx[u_]["scr"][...].astype(BF16)) for u_ in units}
        usol = {u_: x0[u_][0:c] + bkv[u_] for u_ in units}
        for s in range(nsolve):
            for n_, u_ in enumerate(units):
                o_ = (n_ * nsolve + s) * c
                usol[u_] = usol[u_] - cols[o_:o_ + c] * usol[u_][s:s + 1, :]
        ys = {u_: _dot(jnp.concatenate([-m_ra[u_], m_rk[u_]], axis=1).astype(BF16),
                       jnp.concatenate([tile(usol[u_]).astype(BF16), cx[u_]["vbig"]], axis=0)) for u_ in units}
        upd = {u_: _dot_tn(jnp.concatenate([-usol[u_], cx[u_]["v"]], axis=0).astype(BF16), cx[u_]["end"])
               for u_ in units}
        for u_ in units:
            x = cx[u_]
            y_scr[x["rows"], x["ls"]] = x0[u_][c:2 * c] + ys[u_]
            x["scr"][...] = x["scr"][...] * x["egl"] + upd[u_] * mask_ref[...]

    def load_state(scrs, seq):
        for p, scr in enumerate(scrs):
            scr[...] = jnp.concatenate([st_in_ref[seq, p * gw:(p + 1) * gw, :]] * hp, axis=1) * mask_ref[...]

    def store_state(scrs, seq):
        for p, scr in enumerate(scrs):
            st = scr[...]
            st_out_ref[seq, p * gw:(p + 1) * gw, :] = st[:, 0:64] + st[:, 64:128]

    lane_scrs = [st_scrs[i * ng:(i + 1) * ng] for i in range(len(st_scrs) // ng)]
    _run_lanes(lane_scrs, nseq, rps, nsub, c, nst, load_state, store_state, sub_blocks, group=1)

    def finish_chunk(ci, carry):
        rows = pl.ds(ci * pc if isinstance(ci, int) else pl.multiple_of(ci * pc, 8), pc)
        y = y_scr[rows, :]
        mean = _seg_dot(y, ovv_ref[...], 3) * (1.0 / HEAD_V)
        d = y - mean
        var = _seg_dot(d * d, ovv_ref[...], 3) * (1.0 / HEAD_V)
        ln = d * lax.rsqrt(var + RWKV_LN_EPS) * par_ref[5:6, :] + par_ref[6:7, :]
        o_ref[rows, :] = (ln + bonus_scr[rows, :]) * g_scr[rows, :]
        return carry

    if nchunk == 1:
        finish_chunk(0, 0)
    else:
        lax.fori_loop(0, nchunk, finish_chunk, 0)


def _rwkv_call(z, mu, par, wup, aup, gup, ovv, hmask, ehead, mask, shift, st_in, o_prev, *, nrows, g):
    rblk = g.rows
    nsteps = g.nst
    base = g.base_blk
    pc = g.pc
    kern = functools.partial(_rwkv_kernel, c=g.c, nsub=g.nsub, tv=g.tv, nseq=g.nseq, rps=g.rps, nst=g.nst, pc=pc)
    const = lambda b, s: (0, 0)
    return pl.pallas_call(
        kern,
        grid=(g.ngrid, nsteps),
        in_specs=[
            pl.BlockSpec((rblk, SEC), lambda b, s: (base + b * nsteps + s, 1),
                         pipeline_mode=pl.Buffered(1) if g.lanes > 2 and g.full else None),
            pl.BlockSpec((1, SEC), const),
            pl.BlockSpec((8, 256), const),
            pl.BlockSpec((256, 256), const),
            pl.BlockSpec((256, 256), const),
            pl.BlockSpec((256, 256), const),
            pl.BlockSpec((256, 256), const),
            pl.BlockSpec((2 * g.c, 128), const),
            pl.BlockSpec((4 * g.c, 128), const),
            pl.BlockSpec((128, 128), const),
            pl.BlockSpec((g.nseq, 1, SEC), lambda b, s: (b, 0, 0)),
            pl.BlockSpec((g.nseq, 256, 64), lambda b, s: (b, 0, 0)),
            pl.BlockSpec(memory_space=pl.ANY),
        ],
        input_output_aliases={12: 0},
        out_specs=[
            pl.BlockSpec((rblk, 256), lambda b, s: (base + b * nsteps + s, 0)),
            pl.BlockSpec((g.nseq, 1, SEC), lambda b, s: (b, 0, 0)),
            pl.BlockSpec((g.nseq, 256, 64), lambda b, s: (b, 0, 0)),
        ],
        out_shape=[
            jax.ShapeDtypeStruct((nrows, 256), F32),
            jax.ShapeDtypeStruct((g.nb, 1, SEC), F32),
            jax.ShapeDtypeStruct((g.nb, 256, 64), F32),
        ],
        scratch_shapes=[pltpu.VMEM((max(8, g.nseq), SEC), F32), pltpu.VMEM((pc if pc > g.rps else 8, SEC), F32)]
        + [pltpu.VMEM((rblk, 256), F32)] * 9 + [pltpu.VMEM((128, 128), F32)] * (2 * g.lanes),
        compiler_params=_cparams("parallel", "arbitrary"),
    )(z, mu, par, wup, aup, gup, ovv, hmask, ehead, mask, shift, st_in, o_prev)


def _mlstm_kernel(z_ref, par_ref, bcol_ref, cw_ref, sel_ref, ovv_ref, cm_in, nm_in, mm_in, cv_in, o_prev_ref,
                  o_ref, cm_out, nm_out, mm_out, cv_out,
                  *scratch, chunks, nseq, rps, nst, pc):
    del o_prev_ref
    lanes = scratch[0].shape[0]

    def chunk_gates(scr, row0, c, tv):
        xbuf, cm_scr, nm_scr, mm_scr = scr
        rows = pl.ds(row0, c)
        mqk = z_ref[rows, 0:256]
        mv = z_ref[rows, 256:512]
        gi = z_ref[rows, 512:640]
        mo = z_ref[rows, 640:896]
        xbuf[8:8 + c, :] = mqk
        conv = par_ref[0:1, :] + xbuf[5:5 + c, :] * cw_ref[0:1, :]
        for j in range(1, CONV_W):
            conv = conv + xbuf[5 + j:5 + j + c, :] * cw_ref[j:j + 1, :]
        xbuf[5:8, :] = xbuf[8 + tv - 3:8 + tv, :]
        act = _silu(conv)
        q = act[:, 0:128]
        k = act[:, 128:256] * (MLSTM_DK ** -0.5)

        rowi = lax.broadcasted_iota(jnp.int32, (c, 1), 0)
        coli = lax.broadcasted_iota(jnp.int32, (1, c), 1)
        gcol = gi + par_ref[1:2, 0:128]
        lfc = _log_sigmoid(gcol)
        graw = None
        for part in _split3(gi):
            t_ = _dot_nt(sel_ref[...], part)
            graw = t_ if graw is None else graw + t_
        grow = graw + bcol_ref[...]
        lfr = _log_sigmoid(grow)
        if tv < c:
            gcol = jnp.where(rowi < tv, gcol, -jnp.inf)
            lfc = jnp.where(rowi < tv, lfc, 0.0)
            grow = jnp.where(coli < tv, grow, -jnp.inf)
            lfr = jnp.where(coli < tv, lfr, 0.0)
        b_col = _seg_dot_l(_tri(c).astype(BF16), lfc, 3)
        b_row = _seg_dot(lfr, _tri(c, upper=True).astype(BF16), 3)
        return dict(rows=rows, q=q, k=k, mv=mv, mo=mo, gcol=gcol, grow=grow, b_col=b_col, b_row=b_row,
                    mm=mm_scr[...], nm=nm_scr[...], cm=cm_scr[...])

    def chunk(scrs, row0s, c, tv):
        ctx = [chunk_gates(scr, row0, c, tv) for scr, row0 in zip(scrs, row0s)]
        causal = _tri(c)
        lane = lax.broadcasted_iota(jnp.int32, (1, 128), 1)
        items = [(l, h) for l in range(len(ctx)) for h in range(N_HEADS)]
        ks = lambda h: slice(h * MLSTM_DK, (h + 1) * MLSTM_DK)
        vs = lambda h: slice(h * HEAD_V, (h + 1) * HEAD_V)
        qh = {(l, h): ctx[l]["q"][:, ks(h)] for l, h in items}
        kh = {(l, h): ctx[l]["k"][:, ks(h)] for l, h in items}
        vh = {(l, h): ctx[l]["mv"][:, vs(h)].astype(BF16) for l, h in items}
        cmh = {(l, h): ctx[l]["cm"][ks(h), :] for l, h in items}
        nh = {(l, h): ctx[l]["nm"][0:1, ks(h)] for l, h in items}
        qk_raw = {it: _dot_nt(qh[it].astype(BF16), kh[it].astype(BF16)) for it in items}
        q_cm = {it: _dot(qh[it].astype(BF16), cmh[it].astype(BF16)) for it in items}
        bc, ic, mprev, m_t, sc, qk = {}, {}, {}, {}, {}, {}
        for it in items:
            l, h = it
            bc[it] = ctx[l]["b_col"][:, 4 + h:5 + h]
            br = ctx[l]["b_row"][4 + h:5 + h, :]
            ir = ctx[l]["grow"][h:h + 1, :]
            ic[it] = ctx[l]["gcol"][:, h:h + 1]
            dmat = jnp.where(causal, bc[it] - br + ir, -jnp.inf)
            mprev[it] = ctx[l]["mm"][0:1, h:h + 1]
            inter = bc[it] + mprev[it]
            m_t[it] = jnp.maximum(inter, jnp.max(dmat, axis=1, keepdims=True))
            sc[it] = jnp.exp(inter - m_t[it])
            qk[it] = qk_raw[it] * jnp.exp(dmat - m_t[it])
        num = {it: _dot(qk[it].astype(BF16), vh[it]) + sc[it] * q_cm[it] for it in items}
        qk_sum = _seg_dot(jnp.concatenate([qk[it] for it in items], axis=0), jnp.ones((c, HEAD_V), BF16), 2)
        qn_sum = _seg_dot(jnp.concatenate([qh[it] * nh[it] for it in items], axis=0),
                          jnp.ones((MLSTM_DK, HEAD_V), BF16), 2)
        kw, sl, m_new, hn = {}, {}, {}, {}
        for n_, it in enumerate(items):
            den = qk_sum[n_ * c:(n_ + 1) * c] + sc[it] * qn_sum[n_ * c:(n_ + 1) * c]
            hc = num[it] / jnp.maximum(jnp.abs(den), jnp.exp(-m_t[it]))
            m_new[it] = m_t[it][tv - 1:tv, :]
            bl = bc[it][tv - 1:tv, :]
            wl = jnp.exp(bl - bc[it] + ic[it] - m_new[it])
            sl[it] = jnp.exp(bl + mprev[it] - m_new[it])
            kw[it] = kh[it] * wl
            hn[it] = hc
        kv = {it: _dot_tn(kw[it].astype(BF16), vh[it]) for it in items}
        for l, (scr, cx) in enumerate(zip(scrs, ctx)):
            xbuf, cm_scr, nm_scr, mm_scr = scr
            heads = [(l, h) for h in range(N_HEADS)]
            cm_scr[...] = jnp.concatenate([sl[it] * cmh[it] + kv[it] for it in heads], axis=0)
            nm_scr[...] = jnp.concatenate([sl[it] * nh[it] + jnp.sum(kw[it], axis=0, keepdims=True)
                                           for it in heads], axis=1)
            mm_new = cx["mm"]
            for it in heads:
                mm_new = jnp.where(lane == it[1], m_new[it], mm_new)
            mm_scr[...] = mm_new
            o_ref[cx["rows"], :] = jnp.concatenate([hn[it] for it in heads], axis=1)

    def load_state(scr, seq):
        xbuf, cm_scr, nm_scr, mm_scr = scr
        cm_scr[...] = cm_in[seq]
        nm_scr[...] = nm_in[seq]
        mm_scr[...] = mm_in[seq]
        xbuf[0:8, :] = cv_in[seq]

    def store_state(scr, seq):
        xbuf, cm_scr, nm_scr, mm_scr = scr
        cm_out[seq] = cm_scr[...]
        nm_out[seq] = nm_scr[...]
        mm_out[seq] = mm_scr[...]
        cv_out[seq] = xbuf[0:8, :]

    assert nseq == lanes or nst == 1
    step = pl.program_id(1)

    def lane_scr(u):
        return tuple(s_.at[u] for s_ in scratch)

    def run_chunks(grp):
        for start, c, tv, count in chunks:
            def body(ci, carry2, start=start, c=c, tv=tv):
                firsts = [seq * rps + start + ci * c for _, seq in grp]
                chunk([scr for scr, _ in grp],
                      [f if isinstance(f, int) else pl.multiple_of(f, 8) for f in firsts], c, tv)
                return carry2

            if count == 1:
                body(0, 0)
            else:
                lax.fori_loop(0, count, body, 0)

    def seq_group(sg, carry):
        def lane(u, seq):
            scr = lane_scr(u)
            if nst == 1:
                load_state(scr, seq)
            else:
                pl.when(step == 0)(functools.partial(load_state, scr, seq))
            return scr

        def finish(scr, seq):
            if nst == 1:
                store_state(scr, seq)
            else:
                pl.when(step == nst - 1)(functools.partial(store_state, scr, seq))

        if max(c for _, c, _, _ in chunks) <= 16:
            grp = [(lane(u, sg * lanes + u), sg * lanes + u) for u in range(lanes)]
            run_chunks(grp)
            for scr, seq in grp:
                finish(scr, seq)
        else:
            def one_lane(u, carry2):
                seq = sg * lanes + u
                scr = lane(u, seq)
                run_chunks([(scr, seq)])
                finish(scr, seq)
                return carry2

            lax.fori_loop(0, lanes, one_lane, 0)
        return carry

    if nseq == lanes:
        seq_group(0, 0)
    else:
        lax.fori_loop(0, nseq // lanes, seq_group, 0)

    nchunk = (nseq * rps) // pc

    def finish_chunk(ci, carry):
        rows = pl.ds(ci * pc if isinstance(ci, int) else pl.multiple_of(ci * pc, 8), pc)
        o = o_ref[rows, :]
        ms = _seg_dot(o * o, ovv_ref[...], 2) * (1.0 / HEAD_V)
        o_ref[rows, :] = o * lax.rsqrt(ms + EPS) * par_ref[2:3, :] * jax.nn.sigmoid(z_ref[rows, 640:896])
        return carry

    if nchunk == 1:
        finish_chunk(0, 0)
    else:
        lax.fori_loop(0, nchunk, finish_chunk, 0)


def _mlstm_call(z, par, bcol, cw, sel, ovv, cm, nm, mm, cv, o_prev, *, nrows, g):
    if g.full:
        cmain = max(d for d in (MLSTM_CHUNK, 64, 32, 16, 8) if d <= g.rps)
        lead = g.rps % cmain
        chunks = ([(0, lead, lead, 1)] if lead else []) + [(lead, cmain, cmain, g.rps // cmain)]
        assert lead % 8 == 0
    else:
        chunks = [(0, g.rps, g.tv, 1)]
    cmax = max(c for _, c, _, _ in chunks)
    kern = functools.partial(_mlstm_kernel, chunks=tuple(chunks), nseq=g.nseq, rps=g.rps, nst=g.nst, pc=g.pc)
    const = lambda b, s: (0, 0)
    st3 = lambda shp: pl.BlockSpec((g.nseq,) + shp, lambda b, s: (b, 0, 0))
    return pl.pallas_call(
        kern,
        grid=(g.ngrid, g.nst),
        in_specs=[
            pl.BlockSpec((g.rows, SEC), lambda b, s: (g.base_blk + b * g.nst + s, 3)),
            pl.BlockSpec((8, 256), const),
            pl.BlockSpec((8, 1), const),
            pl.BlockSpec((8, 256), const),
            pl.BlockSpec((8, 128), const),
            pl.BlockSpec((256, 256), const),
            st3((128, 64)), st3((1, 128)), st3((1, 128)), st3((8, 256)),
            pl.BlockSpec(memory_space=pl.ANY),
        ],
        input_output_aliases={10: 0},
        out_specs=[
            pl.BlockSpec((g.rows, 256), lambda b, s: (g.base_blk + b * g.nst + s, 0)),
            st3((128, 64)), st3((1, 128)), st3((1, 128)), st3((8, 256)),
        ],
        out_shape=[
            jax.ShapeDtypeStruct((nrows, 256), F32),
            jax.ShapeDtypeStruct((g.nb, 128, 64), F32),
            jax.ShapeDtypeStruct((g.nb, 1, 128), F32),
            jax.ShapeDtypeStruct((g.nb, 1, 128), F32),
            jax.ShapeDtypeStruct((g.nb, 8, 256), F32),
        ],
        scratch_shapes=[
            pltpu.VMEM((g.lanes, 8 + cmax, 256), F32),
            pltpu.VMEM((g.lanes, 128, 64), F32),
            pltpu.VMEM((g.lanes, 1, 128), F32),
            pltpu.VMEM((g.lanes, 1, 128), F32),
        ],
        compiler_params=_cparams("parallel", "arbitrary"),
    )(z, par, bcol, cw, sel, ovv, cm, nm, mm, cv, o_prev)


def _merge_kernel(o0_ref, o1_ref, o2_ref, o3_ref, x_ref, g_ref, wg_ref, wb_ref, wout_ref, out_ref, h_scr, acc_scr):
    n = pl.program_id(1)

    @pl.when(n == 0)
    def _():
        h_scr[...] = _rms_bf16(x_ref[...], g_ref[...])
        acc_scr[...] = jnp.zeros_like(acc_scr)

    gate = jax.nn.sigmoid(_dot(h_scr[...], wg_ref[...]))
    for idx, o_ref in enumerate((o0_ref, o1_ref, o2_ref, o3_ref)):
        @pl.when(n == idx)
        def _(o_ref=o_ref):
            acc_scr[...] += _dot(o_ref[...].astype(BF16), wb_ref[0]) * gate

    @pl.when(n == 3)
    def _():
        out_ref[...] = x_ref[...] + _dot(acc_scr[...].astype(BF16), wout_ref[...])


def _merge_call(outs, x, g, wg, wb, wout, tm, o_tile):
    n = x.shape[0]
    o_spec = pl.BlockSpec((tm, 256), lambda i, j: (o_tile(i), 0))
    return pl.pallas_call(
        _merge_kernel,
        grid=(n // tm, 4),
        in_specs=[
            o_spec, o_spec, o_spec, o_spec,
            pl.BlockSpec((tm, D_MODEL), lambda i, j: (i, 0)),
            pl.BlockSpec((1, D_MODEL), lambda i, j: (0, 0)),
            pl.BlockSpec((D_MODEL, D_MODEL), lambda i, j: (0, j)),
            pl.BlockSpec((1, 256, D_MODEL), lambda i, j: (j, 0, 0)),
            pl.BlockSpec((D_MODEL, D_MODEL), lambda i, j: (0, 0)),
        ],
        out_specs=pl.BlockSpec((tm, D_MODEL), lambda i, j: (i, 0)),
        out_shape=jax.ShapeDtypeStruct((n, D_MODEL), F32),
        scratch_shapes=[pltpu.VMEM((tm, D_MODEL), BF16), pltpu.VMEM((tm, D_MODEL), F32)],
        compiler_params=_cparams("parallel", "arbitrary"),
    )(*outs, x, g, wg, wb, wout)


def _mlp_kernel(x_ref, g_ref, gf_ref, wup_ref, wdn_ref, out_ref, h_scr, acc_scr, *, final):
    j = pl.program_id(1)

    @pl.when(j == 0)
    def _():
        h_scr[...] = _rms_bf16(x_ref[...], g_ref[...])
        acc_scr[...] = jnp.zeros_like(acc_scr)

    u = jnp.maximum(_dot(h_scr[...], wup_ref[...]), 0.0)
    acc_scr[...] += _dot((u * u).astype(BF16), wdn_ref[...])

    @pl.when(j == pl.num_programs(1) - 1)
    def _():
        y = x_ref[...] + acc_scr[...]
        if final:
            y = y * lax.rsqrt(jnp.mean(y * y, axis=-1, keepdims=True) + EPS) * gf_ref[...]
        out_ref[...] = y


def _mlp_call(x, g, gf, wup, wdn, tm, final):
    n = x.shape[0]
    tf = 1024
    return pl.pallas_call(
        functools.partial(_mlp_kernel, final=final),
        grid=(n // tm, D_FF // tf),
        in_specs=[
            pl.BlockSpec((tm, D_MODEL), lambda i, j: (i, 0)),
            pl.BlockSpec((1, D_MODEL), lambda i, j: (0, 0)),
            pl.BlockSpec((1, D_MODEL), lambda i, j: (0, 0)),
            pl.BlockSpec((D_MODEL, tf), lambda i, j: (0, j)),
            pl.BlockSpec((tf, D_MODEL), lambda i, j: (j, 0)),
        ],
        out_specs=pl.BlockSpec((tm, D_MODEL), lambda i, j: (i, 0)),
        out_shape=jax.ShapeDtypeStruct((n, D_MODEL), F32),
        scratch_shapes=[pltpu.VMEM((tm, D_MODEL), BF16), pltpu.VMEM((tm, D_MODEL), F32)],
        compiler_params=_cparams("parallel", "arbitrary"),
    )(x, g, gf, wup, wdn)


def _seg_ones(rows_per_head, cols_per_head):
    r = np.arange(N_HEADS * rows_per_head)[:, None] // rows_per_head
    c = np.arange(N_HEADS * cols_per_head)[None, :] // cols_per_head
    return (r == c).astype(np.float32)


def _rows(*vecs, width=256, nrows=8):
    rows = []
    for v in vecs:
        v = jnp.asarray(v, F32).reshape(-1)
        rows.append(jnp.pad(v, (0, width - v.shape[0])))
    rows += [jnp.zeros((width,), F32)] * (nrows - len(rows))
    return jnp.stack(rows)


def _layout_w_in(w):
    d = w.shape[0]
    zeros = lambda n: jnp.zeros((d, n), w.dtype)
    gl0, ml0, gt0 = 2048, 2832, 3608
    parts = [
        w[:, 0:2048],
        w[:, gl0:gl0 + 512], w[:, gl0 + 512:gl0 + 528], zeros(112), w[:, gl0 + 528:gl0 + 784], zeros(128),
        w[:, ml0:ml0 + 512], w[:, ml0 + 512:ml0 + 520], zeros(120), w[:, ml0 + 520:ml0 + 776], zeros(128),
    ]
    wz = jnp.concatenate(parts, axis=1)
    assert wz.shape[1] == Z_COLS
    return wz.astype(BF16), w[:, gt0:gt0 + 4 * D_MODEL].astype(BF16)


class _Group:
    def __init__(self, nb, trow, tv, row0, max_seq):
        self.nb, self.trow, self.tv, self.row0 = nb, trow, tv, row0
        self.full = tv == trow
        self.c = SUB if self.full else trow
        self.tvs = min(tv, self.c)
        if self.full:
            nblk = trow // self.c
            self.rps = self.c * max(d for d in range(1, min(nblk, 64) + 1) if nblk % d == 0)
        else:
            self.rps = trow
        self.nst = trow // self.rps
        self.nsub = self.rps // self.c
        self.nseq = max(d for d in range(1, max_seq + 1) if nb % d == 0 and row0 % (d * self.rps) == 0)
        self.lanes = 4 if self.nseq % 4 == 0 else 2 if self.nseq % 2 == 0 else 1
        assert self.nseq == self.lanes or self.nst == 1
        self.rows = self.nseq * self.rps
        self.ngrid = nb // self.nseq
        self.base_blk = row0 // self.rows
        self.pc = _pick_tile(self.rps, 384) if self.rps >= 64 else self.rows


def kernel(x_prompt, x_sample, state_hgrn, state_rwkv, state_rwkv_shift, state_gla, state_mlstm_c, state_mlstm_n, state_mlstm_m, state_mlstm_conv, meta_tokens, norm_mix, norm_mlp, norm_final, w_in, hgrn_lb, hgrn_norm, rwkv_mu, rwkv_w0, rwkv_w_up, rwkv_a0, rwkv_a_up, rwkv_g_up, rwkv_k_k, rwkv_k_a, rwkv_r_k, rwkv_ln_w, rwkv_ln_b, gla_gate_up, gla_gate_b, gla_norm, mlstm_conv_w, mlstm_conv_b, mlstm_i_b, mlstm_f_b, mlstm_norm, w_branch, w_out, w_up, w_down):
    depth = w_in.shape[0]
    bp, seq, _ = x_prompt.shape
    bs, dseq, _ = x_sample.shape
    tp = N_META + seq
    assert tp % SUB == 0 and CONV_W - 1 <= dseq <= SAMPLE_PAD
    dt = x_prompt.dtype

    gp = _Group(bp, tp, tp, 0, 4)
    gs = _Group(bs, SAMPLE_PAD, dseq, bp * tp, 16)
    tm = gp.rps
    xs = jnp.pad(x_sample, ((0, 0), (0, SAMPLE_PAD - dseq), (0, 0))).reshape(bs * SAMPLE_PAD, D_MODEL)
    n_real = bp * tp + bs * SAMPLE_PAD
    n_rows = -(-n_real // tm) * tm
    pieces = []
    for b in range(bp):
        pieces += [meta_tokens.astype(dt), x_prompt[b]]
    x = jnp.concatenate(pieces + [xs, jnp.zeros((n_rows - n_real, D_MODEL), dt)], axis=0).astype(F32)

    def mixer_tile(i, split=1):
        t, part = i // split, i % split
        b, s = t // gp.nst, t % gp.nst
        p = (b // gp.nseq) * (gp.nst * gp.nseq) + s * gp.nseq + b % gp.nseq
        return jnp.where(t < bp * gp.nst, p, t) * split + part

    in_split = 2 if tm % 16 == 0 and tm >= 256 else 1
    tm_mlp = 2 * tm if (n_rows // tm) % 2 == 0 and 2 * tm <= 1536 else tm

    ones_vv = jnp.asarray(_seg_ones(64, 64), BF16)
    ones_gl = jnp.asarray(_seg_ones(GLA_DK, 64), BF16)
    mask_hg = jnp.asarray(_seg_ones(64, 64), F32)
    mask_gl = jnp.asarray(_seg_ones(64, GLA_DK), F32)
    sel8 = jnp.asarray(np.eye(8, 128, dtype=np.float32), BF16)
    hmask = {g.c: jnp.asarray(np.kron(np.eye(2), np.ones((g.c, 64))), F32) for g in (gp, gs)}
    ehead = {c_: jnp.concatenate([m_, m_], axis=0).astype(BF16) for c_, m_ in hmask.items()}
    mask_pair = jnp.asarray(np.kron(np.eye(2), np.ones((64, 64))), F32)

    lb_cs = jnp.cumsum(jax.nn.softmax(hgrn_lb.astype(F32), axis=0), axis=0)
    lb_all = lb_cs - lb_cs[:1]

    def init_states(states, nb, zero):
        s_hg, s_rw, shift, s_gl, c_ml, n_ml, m_ml, conv = states
        if zero:
            z = lambda *shp: jnp.zeros((depth, nb) + shp, F32)
            return dict(hg=z(256, 64), rw=z(256, 64), shift=z(1, SEC), gl=z(256, GLA_DK), cm=z(128, 64),
                        nm=z(1, 128), mm=z(1, 128), cv=z(8, 256))
        f = lambda a: a.astype(F32)
        return dict(
            hg=f(s_hg).transpose(0, 1, 2, 4, 3).reshape(depth, nb, 256, 64),
            rw=f(s_rw).reshape(depth, nb, 256, 64),
            shift=f(shift).reshape(depth, nb, 1, SEC),
            gl=f(s_gl).transpose(0, 1, 2, 4, 3).reshape(depth, nb, 256, GLA_DK),
            cm=f(c_ml).reshape(depth, nb, 128, 64),
            nm=f(n_ml).reshape(depth, nb, 1, 128),
            mm=jnp.pad(f(m_ml), ((0, 0), (0, 0), (0, 124))).reshape(depth, nb, 1, 128),
            cv=jnp.pad(f(conv), ((0, 0), (0, 0), (5, 0), (0, 0))),
        )

    sample_states = (state_hgrn, state_rwkv, state_rwkv_shift, state_gla,
                     state_mlstm_c, state_mlstm_n, state_mlstm_m, state_mlstm_conv)
    st_p = init_states(sample_states, bp, True)
    st_s = init_states(sample_states, bs, False)
    new_p = {k: [] for k in st_p}
    new_s = {k: [] for k in st_s}

    for l in range(depth):
        wz, wg = _layout_w_in(w_in[l])
        g_mix = norm_mix[l].reshape(1, D_MODEL).astype(F32)
        z = _in_proj(x, g_mix, wz, tm // in_split, functools.partial(mixer_tile, split=in_split))

        lb = lb_all[l]
        par_hg = _rows(jnp.log(lb), jnp.log1p(-lb), 1.0 - lb, hgrn_norm[l])
        par_gl = _rows(gla_gate_b[l], jnp.zeros((1,)), jnp.zeros((1,)), gla_norm[l])
        gup_gl = jnp.pad(gla_gate_up[l], ((0, 128 - gla_gate_up.shape[1]), (0, 0))).astype(BF16)
        gup_dummy = jnp.zeros((128, 128), BF16)
        par_rw = _rows(rwkv_w0[l], rwkv_a0[l], rwkv_k_k[l], rwkv_k_a[l], rwkv_r_k[l], rwkv_ln_w[l], rwkv_ln_b[l])
        nw, na = rwkv_w_up.shape[1], rwkv_a_up.shape[1]
        wup_p = jnp.pad(rwkv_w_up[l], ((0, 256 - nw), (0, 0))).astype(BF16)
        aup_p = jnp.pad(rwkv_a_up[l], ((nw, 256 - nw - na), (0, 0))).astype(BF16)
        gup_p = jnp.pad(rwkv_g_up[l], ((nw + na, 0), (0, 0))).astype(BF16)
        mu = rwkv_mu[l].reshape(1, SEC).astype(F32)
        gate_b = jnp.concatenate([mlstm_i_b[l], mlstm_f_b[l]]).astype(F32)
        par_ml = _rows(mlstm_conv_b[l], gate_b, mlstm_norm[l])
        bcol_ml = gate_b.reshape(8, 1)
        cw_ml = _rows(*[mlstm_conv_w[l, j] for j in range(CONV_W)])

        o_hg = o_rw = o_gl = o_ml = None
        for g, st, new in ((gp, st_p, new_p), (gs, st_s, new_s)):
            prev = lambda o: jnp.zeros((n_rows, 256), F32) if o is None else o
            o_hg, s_hg = _gla_call(z, 0, par_hg, gup_dummy, ones_vv, ones_vv, mask_hg, st["hg"][l], prev(o_hg),
                                   mode="hgrn", dk=64, nrows=n_rows, g=g)
            o_gl, s_gl = _gla_call(z, 2, par_gl, gup_gl, ones_gl, ones_vv, mask_gl, st["gl"][l], prev(o_gl),
                                   mode="gla", dk=GLA_DK, nrows=n_rows, g=g)
            o_rw, nshift, s_rw = _rwkv_call(z, mu, par_rw, wup_p, aup_p, gup_p, ones_vv, hmask[g.c], ehead[g.c],
                                            mask_pair, st["shift"][l], st["rw"][l], prev(o_rw), nrows=n_rows, g=g)
            o_ml, s_cm, s_nm, s_mm, s_cv = _mlstm_call(z, par_ml, bcol_ml, cw_ml, sel8, ones_vv, st["cm"][l], st["nm"][l],
                                                        st["mm"][l], st["cv"][l], prev(o_ml), nrows=n_rows, g=g)
            for key, val in (("hg", s_hg), ("rw", s_rw), ("shift", nshift), ("gl", s_gl), ("cm", s_cm),
                             ("nm", s_nm), ("mm", s_mm), ("cv", s_cv)):
                new[key].append(val)

        x = _merge_call((o_hg, o_rw, o_gl, o_ml), x, g_mix, wg, w_branch[l].astype(BF16), w_out[l].astype(BF16), tm,
                        mixer_tile)
        x = _mlp_call(x, norm_mlp[l].reshape(1, D_MODEL).astype(F32), norm_final.reshape(1, D_MODEL).astype(F32),
                      w_up[l].astype(BF16), w_down[l].astype(BF16), tm_mlp, final=(l == depth - 1))

    y_prompt = jnp.stack([x[b * tp + N_META:(b + 1) * tp] for b in range(bp)]).astype(dt)
    y_sample = x[gs.row0:gs.row0 + bs * SAMPLE_PAD].reshape(bs, SAMPLE_PAD, D_MODEL)[:, :dseq].astype(dt)

    def finish(new, nb):
        st = {k: jnp.stack(v) for k, v in new.items()}
        return (
            st["hg"].reshape(depth, nb, N_HEADS, 64, 64).transpose(0, 1, 2, 4, 3),
            st["rw"].reshape(depth, nb, N_HEADS, 64, 64),
            st["shift"].reshape(depth, nb, SEC),
            st["gl"].reshape(depth, nb, N_HEADS, 64, GLA_DK).transpose(0, 1, 2, 4, 3),
            st["cm"].reshape(depth, nb, N_HEADS, MLSTM_DK, 64),
            st["nm"].reshape(depth, nb, N_HEADS, MLSTM_DK),
            st["mm"].reshape(depth, nb, 128)[:, :, :N_HEADS],
            st["cv"][:, :, 5:8, :],
        )

    outs_p = tuple(a.astype(dt) for a in finish(new_p, bp))
    outs_s = tuple(a.astype(dt) for a in finish(new_s, bs))
    return (y_prompt, y_sample) + outs_p + outs_s
```

```python
import functools

import numpy as np
import jax
import jax.numpy as jnp
from jax import lax
from jax.experimental import pallas as pl
from jax.experimental.pallas import tpu as pltpu

F32 = jnp.float32
BF16 = jnp.bfloat16

D_MODEL = 1024
N_META = 16
N_HEADS = 4
HEAD_V = 64
GLA_DK = 32
MLSTM_DK = 32
GLA_GATE_TAU = 16.0
RWKV_LN_EPS = 64e-5
CONV_W = 4
D_FF = 4 * D_MODEL
EPS = 1e-6

SEC = 1024
Z_COLS = 4 * SEC
SUB = 16
SAMPLE_PAD = 8
MLSTM_CHUNK = 128
VMEM_LIMIT = 56 * 1024 * 1024


def _cparams(*sem):
    return pltpu.CompilerParams(dimension_semantics=sem, vmem_limit_bytes=VMEM_LIMIT)


def _pick_tile(n, max_tile, mult=8):
    best = None
    t = mult
    while t <= min(n, max_tile):
        if n % t == 0:
            best = t
        t += mult
    assert best is not None, (n, max_tile, mult)
    return best


def _split2(x):
    hi = x.astype(BF16)
    lo = (x - hi.astype(F32)).astype(BF16)
    return hi, lo


def _split3(x):
    hi = x.astype(BF16)
    r1 = x - hi.astype(F32)
    mid = r1.astype(BF16)
    lo = (r1 - mid.astype(F32)).astype(BF16)
    return hi, mid, lo


def _dot(a, b):
    return jnp.dot(a, b, preferred_element_type=F32)


def _dot_nt(a, b):
    return lax.dot_general(a, b, (((1,), (1,)), ((), ())), preferred_element_type=F32)


def _dot_tn(a, b):
    return lax.dot_general(a, b, (((0,), (0,)), ((), ())), preferred_element_type=F32)


def _seg_dot(x, m_bf16, parts):
    ps = _split2(x) if parts == 2 else _split3(x)
    acc = _dot(ps[0], m_bf16)
    for p in ps[1:]:
        acc = acc + _dot(p, m_bf16)
    return acc


def _seg_dot_l(m_bf16, x, parts):
    ps = _split2(x) if parts == 2 else _split3(x)
    acc = _dot(m_bf16, ps[0])
    for p in ps[1:]:
        acc = acc + _dot(m_bf16, p)
    return acc


def _log_sigmoid(x):
    return jnp.minimum(x, 0.0) - jnp.log1p(jnp.exp(-jnp.abs(x)))


def _silu(x):
    return x * jax.nn.sigmoid(x)


def _tri(n, upper=False):
    r = lax.broadcasted_iota(jnp.int32, (n, n), 0)
    c = lax.broadcasted_iota(jnp.int32, (n, n), 1)
    return (r <= c) if upper else (r >= c)


def _cumsum_rows(x, rowi):
    d = 1
    while d < x.shape[0]:
        x = x + jnp.where(rowi >= d, pltpu.roll(x, d, axis=0), 0.0)
        d *= 2
    return x


def _rms_bf16(x, g):
    return (x * lax.rsqrt(jnp.mean(x * x, axis=-1, keepdims=True) + EPS) * g).astype(BF16)


def _in_proj_kernel(x_ref, g_ref, w_ref, z_ref):
    h = _rms_bf16(x_ref[...], g_ref[...])
    for j in range(Z_COLS // SEC):
        z_ref[:, j * SEC:(j + 1) * SEC] = _dot(h, w_ref[:, j * SEC:(j + 1) * SEC])


def _in_proj(x, g, w, tm, out_tile):
    n = x.shape[0]
    return pl.pallas_call(
        _in_proj_kernel,
        grid=(n // tm,),
        in_specs=[
            pl.BlockSpec((tm, D_MODEL), lambda i: (i, 0)),
            pl.BlockSpec((1, D_MODEL), lambda i: (0, 0)),
            pl.BlockSpec((D_MODEL, Z_COLS), lambda i: (0, 0)),
        ],
        out_specs=pl.BlockSpec((tm, Z_COLS), lambda i: (out_tile(i), 0)),
        out_shape=jax.ShapeDtypeStruct((n, Z_COLS), F32),
        compiler_params=_cparams("parallel"),
    )(x, g, w)


def _gla_kernel(z_ref, par_ref, gup_ref, okv_ref, ovv_ref, mask_ref, st_in_ref, o_prev_ref,
                o_ref, st_out_ref, *st_scrs, mode, c, nsub, tv, dk, nseq, rps, nst, pc):
    del o_prev_ref
    voff = 512 if mode == "hgrn" else 256
    rowi = lax.broadcasted_iota(jnp.int32, (c, 1), 0)
    valid = rowi < tv
    hpg = 128 // dk
    groups = [(slice(i * hpg * HEAD_V, (i + 1) * hpg * HEAD_V), slice(i * 128, (i + 1) * 128))
              for i in range(N_HEADS // hpg)]

    def sub_block(st_scr, r0):
        rows = pl.ds(r0, c)
        if mode == "hgrn":
            q = z_ref[rows, 0:256]
            hf = z_ref[rows, 256:512]
            g = z_ref[rows, 768:1024]
            a = par_ref[0:1, :]
            cc = par_ref[1:2, :] + _log_sigmoid(hf)
            loga = jnp.maximum(a, cc) + jnp.log1p(jnp.exp(-jnp.abs(a - cc)))
            k = par_ref[2:3, :] * jax.nn.sigmoid(-hf)
        else:
            q = z_ref[rows, 0:128] * (GLA_DK ** -0.5)
            k = z_ref[rows, 128:256]
            ga = z_ref[rows, 512:640]
            g = z_ref[rows, 640:896]
            gl = _dot(ga.astype(BF16), gup_ref[...]) + par_ref[0:1, 0:128]
            loga = _log_sigmoid(gl) * (1.0 / GLA_GATE_TAU)
        v = z_ref[rows, voff:voff + 256]
        if tv < c:
            loga = jnp.where(valid, loga, 0.0)
            k = jnp.where(valid, k, 0.0)
            v = jnp.where(valid, v, 0.0)
        b = _cumsum_rows(loga, rowi)

        qe = (q * jnp.exp(b)).astype(BF16)
        o = jnp.concatenate([_dot_nt(qe[:, ks], scr[...].astype(BF16))
                             for scr, (_, ks) in zip(st_scr, groups)], axis=1)

        lo_rows = [8 * (s // 8) for s in range(tv)]
        pieces = []
        for s in range(tv):
            rs = slice(lo_rows[s], c)
            e = jnp.exp(jnp.where(rowi[rs] >= s, b[rs] - b[s:s + 1, :], -jnp.inf))
            pieces.append(q[rs] * e * k[s:s + 1, :])
        return rows, lo_rows, k, v, b, o, jnp.concatenate(pieces, axis=0)

    def sub_block_finish(st_scr, part, sm):
        rows, lo_rows, k, v, b, o, _ = part
        acc = {}
        off = 0
        for s in range(tv):
            n = c - lo_rows[s]
            contrib = sm[off:off + n] * v[s:s + 1, :]
            off += n
            acc[lo_rows[s]] = contrib if lo_rows[s] not in acc else acc[lo_rows[s]] + contrib
        for lo, a_ in acc.items():
            o = o + (a_ if lo == 0 else jnp.concatenate([jnp.zeros((lo, 256), F32), a_], axis=0))

        bl = b[c - 1:c, :]
        kdec = (k * jnp.exp(bl - b)).astype(BF16)
        vb = v.astype(BF16)
        ebl = jnp.exp(bl)
        upd = [_dot_tn(vb[:, vs], kdec[:, ks]) for vs, ks in groups]
        for scr, u_, (_, ks) in zip(st_scr, upd, groups):
            scr[...] = scr[...] * ebl[:, ks] + u_ * mask_ref[...]
        o_ref[rows, :] = o

    def sub_blocks(scrs, row0s):
        parts = [sub_block(scr, r0) for scr, r0 in zip(scrs, row0s)]
        sm = _dot(jnp.concatenate([p[-1] for p in parts], axis=0).astype(BF16), okv_ref[...])
        n = parts[0][-1].shape[0]
        for i, (scr, part) in enumerate(zip(scrs, parts)):
            sub_block_finish(scr, part, sm[i * n:(i + 1) * n])

    def load_state(st_scr, seq):
        for scr, (vs, _) in zip(st_scr, groups):
            scr[...] = jnp.concatenate([st_in_ref[seq, vs, :]] * hpg, axis=1) * mask_ref[...]

    def store_state(st_scr, seq):
        for scr, (vs, _) in zip(st_scr, groups):
            st = scr[...]
            acc = st[:, 0:dk]
            for h in range(1, hpg):
                acc = acc + st[:, h * dk:(h + 1) * dk]
            st_out_ref[seq, vs, :] = acc

    ng = len(groups)
    lane_scrs = [st_scrs[i * ng:(i + 1) * ng] for i in range(len(st_scrs) // ng)]
    _run_lanes(lane_scrs, nseq, rps, nsub, c, nst, load_state, store_state, sub_blocks)

    goff = 768 if mode == "hgrn" else 640
    nchunk = (nseq * rps) // pc

    def finish_chunk(ci, carry):
        rows = pl.ds(ci * pc if isinstance(ci, int) else pl.multiple_of(ci * pc, 8), pc)
        o = o_ref[rows, :]
        ms = _seg_dot(o * o, ovv_ref[...], 2) * (1.0 / HEAD_V)
        o_ref[rows, :] = o * lax.rsqrt(ms + EPS) * par_ref[3:4, :] * _silu(z_ref[rows, goff:goff + 256])
        return carry

    if nchunk == 1:
        finish_chunk(0, 0)
    else:
        lax.fori_loop(0, nchunk, finish_chunk, 0)


def _run_lanes(scrs, nseq, rps, nsub, c, nst, load_state, store_state, sub_blocks, group=0):
    lanes = len(scrs)
    assert nseq == lanes or nst == 1
    step = pl.program_id(1)

    def seq_group(sg, carry):
        seqs = [sg * lanes + u for u in range(lanes)]
        for scr, seq in zip(scrs, seqs):
            if nst == 1:
                load_state(scr, seq)
            else:
                pl.when(step == 0)(functools.partial(load_state, scr, seq))

        def starts(j):
            return [seq * rps + j * c if isinstance(seq * rps + j * c, int)
                    else pl.multiple_of(seq * rps + j * c, 8) for seq in seqs]

        def blocks(j, carry2):
            if group:
                sub_blocks(scrs, [starts(j * group + g_) for g_ in range(group)])
            else:
                sub_blocks(scrs, starts(j))
            return carry2

        if group:
            if nsub // group == 1:
                blocks(0, 0)
            elif nsub // group > 1:
                lax.fori_loop(0, nsub // group, blocks, 0, unroll=2 if group == 1 else 1)
            for j in range(nsub - nsub % group, nsub):
                sub_blocks(scrs, [starts(j)])
        elif nsub == 1:
            blocks(0, 0)
        else:
            lax.fori_loop(0, nsub, blocks, 0, unroll=2)
        for scr, seq in zip(scrs, seqs):
            if nst == 1:
                store_state(scr, seq)
            else:
                pl.when(step == nst - 1)(functools.partial(store_state, scr, seq))
        return carry

    if nseq == lanes:
        seq_group(0, 0)
    else:
        lax.fori_loop(0, nseq // lanes, seq_group, 0)


def _gla_call(z, sec, par, gup, okv, ovv, mask, st_in, o_prev, *, mode, dk, nrows, g):
    hk = N_HEADS * dk
    grows = (128 // dk) * HEAD_V
    kern = functools.partial(_gla_kernel, mode=mode, c=g.c, nsub=g.nsub, tv=g.tvs, dk=dk,
                             nseq=g.nseq, rps=g.rps, nst=g.nst, pc=g.pc)
    const = lambda b, s: (0, 0)
    return pl.pallas_call(
        kern,
        grid=(g.ngrid, g.nst),
        in_specs=[
            pl.BlockSpec((g.rows, SEC), lambda b, s: (g.base_blk + b * g.nst + s, sec)),
            pl.BlockSpec((8, 256), const),
            pl.BlockSpec((128, 128), const),
            pl.BlockSpec((hk, 256), const),
            pl.BlockSpec((256, 256), const),
            pl.BlockSpec((grows, 128), const),
            pl.BlockSpec((g.nseq, 256, dk), lambda b, s: (b, 0, 0)),
            pl.BlockSpec(memory_space=pl.ANY),
        ],
        input_output_aliases={7: 0},
        out_specs=[
            pl.BlockSpec((g.rows, 256), lambda b, s: (g.base_blk + b * g.nst + s, 0)),
            pl.BlockSpec((g.nseq, 256, dk), lambda b, s: (b, 0, 0)),
        ],
        out_shape=[
            jax.ShapeDtypeStruct((nrows, 256), F32),
            jax.ShapeDtypeStruct((g.nb, 256, dk), F32),
        ],
        scratch_shapes=[pltpu.VMEM((grows, 128), F32)] * (g.lanes * (256 // grows)),
        compiler_params=_cparams("parallel", "arbitrary"),
    )(z, par, gup, okv, ovv, mask, st_in, o_prev)


def _rwkv_kernel(z_ref, mu_ref, par_ref, wup_ref, aup_ref, gup_ref, ovv_ref, hmask_ref, ehead_ref, mask_ref,
                 shift_ref, st_in_ref, o_prev_ref,
                 o_ref, nshift_ref, st_out_ref,
                 prev_scr, fix_scr, r_scr, lw_scr, k_scr, v_scr, kk_scr, ka_scr, g_scr, bonus_scr, y_scr, *st_scrs,
                 c, nsub, tv, nseq, rps, nst, pc):
    del o_prev_ref
    step = pl.program_id(1)

    def shifted_chunk(ci):
        r0 = ci * pc if isinstance(ci, int) else pl.multiple_of(ci * pc, 8)
        zr = z_ref[pl.ds(r0, pc), :]
        rolled = pltpu.roll(zr, 1, axis=0)
        if pc <= rps:
            cps = rps // pc
            seq = ci // cps
            rowi = lax.broadcasted_iota(jnp.int32, (pc, 1), 0)
            before = z_ref[pl.ds(pl.multiple_of(jnp.maximum(r0 - 8, 0), 8), 8), :][7:8, :]
            first = jnp.where(ci % cps == 0, prev_scr[pl.ds(seq, 1), :], before)
            prev = jnp.where(rowi == 0, first, rolled)
            prev_scr[pl.ds(seq, 1), :] = zr[pc - 1:pc, :]
            nshift_ref[seq] = zr[pc - 1:pc, :]
        else:
            fix_scr[...] = rolled
            for i in range(nseq):
                fix_scr[i * rps:i * rps + 1, :] = shift_ref[i]
                nshift_ref[i] = zr[i * rps + tv - 1:i * rps + tv, :]
            prev = fix_scr[...]
        return r0, zr, prev

    if pc <= rps:
        @pl.when(step == 0)
        def _():
            for i in range(nseq):
                prev_scr[i:i + 1, :] = shift_ref[i]

    def prep_chunk(ci, carry):
        r0, zr, prev = shifted_chunk(ci)
        rows = pl.ds(r0, pc)
        _rwkv_prep_rows(zr, prev, rows)
        return carry

    def _rwkv_prep_rows(zr, prev, rows):
        nrow = zr.shape[0]
        zs = zr + (prev - zr) * mu_ref[...]
        r = zs[:, 0:256]
        k = zs[:, 256:512]
        v = zs[:, 512:768]
        lo = zs[:, 768:1024]
        w0, a0, k_k, k_a, r_k = (par_ref[i:i + 1, :] for i in range(5))
        wl = w0 + _dot(jnp.tanh(lo).astype(BF16), wup_ref[...])
        wexp = -(jnp.maximum(-wl, 0.0) + jnp.log1p(jnp.exp(-jnp.abs(wl)))) - 0.5
        lw = -jnp.exp(wexp)
        a = jax.nn.sigmoid(a0 + _dot(lo.astype(BF16), aup_ref[...]))
        g_scr[rows, :] = _dot(jax.nn.sigmoid(lo).astype(BF16), gup_ref[...])
        kkp = k * k_k
        nrm = jnp.sqrt(_seg_dot(kkp * kkp, ovv_ref[...], 2))
        kk = kkp / jnp.maximum(nrm, 1e-12)
        k2 = k * (1.0 + (a - 1.0) * k_a)
        bonus_scr[rows, :] = _seg_dot(r * k2 * r_k, ovv_ref[...], 3) * v
        ka = kk * a
        if tv < rps:
            okrow = lax.rem(lax.broadcasted_iota(jnp.int32, (nrow, 1), 0), rps) < tv
            lw, k2, v, kk, ka = (jnp.where(okrow, t_, 0.0) for t_ in (lw, k2, v, kk, ka))
        r_scr[rows, :] = r
        lw_scr[rows, :] = lw
        k_scr[rows, :] = k2
        v_scr[rows, :] = v
        kk_scr[rows, :] = kk
        ka_scr[rows, :] = ka

    nchunk = (nseq * rps) // pc
    if nchunk == 1:
        prep_chunk(0, 0)
    else:
        lax.fori_loop(0, nchunk, prep_chunk, 0)

    hp = 2
    ng = N_HEADS // hp
    gw = hp * 64
    rowc = lax.broadcasted_iota(jnp.int32, (c, 1), 0)
    src2 = lax.broadcasted_iota(jnp.int32, (1, 2 * hp * c), 1) & (c - 1)
    src = src2[:, 0:hp * c]
    strict = rowc > src
    incl = rowc >= src

    def sub_blocks(lane_scrs, row0_lists):
        staged = [stage_free(lane_scrs, row0s) for row0s in row0_lists]
        for st_ in staged:
            stage_state(*st_)

    def stage_free(lane_scrs, row0s):
        hm = hmask_ref[...]
        tile = lambda t_: jnp.concatenate([t_] * hp, axis=0) * hm
        nc = hp * c
        nsolve = min(tv, c - 1)
        units, cx = [], {}
        for i, r0 in enumerate(row0s):
            rows = pl.ds(r0, c)
            lw_, r_, k_, v_, kk_, ka_ = (s_[rows, :] for s_ in (lw_scr, r_scr, k_scr, v_scr, kk_scr, ka_scr))
            gam = _cumsum_rows(lw_, rowc)
            gl = gam[c - 1:c, :]
            ginv = jnp.exp(-gam)
            gend = jnp.exp(gl - gam)
            lh, ll = _split2(jnp.concatenate([kk_ * jnp.exp(gam - lw_), r_ * jnp.exp(gam)], axis=0))
            a_, kq_ = ka_ * ginv, k_ * ginv
            ae_, ke_ = ka_ * gend, k_ * gend
            for p in range(ng):
                ls = slice(p * gw, (p + 1) * gw)
                rh, rl = _split2(jnp.concatenate([tile(a_[:, ls]), tile(kq_[:, ls])], axis=0))
                units.append((i, p))
                cx[i, p] = dict(rows=rows, ls=ls, v=v_[:, ls], egl=jnp.exp(gl[:, ls]), lh=lh[:, ls], ll=ll[:, ls],
                                rh=rh, rl=rl, vbig=tile(v_[:, ls]).astype(BF16), scr=lane_scrs[i][p],
                                end=jnp.concatenate([ae_[:, ls], ke_[:, ls]], axis=0).astype(BF16))
        g2 = {u_: _dot_nt(jnp.concatenate([cx[u_]["lh"], cx[u_]["ll"]], axis=0), cx[u_]["rh"]) for u_ in units}
        g3 = {u_: _dot_nt(cx[u_]["lh"], cx[u_]["rl"]) for u_ in units}
        m_ab, m_ra, m_rk, bkv = {}, {}, {}, {}
        for u_ in units:
            gm = g2[u_][0:2 * c] + g2[u_][2 * c:4 * c] + g3[u_]
            m_ab[u_] = jnp.where(strict, gm[0:c, 0:nc], 0.0)
            m_ra[u_] = jnp.where(incl, gm[c:2 * c, 0:nc], 0.0)
            m_rk[u_] = jnp.where(incl, gm[c:2 * c, nc:2 * nc], 0.0)
            bkv[u_] = _dot(jnp.where(strict, gm[0:c, nc:2 * nc], 0.0).astype(BF16), cx[u_]["vbig"])
        m2 = {u_: jnp.concatenate(_split2(m_ab[u_]), axis=1) for u_ in units}
        cols = _dot(jnp.concatenate([jnp.where(src2 == s, m2[u_], jnp.zeros_like(m2[u_]))
                                     for u_ in units for s in range(nsolve)], axis=0), ehead_ref[...])
        return units, cx, m_ra, m_rk, bkv, cols

    def stage_state(units, cx, m_ra, m_rk, bkv, cols):
        nsolve = min(tv, c - 1)
        tile = lambda t_: jnp.concatenate([t_] * hp, axis=0) * hmask_ref[...]
        x0 = {u_: _dot_nt(cx[u_]["lh"], cx[u_]["scr"][...].astype(BF16)) for u_ in units}
        usol = {u_: x0[u_][0:c] + bkv[u_] for u_ in units}
        for s in range(nsolve):
            for n_, u_ in enumerate(units):
                o_ = (n_ * nsolve + s) * c
                usol[u_] = usol[u_] - cols[o_:o_ + c] * usol[u_][s:s + 1, :]
        ys = {u_: _dot(jnp.concatenate([-m_ra[u_], m_rk[u_]], axis=1).astype(BF16),
                       jnp.concatenate([tile(usol[u_]).astype(BF16), cx[u_]["vbig"]], axis=0)) for u_ in units}
        upd = {u_: _dot_tn(jnp.concatenate([-usol[u_], cx[u_]["v"]], axis=0).astype(BF16), cx[u_]["end"])
               for u_ in units}
        for u_ in units:
            x = cx[u_]
            y_scr[x["rows"], x["ls"]] = x0[u_][c:2 * c] + ys[u_]
            x["scr"][...] = x["scr"][...] * x["egl"] + upd[u_] * mask_ref[...]

    def load_state(scrs, seq):
        for p, scr in enumerate(scrs):
            scr[...] = jnp.concatenate([st_in_ref[seq, p * gw:(p + 1) * gw, :]] * hp, axis=1) * mask_ref[...]

    def store_state(scrs, seq):
        for p, scr in enumerate(scrs):
            st = scr[...]
            st_out_ref[seq, p * gw:(p + 1) * gw, :] = st[:, 0:64] + st[:, 64:128]

    lane_scrs = [st_scrs[i * ng:(i + 1) * ng] for i in range(len(st_scrs) // ng)]
    _run_lanes(lane_scrs, nseq, rps, nsub, c, nst, load_state, store_state, sub_blocks, group=1)

    def finish_chunk(ci, carry):
        rows = pl.ds(ci * pc if isinstance(ci, int) else pl.multiple_of(ci * pc, 8), pc)
        y = y_scr[rows, :]
        mean = _seg_dot(y, ovv_ref[...], 3) * (1.0 / HEAD_V)
        d = y - mean
        var = _seg_dot(d * d, ovv_ref[...], 2) * (1.0 / HEAD_V)
        ln = d * lax.rsqrt(var + RWKV_LN_EPS) * par_ref[5:6, :] + par_ref[6:7, :]
        o_ref[rows, :] = (ln + bonus_scr[rows, :]) * g_scr[rows, :]
        return carry

    if nchunk == 1:
        finish_chunk(0, 0)
    else:
        lax.fori_loop(0, nchunk, finish_chunk, 0)


def _rwkv_call(z, mu, par, wup, aup, gup, ovv, hmask, ehead, mask, shift, st_in, o_prev, *, nrows, g):
    rblk = g.rows
    nsteps = g.nst
    base = g.base_blk
    pc = g.pc
    kern = functools.partial(_rwkv_kernel, c=g.c, nsub=g.nsub, tv=g.tv, nseq=g.nseq, rps=g.rps, nst=g.nst, pc=pc)
    const = lambda b, s: (0, 0)
    return pl.pallas_call(
        kern,
        grid=(g.ngrid, nsteps),
        in_specs=[
            pl.BlockSpec((rblk, SEC), lambda b, s: (base + b * nsteps + s, 1),
                         pipeline_mode=pl.Buffered(1) if g.lanes > 2 and g.full else None),
            pl.BlockSpec((1, SEC), const),
            pl.BlockSpec((8, 256), const),
            pl.BlockSpec((256, 256), const),
            pl.BlockSpec((256, 256), const),
            pl.BlockSpec((256, 256), const),
            pl.BlockSpec((256, 256), const),
            pl.BlockSpec((2 * g.c, 128), const),
            pl.BlockSpec((4 * g.c, 128), const),
            pl.BlockSpec((128, 128), const),
            pl.BlockSpec((g.nseq, 1, SEC), lambda b, s: (b, 0, 0)),
            pl.BlockSpec((g.nseq, 256, 64), lambda b, s: (b, 0, 0)),
            pl.BlockSpec(memory_space=pl.ANY),
        ],
        input_output_aliases={12: 0},
        out_specs=[
            pl.BlockSpec((rblk, 256), lambda b, s: (base + b * nsteps + s, 0)),
            pl.BlockSpec((g.nseq, 1, SEC), lambda b, s: (b, 0, 0)),
            pl.BlockSpec((g.nseq, 256, 64), lambda b, s: (b, 0, 0)),
        ],
        out_shape=[
            jax.ShapeDtypeStruct((nrows, 256), F32),
            jax.ShapeDtypeStruct((g.nb, 1, SEC), F32),
            jax.ShapeDtypeStruct((g.nb, 256, 64), F32),
        ],
        scratch_shapes=[pltpu.VMEM((max(8, g.nseq), SEC), F32), pltpu.VMEM((pc if pc > g.rps else 8, SEC), F32)]
        + [pltpu.VMEM((rblk, 256), F32)] * 9 + [pltpu.VMEM((128, 128), F32)] * (2 * g.lanes),
        compiler_params=_cparams("parallel", "arbitrary"),
    )(z, mu, par, wup, aup, gup, ovv, hmask, ehead, mask, shift, st_in, o_prev)


def _mlstm_kernel(z_ref, par_ref, bcol_ref, cw_ref, sel_ref, ovv_ref, cm_in, nm_in, mm_in, cv_in, o_prev_ref,
                  o_ref, cm_out, nm_out, mm_out, cv_out,
                  *scratch, chunks, nseq, rps, nst, pc):
    del o_prev_ref
    lanes = scratch[0].shape[0]

    def chunk_gates(scr, row0, c, tv):
        xbuf, cm_scr, nm_scr, mm_scr = scr
        rows = pl.ds(row0, c)
        mqk = z_ref[rows, 0:256]
        mv = z_ref[rows, 256:512]
        gi = z_ref[rows, 512:640]
        mo = z_ref[rows, 640:896]
        xbuf[8:8 + c, :] = mqk
        conv = par_ref[0:1, :] + xbuf[5:5 + c, :] * cw_ref[0:1, :]
        for j in range(1, CONV_W):
            conv = conv + xbuf[5 + j:5 + j + c, :] * cw_ref[j:j + 1, :]
        xbuf[5:8, :] = xbuf[8 + tv - 3:8 + tv, :]
        act = _silu(conv)
        q = act[:, 0:128]
        k = act[:, 128:256] * (MLSTM_DK ** -0.5)

        rowi = lax.broadcasted_iota(jnp.int32, (c, 1), 0)
        coli = lax.broadcasted_iota(jnp.int32, (1, c), 1)
        gcol = gi + par_ref[1:2, 0:128]
        lfc = _log_sigmoid(gcol)
        graw = None
        for part in _split3(gi):
            t_ = _dot_nt(sel_ref[...], part)
            graw = t_ if graw is None else graw + t_
        grow = graw + bcol_ref[...]
        lfr = _log_sigmoid(grow)
        if tv < c:
            gcol = jnp.where(rowi < tv, gcol, -jnp.inf)
            lfc = jnp.where(rowi < tv, lfc, 0.0)
            grow = jnp.where(coli < tv, grow, -jnp.inf)
            lfr = jnp.where(coli < tv, lfr, 0.0)
        b_col = _seg_dot_l(_tri(c).astype(BF16), lfc, 3)
        b_row = _seg_dot(lfr, _tri(c, upper=True).astype(BF16), 3)
        return dict(rows=rows, q=q, k=k, mv=mv, mo=mo, gcol=gcol, grow=grow, b_col=b_col, b_row=b_row,
                    mm=mm_scr[...], nm=nm_scr[...], cm=cm_scr[...])

    def chunk(scrs, row0s, c, tv):
        ctx = [chunk_gates(scr, row0, c, tv) for scr, row0 in zip(scrs, row0s)]
        causal = _tri(c)
        lane = lax.broadcasted_iota(jnp.int32, (1, 128), 1)
        items = [(l, h) for l in range(len(ctx)) for h in range(N_HEADS)]
        ks = lambda h: slice(h * MLSTM_DK, (h + 1) * MLSTM_DK)
        vs = lambda h: slice(h * HEAD_V, (h + 1) * HEAD_V)
        qh = {(l, h): ctx[l]["q"][:, ks(h)] for l, h in items}
        kh = {(l, h): ctx[l]["k"][:, ks(h)] for l, h in items}
        vh = {(l, h): ctx[l]["mv"][:, vs(h)].astype(BF16) for l, h in items}
        cmh = {(l, h): ctx[l]["cm"][ks(h), :] for l, h in items}
        nh = {(l, h): ctx[l]["nm"][0:1, ks(h)] for l, h in items}
        qk_raw = {it: _dot_nt(qh[it].astype(BF16), kh[it].astype(BF16)) for it in items}
        q_cm = {it: _dot(qh[it].astype(BF16), cmh[it].astype(BF16)) for it in items}
        bc, ic, mprev, m_t, sc, qk = {}, {}, {}, {}, {}, {}
        for it in items:
            l, h = it
            bc[it] = ctx[l]["b_col"][:, 4 + h:5 + h]
            br = ctx[l]["b_row"][4 + h:5 + h, :]
            ir = ctx[l]["grow"][h:h + 1, :]
            ic[it] = ctx[l]["gcol"][:, h:h + 1]
            dmat = jnp.where(causal, bc[it] - br + ir, -jnp.inf)
            mprev[it] = ctx[l]["mm"][0:1, h:h + 1]
            inter = bc[it] + mprev[it]
            m_t[it] = jnp.maximum(inter, jnp.max(dmat, axis=1, keepdims=True))
            sc[it] = jnp.exp(inter - m_t[it])
            qk[it] = qk_raw[it] * jnp.exp(dmat - m_t[it])
        num = {it: _dot(qk[it].astype(BF16), vh[it]) + sc[it] * q_cm[it] for it in items}
        qk_sum = _seg_dot(jnp.concatenate([qk[it] for it in items], axis=0), jnp.ones((c, HEAD_V), BF16), 2)
        qn_sum = _seg_dot(jnp.concatenate([qh[it] * nh[it] for it in items], axis=0),
                          jnp.ones((MLSTM_DK, HEAD_V), BF16), 2)
        kw, sl, m_new, hn = {}, {}, {}, {}
        for n_, it in enumerate(items):
            den = qk_sum[n_ * c:(n_ + 1) * c] + sc[it] * qn_sum[n_ * c:(n_ + 1) * c]
            hc = num[it] / jnp.maximum(jnp.abs(den), jnp.exp(-m_t[it]))
            m_new[it] = m_t[it][tv - 1:tv, :]
            bl = bc[it][tv - 1:tv, :]
            wl = jnp.exp(bl - bc[it] + ic[it] - m_new[it])
            sl[it] = jnp.exp(bl + mprev[it] - m_new[it])
            kw[it] = kh[it] * wl
            hn[it] = hc
        kv = {it: _dot_tn(kw[it].astype(BF16), vh[it]) for it in items}
        for l, (scr, cx) in enumerate(zip(scrs, ctx)):
            xbuf, cm_scr, nm_scr, mm_scr = scr
            heads = [(l, h) for h in range(N_HEADS)]
            cm_scr[...] = jnp.concatenate([sl[it] * cmh[it] + kv[it] for it in heads], axis=0)
            nm_scr[...] = jnp.concatenate([sl[it] * nh[it] + jnp.sum(kw[it], axis=0, keepdims=True)
                                           for it in heads], axis=1)
            mm_new = cx["mm"]
            for it in heads:
                mm_new = jnp.where(lane == it[1], m_new[it], mm_new)
            mm_scr[...] = mm_new
            o_ref[cx["rows"], :] = jnp.concatenate([hn[it] for it in heads], axis=1)

    def load_state(scr, seq):
        xbuf, cm_scr, nm_scr, mm_scr = scr
        cm_scr[...] = cm_in[seq]
        nm_scr[...] = nm_in[seq]
        mm_scr[...] = mm_in[seq]
        xbuf[0:8, :] = cv_in[seq]

    def store_state(scr, seq):
        xbuf, cm_scr, nm_scr, mm_scr = scr
        cm_out[seq] = cm_scr[...]
        nm_out[seq] = nm_scr[...]
        mm_out[seq] = mm_scr[...]
        cv_out[seq] = xbuf[0:8, :]

    assert nseq == lanes or nst == 1
    step = pl.program_id(1)

    def lane_scr(u):
        return tuple(s_.at[u] for s_ in scratch)

    def run_chunks(grp):
        for start, c, tv, count in chunks:
            def body(ci, carry2, start=start, c=c, tv=tv):
                firsts = [seq * rps + start + ci * c for _, seq in grp]
                chunk([scr for scr, _ in grp],
                      [f if isinstance(f, int) else pl.multiple_of(f, 8) for f in firsts], c, tv)
                return carry2

            if count == 1:
                body(0, 0)
            else:
                lax.fori_loop(0, count, body, 0)

    def seq_group(sg, carry):
        def lane(u, seq):
            scr = lane_scr(u)
            if nst == 1:
                load_state(scr, seq)
            else:
                pl.when(step == 0)(functools.partial(load_state, scr, seq))
            return scr

        def finish(scr, seq):
            if nst == 1:
                store_state(scr, seq)
            else:
                pl.when(step == nst - 1)(functools.partial(store_state, scr, seq))

        if max(c for _, c, _, _ in chunks) <= 16:
            grp = [(lane(u, sg * lanes + u), sg * lanes + u) for u in range(lanes)]
            run_chunks(grp)
            for scr, seq in grp:
                finish(scr, seq)
        else:
            def one_lane(u, carry2):
                seq = sg * lanes + u
                scr = lane(u, seq)
                run_chunks([(scr, seq)])
                finish(scr, seq)
                return carry2

            lax.fori_loop(0, lanes, one_lane, 0)
        return carry

    if nseq == lanes:
        seq_group(0, 0)
    else:
        lax.fori_loop(0, nseq // lanes, seq_group, 0)

    nchunk = (nseq * rps) // pc

    def finish_chunk(ci, carry):
        rows = pl.ds(ci * pc if isinstance(ci, int) else pl.multiple_of(ci * pc, 8), pc)
        o = o_ref[rows, :]
        ms = _seg_dot(o * o, ovv_ref[...], 2) * (1.0 / HEAD_V)
        o_ref[rows, :] = o * lax.rsqrt(ms + EPS) * par_ref[2:3, :] * jax.nn.sigmoid(z_ref[rows, 640:896])
        return carry

    if nchunk == 1:
        finish_chunk(0, 0)
    else:
        lax.fori_loop(0, nchunk, finish_chunk, 0)


def _mlstm_call(z, par, bcol, cw, sel, ovv, cm, nm, mm, cv, o_prev, *, nrows, g):
    if g.full:
        cmain = max(d for d in (MLSTM_CHUNK, 64, 32, 16, 8) if d <= g.rps)
        lead = g.rps % cmain
        chunks = ([(0, lead, lead, 1)] if lead else []) + [(lead, cmain, cmain, g.rps // cmain)]
        assert lead % 8 == 0
    else:
        chunks = [(0, g.rps, g.tv, 1)]
    cmax = max(c for _, c, _, _ in chunks)
    kern = functools.partial(_mlstm_kernel, chunks=tuple(chunks), nseq=g.nseq, rps=g.rps, nst=g.nst, pc=g.pc)
    const = lambda b, s: (0, 0)
    st3 = lambda shp: pl.BlockSpec((g.nseq,) + shp, lambda b, s: (b, 0, 0))
    return pl.pallas_call(
        kern,
        grid=(g.ngrid, g.nst),
        in_specs=[
            pl.BlockSpec((g.rows, SEC), lambda b, s: (g.base_blk + b * g.nst + s, 3)),
            pl.BlockSpec((8, 256), const),
            pl.BlockSpec((8, 1), const),
            pl.BlockSpec((8, 256), const),
            pl.BlockSpec((8, 128), const),
            pl.BlockSpec((256, 256), const),
            st3((128, 64)), st3((1, 128)), st3((1, 128)), st3((8, 256)),
            pl.BlockSpec(memory_space=pl.ANY),
        ],
        input_output_aliases={10: 0},
        out_specs=[
            pl.BlockSpec((g.rows, 256), lambda b, s: (g.base_blk + b * g.nst + s, 0)),
            st3((128, 64)), st3((1, 128)), st3((1, 128)), st3((8, 256)),
        ],
        out_shape=[
            jax.ShapeDtypeStruct((nrows, 256), F32),
            jax.ShapeDtypeStruct((g.nb, 128, 64), F32),
            jax.ShapeDtypeStruct((g.nb, 1, 128), F32),
            jax.ShapeDtypeStruct((g.nb, 1, 128), F32),
            jax.ShapeDtypeStruct((g.nb, 8, 256), F32),
        ],
        scratch_shapes=[
            pltpu.VMEM((g.lanes, 8 + cmax, 256), F32),
            pltpu.VMEM((g.lanes, 128, 64), F32),
            pltpu.VMEM((g.lanes, 1, 128), F32),
            pltpu.VMEM((g.lanes, 1, 128), F32),
        ],
        compiler_params=_cparams("parallel", "arbitrary"),
    )(z, par, bcol, cw, sel, ovv, cm, nm, mm, cv, o_prev)


def _merge_kernel(o0_ref, o1_ref, o2_ref, o3_ref, x_ref, g_ref, wg_ref, wb_ref, wout_ref, out_ref, h_scr, acc_scr):
    n = pl.program_id(1)

    @pl.when(n == 0)
    def _():
        h_scr[...] = _rms_bf16(x_ref[...], g_ref[...])
        acc_scr[...] = jnp.zeros_like(acc_scr)

    gate = jax.nn.sigmoid(_dot(h_scr[...], wg_ref[...]))
    for idx, o_ref in enumerate((o0_ref, o1_ref, o2_ref, o3_ref)):
        @pl.when(n == idx)
        def _(o_ref=o_ref):
            acc_scr[...] += _dot(o_ref[...].astype(BF16), wb_ref[0]) * gate

    @pl.when(n == 3)
    def _():
        out_ref[...] = x_ref[...] + _dot(acc_scr[...].astype(BF16), wout_ref[...])


def _merge_call(outs, x, g, wg, wb, wout, tm, o_tile):
    n = x.shape[0]
    o_spec = pl.BlockSpec((tm, 256), lambda i, j: (o_tile(i), 0))
    return pl.pallas_call(
        _merge_kernel,
        grid=(n // tm, 4),
        in_specs=[
            o_spec, o_spec, o_spec, o_spec,
            pl.BlockSpec((tm, D_MODEL), lambda i, j: (i, 0)),
            pl.BlockSpec((1, D_MODEL), lambda i, j: (0, 0)),
            pl.BlockSpec((D_MODEL, D_MODEL), lambda i, j: (0, j)),
            pl.BlockSpec((1, 256, D_MODEL), lambda i, j: (j, 0, 0)),
            pl.BlockSpec((D_MODEL, D_MODEL), lambda i, j: (0, 0)),
        ],
        out_specs=pl.BlockSpec((tm, D_MODEL), lambda i, j: (i, 0)),
        out_shape=jax.ShapeDtypeStruct((n, D_MODEL), F32),
        scratch_shapes=[pltpu.VMEM((tm, D_MODEL), BF16), pltpu.VMEM((tm, D_MODEL), F32)],
        compiler_params=_cparams("parallel", "arbitrary"),
    )(*outs, x, g, wg, wb, wout)


def _mlp_kernel(x_ref, g_ref, gf_ref, wup_ref, wdn_ref, out_ref, h_scr, acc_scr, *, final):
    j = pl.program_id(1)

    @pl.when(j == 0)
    def _():
        h_scr[...] = _rms_bf16(x_ref[...], g_ref[...])
        acc_scr[...] = jnp.zeros_like(acc_scr)

    u = jnp.maximum(_dot(h_scr[...], wup_ref[...]), 0.0)
    acc_scr[...] += _dot((u * u).astype(BF16), wdn_ref[...])

    @pl.when(j == pl.num_programs(1) - 1)
    def _():
        y = x_ref[...] + acc_scr[...]
        if final:
            y = y * lax.rsqrt(jnp.mean(y * y, axis=-1, keepdims=True) + EPS) * gf_ref[...]
        out_ref[...] = y


def _mlp_call(x, g, gf, wup, wdn, tm, final):
    n = x.shape[0]
    tf = 1024
    return pl.pallas_call(
        functools.partial(_mlp_kernel, final=final),
        grid=(n // tm, D_FF // tf),
        in_specs=[
            pl.BlockSpec((tm, D_MODEL), lambda i, j: (i, 0)),
            pl.BlockSpec((1, D_MODEL), lambda i, j: (0, 0)),
            pl.BlockSpec((1, D_MODEL), lambda i, j: (0, 0)),
            pl.BlockSpec((D_MODEL, tf), lambda i, j: (0, j)),
            pl.BlockSpec((tf, D_MODEL), lambda i, j: (j, 0)),
        ],
        out_specs=pl.BlockSpec((tm, D_MODEL), lambda i, j: (i, 0)),
        out_shape=jax.ShapeDtypeStruct((n, D_MODEL), F32),
        scratch_shapes=[pltpu.VMEM((tm, D_MODEL), BF16), pltpu.VMEM((tm, D_MODEL), F32)],
        compiler_params=_cparams("parallel", "arbitrary"),
    )(x, g, gf, wup, wdn)


def _seg_ones(rows_per_head, cols_per_head):
    r = np.arange(N_HEADS * rows_per_head)[:, None] // rows_per_head
    c = np.arange(N_HEADS * cols_per_head)[None, :] // cols_per_head
    return (r == c).astype(np.float32)


def _rows(*vecs, width=256, nrows=8):
    rows = []
    for v in vecs:
        v = jnp.asarray(v, F32).reshape(-1)
        rows.append(jnp.pad(v, (0, width - v.shape[0])))
    rows += [jnp.zeros((width,), F32)] * (nrows - len(rows))
    return jnp.stack(rows)


def _layout_w_in(w):
    d = w.shape[0]
    zeros = lambda n: jnp.zeros((d, n), w.dtype)
    gl0, ml0, gt0 = 2048, 2832, 3608
    parts = [
        w[:, 0:2048],
        w[:, gl0:gl0 + 512], w[:, gl0 + 512:gl0 + 528], zeros(112), w[:, gl0 + 528:gl0 + 784], zeros(128),
        w[:, ml0:ml0 + 512], w[:, ml0 + 512:ml0 + 520], zeros(120), w[:, ml0 + 520:ml0 + 776], zeros(128),
    ]
    wz = jnp.concatenate(parts, axis=1)
    assert wz.shape[1] == Z_COLS
    return wz.astype(BF16), w[:, gt0:gt0 + 4 * D_MODEL].astype(BF16)


class _Group:
    def __init__(self, nb, trow, tv, row0, max_seq):
        self.nb, self.trow, self.tv, self.row0 = nb, trow, tv, row0
        self.full = tv == trow
        self.c = SUB if self.full else trow
        self.tvs = min(tv, self.c)
        if self.full:
            nblk = trow // self.c
            self.rps = self.c * max(d for d in range(1, min(nblk, 64) + 1) if nblk % d == 0)
        else:
            self.rps = trow
        self.nst = trow // self.rps
        self.nsub = self.rps // self.c
        self.nseq = max(d for d in range(1, max_seq + 1) if nb % d == 0 and row0 % (d * self.rps) == 0)
        self.lanes = (8 if not self.full and self.nseq % 8 == 0 else 4 if self.nseq % 4 == 0
                      else 2 if self.nseq % 2 == 0 else 1)
        assert self.nseq == self.lanes or self.nst == 1
        self.rows = self.nseq * self.rps
        self.ngrid = nb // self.nseq
        self.base_blk = row0 // self.rows
        self.pc = _pick_tile(self.rps, 384) if self.rps >= 64 else self.rows


def kernel(x_prompt, x_sample, state_hgrn, state_rwkv, state_rwkv_shift, state_gla, state_mlstm_c, state_mlstm_n, state_mlstm_m, state_mlstm_conv, meta_tokens, norm_mix, norm_mlp, norm_final, w_in, hgrn_lb, hgrn_norm, rwkv_mu, rwkv_w0, rwkv_w_up, rwkv_a0, rwkv_a_up, rwkv_g_up, rwkv_k_k, rwkv_k_a, rwkv_r_k, rwkv_ln_w, rwkv_ln_b, gla_gate_up, gla_gate_b, gla_norm, mlstm_conv_w, mlstm_conv_b, mlstm_i_b, mlstm_f_b, mlstm_norm, w_branch, w_out, w_up, w_down):
    depth = w_in.shape[0]
    bp, seq, _ = x_prompt.shape
    bs, dseq, _ = x_sample.shape
    tp = N_META + seq
    assert tp % SUB == 0 and CONV_W - 1 <= dseq <= SAMPLE_PAD
    dt = x_prompt.dtype

    gp = _Group(bp, tp, tp, 0, 4)
    gs = _Group(bs, SAMPLE_PAD, dseq, bp * tp, 16)
    tm = gp.rps
    xs = jnp.pad(x_sample, ((0, 0), (0, SAMPLE_PAD - dseq), (0, 0))).reshape(bs * SAMPLE_PAD, D_MODEL)
    n_real = bp * tp + bs * SAMPLE_PAD
    n_rows = -(-n_real // tm) * tm
    pieces = []
    for b in range(bp):
        pieces += [meta_tokens.astype(dt), x_prompt[b]]
    x = jnp.concatenate(pieces + [xs, jnp.zeros((n_rows - n_real, D_MODEL), dt)], axis=0).astype(F32)

    def mixer_tile(i, split=1):
        t, part = i // split, i % split
        b, s = t // gp.nst, t % gp.nst
        p = (b // gp.nseq) * (gp.nst * gp.nseq) + s * gp.nseq + b % gp.nseq
        return jnp.where(t < bp * gp.nst, p, t) * split + part

    in_split = 2 if tm % 16 == 0 and tm >= 256 else 1
    tm_mlp = 2 * tm if (n_rows // tm) % 2 == 0 and 2 * tm <= 1536 else tm

    ones_vv = jnp.asarray(_seg_ones(64, 64), BF16)
    ones_gl = jnp.asarray(_seg_ones(GLA_DK, 64), BF16)
    mask_gl = jnp.asarray(_seg_ones(64, GLA_DK), F32)
    sel8 = jnp.asarray(np.eye(8, 128, dtype=np.float32), BF16)
    hmask = {g.c: jnp.asarray(np.kron(np.eye(2), np.ones((g.c, 64))), F32) for g in (gp, gs)}
    ehead = {c_: jnp.concatenate([m_, m_], axis=0).astype(BF16) for c_, m_ in hmask.items()}
    mask_pair = jnp.asarray(np.kron(np.eye(2), np.ones((64, 64))), F32)

    lb_cs = jnp.cumsum(jax.nn.softmax(hgrn_lb.astype(F32), axis=0), axis=0)
    lb_all = lb_cs - lb_cs[:1]

    def init_states(states, nb, zero):
        s_hg, s_rw, shift, s_gl, c_ml, n_ml, m_ml, conv = states
        if zero:
            z = lambda *shp: jnp.zeros((depth, nb) + shp, F32)
            return dict(hg=z(256, 64), rw=z(256, 64), shift=z(1, SEC), gl=z(256, GLA_DK), cm=z(128, 64),
                        nm=z(1, 128), mm=z(1, 128), cv=z(8, 256))
        f = lambda a: a.astype(F32)
        return dict(
            hg=f(s_hg).transpose(0, 1, 2, 4, 3).reshape(depth, nb, 256, 64),
            rw=f(s_rw).reshape(depth, nb, 256, 64),
            shift=f(shift).reshape(depth, nb, 1, SEC),
            gl=f(s_gl).transpose(0, 1, 2, 4, 3).reshape(depth, nb, 256, GLA_DK),
            cm=f(c_ml).reshape(depth, nb, 128, 64),
            nm=f(n_ml).reshape(depth, nb, 1, 128),
            mm=jnp.pad(f(m_ml), ((0, 0), (0, 0), (0, 124))).reshape(depth, nb, 1, 128),
            cv=jnp.pad(f(conv), ((0, 0), (0, 0), (5, 0), (0, 0))),
        )

    sample_states = (state_hgrn, state_rwkv, state_rwkv_shift, state_gla,
                     state_mlstm_c, state_mlstm_n, state_mlstm_m, state_mlstm_conv)
    st_p = init_states(sample_states, bp, True)
    st_s = init_states(sample_states, bs, False)
    new_p = {k: [] for k in st_p}
    new_s = {k: [] for k in st_s}

    for l in range(depth):
        wz, wg = _layout_w_in(w_in[l])
        g_mix = norm_mix[l].reshape(1, D_MODEL).astype(F32)
        z = _in_proj(x, g_mix, wz, tm // in_split, functools.partial(mixer_tile, split=in_split))

        lb = lb_all[l]
        par_hg = _rows(jnp.log(lb), jnp.log1p(-lb), 1.0 - lb, hgrn_norm[l])
        par_gl = _rows(gla_gate_b[l], jnp.zeros((1,)), jnp.zeros((1,)), gla_norm[l])
        gup_gl = jnp.pad(gla_gate_up[l], ((0, 128 - gla_gate_up.shape[1]), (0, 0))).astype(BF16)
        gup_dummy = jnp.zeros((128, 128), BF16)
        par_rw = _rows(rwkv_w0[l], rwkv_a0[l], rwkv_k_k[l], rwkv_k_a[l], rwkv_r_k[l], rwkv_ln_w[l], rwkv_ln_b[l])
        nw, na = rwkv_w_up.shape[1], rwkv_a_up.shape[1]
        wup_p = jnp.pad(rwkv_w_up[l], ((0, 256 - nw), (0, 0))).astype(BF16)
        aup_p = jnp.pad(rwkv_a_up[l], ((nw, 256 - nw - na), (0, 0))).astype(BF16)
        gup_p = jnp.pad(rwkv_g_up[l], ((nw + na, 0), (0, 0))).astype(BF16)
        mu = rwkv_mu[l].reshape(1, SEC).astype(F32)
        gate_b = jnp.concatenate([mlstm_i_b[l], mlstm_f_b[l]]).astype(F32)
        par_ml = _rows(mlstm_conv_b[l], gate_b, mlstm_norm[l])
        bcol_ml = gate_b.reshape(8, 1)
        cw_ml = _rows(*[mlstm_conv_w[l, j] for j in range(CONV_W)])

        o_hg = o_rw = o_gl = o_ml = None
        for g, st, new in ((gp, st_p, new_p), (gs, st_s, new_s)):
            prev = lambda o: jnp.zeros((n_rows, 256), F32) if o is None else o
            o_hg, s_hg = _gla_call(z, 0, par_hg, gup_dummy, ones_vv, ones_vv, mask_pair, st["hg"][l], prev(o_hg),
                                   mode="hgrn", dk=64, nrows=n_rows, g=g)
            o_gl, s_gl = _gla_call(z, 2, par_gl, gup_gl, ones_gl, ones_vv, mask_gl, st["gl"][l], prev(o_gl),
                                   mode="gla", dk=GLA_DK, nrows=n_rows, g=g)
            o_rw, nshift, s_rw = _rwkv_call(z, mu, par_rw, wup_p, aup_p, gup_p, ones_vv, hmask[g.c], ehead[g.c],
                                            mask_pair, st["shift"][l], st["rw"][l], prev(o_rw), nrows=n_rows, g=g)
            o_ml, s_cm, s_nm, s_mm, s_cv = _mlstm_call(z, par_ml, bcol_ml, cw_ml, sel8, ones_vv, st["cm"][l], st["nm"][l],
                                                        st["mm"][l], st["cv"][l], prev(o_ml), nrows=n_rows, g=g)
            for key, val in (("hg", s_hg), ("rw", s_rw), ("shift", nshift), ("gl", s_gl), ("cm", s_cm),
                             ("nm", s_nm), ("mm", s_mm), ("cv", s_cv)):
                new[key].append(val)

        x = _merge_call((o_hg, o_rw, o_gl, o_ml), x, g_mix, wg, w_branch[l].astype(BF16), w_out[l].astype(BF16), tm,
                        mixer_tile)
        x = _mlp_call(x, norm_mlp[l].reshape(1, D_MODEL).astype(F32), norm_final.reshape(1, D_MODEL).astype(F32),
                      w_up[l].astype(BF16), w_down[l].astype(BF16), tm_mlp, final=(l == depth - 1))

    y_prompt = jnp.stack([x[b * tp + N_META:(b + 1) * tp] for b in range(bp)]).astype(dt)
    y_sample = x[gs.row0:gs.row0 + bs * SAMPLE_PAD].reshape(bs, SAMPLE_PAD, D_MODEL)[:, :dseq].astype(dt)

    def finish(new, nb):
        st = {k: jnp.stack(v) for k, v in new.items()}
        return (
            st["hg"].reshape(depth, nb, N_HEADS, 64, 64).transpose(0, 1, 2, 4, 3),
            st["rw"].reshape(depth, nb, N_HEADS, 64, 64),
            st["shift"].reshape(depth, nb, SEC),
            st["gl"].reshape(depth, nb, N_HEADS, 64, GLA_DK).transpose(0, 1, 2, 4, 3),
            st["cm"].reshape(depth, nb, N_HEADS, MLSTM_DK, 64),
            st["nm"].reshape(depth, nb, N_HEADS, MLSTM_DK),
            st["mm"].reshape(depth, nb, 128)[:, :, :N_HEADS],
            st["cv"][:, :, 5:8, :],
        )

    outs_p = tuple(a.astype(dt) for a in finish(new_p, bp))
    outs_s = tuple(a.astype(dt) for a in finish(new_s, bs))
    return (y_prompt, y_sample) + outs_p + outs_s
```

```python
import functools

import numpy as np
import jax
import jax.numpy as jnp
from jax import lax
from jax.experimental import pallas as pl
from jax.experimental.pallas import tpu as pltpu

F32 = jnp.float32
BF16 = jnp.bfloat16

D_MODEL = 1024
N_META = 16
N_HEADS = 4
HEAD_V = 64
GLA_DK = 32
MLSTM_DK = 32
GLA_GATE_TAU = 16.0
RWKV_LN_EPS = 64e-5
CONV_W = 4
D_FF = 4 * D_MODEL
EPS = 1e-6

SEC = 1024
Z_COLS = 4 * SEC
SUB = 16
SAMPLE_PAD = 8
MLSTM_CHUNK = 128
VMEM_LIMIT = 56 * 1024 * 1024


def _cparams(*sem):
    return pltpu.CompilerParams(dimension_semantics=sem, vmem_limit_bytes=VMEM_LIMIT)


def _pick_tile(n, max_tile, mult=8):
    best = None
    t = mult
    while t <= min(n, max_tile):
        if n % t == 0:
            best = t
        t += mult
    assert best is not None, (n, max_tile, mult)
    return best


def _split2(x):
    hi = x.astype(BF16)
    lo = (x - hi.astype(F32)).astype(BF16)
    return hi, lo


def _split3(x):
    hi = x.astype(BF16)
    r1 = x - hi.astype(F32)
    mid = r1.astype(BF16)
    lo = (r1 - mid.astype(F32)).astype(BF16)
    return hi, mid, lo


def _dot(a, b):
    return jnp.dot(a, b, preferred_element_type=F32)


def _dot_nt(a, b):
    return lax.dot_general(a, b, (((1,), (1,)), ((), ())), preferred_element_type=F32)


def _dot_tn(a, b):
    return lax.dot_general(a, b, (((0,), (0,)), ((), ())), preferred_element_type=F32)


def _seg_dot(x, m_bf16, parts):
    ps = _split2(x) if parts == 2 else _split3(x)
    acc = _dot(ps[0], m_bf16)
    for p in ps[1:]:
        acc = acc + _dot(p, m_bf16)
    return acc


def _seg_dot_l(m_bf16, x, parts):
    ps = _split2(x) if parts == 2 else _split3(x)
    acc = _dot(m_bf16, ps[0])
    for p in ps[1:]:
        acc = acc + _dot(m_bf16, p)
    return acc


def _log_sigmoid(x):
    return jnp.minimum(x, 0.0) - jnp.log1p(jnp.exp(-jnp.abs(x)))


def _silu(x):
    return x * jax.nn.sigmoid(x)


def _tri(n, upper=False):
    r = lax.broadcasted_iota(jnp.int32, (n, n), 0)
    c = lax.broadcasted_iota(jnp.int32, (n, n), 1)
    return (r <= c) if upper else (r >= c)


def _cumsum_rows(x, rowi):
    d = 1
    while d < x.shape[0]:
        x = x + jnp.where(rowi >= d, pltpu.roll(x, d, axis=0), 0.0)
        d *= 2
    return x


def _rms_bf16(x, g):
    return (x * lax.rsqrt(jnp.mean(x * x, axis=-1, keepdims=True) + EPS) * g).astype(BF16)


def _in_proj_kernel(x_ref, g_ref, w_ref, z_ref):
    h = _rms_bf16(x_ref[...], g_ref[...])
    for j in range(Z_COLS // SEC):
        z_ref[:, j * SEC:(j + 1) * SEC] = _dot(h, w_ref[:, j * SEC:(j + 1) * SEC])


def _in_proj(x, g, w, tm, out_tile):
    n = x.shape[0]
    return pl.pallas_call(
        _in_proj_kernel,
        grid=(n // tm,),
        in_specs=[
            pl.BlockSpec((tm, D_MODEL), lambda i: (i, 0)),
            pl.BlockSpec((1, D_MODEL), lambda i: (0, 0)),
            pl.BlockSpec((D_MODEL, Z_COLS), lambda i: (0, 0)),
        ],
        out_specs=pl.BlockSpec((tm, Z_COLS), lambda i: (out_tile(i), 0)),
        out_shape=jax.ShapeDtypeStruct((n, Z_COLS), F32),
        compiler_params=_cparams("parallel"),
    )(x, g, w)


def _gla_kernel(z_ref, par_ref, gup_ref, okv_ref, ovv_ref, mask_ref, st_in_ref, o_prev_ref,
                o_ref, st_out_ref, *st_scrs, mode, c, nsub, tv, dk, nseq, rps, nst, pc):
    del o_prev_ref
    voff = 512 if mode == "hgrn" else 256
    rowi = lax.broadcasted_iota(jnp.int32, (c, 1), 0)
    valid = rowi < tv
    hpg = 128 // dk
    groups = [(slice(i * hpg * HEAD_V, (i + 1) * hpg * HEAD_V), slice(i * 128, (i + 1) * 128))
              for i in range(N_HEADS // hpg)]

    def sub_block(st_scr, r0):
        rows = pl.ds(r0, c)
        if mode == "hgrn":
            q = z_ref[rows, 0:256]
            hf = z_ref[rows, 256:512]
            a = par_ref[0:1, :]
            cc = par_ref[1:2, :] + _log_sigmoid(hf)
            loga = jnp.maximum(a, cc) + jnp.log1p(jnp.exp(-jnp.abs(a - cc)))
            k = par_ref[2:3, :] * jax.nn.sigmoid(-hf)
        else:
            q = z_ref[rows, 0:128] * (GLA_DK ** -0.5)
            k = z_ref[rows, 128:256]
            ga = z_ref[rows, 512:640]
            gl = _dot(ga.astype(BF16), gup_ref[...]) + par_ref[0:1, 0:128]
            loga = _log_sigmoid(gl) * (1.0 / GLA_GATE_TAU)
        v = z_ref[rows, voff:voff + 256]
        if tv < c:
            loga = jnp.where(valid, loga, 0.0)
            k = jnp.where(valid, k, 0.0)
            v = jnp.where(valid, v, 0.0)
        b = _cumsum_rows(loga, rowi)

        qe = (q * jnp.exp(b)).astype(BF16)
        o = jnp.concatenate([_dot_nt(qe[:, ks], scr[...].astype(BF16))
                             for scr, (_, ks) in zip(st_scr, groups)], axis=1)

        lo_rows = [8 * (s // 8) for s in range(tv)]
        pieces = []
        for s in range(tv):
            rs = slice(lo_rows[s], c)
            e = jnp.exp(jnp.where(rowi[rs] >= s, b[rs] - b[s:s + 1, :], -jnp.inf))
            pieces.append(q[rs] * e * k[s:s + 1, :])
        return rows, lo_rows, k, v, b, o, jnp.concatenate(pieces, axis=0)

    def sub_block_finish(st_scr, part, sm):
        rows, lo_rows, k, v, b, o, _ = part
        acc = {}
        off = 0
        for s in range(tv):
            n = c - lo_rows[s]
            contrib = sm[off:off + n] * v[s:s + 1, :]
            off += n
            acc[lo_rows[s]] = contrib if lo_rows[s] not in acc else acc[lo_rows[s]] + contrib
        for lo, a_ in acc.items():
            o = o + (a_ if lo == 0 else jnp.concatenate([jnp.zeros((lo, 256), F32), a_], axis=0))

        bl = b[c - 1:c, :]
        kdec = (k * jnp.exp(bl - b)).astype(BF16)
        vb = v.astype(BF16)
        ebl = jnp.exp(bl)
        upd = [_dot_tn(vb[:, vs], kdec[:, ks]) for vs, ks in groups]
        for scr, u_, (_, ks) in zip(st_scr, upd, groups):
            scr[...] = scr[...] * ebl[:, ks] + u_ * mask_ref[...]
        o_ref[rows, :] = o

    def sub_blocks(scrs, row0s):
        parts = [sub_block(scr, r0) for scr, r0 in zip(scrs, row0s)]
        sm = _dot(jnp.concatenate([p[-1] for p in parts], axis=0).astype(BF16), okv_ref[...])
        n = parts[0][-1].shape[0]
        for i, (scr, part) in enumerate(zip(scrs, parts)):
            sub_block_finish(scr, part, sm[i * n:(i + 1) * n])

    def load_state(st_scr, seq):
        for scr, (vs, _) in zip(st_scr, groups):
            scr[...] = jnp.concatenate([st_in_ref[seq, vs, :]] * hpg, axis=1) * mask_ref[...]

    def store_state(st_scr, seq):
        for scr, (vs, _) in zip(st_scr, groups):
            st = scr[...]
            acc = st[:, 0:dk]
            for h in range(1, hpg):
                acc = acc + st[:, h * dk:(h + 1) * dk]
            st_out_ref[seq, vs, :] = acc

    ng = len(groups)
    lane_scrs = [st_scrs[i * ng:(i + 1) * ng] for i in range(len(st_scrs) // ng)]
    _run_lanes(lane_scrs, nseq, rps, nsub, c, nst, load_state, store_state, sub_blocks)

    goff = 768 if mode == "hgrn" else 640
    nchunk = (nseq * rps) // pc

    def finish_chunk(ci, carry):
        rows = pl.ds(ci * pc if isinstance(ci, int) else pl.multiple_of(ci * pc, 8), pc)
        o = o_ref[rows, :]
        ms = _seg_dot(o * o, ovv_ref[...], 2) * (1.0 / HEAD_V)
        o_ref[rows, :] = o * lax.rsqrt(ms + EPS) * par_ref[3:4, :] * _silu(z_ref[rows, goff:goff + 256])
        return carry

    if nchunk == 1:
        finish_chunk(0, 0)
    else:
        lax.fori_loop(0, nchunk, finish_chunk, 0)


def _run_lanes(scrs, nseq, rps, nsub, c, nst, load_state, store_state, sub_blocks):
    lanes = len(scrs)
    assert nseq == lanes or nst == 1
    step = pl.program_id(1)

    def seq_group(sg, carry):
        seqs = [sg * lanes + u for u in range(lanes)]
        for scr, seq in zip(scrs, seqs):
            if nst == 1:
                load_state(scr, seq)
            else:
                pl.when(step == 0)(functools.partial(load_state, scr, seq))

        def blocks(j, carry2):
            starts = [seq * rps + j * c for seq in seqs]
            sub_blocks(scrs, [s_ if isinstance(s_, int) else pl.multiple_of(s_, 8) for s_ in starts])
            return carry2

        if nsub == 1:
            blocks(0, 0)
        else:
            lax.fori_loop(0, nsub, blocks, 0, unroll=2)
        for scr, seq in zip(scrs, seqs):
            if nst == 1:
                store_state(scr, seq)
            else:
                pl.when(step == nst - 1)(functools.partial(store_state, scr, seq))
        return carry

    if nseq == lanes:
        seq_group(0, 0)
    else:
        lax.fori_loop(0, nseq // lanes, seq_group, 0)


def _gla_call(z, sec, par, gup, okv, ovv, mask, st_in, o_prev, *, mode, dk, nrows, g):
    hk = N_HEADS * dk
    grows = (128 // dk) * HEAD_V
    kern = functools.partial(_gla_kernel, mode=mode, c=g.c, nsub=g.nsub, tv=g.tvs, dk=dk,
                             nseq=g.nseq, rps=g.rps, nst=g.nst, pc=g.pc)
    const = lambda b, s: (0, 0)
    return pl.pallas_call(
        kern,
        grid=(g.ngrid, g.nst),
        in_specs=[
            pl.BlockSpec((g.rows, SEC), lambda b, s: (g.base_blk + b * g.nst + s, sec)),
            pl.BlockSpec((8, 256), const),
            pl.BlockSpec((128, 128), const),
            pl.BlockSpec((hk, 256), const),
            pl.BlockSpec((256, 256), const),
            pl.BlockSpec((grows, 128), const),
            pl.BlockSpec((g.nseq, 256, dk), lambda b, s: (b, 0, 0)),
            pl.BlockSpec(memory_space=pl.ANY),
        ],
        input_output_aliases={7: 0},
        out_specs=[
            pl.BlockSpec((g.rows, 256), lambda b, s: (g.base_blk + b * g.nst + s, 0)),
            pl.BlockSpec((g.nseq, 256, dk), lambda b, s: (b, 0, 0)),
        ],
        out_shape=[
            jax.ShapeDtypeStruct((nrows, 256), F32),
            jax.ShapeDtypeStruct((g.nb, 256, dk), F32),
        ],
        scratch_shapes=[pltpu.VMEM((grows, 128), F32)] * (g.lanes * (256 // grows)),
        compiler_params=_cparams("parallel", "arbitrary"),
    )(z, par, gup, okv, ovv, mask, st_in, o_prev)


def _rwkv_kernel(z_ref, mu_ref, par_ref, wup_ref, aup_ref, gup_ref, ovv_ref, hmask_ref, ehead_ref, mask_ref,
                 shift_ref, st_in_ref, o_prev_ref,
                 o_ref, nshift_ref, st_out_ref,
                 prev_scr, fix_scr, r_scr, lw_scr, k_scr, v_scr, kk_scr, ka_scr, g_scr, bonus_scr, y_scr, *st_scrs,
                 c, nsub, tv, nseq, rps, nst, pc):
    del o_prev_ref
    step = pl.program_id(1)

    def shifted_chunk(ci):
        r0 = ci * pc if isinstance(ci, int) else pl.multiple_of(ci * pc, 8)
        zr = z_ref[pl.ds(r0, pc), :]
        rolled = pltpu.roll(zr, 1, axis=0)
        if pc <= rps:
            cps = rps // pc
            seq = ci // cps
            rowi = lax.broadcasted_iota(jnp.int32, (pc, 1), 0)
            before = z_ref[pl.ds(pl.multiple_of(jnp.maximum(r0 - 8, 0), 8), 8), :][7:8, :]
            first = jnp.where(ci % cps == 0, prev_scr[pl.ds(seq, 1), :], before)
            prev = jnp.where(rowi == 0, first, rolled)
            prev_scr[pl.ds(seq, 1), :] = zr[pc - 1:pc, :]
            nshift_ref[seq] = zr[pc - 1:pc, :]
        else:
            fix_scr[...] = rolled
            for i in range(nseq):
                fix_scr[i * rps:i * rps + 1, :] = shift_ref[i]
                nshift_ref[i] = zr[i * rps + tv - 1:i * rps + tv, :]
            prev = fix_scr[...]
        return r0, zr, prev

    if pc <= rps:
        @pl.when(step == 0)
        def _():
            for i in range(nseq):
                prev_scr[i:i + 1, :] = shift_ref[i]

    def prep_chunk(ci, carry):
        r0, zr, prev = shifted_chunk(ci)
        rows = pl.ds(r0, pc)
        _rwkv_prep_rows(zr, prev, rows)
        return carry

    def _rwkv_prep_rows(zr, prev, rows):
        nrow = zr.shape[0]
        zs = zr + (prev - zr) * mu_ref[...]
        r = zs[:, 0:256]
        k = zs[:, 256:512]
        v = zs[:, 512:768]
        lo = zs[:, 768:1024]
        w0, a0, k_k, k_a, r_k = (par_ref[i:i + 1, :] for i in range(5))
        wl = w0 + _dot(jnp.tanh(lo).astype(BF16), wup_ref[...])
        wexp = -(jnp.maximum(-wl, 0.0) + jnp.log1p(jnp.exp(-jnp.abs(wl)))) - 0.5
        lw = -jnp.exp(wexp)
        a = jax.nn.sigmoid(a0 + _dot(lo.astype(BF16), aup_ref[...]))
        g_scr[rows, :] = _dot(jax.nn.sigmoid(lo).astype(BF16), gup_ref[...])
        kkp = k * k_k
        nrm = jnp.sqrt(_seg_dot(kkp * kkp, ovv_ref[...], 2))
        kk = kkp / jnp.maximum(nrm, 1e-12)
        k2 = k * (1.0 + (a - 1.0) * k_a)
        bonus_scr[rows, :] = _seg_dot(r * k2 * r_k, ovv_ref[...], 3) * v
        ka = kk * a
        if tv < rps:
            okrow = lax.rem(lax.broadcasted_iota(jnp.int32, (nrow, 1), 0), rps) < tv
            lw, k2, v, kk, ka = (jnp.where(okrow, t_, 0.0) for t_ in (lw, k2, v, kk, ka))
        r_scr[rows, :] = r
        lw_scr[rows, :] = lw
        k_scr[rows, :] = k2
        v_scr[rows, :] = v
        kk_scr[rows, :] = kk
        ka_scr[rows, :] = ka

    nchunk = (nseq * rps) // pc
    if nchunk == 1:
        prep_chunk(0, 0)
    else:
        lax.fori_loop(0, nchunk, prep_chunk, 0)

    hp = 2
    ng = N_HEADS // hp
    gw = hp * 64
    rowc = lax.broadcasted_iota(jnp.int32, (c, 1), 0)
    src2 = lax.broadcasted_iota(jnp.int32, (1, 2 * hp * c), 1) & (c - 1)
    src = src2[:, 0:hp * c]
    strict = rowc > src
    incl = rowc >= src

    def sub_blocks(lane_scrs, row0s):
        stage_state(*stage_free(lane_scrs, row0s))

    def stage_free(lane_scrs, row0s):
        hm = hmask_ref[...]
        tile = lambda t_: jnp.concatenate([t_] * hp, axis=0) * hm
        nc = hp * c
        nsolve = min(tv, c - 1)
        units, cx = [], {}
        for i, r0 in enumerate(row0s):
            rows = pl.ds(r0, c)
            lw_, r_, k_, v_, kk_, ka_ = (s_[rows, :] for s_ in (lw_scr, r_scr, k_scr, v_scr, kk_scr, ka_scr))
            gam = _cumsum_rows(lw_, rowc)
            gl = gam[c - 1:c, :]
            ginv = jnp.exp(-gam)
            gend = jnp.exp(gl - gam)
            lh, ll = _split2(jnp.concatenate([kk_ * jnp.exp(gam - lw_), r_ * jnp.exp(gam)], axis=0))
            a_, kq_ = ka_ * ginv, k_ * ginv
            ae_, ke_ = ka_ * gend, k_ * gend
            for p in range(ng):
                ls = slice(p * gw, (p + 1) * gw)
                rh, rl = _split2(jnp.concatenate([tile(a_[:, ls]), tile(kq_[:, ls])], axis=0))
                units.append((i, p))
                cx[i, p] = dict(rows=rows, ls=ls, v=v_[:, ls], egl=jnp.exp(gl[:, ls]), lh=lh[:, ls], ll=ll[:, ls],
                                rh=rh, rl=rl, vbig=tile(v_[:, ls]).astype(BF16), scr=lane_scrs[i][p],
                                end=jnp.concatenate([ae_[:, ls], ke_[:, ls]], axis=0).astype(BF16))
        g2 = {u_: _dot_nt(jnp.concatenate([cx[u_]["lh"], cx[u_]["ll"]], axis=0), cx[u_]["rh"]) for u_ in units}
        g3 = {u_: _dot_nt(cx[u_]["lh"], cx[u_]["rl"]) for u_ in units}
        m_ab, m_ra, m_rk, bkv = {}, {}, {}, {}
        for u_ in units:
            gm = g2[u_][0:2 * c] + g2[u_][2 * c:4 * c] + g3[u_]
            m_ab[u_] = jnp.where(strict, gm[0:c, 0:nc], 0.0)
            m_ra[u_] = jnp.where(incl, gm[c:2 * c, 0:nc], 0.0)
            m_rk[u_] = jnp.where(incl, gm[c:2 * c, nc:2 * nc], 0.0)
            bkv[u_] = _dot(jnp.where(strict, gm[0:c, nc:2 * nc], 0.0).astype(BF16), cx[u_]["vbig"])
        m2 = {u_: jnp.concatenate(_split2(m_ab[u_]), axis=1) for u_ in units}
        cols = _dot(jnp.concatenate([jnp.where(src2 == s, m2[u_], jnp.zeros_like(m2[u_]))
                                     for u_ in units for s in range(nsolve)], axis=0), ehead_ref[...])
        return units, cx, m_ra, m_rk, bkv, cols

    def stage_state(units, cx, m_ra, m_rk, bkv, cols):
        nsolve = min(tv, c - 1)
        tile = lambda t_: jnp.concatenate([t_] * hp, axis=0) * hmask_ref[...]
        x0 = {u_: _dot_nt(cx[u_]["lh"], cx[u_]["scr"][...].astype(BF16)) for u_ in units}
        usol = {u_: x0[u_][0:c] + bkv[u_] for u_ in units}
        for s in range(nsolve):
            for n_, u_ in enumerate(units):
                o_ = (n_ * nsolve + s) * c
                usol[u_] = usol[u_] - cols[o_:o_ + c] * usol[u_][s:s + 1, :]
        ys = {u_: _dot(jnp.concatenate([-m_ra[u_], m_rk[u_]], axis=1).astype(BF16),
                       jnp.concatenate([tile(usol[u_]).astype(BF16), cx[u_]["vbig"]], axis=0)) for u_ in units}
        upd = {u_: _dot_tn(jnp.concatenate([-usol[u_], cx[u_]["v"]], axis=0).astype(BF16), cx[u_]["end"])
               for u_ in units}
        for u_ in units:
            x = cx[u_]
            y_scr[x["rows"], x["ls"]] = x0[u_][c:2 * c] + ys[u_]
            x["scr"][...] = x["scr"][...] * x["egl"] + upd[u_] * mask_ref[...]

    def load_state(scrs, seq):
        for p, scr in enumerate(scrs):
            scr[...] = jnp.concatenate([st_in_ref[seq, p * gw:(p + 1) * gw, :]] * hp, axis=1) * mask_ref[...]

    def store_state(scrs, seq):
        for p, scr in enumerate(scrs):
            st = scr[...]
            st_out_ref[seq, p * gw:(p + 1) * gw, :] = st[:, 0:64] + st[:, 64:128]

    lane_scrs = [st_scrs[i * ng:(i + 1) * ng] for i in range(len(st_scrs) // ng)]
    _run_lanes(lane_scrs, nseq, rps, nsub, c, nst, load_state, store_state, sub_blocks)

    def finish_chunk(ci, carry):
        rows = pl.ds(ci * pc if isinstance(ci, int) else pl.multiple_of(ci * pc, 8), pc)
        y = y_scr[rows, :]
        mean = _seg_dot(y, ovv_ref[...], 3) * (1.0 / HEAD_V)
        d = y - mean
        var = _seg_dot(d * d, ovv_ref[...], 2) * (1.0 / HEAD_V)
        ln = d * lax.rsqrt(var + RWKV_LN_EPS) * par_ref[5:6, :] + par_ref[6:7, :]
        o_ref[rows, :] = (ln + bonus_scr[rows, :]) * g_scr[rows, :]
        return carry

    if nchunk == 1:
        finish_chunk(0, 0)
    else:
        lax.fori_loop(0, nchunk, finish_chunk, 0)


def _rwkv_call(z, mu, par, wup, aup, gup, ovv, hmask, ehead, mask, shift, st_in, o_prev, *, nrows, g):
    rblk = g.rows
    nsteps = g.nst
    base = g.base_blk
    pc = g.pc
    kern = functools.partial(_rwkv_kernel, c=g.c, nsub=g.nsub, tv=g.tv, nseq=g.nseq, rps=g.rps, nst=g.nst, pc=pc)
    const = lambda b, s: (0, 0)
    return pl.pallas_call(
        kern,
        grid=(g.ngrid, nsteps),
        in_specs=[
            pl.BlockSpec((rblk, SEC), lambda b, s: (base + b * nsteps + s, 1),
                         pipeline_mode=pl.Buffered(1) if g.lanes > 2 and g.full else None),
            pl.BlockSpec((1, SEC), const),
            pl.BlockSpec((8, 256), const),
            pl.BlockSpec((256, 256), const),
            pl.BlockSpec((256, 256), const),
            pl.BlockSpec((256, 256), const),
            pl.BlockSpec((256, 256), const),
            pl.BlockSpec((2 * g.c, 128), const),
            pl.BlockSpec((4 * g.c, 128), const),
            pl.BlockSpec((128, 128), const),
            pl.BlockSpec((g.nseq, 1, SEC), lambda b, s: (b, 0, 0)),
            pl.BlockSpec((g.nseq, 256, 64), lambda b, s: (b, 0, 0)),
            pl.BlockSpec(memory_space=pl.ANY),
        ],
        input_output_aliases={12: 0},
        out_specs=[
            pl.BlockSpec((rblk, 256), lambda b, s: (base + b * nsteps + s, 0)),
            pl.BlockSpec((g.nseq, 1, SEC), lambda b, s: (b, 0, 0)),
            pl.BlockSpec((g.nseq, 256, 64), lambda b, s: (b, 0, 0)),
        ],
        out_shape=[
            jax.ShapeDtypeStruct((nrows, 256), F32),
            jax.ShapeDtypeStruct((g.nb, 1, SEC), F32),
            jax.ShapeDtypeStruct((g.nb, 256, 64), F32),
        ],
        scratch_shapes=[pltpu.VMEM((max(8, g.nseq), SEC), F32), pltpu.VMEM((pc if pc > g.rps else 8, SEC), F32)]
        + [pltpu.VMEM((rblk, 256), F32)] * 9 + [pltpu.VMEM((128, 128), F32)] * (2 * g.lanes),
        compiler_params=_cparams("parallel", "arbitrary"),
    )(z, mu, par, wup, aup, gup, ovv, hmask, ehead, mask, shift, st_in, o_prev)


def _mlstm_kernel(z_ref, par_ref, bcol_ref, cw_ref, sel_ref, ovv_ref, cm_in, nm_in, mm_in, cv_in, o_prev_ref,
                  o_ref, cm_out, nm_out, mm_out, cv_out,
                  *scratch, chunks, nseq, rps, nst, pc):
    del o_prev_ref
    lanes = scratch[0].shape[0]

    def chunk_gates(scr, row0, c, tv):
        xbuf, cm_scr, nm_scr, mm_scr = scr
        rows = pl.ds(row0, c)
        mqk = z_ref[rows, 0:256]
        mv = z_ref[rows, 256:512]
        gi = z_ref[rows, 512:640]
        mo = z_ref[rows, 640:896]
        xbuf[8:8 + c, :] = mqk
        conv = par_ref[0:1, :] + xbuf[5:5 + c, :] * cw_ref[0:1, :]
        for j in range(1, CONV_W):
            conv = conv + xbuf[5 + j:5 + j + c, :] * cw_ref[j:j + 1, :]
        xbuf[5:8, :] = xbuf[8 + tv - 3:8 + tv, :]
        act = _silu(conv)
        q = act[:, 0:128]
        k = act[:, 128:256] * (MLSTM_DK ** -0.5)

        rowi = lax.broadcasted_iota(jnp.int32, (c, 1), 0)
        coli = lax.broadcasted_iota(jnp.int32, (1, c), 1)
        gcol = gi + par_ref[1:2, 0:128]
        lfc = _log_sigmoid(gcol)
        graw = None
        for part in _split3(gi):
            t_ = _dot_nt(sel_ref[...], part)
            graw = t_ if graw is None else graw + t_
        grow = graw + bcol_ref[...]
        lfr = _log_sigmoid(grow)
        if tv < c:
            gcol = jnp.where(rowi < tv, gcol, -jnp.inf)
            lfc = jnp.where(rowi < tv, lfc, 0.0)
            grow = jnp.where(coli < tv, grow, -jnp.inf)
            lfr = jnp.where(coli < tv, lfr, 0.0)
        b_col = _seg_dot_l(_tri(c).astype(BF16), lfc, 3)
        b_row = _seg_dot(lfr, _tri(c, upper=True).astype(BF16), 3)
        return dict(rows=rows, q=q, k=k, mv=mv, mo=mo, gcol=gcol, grow=grow, b_col=b_col, b_row=b_row,
                    mm=mm_scr[...], nm=nm_scr[...], cm=cm_scr[...])

    def chunk(scrs, row0s, c, tv):
        ctx = [chunk_gates(scr, row0, c, tv) for scr, row0 in zip(scrs, row0s)]
        causal = _tri(c)
        lane = lax.broadcasted_iota(jnp.int32, (1, 128), 1)
        items = [(l, h) for l in range(len(ctx)) for h in range(N_HEADS)]
        ks = lambda h: slice(h * MLSTM_DK, (h + 1) * MLSTM_DK)
        vs = lambda h: slice(h * HEAD_V, (h + 1) * HEAD_V)
        qh = {(l, h): ctx[l]["q"][:, ks(h)] for l, h in items}
        kh = {(l, h): ctx[l]["k"][:, ks(h)] for l, h in items}
        vh = {(l, h): ctx[l]["mv"][:, vs(h)].astype(BF16) for l, h in items}
        cmh = {(l, h): ctx[l]["cm"][ks(h), :] for l, h in items}
        nh = {(l, h): ctx[l]["nm"][0:1, ks(h)] for l, h in items}
        qk_raw = {it: _dot_nt(qh[it].astype(BF16), kh[it].astype(BF16)) for it in items}
        q_cm = {it: _dot(qh[it].astype(BF16), cmh[it].astype(BF16)) for it in items}
        bc, ic, mprev, m_t, sc, qk = {}, {}, {}, {}, {}, {}
        for it in items:
            l, h = it
            bc[it] = ctx[l]["b_col"][:, 4 + h:5 + h]
            br = ctx[l]["b_row"][4 + h:5 + h, :]
            ir = ctx[l]["grow"][h:h + 1, :]
            ic[it] = ctx[l]["gcol"][:, h:h + 1]
            dmat = jnp.where(causal, bc[it] - br + ir, -jnp.inf)
            mprev[it] = ctx[l]["mm"][0:1, h:h + 1]
            inter = bc[it] + mprev[it]
            m_t[it] = jnp.maximum(inter, jnp.max(dmat, axis=1, keepdims=True))
            sc[it] = jnp.exp(inter - m_t[it])
            qk[it] = qk_raw[it] * jnp.exp(dmat - m_t[it])
        num = {it: _dot(qk[it].astype(BF16), vh[it]) + sc[it] * q_cm[it] for it in items}
        qk_sum = _seg_dot(jnp.concatenate([qk[it] for it in items], axis=0), jnp.ones((c, HEAD_V), BF16), 2)
        qn_sum = _seg_dot(jnp.concatenate([qh[it] * nh[it] for it in items], axis=0),
                          jnp.ones((MLSTM_DK, HEAD_V), BF16), 2)
        kw, sl, m_new, hn = {}, {}, {}, {}
        for n_, it in enumerate(items):
            den = qk_sum[n_ * c:(n_ + 1) * c] + sc[it] * qn_sum[n_ * c:(n_ + 1) * c]
            hc = num[it] / jnp.maximum(jnp.abs(den), jnp.exp(-m_t[it]))
            m_new[it] = m_t[it][tv - 1:tv, :]
            bl = bc[it][tv - 1:tv, :]
            wl = jnp.exp(bl - bc[it] + ic[it] - m_new[it])
            sl[it] = jnp.exp(bl + mprev[it] - m_new[it])
            kw[it] = kh[it] * wl
            hn[it] = hc
        kv = {it: _dot_tn(kw[it].astype(BF16), vh[it]) for it in items}
        for l, (scr, cx) in enumerate(zip(scrs, ctx)):
            xbuf, cm_scr, nm_scr, mm_scr = scr
            heads = [(l, h) for h in range(N_HEADS)]
            cm_scr[...] = jnp.concatenate([sl[it] * cmh[it] + kv[it] for it in heads], axis=0)
            nm_scr[...] = jnp.concatenate([sl[it] * nh[it] + jnp.sum(kw[it], axis=0, keepdims=True)
                                           for it in heads], axis=1)
            mm_new = cx["mm"]
            for it in heads:
                mm_new = jnp.where(lane == it[1], m_new[it], mm_new)
            mm_scr[...] = mm_new
            o_ref[cx["rows"], :] = jnp.concatenate([hn[it] for it in heads], axis=1)

    def load_state(scr, seq):
        xbuf, cm_scr, nm_scr, mm_scr = scr
        cm_scr[...] = cm_in[seq]
        nm_scr[...] = nm_in[seq]
        mm_scr[...] = mm_in[seq]
        xbuf[0:8, :] = cv_in[seq]

    def store_state(scr, seq):
        xbuf, cm_scr, nm_scr, mm_scr = scr
        cm_out[seq] = cm_scr[...]
        nm_out[seq] = nm_scr[...]
        mm_out[seq] = mm_scr[...]
        cv_out[seq] = xbuf[0:8, :]

    assert nseq == lanes or nst == 1
    step = pl.program_id(1)

    def lane_scr(u):
        return tuple(s_.at[u] for s_ in scratch)

    def run_chunks(grp):
        for start, c, tv, count in chunks:
            def body(ci, carry2, start=start, c=c, tv=tv):
                firsts = [seq * rps + start + ci * c for _, seq in grp]
                chunk([scr for scr, _ in grp],
                      [f if isinstance(f, int) else pl.multiple_of(f, 8) for f in firsts], c, tv)
                return carry2

            if count == 1:
                body(0, 0)
            else:
                lax.fori_loop(0, count, body, 0)

    def seq_group(sg, carry):
        def lane(u, seq):
            scr = lane_scr(u)
            if nst == 1:
                load_state(scr, seq)
            else:
                pl.when(step == 0)(functools.partial(load_state, scr, seq))
            return scr

        def finish(scr, seq):
            if nst == 1:
                store_state(scr, seq)
            else:
                pl.when(step == nst - 1)(functools.partial(store_state, scr, seq))

        if max(c for _, c, _, _ in chunks) <= 16:
            grp = [(lane(u, sg * lanes + u), sg * lanes + u) for u in range(lanes)]
            run_chunks(grp)
            for scr, seq in grp:
                finish(scr, seq)
        else:
            def one_lane(u, carry2):
                seq = sg * lanes + u
                scr = lane(u, seq)
                run_chunks([(scr, seq)])
                finish(scr, seq)
                return carry2

            lax.fori_loop(0, lanes, one_lane, 0)
        return carry

    if nseq == lanes:
        seq_group(0, 0)
    else:
        lax.fori_loop(0, nseq // lanes, seq_group, 0)

    nchunk = (nseq * rps) // pc

    def finish_chunk(ci, carry):
        rows = pl.ds(ci * pc if isinstance(ci, int) else pl.multiple_of(ci * pc, 8), pc)
        o = o_ref[rows, :]
        ms = _seg_dot(o * o, ovv_ref[...], 2) * (1.0 / HEAD_V)
        o_ref[rows, :] = o * lax.rsqrt(ms + EPS) * par_ref[2:3, :] * jax.nn.sigmoid(z_ref[rows, 640:896])
        return carry

    if nchunk == 1:
        finish_chunk(0, 0)
    else:
        lax.fori_loop(0, nchunk, finish_chunk, 0)


def _mlstm_call(z, par, bcol, cw, sel, ovv, cm, nm, mm, cv, o_prev, *, nrows, g):
    if g.full:
        cmain = max(d for d in (MLSTM_CHUNK, 64, 32, 16, 8) if d <= g.rps)
        lead = g.rps % cmain
        chunks = ([(0, lead, lead, 1)] if lead else []) + [(lead, cmain, cmain, g.rps // cmain)]
        assert lead % 8 == 0
    else:
        chunks = [(0, g.rps, g.tv, 1)]
    cmax = max(c for _, c, _, _ in chunks)
    kern = functools.partial(_mlstm_kernel, chunks=tuple(chunks), nseq=g.nseq, rps=g.rps, nst=g.nst, pc=g.pc)
    const = lambda b, s: (0, 0)
    st3 = lambda shp: pl.BlockSpec((g.nseq,) + shp, lambda b, s: (b, 0, 0))
    return pl.pallas_call(
        kern,
        grid=(g.ngrid, g.nst),
        in_specs=[
            pl.BlockSpec((g.rows, SEC), lambda b, s: (g.base_blk + b * g.nst + s, 3)),
            pl.BlockSpec((8, 256), const),
            pl.BlockSpec((8, 1), const),
            pl.BlockSpec((8, 256), const),
            pl.BlockSpec((8, 128), const),
            pl.BlockSpec((256, 256), const),
            st3((128, 64)), st3((1, 128)), st3((1, 128)), st3((8, 256)),
            pl.BlockSpec(memory_space=pl.ANY),
        ],
        input_output_aliases={10: 0},
        out_specs=[
            pl.BlockSpec((g.rows, 256), lambda b, s: (g.base_blk + b * g.nst + s, 0)),
            st3((128, 64)), st3((1, 128)), st3((1, 128)), st3((8, 256)),
        ],
        out_shape=[
            jax.ShapeDtypeStruct((nrows, 256), F32),
            jax.ShapeDtypeStruct((g.nb, 128, 64), F32),
            jax.ShapeDtypeStruct((g.nb, 1, 128), F32),
            jax.ShapeDtypeStruct((g.nb, 1, 128), F32),
            jax.ShapeDtypeStruct((g.nb, 8, 256), F32),
        ],
        scratch_shapes=[
            pltpu.VMEM((g.lanes, 8 + cmax, 256), F32),
            pltpu.VMEM((g.lanes, 128, 64), F32),
            pltpu.VMEM((g.lanes, 1, 128), F32),
            pltpu.VMEM((g.lanes, 1, 128), F32),
        ],
        compiler_params=_cparams("parallel", "arbitrary"),
    )(z, par, bcol, cw, sel, ovv, cm, nm, mm, cv, o_prev)


def _merge_kernel(o0_ref, o1_ref, o2_ref, o3_ref, x_ref, g_ref, wg_ref, wb_ref, wout_ref, out_ref, h_scr, acc_scr):
    n = pl.program_id(1)

    @pl.when(n == 0)
    def _():
        h_scr[...] = _rms_bf16(x_ref[...], g_ref[...])
        acc_scr[...] = jnp.zeros_like(acc_scr)

    gate = jax.nn.sigmoid(_dot(h_scr[...], wg_ref[...]))
    for idx, o_ref in enumerate((o0_ref, o1_ref, o2_ref, o3_ref)):
        @pl.when(n == idx)
        def _(o_ref=o_ref):
            acc_scr[...] += _dot(o_ref[...].astype(BF16), wb_ref[0]) * gate

    @pl.when(n == 3)
    def _():
        out_ref[...] = x_ref[...] + _dot(acc_scr[...].astype(BF16), wout_ref[...])


def _merge_call(outs, x, g, wg, wb, wout, tm, o_tile):
    n = x.shape[0]
    o_spec = pl.BlockSpec((tm, 256), lambda i, j: (o_tile(i), 0))
    return pl.pallas_call(
        _merge_kernel,
        grid=(n // tm, 4),
        in_specs=[
            o_spec, o_spec, o_spec, o_spec,
            pl.BlockSpec((tm, D_MODEL), lambda i, j: (i, 0)),
            pl.BlockSpec((1, D_MODEL), lambda i, j: (0, 0)),
            pl.BlockSpec((D_MODEL, D_MODEL), lambda i, j: (0, j)),
            pl.BlockSpec((1, 256, D_MODEL), lambda i, j: (j, 0, 0)),
            pl.BlockSpec((D_MODEL, D_MODEL), lambda i, j: (0, 0)),
        ],
        out_specs=pl.BlockSpec((tm, D_MODEL), lambda i, j: (i, 0)),
        out_shape=jax.ShapeDtypeStruct((n, D_MODEL), F32),
        scratch_shapes=[pltpu.VMEM((tm, D_MODEL), BF16), pltpu.VMEM((tm, D_MODEL), F32)],
        compiler_params=_cparams("parallel", "arbitrary"),
    )(*outs, x, g, wg, wb, wout)


def _mlp_kernel(x_ref, g_ref, gf_ref, wup_ref, wdn_ref, out_ref, h_scr, acc_scr, *, final):
    j = pl.program_id(1)

    @pl.when(j == 0)
    def _():
        h_scr[...] = _rms_bf16(x_ref[...], g_ref[...])
        acc_scr[...] = jnp.zeros_like(acc_scr)

    u = jnp.maximum(_dot(h_scr[...], wup_ref[...]), 0.0)
    acc_scr[...] += _dot((u * u).astype(BF16), wdn_ref[...])

    @pl.when(j == pl.num_programs(1) - 1)
    def _():
        y = x_ref[...] + acc_scr[...]
        if final:
            y = y * lax.rsqrt(jnp.mean(y * y, axis=-1, keepdims=True) + EPS) * gf_ref[...]
        out_ref[...] = y


def _mlp_call(x, g, gf, wup, wdn, tm, final):
    n = x.shape[0]
    tf = 1024
    return pl.pallas_call(
        functools.partial(_mlp_kernel, final=final),
        grid=(n // tm, D_FF // tf),
        in_specs=[
            pl.BlockSpec((tm, D_MODEL), lambda i, j: (i, 0)),
            pl.BlockSpec((1, D_MODEL), lambda i, j: (0, 0)),
            pl.BlockSpec((1, D_MODEL), lambda i, j: (0, 0)),
            pl.BlockSpec((D_MODEL, tf), lambda i, j: (0, j)),
            pl.BlockSpec((tf, D_MODEL), lambda i, j: (j, 0)),
        ],
        out_specs=pl.BlockSpec((tm, D_MODEL), lambda i, j: (i, 0)),
        out_shape=jax.ShapeDtypeStruct((n, D_MODEL), F32),
        scratch_shapes=[pltpu.VMEM((tm, D_MODEL), BF16), pltpu.VMEM((tm, D_MODEL), F32)],
        compiler_params=_cparams("parallel", "arbitrary"),
    )(x, g, gf, wup, wdn)


def _seg_ones(rows_per_head, cols_per_head):
    r = np.arange(N_HEADS * rows_per_head)[:, None] // rows_per_head
    c = np.arange(N_HEADS * cols_per_head)[None, :] // cols_per_head
    return (r == c).astype(np.float32)


def _rows(*vecs, width=256, nrows=8):
    rows = []
    for v in vecs:
        v = jnp.asarray(v, F32).reshape(-1)
        rows.append(jnp.pad(v, (0, width - v.shape[0])))
    rows += [jnp.zeros((width,), F32)] * (nrows - len(rows))
    return jnp.stack(rows)


def _layout_w_in(w):
    d = w.shape[0]
    zeros = lambda n: jnp.zeros((d, n), w.dtype)
    gl0, ml0, gt0 = 2048, 2832, 3608
    parts = [
        w[:, 0:2048],
        w[:, gl0:gl0 + 512], w[:, gl0 + 512:gl0 + 528], zeros(112), w[:, gl0 + 528:gl0 + 784], zeros(128),
        w[:, ml0:ml0 + 512], w[:, ml0 + 512:ml0 + 520], zeros(120), w[:, ml0 + 520:ml0 + 776], zeros(128),
    ]
    wz = jnp.concatenate(parts, axis=1)
    assert wz.shape[1] == Z_COLS
    return wz.astype(BF16), w[:, gt0:gt0 + 4 * D_MODEL].astype(BF16)


class _Group:
    def __init__(self, nb, trow, tv, row0, max_seq):
        self.nb, self.trow, self.tv, self.row0 = nb, trow, tv, row0
        self.full = tv == trow
        self.c = SUB if self.full else trow
        self.tvs = min(tv, self.c)
        if self.full:
            nblk = trow // self.c
            self.rps = self.c * max(d for d in range(1, min(nblk, 64) + 1) if nblk % d == 0)
        else:
            self.rps = trow
        self.nst = trow // self.rps
        self.nsub = self.rps // self.c
        self.nseq = max(d for d in range(1, max_seq + 1) if nb % d == 0 and row0 % (d * self.rps) == 0)
        self.lanes = (8 if not self.full and self.nseq % 8 == 0 else 4 if self.nseq % 4 == 0
                      else 2 if self.nseq % 2 == 0 else 1)
        assert self.nseq == self.lanes or self.nst == 1
        self.rows = self.nseq * self.rps
        self.ngrid = nb // self.nseq
        self.base_blk = row0 // self.rows
        self.pc = _pick_tile(self.rps, 384) if self.rps >= 64 else self.rows


def kernel(x_prompt, x_sample, state_hgrn, state_rwkv, state_rwkv_shift, state_gla, state_mlstm_c, state_mlstm_n, state_mlstm_m, state_mlstm_conv, meta_tokens, norm_mix, norm_mlp, norm_final, w_in, hgrn_lb, hgrn_norm, rwkv_mu, rwkv_w0, rwkv_w_up, rwkv_a0, rwkv_a_up, rwkv_g_up, rwkv_k_k, rwkv_k_a, rwkv_r_k, rwkv_ln_w, rwkv_ln_b, gla_gate_up, gla_gate_b, gla_norm, mlstm_conv_w, mlstm_conv_b, mlstm_i_b, mlstm_f_b, mlstm_norm, w_branch, w_out, w_up, w_down):
    depth = w_in.shape[0]
    bp, seq, _ = x_prompt.shape
    bs, dseq, _ = x_sample.shape
    tp = N_META + seq
    assert tp % SUB == 0 and CONV_W - 1 <= dseq <= SAMPLE_PAD
    dt = x_prompt.dtype

    gp = _Group(bp, tp, tp, 0, 4)
    gs = _Group(bs, SAMPLE_PAD, dseq, bp * tp, 16)
    tm = gp.rps
    xs = jnp.pad(x_sample, ((0, 0), (0, SAMPLE_PAD - dseq), (0, 0))).reshape(bs * SAMPLE_PAD, D_MODEL)
    n_real = bp * tp + bs * SAMPLE_PAD
    n_rows = -(-n_real // tm) * tm
    pieces = []
    for b in range(bp):
        pieces += [meta_tokens.astype(dt), x_prompt[b]]
    x = jnp.concatenate(pieces + [xs, jnp.zeros((n_rows - n_real, D_MODEL), dt)], axis=0).astype(F32)

    def mixer_tile(i):
        b, s = i // gp.nst, i % gp.nst
        p = (b // gp.nseq) * (gp.nst * gp.nseq) + s * gp.nseq + b % gp.nseq
        return jnp.where(i < bp * gp.nst, p, i)

    tm_mlp = 2 * tm if (n_rows // tm) % 2 == 0 and 2 * tm <= 1536 else tm

    ones_vv = jnp.asarray(_seg_ones(64, 64), BF16)
    ones_gl = jnp.asarray(_seg_ones(GLA_DK, 64), BF16)
    mask_gl = jnp.asarray(_seg_ones(64, GLA_DK), F32)
    sel8 = jnp.asarray(np.eye(8, 128, dtype=np.float32), BF16)
    hmask = {g.c: jnp.asarray(np.kron(np.eye(2), np.ones((g.c, 64))), F32) for g in (gp, gs)}
    ehead = {c_: jnp.concatenate([m_, m_], axis=0).astype(BF16) for c_, m_ in hmask.items()}
    mask_pair = jnp.asarray(np.kron(np.eye(2), np.ones((64, 64))), F32)

    lb_cs = jnp.cumsum(jax.nn.softmax(hgrn_lb.astype(F32), axis=0), axis=0)
    lb_all = lb_cs - lb_cs[:1]

    def init_states(states, nb, zero):
        s_hg, s_rw, shift, s_gl, c_ml, n_ml, m_ml, conv = states
        if zero:
            z = lambda *shp: jnp.zeros((depth, nb) + shp, F32)
            return dict(hg=z(256, 64), rw=z(256, 64), shift=z(1, SEC), gl=z(256, GLA_DK), cm=z(128, 64),
                        nm=z(1, 128), mm=z(1, 128), cv=z(8, 256))
        f = lambda a: a.astype(F32)
        return dict(
            hg=f(s_hg).transpose(0, 1, 2, 4, 3).reshape(depth, nb, 256, 64),
            rw=f(s_rw).reshape(depth, nb, 256, 64),
            shift=f(shift).reshape(depth, nb, 1, SEC),
            gl=f(s_gl).transpose(0, 1, 2, 4, 3).reshape(depth, nb, 256, GLA_DK),
            cm=f(c_ml).reshape(depth, nb, 128, 64),
            nm=f(n_ml).reshape(depth, nb, 1, 128),
            mm=jnp.pad(f(m_ml), ((0, 0), (0, 0), (0, 124))).reshape(depth, nb, 1, 128),
            cv=jnp.pad(f(conv), ((0, 0), (0, 0), (5, 0), (0, 0))),
        )

    sample_states = (state_hgrn, state_rwkv, state_rwkv_shift, state_gla,
                     state_mlstm_c, state_mlstm_n, state_mlstm_m, state_mlstm_conv)
    st_p = init_states(sample_states, bp, True)
    st_s = init_states(sample_states, bs, False)
    new_p = {k: [] for k in st_p}
    new_s = {k: [] for k in st_s}

    for l in range(depth):
        wz, wg = _layout_w_in(w_in[l])
        g_mix = norm_mix[l].reshape(1, D_MODEL).astype(F32)
        z = _in_proj(x, g_mix, wz, tm, mixer_tile)

        lb = lb_all[l]
        par_hg = _rows(jnp.log(lb), jnp.log1p(-lb), 1.0 - lb, hgrn_norm[l])
        par_gl = _rows(gla_gate_b[l], jnp.zeros((1,)), jnp.zeros((1,)), gla_norm[l])
        gup_gl = jnp.pad(gla_gate_up[l], ((0, 128 - gla_gate_up.shape[1]), (0, 0))).astype(BF16)
        gup_dummy = jnp.zeros((128, 128), BF16)
        par_rw = _rows(rwkv_w0[l], rwkv_a0[l], rwkv_k_k[l], rwkv_k_a[l], rwkv_r_k[l], rwkv_ln_w[l], rwkv_ln_b[l])
        nw, na = rwkv_w_up.shape[1], rwkv_a_up.shape[1]
        wup_p = jnp.pad(rwkv_w_up[l], ((0, 256 - nw), (0, 0))).astype(BF16)
        aup_p = jnp.pad(rwkv_a_up[l], ((nw, 256 - nw - na), (0, 0))).astype(BF16)
        gup_p = jnp.pad(rwkv_g_up[l], ((nw + na, 0), (0, 0))).astype(BF16)
        mu = rwkv_mu[l].reshape(1, SEC).astype(F32)
        gate_b = jnp.concatenate([mlstm_i_b[l], mlstm_f_b[l]]).astype(F32)
        par_ml = _rows(mlstm_conv_b[l], gate_b, mlstm_norm[l])
        bcol_ml = gate_b.reshape(8, 1)
        cw_ml = _rows(*[mlstm_conv_w[l, j] for j in range(CONV_W)])

        o_hg = o_rw = o_gl = o_ml = None
        for g, st, new in ((gp, st_p, new_p), (gs, st_s, new_s)):
            prev = lambda o: jnp.zeros((n_rows, 256), F32) if o is None else o
            o_hg, s_hg = _gla_call(z, 0, par_hg, gup_dummy, ones_vv, ones_vv, mask_pair, st["hg"][l], prev(o_hg),
                                   mode="hgrn", dk=64, nrows=n_rows, g=g)
            o_gl, s_gl = _gla_call(z, 2, par_gl, gup_gl, ones_gl, ones_vv, mask_gl, st["gl"][l], prev(o_gl),
                                   mode="gla", dk=GLA_DK, nrows=n_rows, g=g)
            o_rw, nshift, s_rw = _rwkv_call(z, mu, par_rw, wup_p, aup_p, gup_p, ones_vv, hmask[g.c], ehead[g.c],
                                            mask_pair, st["shift"][l], st["rw"][l], prev(o_rw), nrows=n_rows, g=g)
            o_ml, s_cm, s_nm, s_mm, s_cv = _mlstm_call(z, par_ml, bcol_ml, cw_ml, sel8, ones_vv, st["cm"][l], st["nm"][l],
                                                        st["mm"][l], st["cv"][l], prev(o_ml), nrows=n_rows, g=g)
            for key, val in (("hg", s_hg), ("rw", s_rw), ("shift", nshift), ("gl", s_gl), ("cm", s_cm),
                             ("nm", s_nm), ("mm", s_mm), ("cv", s_cv)):
                new[key].append(val)

        x = _merge_call((o_hg, o_rw, o_gl, o_ml), x, g_mix, wg, w_branch[l].astype(BF16), w_out[l].astype(BF16), tm,
                        mixer_tile)
        x = _mlp_call(x, norm_mlp[l].reshape(1, D_MODEL).astype(F32), norm_final.reshape(1, D_MODEL).astype(F32),
                      w_up[l].astype(BF16), w_down[l].astype(BF16), tm_mlp, final=(l == depth - 1))

    y_prompt = jnp.stack([x[b * tp + N_META:(b + 1) * tp] for b in range(bp)]).astype(dt)
    y_sample = x[gs.row0:gs.row0 + bs * SAMPLE_PAD].reshape(bs, SAMPLE_PAD, D_MODEL)[:, :dseq].astype(dt)

    def finish(new, nb):
        st = {k: jnp.stack(v) for k, v in new.items()}
        return (
            st["hg"].reshape(depth, nb, N_HEADS, 64, 64).transpose(0, 1, 2, 4, 3),
            st["rw"].reshape(depth, nb, N_HEADS, 64, 64),
            st["shift"].reshape(depth, nb, SEC),
            st["gl"].reshape(depth, nb, N_HEADS, 64, GLA_DK).transpose(0, 1, 2, 4, 3),
            st["cm"].reshape(depth, nb, N_HEADS, MLSTM_DK, 64),
            st["nm"].reshape(depth, nb, N_HEADS, MLSTM_DK),
            st["mm"].reshape(depth, nb, 128)[:, :, :N_HEADS],
            st["cv"][:, :, 5:8, :],
        )

    outs_p = tuple(a.astype(dt) for a in finish(new_p, bp))
    outs_s = tuple(a.astype(dt) for a in finish(new_s, bs))
    return (y_prompt, y_sample) + outs_p + outs_s
```

```python
import functools

import numpy as np
import jax
import jax.numpy as jnp
from jax import lax
from jax.experimental import pallas as pl
from jax.experimental.pallas import tpu as pltpu

F32 = jnp.float32
BF16 = jnp.bfloat16

D_MODEL = 1024
N_META = 16
N_HEADS = 4
HEAD_V = 64
GLA_DK = 32
MLSTM_DK = 32
GLA_GATE_TAU = 16.0
RWKV_LN_EPS = 64e-5
CONV_W = 4
D_FF = 4 * D_MODEL
EPS = 1e-6

SEC = 1024
Z_COLS = 4 * SEC
SUB = 16
SAMPLE_PAD = 8
MLSTM_CHUNK = 128
VMEM_LIMIT = 56 * 1024 * 1024


def _cparams(*sem):
    return pltpu.CompilerParams(dimension_semantics=sem, vmem_limit_bytes=VMEM_LIMIT)


def _pick_tile(n, max_tile, mult=8):
    best = None
    t = mult
    while t <= min(n, max_tile):
        if n % t == 0:
            best = t
        t += mult
    assert best is not None, (n, max_tile, mult)
    return best


def _split2(x):
    hi = x.astype(BF16)
    lo = (x - hi.astype(F32)).astype(BF16)
    return hi, lo


def _split3(x):
    hi = x.astype(BF16)
    r1 = x - hi.astype(F32)
    mid = r1.astype(BF16)
    lo = (r1 - mid.astype(F32)).astype(BF16)
    return hi, mid, lo


def _dot(a, b):
    return jnp.dot(a, b, preferred_element_type=F32)


def _dot_nt(a, b):
    return lax.dot_general(a, b, (((1,), (1,)), ((), ())), preferred_element_type=F32)


def _dot_tn(a, b):
    return lax.dot_general(a, b, (((0,), (0,)), ((), ())), preferred_element_type=F32)


def _seg_dot(x, m_bf16, parts):
    ps = _split2(x) if parts == 2 else _split3(x)
    acc = _dot(ps[0], m_bf16)
    for p in ps[1:]:
        acc = acc + _dot(p, m_bf16)
    return acc


def _seg_dot_l(m_bf16, x, parts):
    ps = _split2(x) if parts == 2 else _split3(x)
    acc = _dot(m_bf16, ps[0])
    for p in ps[1:]:
        acc = acc + _dot(m_bf16, p)
    return acc


def _log_sigmoid(x):
    return jnp.minimum(x, 0.0) - jnp.log1p(jnp.exp(-jnp.abs(x)))


def _silu(x):
    return x * jax.nn.sigmoid(x)


def _tri(n, upper=False):
    r = lax.broadcasted_iota(jnp.int32, (n, n), 0)
    c = lax.broadcasted_iota(jnp.int32, (n, n), 1)
    return (r <= c) if upper else (r >= c)


def _cumsum_rows(x, rowi):
    d = 1
    while d < x.shape[0]:
        x = x + jnp.where(rowi >= d, pltpu.roll(x, d, axis=0), 0.0)
        d *= 2
    return x


def _rms_bf16(x, g):
    return (x * lax.rsqrt(jnp.mean(x * x, axis=-1, keepdims=True) + EPS) * g).astype(BF16)


def _in_proj_kernel(x_ref, g_ref, w_ref, z_ref):
    h = _rms_bf16(x_ref[...], g_ref[...])
    for j in range(Z_COLS // SEC):
        z_ref[:, j * SEC:(j + 1) * SEC] = _dot(h, w_ref[:, j * SEC:(j + 1) * SEC])


def _in_proj(x, g, w, tm, out_tile):
    n = x.shape[0]
    return pl.pallas_call(
        _in_proj_kernel,
        grid=(n // tm,),
        in_specs=[
            pl.BlockSpec((tm, D_MODEL), lambda i: (i, 0)),
            pl.BlockSpec((1, D_MODEL), lambda i: (0, 0)),
            pl.BlockSpec((D_MODEL, Z_COLS), lambda i: (0, 0)),
        ],
        out_specs=pl.BlockSpec((tm, Z_COLS), lambda i: (out_tile(i), 0)),
        out_shape=jax.ShapeDtypeStruct((n, Z_COLS), F32),
        compiler_params=_cparams("parallel"),
    )(x, g, w)


def _gla_kernel(z_ref, par_ref, gup_ref, okv_ref, ovv_ref, mask_ref, st_in_ref, o_prev_ref,
                o_ref, st_out_ref, *st_scrs, mode, c, nsub, tv, dk, nseq, rps, nst, pc):
    del o_prev_ref
    voff = 512 if mode == "hgrn" else 256
    rowi = lax.broadcasted_iota(jnp.int32, (c, 1), 0)
    valid = rowi < tv
    hpg = 128 // dk
    groups = [(slice(i * hpg * HEAD_V, (i + 1) * hpg * HEAD_V), slice(i * 128, (i + 1) * 128))
              for i in range(N_HEADS // hpg)]

    def sub_block(st_scr, r0):
        rows = pl.ds(r0, c)
        if mode == "hgrn":
            q = z_ref[rows, 0:256]
            hf = z_ref[rows, 256:512]
            a = par_ref[0:1, :]
            cc = par_ref[1:2, :] + _log_sigmoid(hf)
            loga = jnp.maximum(a, cc) + jnp.log1p(jnp.exp(-jnp.abs(a - cc)))
            k = par_ref[2:3, :] * jax.nn.sigmoid(-hf)
        else:
            q = z_ref[rows, 0:128] * (GLA_DK ** -0.5)
            k = z_ref[rows, 128:256]
            ga = z_ref[rows, 512:640]
            gl = _dot(ga.astype(BF16), gup_ref[...]) + par_ref[0:1, 0:128]
            loga = _log_sigmoid(gl) * (1.0 / GLA_GATE_TAU)
        v = z_ref[rows, voff:voff + 256]
        if tv < c:
            loga = jnp.where(valid, loga, 0.0)
            k = jnp.where(valid, k, 0.0)
            v = jnp.where(valid, v, 0.0)
        b = _cumsum_rows(loga, rowi)

        qe = (q * jnp.exp(b)).astype(BF16)
        o = jnp.concatenate([_dot_nt(qe[:, ks], scr[...].astype(BF16))
                             for scr, (_, ks) in zip(st_scr, groups)], axis=1)

        lo_rows = [8 * (s // 8) for s in range(tv)]
        pieces = []
        for s in range(tv):
            rs = slice(lo_rows[s], c)
            e = jnp.exp(jnp.where(rowi[rs] >= s, b[rs] - b[s:s + 1, :], -jnp.inf))
            pieces.append(q[rs] * e * k[s:s + 1, :])
        return rows, lo_rows, k, v, b, o, jnp.concatenate(pieces, axis=0)

    def sub_block_finish(st_scr, part, sm):
        rows, lo_rows, k, v, b, o, _ = part
        acc = {}
        off = 0
        for s in range(tv):
            n = c - lo_rows[s]
            contrib = sm[off:off + n] * v[s:s + 1, :]
            off += n
            acc[lo_rows[s]] = contrib if lo_rows[s] not in acc else acc[lo_rows[s]] + contrib
        for lo, a_ in acc.items():
            o = o + (a_ if lo == 0 else jnp.concatenate([jnp.zeros((lo, 256), F32), a_], axis=0))

        bl = b[c - 1:c, :]
        kdec = (k * jnp.exp(bl - b)).astype(BF16)
        vb = v.astype(BF16)
        ebl = jnp.exp(bl)
        upd = [_dot_tn(vb[:, vs], kdec[:, ks]) for vs, ks in groups]
        for scr, u_, (_, ks) in zip(st_scr, upd, groups):
            scr[...] = scr[...] * ebl[:, ks] + u_ * mask_ref[...]
        o_ref[rows, :] = o

    def sub_blocks(scrs, row0s):
        parts = [sub_block(scr, r0) for scr, r0 in zip(scrs, row0s)]
        sm = _dot(jnp.concatenate([p[-1] for p in parts], axis=0).astype(BF16), okv_ref[...])
        n = parts[0][-1].shape[0]
        for i, (scr, part) in enumerate(zip(scrs, parts)):
            sub_block_finish(scr, part, sm[i * n:(i + 1) * n])

    def load_state(st_scr, seq):
        for scr, (vs, _) in zip(st_scr, groups):
            scr[...] = jnp.concatenate([st_in_ref[seq, vs, :]] * hpg, axis=1) * mask_ref[...]

    def store_state(st_scr, seq):
        for scr, (vs, _) in zip(st_scr, groups):
            st = scr[...]
            acc = st[:, 0:dk]
            for h in range(1, hpg):
                acc = acc + st[:, h * dk:(h + 1) * dk]
            st_out_ref[seq, vs, :] = acc

    ng = len(groups)
    lane_scrs = [st_scrs[i * ng:(i + 1) * ng] for i in range(len(st_scrs) // ng)]
    _run_lanes(lane_scrs, nseq, rps, nsub, c, nst, load_state, store_state, sub_blocks)

    goff = 768 if mode == "hgrn" else 640
    nchunk = (nseq * rps) // pc

    def finish_chunk(ci, carry):
        rows = pl.ds(ci * pc if isinstance(ci, int) else pl.multiple_of(ci * pc, 8), pc)
        o = o_ref[rows, :]
        ms = _seg_dot(o * o, ovv_ref[...], 2) * (1.0 / HEAD_V)
        o_ref[rows, :] = o * lax.rsqrt(ms + EPS) * par_ref[3:4, :] * _silu(z_ref[rows, goff:goff + 256])
        return carry

    if nchunk == 1:
        finish_chunk(0, 0)
    else:
        lax.fori_loop(0, nchunk, finish_chunk, 0)


def _run_lanes(scrs, nseq, rps, nsub, c, nst, load_state, store_state, sub_blocks):
    lanes = len(scrs)
    assert nseq == lanes or nst == 1
    step = pl.program_id(1)

    def seq_group(sg, carry):
        seqs = [sg * lanes + u for u in range(lanes)]
        for scr, seq in zip(scrs, seqs):
            if nst == 1:
                load_state(scr, seq)
            else:
                pl.when(step == 0)(functools.partial(load_state, scr, seq))

        def blocks(j, carry2):
            starts = [seq * rps + j * c for seq in seqs]
            sub_blocks(scrs, [s_ if isinstance(s_, int) else pl.multiple_of(s_, 8) for s_ in starts])
            return carry2

        if nsub == 1:
            blocks(0, 0)
        else:
            lax.fori_loop(0, nsub, blocks, 0, unroll=2)
        for scr, seq in zip(scrs, seqs):
            if nst == 1:
                store_state(scr, seq)
            else:
                pl.when(step == nst - 1)(functools.partial(store_state, scr, seq))
        return carry

    if nseq == lanes:
        seq_group(0, 0)
    else:
        lax.fori_loop(0, nseq // lanes, seq_group, 0)


def _gla_call(z, sec, par, gup, okv, ovv, mask, st_in, o_prev, *, mode, dk, nrows, g):
    hk = N_HEADS * dk
    grows = (128 // dk) * HEAD_V
    kern = functools.partial(_gla_kernel, mode=mode, c=g.c, nsub=g.nsub, tv=g.tvs, dk=dk,
                             nseq=g.nseq, rps=g.rps, nst=g.nst, pc=g.pc)
    const = lambda b, s: (0, 0)
    return pl.pallas_call(
        kern,
        grid=(g.ngrid, g.nst),
        in_specs=[
            pl.BlockSpec((g.rows, SEC), lambda b, s: (g.base_blk + b * g.nst + s, sec)),
            pl.BlockSpec((8, 256), const),
            pl.BlockSpec((128, 128), const),
            pl.BlockSpec((hk, 256), const),
            pl.BlockSpec((256, 256), const),
            pl.BlockSpec((grows, 128), const),
            pl.BlockSpec((g.nseq, 256, dk), lambda b, s: (b, 0, 0)),
            pl.BlockSpec(memory_space=pl.ANY),
        ],
        input_output_aliases={7: 0},
        out_specs=[
            pl.BlockSpec((g.rows, 256), lambda b, s: (g.base_blk + b * g.nst + s, 0)),
            pl.BlockSpec((g.nseq, 256, dk), lambda b, s: (b, 0, 0)),
        ],
        out_shape=[
            jax.ShapeDtypeStruct((nrows, 256), F32),
            jax.ShapeDtypeStruct((g.nb, 256, dk), F32),
        ],
        scratch_shapes=[pltpu.VMEM((grows, 128), F32)] * (g.lanes * (256 // grows)),
        compiler_params=_cparams("parallel", "arbitrary"),
    )(z, par, gup, okv, ovv, mask, st_in, o_prev)


def _rwkv_kernel(z_ref, mu_ref, par_ref, wup_ref, aup_ref, gup_ref, ovv_ref, hmask_ref, ehead_ref, mask_ref,
                 shift_ref, st_in_ref, o_prev_ref,
                 o_ref, nshift_ref, st_out_ref,
                 prev_scr, fix_scr, r_scr, lw_scr, k_scr, v_scr, kk_scr, ka_scr, g_scr, bonus_scr, y_scr, *st_scrs,
                 c, nsub, tv, nseq, rps, nst, pc):
    del o_prev_ref
    step = pl.program_id(1)

    def shifted_chunk(ci):
        r0 = ci * pc if isinstance(ci, int) else pl.multiple_of(ci * pc, 8)
        zr = z_ref[pl.ds(r0, pc), :]
        rolled = pltpu.roll(zr, 1, axis=0)
        if pc <= rps:
            cps = rps // pc
            seq = ci // cps
            rowi = lax.broadcasted_iota(jnp.int32, (pc, 1), 0)
            before = z_ref[pl.ds(pl.multiple_of(jnp.maximum(r0 - 8, 0), 8), 8), :][7:8, :]
            first = jnp.where(ci % cps == 0, prev_scr[pl.ds(seq, 1), :], before)
            prev = jnp.where(rowi == 0, first, rolled)
            prev_scr[pl.ds(seq, 1), :] = zr[pc - 1:pc, :]
            nshift_ref[seq] = zr[pc - 1:pc, :]
        else:
            fix_scr[...] = rolled
            for i in range(nseq):
                fix_scr[i * rps:i * rps + 1, :] = shift_ref[i]
                nshift_ref[i] = zr[i * rps + tv - 1:i * rps + tv, :]
            prev = fix_scr[...]
        return r0, zr, prev

    if pc <= rps:
        @pl.when(step == 0)
        def _():
            for i in range(nseq):
                prev_scr[i:i + 1, :] = shift_ref[i]

    def prep_chunk(ci, carry):
        r0, zr, prev = shifted_chunk(ci)
        rows = pl.ds(r0, pc)
        _rwkv_prep_rows(zr, prev, rows)
        return carry

    def _rwkv_prep_rows(zr, prev, rows):
        nrow = zr.shape[0]
        zs = zr + (prev - zr) * mu_ref[...]
        r = zs[:, 0:256]
        k = zs[:, 256:512]
        v = zs[:, 512:768]
        lo = zs[:, 768:1024]
        w0, a0, k_k, k_a, r_k = (par_ref[i:i + 1, :] for i in range(5))
        wl = w0 + _dot(jnp.tanh(lo).astype(BF16), wup_ref[...])
        wexp = -(jnp.maximum(-wl, 0.0) + jnp.log1p(jnp.exp(-jnp.abs(wl)))) - 0.5
        lw = -jnp.exp(wexp)
        a = jax.nn.sigmoid(a0 + _dot(lo.astype(BF16), aup_ref[...]))
        g_scr[rows, :] = _dot(jax.nn.sigmoid(lo).astype(BF16), gup_ref[...])
        kkp = k * k_k
        nrm = jnp.sqrt(_seg_dot(kkp * kkp, ovv_ref[...], 2))
        kk = kkp / jnp.maximum(nrm, 1e-12)
        k2 = k * (1.0 + (a - 1.0) * k_a)
        bonus_scr[rows, :] = _seg_dot(r * k2 * r_k, ovv_ref[...], 3) * v
        ka = kk * a
        if tv < rps:
            okrow = lax.rem(lax.broadcasted_iota(jnp.int32, (nrow, 1), 0), rps) < tv
            lw, k2, v, kk, ka = (jnp.where(okrow, t_, 0.0) for t_ in (lw, k2, v, kk, ka))
        r_scr[rows, :] = r
        lw_scr[rows, :] = lw
        k_scr[rows, :] = k2
        v_scr[rows, :] = v
        kk_scr[rows, :] = kk
        ka_scr[rows, :] = ka

    nchunk = (nseq * rps) // pc
    if nchunk == 1:
        prep_chunk(0, 0)
    else:
        lax.fori_loop(0, nchunk, prep_chunk, 0)

    hp = 2
    ng = N_HEADS // hp
    gw = hp * 64
    rowc = lax.broadcasted_iota(jnp.int32, (c, 1), 0)
    src2 = lax.broadcasted_iota(jnp.int32, (1, 2 * hp * c), 1) & (c - 1)
    src = src2[:, 0:hp * c]
    strict = rowc > src
    incl = rowc >= src

    def sub_blocks(lane_scrs, row0s):
        stage_state(*stage_free(lane_scrs, row0s))

    def stage_free(lane_scrs, row0s):
        hm = hmask_ref[...]
        tile = lambda t_: jnp.concatenate([t_] * hp, axis=0) * hm
        nc = hp * c
        nsolve = min(tv, c - 1)
        units, cx = [], {}
        for i, r0 in enumerate(row0s):
            rows = pl.ds(r0, c)
            lw_, r_, k_, v_, kk_, ka_ = (s_[rows, :] for s_ in (lw_scr, r_scr, k_scr, v_scr, kk_scr, ka_scr))
            gam = _cumsum_rows(lw_, rowc)
            gl = gam[c - 1:c, :]
            ginv = jnp.exp(-gam)
            gend = jnp.exp(gl - gam)
            lh, ll = _split2(jnp.concatenate([kk_ * jnp.exp(gam - lw_), r_ * jnp.exp(gam)], axis=0))
            a_, kq_ = ka_ * ginv, k_ * ginv
            ae_, ke_ = ka_ * gend, k_ * gend
            for p in range(ng):
                ls = slice(p * gw, (p + 1) * gw)
                rh, rl = _split2(jnp.concatenate([tile(a_[:, ls]), tile(kq_[:, ls])], axis=0))
                units.append((i, p))
                cx[i, p] = dict(rows=rows, ls=ls, v=v_[:, ls], egl=jnp.exp(gl[:, ls]), lh=lh[:, ls], ll=ll[:, ls],
                                rh=rh, rl=rl, vbig=tile(v_[:, ls]).astype(BF16), scr=lane_scrs[i][p],
                                end=jnp.concatenate([ae_[:, ls], ke_[:, ls]], axis=0).astype(BF16))
        g2 = {u_: _dot_nt(jnp.concatenate([cx[u_]["lh"], cx[u_]["ll"]], axis=0), cx[u_]["rh"]) for u_ in units}
        g3 = {u_: _dot_nt(cx[u_]["lh"], cx[u_]["rl"]) for u_ in units}
        m_ab, m_ra, m_rk, bkv = {}, {}, {}, {}
        for u_ in units:
            gm = g2[u_][0:2 * c] + g2[u_][2 * c:4 * c] + g3[u_]
            m_ab[u_] = jnp.where(strict, gm[0:c, 0:nc], 0.0)
            m_ra[u_] = jnp.where(incl, gm[c:2 * c, 0:nc], 0.0)
            m_rk[u_] = jnp.where(incl, gm[c:2 * c, nc:2 * nc], 0.0)
            bkv[u_] = _dot(jnp.where(strict, gm[0:c, nc:2 * nc], 0.0).astype(BF16), cx[u_]["vbig"])
        m2 = {u_: jnp.concatenate(_split2(m_ab[u_]), axis=1) for u_ in units}
        cols = _dot(jnp.concatenate([jnp.where(src2 == s, m2[u_], jnp.zeros_like(m2[u_]))
                                     for u_ in units for s in range(nsolve)], axis=0), ehead_ref[...])
        return units, cx, m_ra, m_rk, bkv, cols

    def stage_state(units, cx, m_ra, m_rk, bkv, cols):
        nsolve = min(tv, c - 1)
        tile = lambda t_: jnp.concatenate([t_] * hp, axis=0) * hmask_ref[...]
        x0 = {u_: _dot_nt(cx[u_]["lh"], cx[u_]["scr"][...].astype(BF16)) for u_ in units}
        usol = {u_: x0[u_][0:c] + bkv[u_] for u_ in units}
        for s in range(nsolve):
            for n_, u_ in enumerate(units):
                o_ = (n_ * nsolve + s) * c
                usol[u_] = usol[u_] - cols[o_:o_ + c] * usol[u_][s:s + 1, :]
        ys = {u_: _dot(jnp.concatenate([-m_ra[u_], m_rk[u_]], axis=1).astype(BF16),
                       jnp.concatenate([tile(usol[u_]).astype(BF16), cx[u_]["vbig"]], axis=0)) for u_ in units}
        upd = {u_: _dot_tn(jnp.concatenate([-usol[u_], cx[u_]["v"]], axis=0).astype(BF16), cx[u_]["end"])
               for u_ in units}
        for u_ in units:
            x = cx[u_]
            y_scr[x["rows"], x["ls"]] = x0[u_][c:2 * c] + ys[u_]
            x["scr"][...] = x["scr"][...] * x["egl"] + upd[u_] * mask_ref[...]

    def load_state(scrs, seq):
        for p, scr in enumerate(scrs):
            scr[...] = jnp.concatenate([st_in_ref[seq, p * gw:(p + 1) * gw, :]] * hp, axis=1) * mask_ref[...]

    def store_state(scrs, seq):
        for p, scr in enumerate(scrs):
            st = scr[...]
            st_out_ref[seq, p * gw:(p + 1) * gw, :] = st[:, 0:64] + st[:, 64:128]

    lane_scrs = [st_scrs[i * ng:(i + 1) * ng] for i in range(len(st_scrs) // ng)]
    _run_lanes(lane_scrs, nseq, rps, nsub, c, nst, load_state, store_state, sub_blocks)

    def finish_chunk(ci, carry):
        rows = pl.ds(ci * pc if isinstance(ci, int) else pl.multiple_of(ci * pc, 8), pc)
        y = y_scr[rows, :]
        mean = _seg_dot(y, ovv_ref[...], 3) * (1.0 / HEAD_V)
        d = y - mean
        var = _seg_dot(d * d, ovv_ref[...], 2) * (1.0 / HEAD_V)
        ln = d * lax.rsqrt(var + RWKV_LN_EPS) * par_ref[5:6, :] + par_ref[6:7, :]
        o_ref[rows, :] = (ln + bonus_scr[rows, :]) * g_scr[rows, :]
        return carry

    if nchunk == 1:
        finish_chunk(0, 0)
    else:
        lax.fori_loop(0, nchunk, finish_chunk, 0)


def _rwkv_call(z, mu, par, wup, aup, gup, ovv, hmask, ehead, mask, shift, st_in, o_prev, *, nrows, g):
    rblk = g.rows
    nsteps = g.nst
    base = g.base_blk
    pc = g.pc
    kern = functools.partial(_rwkv_kernel, c=g.c, nsub=g.nsub, tv=g.tv, nseq=g.nseq, rps=g.rps, nst=g.nst, pc=pc)
    const = lambda b, s: (0, 0)
    return pl.pallas_call(
        kern,
        grid=(g.ngrid, nsteps),
        in_specs=[
            pl.BlockSpec((rblk, SEC), lambda b, s: (base + b * nsteps + s, 1),
                         pipeline_mode=pl.Buffered(1) if g.lanes > 2 and g.full else None),
            pl.BlockSpec((1, SEC), const),
            pl.BlockSpec((8, 256), const),
            pl.BlockSpec((256, 256), const),
            pl.BlockSpec((256, 256), const),
            pl.BlockSpec((256, 256), const),
            pl.BlockSpec((256, 256), const),
            pl.BlockSpec((2 * g.c, 128), const),
            pl.BlockSpec((4 * g.c, 128), const),
            pl.BlockSpec((128, 128), const),
            pl.BlockSpec((g.nseq, 1, SEC), lambda b, s: (b, 0, 0)),
            pl.BlockSpec((g.nseq, 256, 64), lambda b, s: (b, 0, 0)),
            pl.BlockSpec(memory_space=pl.ANY),
        ],
        input_output_aliases={12: 0},
        out_specs=[
            pl.BlockSpec((rblk, 256), lambda b, s: (base + b * nsteps + s, 0)),
            pl.BlockSpec((g.nseq, 1, SEC), lambda b, s: (b, 0, 0)),
            pl.BlockSpec((g.nseq, 256, 64), lambda b, s: (b, 0, 0)),
        ],
        out_shape=[
            jax.ShapeDtypeStruct((nrows, 256), F32),
            jax.ShapeDtypeStruct((g.nb, 1, SEC), F32),
            jax.ShapeDtypeStruct((g.nb, 256, 64), F32),
        ],
        scratch_shapes=[pltpu.VMEM((max(8, g.nseq), SEC), F32), pltpu.VMEM((pc if pc > g.rps else 8, SEC), F32)]
        + [pltpu.VMEM((rblk, 256), F32)] * 9 + [pltpu.VMEM((128, 128), F32)] * (2 * g.lanes),
        compiler_params=_cparams("parallel", "arbitrary"),
    )(z, mu, par, wup, aup, gup, ovv, hmask, ehead, mask, shift, st_in, o_prev)


def _mlstm_kernel(z_ref, par_ref, bcol_ref, cw_ref, sel_ref, ovv_ref, cm_in, nm_in, mm_in, cv_in, o_prev_ref,
                  o_ref, cm_out, nm_out, mm_out, cv_out,
                  *scratch, chunks, nseq, rps, nst, pc):
    del o_prev_ref
    lanes = scratch[0].shape[0]

    def chunk_gates(scr, row0, c, tv):
        xbuf, cm_scr, nm_scr, mm_scr = scr
        rows = pl.ds(row0, c)
        mqk = z_ref[rows, 0:256]
        mv = z_ref[rows, 256:512]
        gi = z_ref[rows, 512:640]
        mo = z_ref[rows, 640:896]
        xbuf[8:8 + c, :] = mqk
        conv = par_ref[0:1, :] + xbuf[5:5 + c, :] * cw_ref[0:1, :]
        for j in range(1, CONV_W):
            conv = conv + xbuf[5 + j:5 + j + c, :] * cw_ref[j:j + 1, :]
        xbuf[5:8, :] = xbuf[8 + tv - 3:8 + tv, :]
        act = _silu(conv)
        q = act[:, 0:128]
        k = act[:, 128:256] * (MLSTM_DK ** -0.5)

        rowi = lax.broadcasted_iota(jnp.int32, (c, 1), 0)
        coli = lax.broadcasted_iota(jnp.int32, (1, c), 1)
        gcol = gi + par_ref[1:2, 0:128]
        lfc = _log_sigmoid(gcol)
        graw = None
        for part in _split3(gi):
            t_ = _dot_nt(sel_ref[...], part)
            graw = t_ if graw is None else graw + t_
        grow = graw + bcol_ref[...]
        lfr = _log_sigmoid(grow)
        if tv < c:
            gcol = jnp.where(rowi < tv, gcol, -jnp.inf)
            lfc = jnp.where(rowi < tv, lfc, 0.0)
            grow = jnp.where(coli < tv, grow, -jnp.inf)
            lfr = jnp.where(coli < tv, lfr, 0.0)
        b_col = _seg_dot_l(_tri(c).astype(BF16), lfc, 3)
        b_row = _seg_dot(lfr, _tri(c, upper=True).astype(BF16), 3)
        return dict(rows=rows, q=q, k=k, mv=mv, mo=mo, gcol=gcol, grow=grow, b_col=b_col, b_row=b_row,
                    mm=mm_scr[...], nm=nm_scr[...], cm=cm_scr[...])

    def chunk(scrs, row0s, c, tv):
        ctx = [chunk_gates(scr, row0, c, tv) for scr, row0 in zip(scrs, row0s)]
        causal = _tri(c)
        lane = lax.broadcasted_iota(jnp.int32, (1, 128), 1)
        items = [(l, h) for l in range(len(ctx)) for h in range(N_HEADS)]
        ks = lambda h: slice(h * MLSTM_DK, (h + 1) * MLSTM_DK)
        vs = lambda h: slice(h * HEAD_V, (h + 1) * HEAD_V)
        qh = {(l, h): ctx[l]["q"][:, ks(h)] for l, h in items}
        kh = {(l, h): ctx[l]["k"][:, ks(h)] for l, h in items}
        vh = {(l, h): ctx[l]["mv"][:, vs(h)].astype(BF16) for l, h in items}
        cmh = {(l, h): ctx[l]["cm"][ks(h), :] for l, h in items}
        nh = {(l, h): ctx[l]["nm"][0:1, ks(h)] for l, h in items}
        qk_raw = {it: _dot_nt(qh[it].astype(BF16), kh[it].astype(BF16)) for it in items}
        q_cm = {it: _dot(qh[it].astype(BF16), cmh[it].astype(BF16)) for it in items}
        bc, ic, mprev, m_t, sc, qk = {}, {}, {}, {}, {}, {}
        for it in items:
            l, h = it
            bc[it] = ctx[l]["b_col"][:, 4 + h:5 + h]
            br = ctx[l]["b_row"][4 + h:5 + h, :]
            ir = ctx[l]["grow"][h:h + 1, :]
            ic[it] = ctx[l]["gcol"][:, h:h + 1]
            dmat = jnp.where(causal, bc[it] - br + ir, -jnp.inf)
            mprev[it] = ctx[l]["mm"][0:1, h:h + 1]
            inter = bc[it] + mprev[it]
            m_t[it] = jnp.maximum(inter, jnp.max(dmat, axis=1, keepdims=True))
            sc[it] = jnp.exp(inter - m_t[it])
            qk[it] = qk_raw[it] * jnp.exp(dmat - m_t[it])
        num = {it: _dot(qk[it].astype(BF16), vh[it]) + sc[it] * q_cm[it] for it in items}
        qk_sum = _seg_dot(jnp.concatenate([qk[it] for it in items], axis=0), jnp.ones((c, HEAD_V), BF16), 2)
        qn_sum = _seg_dot(jnp.concatenate([qh[it] * nh[it] for it in items], axis=0),
                          jnp.ones((MLSTM_DK, HEAD_V), BF16), 2)
        kw, sl, m_new, hn = {}, {}, {}, {}
        for n_, it in enumerate(items):
            den = qk_sum[n_ * c:(n_ + 1) * c] + sc[it] * qn_sum[n_ * c:(n_ + 1) * c]
            hc = num[it] / jnp.maximum(jnp.abs(den), jnp.exp(-m_t[it]))
            m_new[it] = m_t[it][tv - 1:tv, :]
            bl = bc[it][tv - 1:tv, :]
            wl = jnp.exp(bl - bc[it] + ic[it] - m_new[it])
            sl[it] = jnp.exp(bl + mprev[it] - m_new[it])
            kw[it] = kh[it] * wl
            hn[it] = hc
        kv = {it: _dot_tn(kw[it].astype(BF16), vh[it]) for it in items}
        for l, (scr, cx) in enumerate(zip(scrs, ctx)):
            xbuf, cm_scr, nm_scr, mm_scr = scr
            heads = [(l, h) for h in range(N_HEADS)]
            cm_scr[...] = jnp.concatenate([sl[it] * cmh[it] + kv[it] for it in heads], axis=0)
            nm_scr[...] = jnp.concatenate([sl[it] * nh[it] + jnp.sum(kw[it], axis=0, keepdims=True)
                                           for it in heads], axis=1)
            mm_new = cx["mm"]
            for it in heads:
                mm_new = jnp.where(lane == it[1], m_new[it], mm_new)
            mm_scr[...] = mm_new
            o_ref[cx["rows"], :] = jnp.concatenate([hn[it] for it in heads], axis=1)

    def load_state(scr, seq):
        xbuf, cm_scr, nm_scr, mm_scr = scr
        cm_scr[...] = cm_in[seq]
        nm_scr[...] = nm_in[seq]
        mm_scr[...] = mm_in[seq]
        xbuf[0:8, :] = cv_in[seq]

    def store_state(scr, seq):
        xbuf, cm_scr, nm_scr, mm_scr = scr
        cm_out[seq] = cm_scr[...]
        nm_out[seq] = nm_scr[...]
        mm_out[seq] = mm_scr[...]
        cv_out[seq] = xbuf[0:8, :]

    assert nseq == lanes or nst == 1
    step = pl.program_id(1)

    def lane_scr(u):
        return tuple(s_.at[u] for s_ in scratch)

    def run_chunks(grp):
        for start, c, tv, count in chunks:
            def body(ci, carry2, start=start, c=c, tv=tv):
                firsts = [seq * rps + start + ci * c for _, seq in grp]
                chunk([scr for scr, _ in grp],
                      [f if isinstance(f, int) else pl.multiple_of(f, 8) for f in firsts], c, tv)
                return carry2

            if count == 1:
                body(0, 0)
            else:
                lax.fori_loop(0, count, body, 0)

    def seq_group(sg, carry):
        def lane(u, seq):
            scr = lane_scr(u)
            if nst == 1:
                load_state(scr, seq)
            else:
                pl.when(step == 0)(functools.partial(load_state, scr, seq))
            return scr

        def finish(scr, seq):
            if nst == 1:
                store_state(scr, seq)
            else:
                pl.when(step == nst - 1)(functools.partial(store_state, scr, seq))

        if max(c for _, c, _, _ in chunks) <= 16:
            grp = [(lane(u, sg * lanes + u), sg * lanes + u) for u in range(lanes)]
            run_chunks(grp)
            for scr, seq in grp:
                finish(scr, seq)
        else:
            def one_lane(u, carry2):
                seq = sg * lanes + u
                scr = lane(u, seq)
                run_chunks([(scr, seq)])
                finish(scr, seq)
                return carry2

            lax.fori_loop(0, lanes, one_lane, 0)
        return carry

    if nseq == lanes:
        seq_group(0, 0)
    else:
        lax.fori_loop(0, nseq // lanes, seq_group, 0)

    nchunk = (nseq * rps) // pc

    def finish_chunk(ci, carry):
        rows = pl.ds(ci * pc if isinstance(ci, int) else pl.multiple_of(ci * pc, 8), pc)
        o = o_ref[rows, :]
        ms = _seg_dot(o * o, ovv_ref[...], 2) * (1.0 / HEAD_V)
        o_ref[rows, :] = o * lax.rsqrt(ms + EPS) * par_ref[2:3, :] * jax.nn.sigmoid(z_ref[rows, 640:896])
        return carry

    if nchunk == 1:
        finish_chunk(0, 0)
    else:
        lax.fori_loop(0, nchunk, finish_chunk, 0)


def _mlstm_call(z, par, bcol, cw, sel, ovv, cm, nm, mm, cv, o_prev, *, nrows, g):
    if g.full:
        cmain = max(d for d in (MLSTM_CHUNK, 64, 32, 16, 8) if d <= g.rps)
        lead = g.rps % cmain
        chunks = ([(0, lead, lead, 1)] if lead else []) + [(lead, cmain, cmain, g.rps // cmain)]
        assert lead % 8 == 0
    else:
        chunks = [(0, g.rps, g.tv, 1)]
    cmax = max(c for _, c, _, _ in chunks)
    kern = functools.partial(_mlstm_kernel, chunks=tuple(chunks), nseq=g.nseq, rps=g.rps, nst=g.nst, pc=g.pc)
    const = lambda b, s: (0, 0)
    st3 = lambda shp: pl.BlockSpec((g.nseq,) + shp, lambda b, s: (b, 0, 0))
    return pl.pallas_call(
        kern,
        grid=(g.ngrid, g.nst),
        in_specs=[
            pl.BlockSpec((g.rows, SEC), lambda b, s: (g.base_blk + b * g.nst + s, 3)),
            pl.BlockSpec((8, 256), const),
            pl.BlockSpec((8, 1), const),
            pl.BlockSpec((8, 256), const),
            pl.BlockSpec((8, 128), const),
            pl.BlockSpec((256, 256), const),
            st3((128, 64)), st3((1, 128)), st3((1, 128)), st3((8, 256)),
            pl.BlockSpec(memory_space=pl.ANY),
        ],
        input_output_aliases={10: 0},
        out_specs=[
            pl.BlockSpec((g.rows, 256), lambda b, s: (g.base_blk + b * g.nst + s, 0)),
            st3((128, 64)), st3((1, 128)), st3((1, 128)), st3((8, 256)),
        ],
        out_shape=[
            jax.ShapeDtypeStruct((nrows, 256), F32),
            jax.ShapeDtypeStruct((g.nb, 128, 64), F32),
            jax.ShapeDtypeStruct((g.nb, 1, 128), F32),
            jax.ShapeDtypeStruct((g.nb, 1, 128), F32),
            jax.ShapeDtypeStruct((g.nb, 8, 256), F32),
        ],
        scratch_shapes=[
            pltpu.VMEM((g.lanes, 8 + cmax, 256), F32),
            pltpu.VMEM((g.lanes, 128, 64), F32),
            pltpu.VMEM((g.lanes, 1, 128), F32),
            pltpu.VMEM((g.lanes, 1, 128), F32),
        ],
        compiler_params=_cparams("parallel", "arbitrary"),
    )(z, par, bcol, cw, sel, ovv, cm, nm, mm, cv, o_prev)


def _merge_kernel(o0_ref, o1_ref, o2_ref, o3_ref, x_ref, g_ref, wg_ref, wb_ref, wout_ref, out_ref):
    h = _rms_bf16(x_ref[...], g_ref[...])
    mix = None
    for n, o_ref in enumerate((o0_ref, o1_ref, o2_ref, o3_ref)):
        gate = jax.nn.sigmoid(_dot(h, wg_ref[:, n * D_MODEL:(n + 1) * D_MODEL]))
        t = _dot(o_ref[...].astype(BF16), wb_ref[n]) * gate
        mix = t if mix is None else mix + t
    out_ref[...] = x_ref[...] + _dot(mix.astype(BF16), wout_ref[...])


def _merge_call(outs, x, g, wg, wb, wout, tm, o_tile):
    n = x.shape[0]
    o_spec = pl.BlockSpec((tm, 256), lambda i: (o_tile(i), 0))
    return pl.pallas_call(
        _merge_kernel,
        grid=(n // tm,),
        in_specs=[
            o_spec, o_spec, o_spec, o_spec,
            pl.BlockSpec((tm, D_MODEL), lambda i: (i, 0)),
            pl.BlockSpec((1, D_MODEL), lambda i: (0, 0)),
            pl.BlockSpec((D_MODEL, 4 * D_MODEL), lambda i: (0, 0)),
            pl.BlockSpec((4, 256, D_MODEL), lambda i: (0, 0, 0)),
            pl.BlockSpec((D_MODEL, D_MODEL), lambda i: (0, 0)),
        ],
        out_specs=pl.BlockSpec((tm, D_MODEL), lambda i: (i, 0)),
        out_shape=jax.ShapeDtypeStruct((n, D_MODEL), F32),
        compiler_params=_cparams("parallel"),
    )(*outs, x, g, wg, wb, wout)


def _mlp_kernel(x_ref, g_ref, gf_ref, wup_ref, wdn_ref, out_ref, h_scr, acc_scr, *, final):
    j = pl.program_id(1)

    @pl.when(j == 0)
    def _():
        h_scr[...] = _rms_bf16(x_ref[...], g_ref[...])
        acc_scr[...] = jnp.zeros_like(acc_scr)

    u = jnp.maximum(_dot(h_scr[...], wup_ref[...]), 0.0)
    acc_scr[...] += _dot((u * u).astype(BF16), wdn_ref[...])

    @pl.when(j == pl.num_programs(1) - 1)
    def _():
        y = x_ref[...] + acc_scr[...]
        if final:
            y = y * lax.rsqrt(jnp.mean(y * y, axis=-1, keepdims=True) + EPS) * gf_ref[...]
        out_ref[...] = y


def _mlp_call(x, g, gf, wup, wdn, tm, final):
    n = x.shape[0]
    tf = 1024
    return pl.pallas_call(
        functools.partial(_mlp_kernel, final=final),
        grid=(n // tm, D_FF // tf),
        in_specs=[
            pl.BlockSpec((tm, D_MODEL), lambda i, j: (i, 0)),
            pl.BlockSpec((1, D_MODEL), lambda i, j: (0, 0)),
            pl.BlockSpec((1, D_MODEL), lambda i, j: (0, 0)),
            pl.BlockSpec((D_MODEL, tf), lambda i, j: (0, j)),
            pl.BlockSpec((tf, D_MODEL), lambda i, j: (j, 0)),
        ],
        out_specs=pl.BlockSpec((tm, D_MODEL), lambda i, j: (i, 0)),
        out_shape=jax.ShapeDtypeStruct((n, D_MODEL), F32),
        scratch_shapes=[pltpu.VMEM((tm, D_MODEL), BF16), pltpu.VMEM((tm, D_MODEL), F32)],
        compiler_params=_cparams("parallel", "arbitrary"),
    )(x, g, gf, wup, wdn)


def _seg_ones(rows_per_head, cols_per_head):
    r = np.arange(N_HEADS * rows_per_head)[:, None] // rows_per_head
    c = np.arange(N_HEADS * cols_per_head)[None, :] // cols_per_head
    return (r == c).astype(np.float32)


def _rows(*vecs, width=256, nrows=8):
    rows = []
    for v in vecs:
        v = jnp.asarray(v, F32).reshape(-1)
        rows.append(jnp.pad(v, (0, width - v.shape[0])))
    rows += [jnp.zeros((width,), F32)] * (nrows - len(rows))
    return jnp.stack(rows)


def _layout_w_in(w):
    d = w.shape[0]
    zeros = lambda n: jnp.zeros((d, n), w.dtype)
    gl0, ml0, gt0 = 2048, 2832, 3608
    parts = [
        w[:, 0:2048],
        w[:, gl0:gl0 + 512], w[:, gl0 + 512:gl0 + 528], zeros(112), w[:, gl0 + 528:gl0 + 784], zeros(128),
        w[:, ml0:ml0 + 512], w[:, ml0 + 512:ml0 + 520], zeros(120), w[:, ml0 + 520:ml0 + 776], zeros(128),
    ]
    wz = jnp.concatenate(parts, axis=1)
    assert wz.shape[1] == Z_COLS
    return wz.astype(BF16), w[:, gt0:gt0 + 4 * D_MODEL].astype(BF16)


class _Group:
    def __init__(self, nb, trow, tv, row0, max_seq):
        self.nb, self.trow, self.tv, self.row0 = nb, trow, tv, row0
        self.full = tv == trow
        self.c = SUB if self.full else trow
        self.tvs = min(tv, self.c)
        if self.full:
            nblk = trow // self.c
            self.rps = self.c * max(d for d in range(1, min(nblk, 64) + 1) if nblk % d == 0)
        else:
            self.rps = trow
        self.nst = trow // self.rps
        self.nsub = self.rps // self.c
        self.nseq = max(d for d in range(1, max_seq + 1) if nb % d == 0 and row0 % (d * self.rps) == 0)
        self.lanes = (8 if not self.full and self.nseq % 8 == 0 else 4 if self.nseq % 4 == 0
                      else 2 if self.nseq % 2 == 0 else 1)
        assert self.nseq == self.lanes or self.nst == 1
        self.rows = self.nseq * self.rps
        self.ngrid = nb // self.nseq
        self.base_blk = row0 // self.rows
        self.pc = _pick_tile(self.rps, 384) if self.rps >= 64 else self.rows


def kernel(x_prompt, x_sample, state_hgrn, state_rwkv, state_rwkv_shift, state_gla, state_mlstm_c, state_mlstm_n, state_mlstm_m, state_mlstm_conv, meta_tokens, norm_mix, norm_mlp, norm_final, w_in, hgrn_lb, hgrn_norm, rwkv_mu, rwkv_w0, rwkv_w_up, rwkv_a0, rwkv_a_up, rwkv_g_up, rwkv_k_k, rwkv_k_a, rwkv_r_k, rwkv_ln_w, rwkv_ln_b, gla_gate_up, gla_gate_b, gla_norm, mlstm_conv_w, mlstm_conv_b, mlstm_i_b, mlstm_f_b, mlstm_norm, w_branch, w_out, w_up, w_down):
    depth = w_in.shape[0]
    bp, seq, _ = x_prompt.shape
    bs, dseq, _ = x_sample.shape
    tp = N_META + seq
    assert tp % SUB == 0 and CONV_W - 1 <= dseq <= SAMPLE_PAD
    dt = x_prompt.dtype

    gp = _Group(bp, tp, tp, 0, 4)
    gs = _Group(bs, SAMPLE_PAD, dseq, bp * tp, 16)
    tm = gp.rps
    xs = jnp.pad(x_sample, ((0, 0), (0, SAMPLE_PAD - dseq), (0, 0))).reshape(bs * SAMPLE_PAD, D_MODEL)
    n_real = bp * tp + bs * SAMPLE_PAD
    n_rows = -(-n_real // tm) * tm
    pieces = []
    for b in range(bp):
        pieces += [meta_tokens.astype(dt), x_prompt[b]]
    x = jnp.concatenate(pieces + [xs, jnp.zeros((n_rows - n_real, D_MODEL), dt)], axis=0).astype(F32)

    def mixer_tile(i):
        b, s = i // gp.nst, i % gp.nst
        p = (b // gp.nseq) * (gp.nst * gp.nseq) + s * gp.nseq + b % gp.nseq
        return jnp.where(i < bp * gp.nst, p, i)

    tm_mlp = 2 * tm if (n_rows // tm) % 2 == 0 and 2 * tm <= 1536 else tm
    merge_split = 2 if tm % 16 == 0 and tm >= 256 else 1

    ones_vv = jnp.asarray(_seg_ones(64, 64), BF16)
    ones_gl = jnp.asarray(_seg_ones(GLA_DK, 64), BF16)
    mask_gl = jnp.asarray(_seg_ones(64, GLA_DK), F32)
    sel8 = jnp.asarray(np.eye(8, 128, dtype=np.float32), BF16)
    hmask = {g.c: jnp.asarray(np.kron(np.eye(2), np.ones((g.c, 64))), F32) for g in (gp, gs)}
    ehead = {c_: jnp.concatenate([m_, m_], axis=0).astype(BF16) for c_, m_ in hmask.items()}
    mask_pair = jnp.asarray(np.kron(np.eye(2), np.ones((64, 64))), F32)

    lb_cs = jnp.cumsum(jax.nn.softmax(hgrn_lb.astype(F32), axis=0), axis=0)
    lb_all = lb_cs - lb_cs[:1]

    def init_states(states, nb, zero):
        s_hg, s_rw, shift, s_gl, c_ml, n_ml, m_ml, conv = states
        if zero:
            z = lambda *shp: jnp.zeros((depth, nb) + shp, F32)
            return dict(hg=z(256, 64), rw=z(256, 64), shift=z(1, SEC), gl=z(256, GLA_DK), cm=z(128, 64),
                        nm=z(1, 128), mm=z(1, 128), cv=z(8, 256))
        f = lambda a: a.astype(F32)
        return dict(
            hg=f(s_hg).transpose(0, 1, 2, 4, 3).reshape(depth, nb, 256, 64),
            rw=f(s_rw).reshape(depth, nb, 256, 64),
            shift=f(shift).reshape(depth, nb, 1, SEC),
            gl=f(s_gl).transpose(0, 1, 2, 4, 3).reshape(depth, nb, 256, GLA_DK),
            cm=f(c_ml).reshape(depth, nb, 128, 64),
            nm=f(n_ml).reshape(depth, nb, 1, 128),
            mm=jnp.pad(f(m_ml), ((0, 0), (0, 0), (0, 124))).reshape(depth, nb, 1, 128),
            cv=jnp.pad(f(conv), ((0, 0), (0, 0), (5, 0), (0, 0))),
        )

    sample_states = (state_hgrn, state_rwkv, state_rwkv_shift, state_gla,
                     state_mlstm_c, state_mlstm_n, state_mlstm_m, state_mlstm_conv)
    st_p = init_states(sample_states, bp, True)
    st_s = init_states(sample_states, bs, False)
    new_p = {k: [] for k in st_p}
    new_s = {k: [] for k in st_s}

    for l in range(depth):
        wz, wg = _layout_w_in(w_in[l])
        g_mix = norm_mix[l].reshape(1, D_MODEL).astype(F32)
        z = _in_proj(x, g_mix, wz, tm, mixer_tile)

        lb = lb_all[l]
        par_hg = _rows(jnp.log(lb), jnp.log1p(-lb), 1.0 - lb, hgrn_norm[l])
        par_gl = _rows(gla_gate_b[l], jnp.zeros((1,)), jnp.zeros((1,)), gla_norm[l])
        gup_gl = jnp.pad(gla_gate_up[l], ((0, 128 - gla_gate_up.shape[1]), (0, 0))).astype(BF16)
        gup_dummy = jnp.zeros((128, 128), BF16)
        par_rw = _rows(rwkv_w0[l], rwkv_a0[l], rwkv_k_k[l], rwkv_k_a[l], rwkv_r_k[l], rwkv_ln_w[l], rwkv_ln_b[l])
        nw, na = rwkv_w_up.shape[1], rwkv_a_up.shape[1]
        wup_p = jnp.pad(rwkv_w_up[l], ((0, 256 - nw), (0, 0))).astype(BF16)
        aup_p = jnp.pad(rwkv_a_up[l], ((nw, 256 - nw - na), (0, 0))).astype(BF16)
        gup_p = jnp.pad(rwkv_g_up[l], ((nw + na, 0), (0, 0))).astype(BF16)
        mu = rwkv_mu[l].reshape(1, SEC).astype(F32)
        gate_b = jnp.concatenate([mlstm_i_b[l], mlstm_f_b[l]]).astype(F32)
        par_ml = _rows(mlstm_conv_b[l], gate_b, mlstm_norm[l])
        bcol_ml = gate_b.reshape(8, 1)
        cw_ml = _rows(*[mlstm_conv_w[l, j] for j in range(CONV_W)])

        o_hg = o_rw = o_gl = o_ml = None
        for g, st, new in ((gp, st_p, new_p), (gs, st_s, new_s)):
            prev = lambda o: jnp.zeros((n_rows, 256), F32) if o is None else o
            o_hg, s_hg = _gla_call(z, 0, par_hg, gup_dummy, ones_vv, ones_vv, mask_pair, st["hg"][l], prev(o_hg),
                                   mode="hgrn", dk=64, nrows=n_rows, g=g)
            o_gl, s_gl = _gla_call(z, 2, par_gl, gup_gl, ones_gl, ones_vv, mask_gl, st["gl"][l], prev(o_gl),
                                   mode="gla", dk=GLA_DK, nrows=n_rows, g=g)
            o_rw, nshift, s_rw = _rwkv_call(z, mu, par_rw, wup_p, aup_p, gup_p, ones_vv, hmask[g.c], ehead[g.c],
                                            mask_pair, st["shift"][l], st["rw"][l], prev(o_rw), nrows=n_rows, g=g)
            o_ml, s_cm, s_nm, s_mm, s_cv = _mlstm_call(z, par_ml, bcol_ml, cw_ml, sel8, ones_vv, st["cm"][l], st["nm"][l],
                                                        st["mm"][l], st["cv"][l], prev(o_ml), nrows=n_rows, g=g)
            for key, val in (("hg", s_hg), ("rw", s_rw), ("shift", nshift), ("gl", s_gl), ("cm", s_cm),
                             ("nm", s_nm), ("mm", s_mm), ("cv", s_cv)):
                new[key].append(val)

        x = _merge_call((o_hg, o_rw, o_gl, o_ml), x, g_mix, wg, w_branch[l].astype(BF16), w_out[l].astype(BF16),
                        tm // merge_split, lambda i: mixer_tile(i // merge_split) * merge_split + i % merge_split)
        x = _mlp_call(x, norm_mlp[l].reshape(1, D_MODEL).astype(F32), norm_final.reshape(1, D_MODEL).astype(F32),
                      w_up[l].astype(BF16), w_down[l].astype(BF16), tm_mlp, final=(l == depth - 1))

    y_prompt = jnp.stack([x[b * tp + N_META:(b + 1) * tp] for b in range(bp)]).astype(dt)
    y_sample = x[gs.row0:gs.row0 + bs * SAMPLE_PAD].reshape(bs, SAMPLE_PAD, D_MODEL)[:, :dseq].astype(dt)

    def finish(new, nb):
        st = {k: jnp.stack(v) for k, v in new.items()}
        return (
            st["hg"].reshape(depth, nb, N_HEADS, 64, 64).transpose(0, 1, 2, 4, 3),
            st["rw"].reshape(depth, nb, N_HEADS, 64, 64),
            st["shift"].reshape(depth, nb, SEC),
            st["gl"].reshape(depth, nb, N_HEADS, 64, GLA_DK).transpose(0, 1, 2, 4, 3),
            st["cm"].reshape(depth, nb, N_HEADS, MLSTM_DK, 64),
            st["nm"].reshape(depth, nb, N_HEADS, MLSTM_DK),
            st["mm"].reshape(depth, nb, 128)[:, :, :N_HEADS],
            st["cv"][:, :, 5:8, :],
        )

    outs_p = tuple(a.astype(dt) for a in finish(new_p, bp))
    outs_s = tuple(a.astype(dt) for a in finish(new_s, bs))
    return (y_prompt, y_sample) + outs_p + outs_s
```

```python
import functools

import numpy as np
import jax
import jax.numpy as jnp
from jax import lax
from jax.experimental import pallas as pl
from jax.experimental.pallas import tpu as pltpu

F32 = jnp.float32
BF16 = jnp.bfloat16

D_MODEL = 1024
N_META = 16
N_HEADS = 4
HEAD_V = 64
GLA_DK = 32
MLSTM_DK = 32
GLA_GATE_TAU = 16.0
RWKV_LN_EPS = 64e-5
CONV_W = 4
D_FF = 4 * D_MODEL
EPS = 1e-6

SEC = 1024
Z_COLS = 4 * SEC
SUB = 16
SAMPLE_PAD = 8
MLSTM_CHUNK = 128
VMEM_LIMIT = 56 * 1024 * 1024


def _cparams(*sem):
    return pltpu.CompilerParams(dimension_semantics=sem, vmem_limit_bytes=VMEM_LIMIT)


def _pick_tile(n, max_tile, mult=8):
    best = None
    t = mult
    while t <= min(n, max_tile):
        if n % t == 0:
            best = t
        t += mult
    assert best is not None, (n, max_tile, mult)
    return best


def _split2(x):
    hi = x.astype(BF16)
    lo = (x - hi.astype(F32)).astype(BF16)
    return hi, lo


def _split3(x):
    hi = x.astype(BF16)
    r1 = x - hi.astype(F32)
    mid = r1.astype(BF16)
    lo = (r1 - mid.astype(F32)).astype(BF16)
    return hi, mid, lo


def _dot(a, b):
    return jnp.dot(a, b, preferred_element_type=F32)


def _dot_nt(a, b):
    return lax.dot_general(a, b, (((1,), (1,)), ((), ())), preferred_element_type=F32)


def _dot_tn(a, b):
    return lax.dot_general(a, b, (((0,), (0,)), ((), ())), preferred_element_type=F32)


def _seg_dot(x, m_bf16, parts):
    ps = _split2(x) if parts == 2 else _split3(x)
    acc = _dot(ps[0], m_bf16)
    for p in ps[1:]:
        acc = acc + _dot(p, m_bf16)
    return acc


def _seg_dot_l(m_bf16, x, parts):
    ps = _split2(x) if parts == 2 else _split3(x)
    acc = _dot(m_bf16, ps[0])
    for p in ps[1:]:
        acc = acc + _dot(m_bf16, p)
    return acc


def _log_sigmoid(x):
    return jnp.minimum(x, 0.0) - jnp.log1p(jnp.exp(-jnp.abs(x)))


def _silu(x):
    return x * jax.nn.sigmoid(x)


def _tri(n, upper=False):
    r = lax.broadcasted_iota(jnp.int32, (n, n), 0)
    c = lax.broadcasted_iota(jnp.int32, (n, n), 1)
    return (r <= c) if upper else (r >= c)


def _cumsum_rows(x, rowi):
    d = 1
    while d < x.shape[0]:
        x = x + jnp.where(rowi >= d, pltpu.roll(x, d, axis=0), 0.0)
        d *= 2
    return x


def _rms_bf16(x, g):
    return (x * lax.rsqrt(jnp.mean(x * x, axis=-1, keepdims=True) + EPS) * g).astype(BF16)


def _in_proj_kernel(x_ref, g_ref, w_ref, z_ref):
    h = _rms_bf16(x_ref[...], g_ref[...])
    for j in range(Z_COLS // SEC):
        z_ref[:, j * SEC:(j + 1) * SEC] = _dot(h, w_ref[:, j * SEC:(j + 1) * SEC])


def _in_proj(x, g, w, tm, out_tile):
    n = x.shape[0]
    return pl.pallas_call(
        _in_proj_kernel,
        grid=(n // tm,),
        in_specs=[
            pl.BlockSpec((tm, D_MODEL), lambda i: (i, 0)),
            pl.BlockSpec((1, D_MODEL), lambda i: (0, 0)),
            pl.BlockSpec((D_MODEL, Z_COLS), lambda i: (0, 0)),
        ],
        out_specs=pl.BlockSpec((tm, Z_COLS), lambda i: (out_tile(i), 0)),
        out_shape=jax.ShapeDtypeStruct((n, Z_COLS), F32),
        compiler_params=_cparams("parallel"),
    )(x, g, w)


def _gla_kernel(z_ref, par_ref, gup_ref, okv_ref, ovv_ref, mask_ref, st_in_ref, o_prev_ref,
                o_ref, st_out_ref, *st_scrs, mode, c, nsub, tv, dk, nseq, rps, nst, pc):
    del o_prev_ref
    voff = 512 if mode == "hgrn" else 256
    rowi = lax.broadcasted_iota(jnp.int32, (c, 1), 0)
    valid = rowi < tv
    hpg = 128 // dk
    groups = [(slice(i * hpg * HEAD_V, (i + 1) * hpg * HEAD_V), slice(i * 128, (i + 1) * 128))
              for i in range(N_HEADS // hpg)]

    def sub_block(st_scr, r0):
        rows = pl.ds(r0, c)
        if mode == "hgrn":
            q = z_ref[rows, 0:256]
            hf = z_ref[rows, 256:512]
            a = par_ref[0:1, :]
            cc = par_ref[1:2, :] + _log_sigmoid(hf)
            loga = jnp.maximum(a, cc) + jnp.log1p(jnp.exp(-jnp.abs(a - cc)))
            k = par_ref[2:3, :] * jax.nn.sigmoid(-hf)
        else:
            q = z_ref[rows, 0:128] * (GLA_DK ** -0.5)
            k = z_ref[rows, 128:256]
            ga = z_ref[rows, 512:640]
            gl = _dot(ga.astype(BF16), gup_ref[...]) + par_ref[0:1, 0:128]
            loga = _log_sigmoid(gl) * (1.0 / GLA_GATE_TAU)
        v = z_ref[rows, voff:voff + 256]
        if tv < c:
            loga = jnp.where(valid, loga, 0.0)
            k = jnp.where(valid, k, 0.0)
            v = jnp.where(valid, v, 0.0)
        b = _cumsum_rows(loga, rowi)

        qe = (q * jnp.exp(b)).astype(BF16)
        o = jnp.concatenate([_dot_nt(qe[:, ks], scr[...].astype(BF16))
                             for scr, (_, ks) in zip(st_scr, groups)], axis=1)

        lo_rows = [8 * (s // 8) for s in range(tv)]
        pieces = []
        for s in range(tv):
            rs = slice(lo_rows[s], c)
            e = jnp.exp(jnp.where(rowi[rs] >= s, b[rs] - b[s:s + 1, :], -jnp.inf))
            pieces.append(q[rs] * e * k[s:s + 1, :])
        return rows, lo_rows, k, v, b, o, jnp.concatenate(pieces, axis=0)

    def sub_block_finish(st_scr, part, sm):
        rows, lo_rows, k, v, b, o, _ = part
        acc = {}
        off = 0
        for s in range(tv):
            n = c - lo_rows[s]
            contrib = sm[off:off + n] * v[s:s + 1, :]
            off += n
            acc[lo_rows[s]] = contrib if lo_rows[s] not in acc else acc[lo_rows[s]] + contrib
        for lo, a_ in acc.items():
            o = o + (a_ if lo == 0 else jnp.concatenate([jnp.zeros((lo, 256), F32), a_], axis=0))

        bl = b[c - 1:c, :]
        kdec = (k * jnp.exp(bl - b)).astype(BF16)
        vb = v.astype(BF16)
        ebl = jnp.exp(bl)
        upd = [_dot_tn(vb[:, vs], kdec[:, ks]) for vs, ks in groups]
        for scr, u_, (_, ks) in zip(st_scr, upd, groups):
            scr[...] = scr[...] * ebl[:, ks] + u_ * mask_ref[...]
        o_ref[rows, :] = o

    def sub_blocks(scrs, row0s):
        parts = [sub_block(scr, r0) for scr, r0 in zip(scrs, row0s)]
        sm = _dot(jnp.concatenate([p[-1] for p in parts], axis=0).astype(BF16), okv_ref[...])
        n = parts[0][-1].shape[0]
        for i, (scr, part) in enumerate(zip(scrs, parts)):
            sub_block_finish(scr, part, sm[i * n:(i + 1) * n])

    def load_state(st_scr, seq):
        for scr, (vs, _) in zip(st_scr, groups):
            scr[...] = jnp.concatenate([st_in_ref[seq, vs, :]] * hpg, axis=1) * mask_ref[...]

    def store_state(st_scr, seq):
        for scr, (vs, _) in zip(st_scr, groups):
            st = scr[...]
            acc = st[:, 0:dk]
            for h in range(1, hpg):
                acc = acc + st[:, h * dk:(h + 1) * dk]
            st_out_ref[seq, vs, :] = acc

    ng = len(groups)
    lane_scrs = [st_scrs[i * ng:(i + 1) * ng] for i in range(len(st_scrs) // ng)]
    _run_lanes(lane_scrs, nseq, rps, nsub, c, nst, load_state, store_state, sub_blocks)

    goff = 768 if mode == "hgrn" else 640
    nchunk = (nseq * rps) // pc

    def finish_chunk(ci, carry):
        rows = pl.ds(ci * pc if isinstance(ci, int) else pl.multiple_of(ci * pc, 8), pc)
        o = o_ref[rows, :]
        ms = _seg_dot(o * o, ovv_ref[...], 2) * (1.0 / HEAD_V)
        o_ref[rows, :] = o * lax.rsqrt(ms + EPS) * par_ref[3:4, :] * _silu(z_ref[rows, goff:goff + 256])
        return carry

    if nchunk == 1:
        finish_chunk(0, 0)
    else:
        lax.fori_loop(0, nchunk, finish_chunk, 0)


def _run_lanes(scrs, nseq, rps, nsub, c, nst, load_state, store_state, sub_blocks):
    lanes = len(scrs)
    assert nseq == lanes or nst == 1
    step = pl.program_id(1)

    def seq_group(sg, carry):
        seqs = [sg * lanes + u for u in range(lanes)]
        for scr, seq in zip(scrs, seqs):
            if nst == 1:
                load_state(scr, seq)
            else:
                pl.when(step == 0)(functools.partial(load_state, scr, seq))

        def blocks(j, carry2):
            starts = [seq * rps + j * c for seq in seqs]
            sub_blocks(scrs, [s_ if isinstance(s_, int) else pl.multiple_of(s_, 8) for s_ in starts])
            return carry2

        if nsub == 1:
            blocks(0, 0)
        else:
            lax.fori_loop(0, nsub, blocks, 0, unroll=3)
        for scr, seq in zip(scrs, seqs):
            if nst == 1:
                store_state(scr, seq)
            else:
                pl.when(step == nst - 1)(functools.partial(store_state, scr, seq))
        return carry

    if nseq == lanes:
        seq_group(0, 0)
    else:
        lax.fori_loop(0, nseq // lanes, seq_group, 0)


def _gla_call(z, sec, par, gup, okv, ovv, mask, st_in, o_prev, *, mode, dk, nrows, g):
    hk = N_HEADS * dk
    grows = (128 // dk) * HEAD_V
    kern = functools.partial(_gla_kernel, mode=mode, c=g.c, nsub=g.nsub, tv=g.tvs, dk=dk,
                             nseq=g.nseq, rps=g.rps, nst=g.nst, pc=g.pc)
    const = lambda b, s: (0, 0)
    return pl.pallas_call(
        kern,
        grid=(g.ngrid, g.nst),
        in_specs=[
            pl.BlockSpec((g.rows, SEC), lambda b, s: (g.base_blk + b * g.nst + s, sec)),
            pl.BlockSpec((8, 256), const),
            pl.BlockSpec((128, 128), const),
            pl.BlockSpec((hk, 256), const),
            pl.BlockSpec((256, 256), const),
            pl.BlockSpec((grows, 128), const),
            pl.BlockSpec((g.nseq, 256, dk), lambda b, s: (b, 0, 0)),
            pl.BlockSpec(memory_space=pl.ANY),
        ],
        input_output_aliases={7: 0},
        out_specs=[
            pl.BlockSpec((g.rows, 256), lambda b, s: (g.base_blk + b * g.nst + s, 0)),
            pl.BlockSpec((g.nseq, 256, dk), lambda b, s: (b, 0, 0)),
        ],
        out_shape=[
            jax.ShapeDtypeStruct((nrows, 256), F32),
            jax.ShapeDtypeStruct((g.nb, 256, dk), F32),
        ],
        scratch_shapes=[pltpu.VMEM((grows, 128), F32)] * (g.lanes * (256 // grows)),
        compiler_params=_cparams("parallel", "arbitrary"),
    )(z, par, gup, okv, ovv, mask, st_in, o_prev)


def _rwkv_kernel(z_ref, mu_ref, par_ref, wup_ref, aup_ref, gup_ref, ovv_ref, hmask_ref, ehead_ref, mask_ref,
                 shift_ref, st_in_ref, o_prev_ref,
                 o_ref, nshift_ref, st_out_ref,
                 prev_scr, fix_scr, r_scr, lw_scr, k_scr, v_scr, kk_scr, ka_scr, g_scr, bonus_scr, y_scr, *st_scrs,
                 c, nsub, tv, nseq, rps, nst, pc):
    del o_prev_ref
    step = pl.program_id(1)

    def shifted_chunk(ci):
        r0 = ci * pc if isinstance(ci, int) else pl.multiple_of(ci * pc, 8)
        zr = z_ref[pl.ds(r0, pc), :]
        rolled = pltpu.roll(zr, 1, axis=0)
        if pc <= rps:
            cps = rps // pc
            seq = ci // cps
            rowi = lax.broadcasted_iota(jnp.int32, (pc, 1), 0)
            before = z_ref[pl.ds(pl.multiple_of(jnp.maximum(r0 - 8, 0), 8), 8), :][7:8, :]
            first = jnp.where(ci % cps == 0, prev_scr[pl.ds(seq, 1), :], before)
            prev = jnp.where(rowi == 0, first, rolled)
            prev_scr[pl.ds(seq, 1), :] = zr[pc - 1:pc, :]
            nshift_ref[seq] = zr[pc - 1:pc, :]
        else:
            fix_scr[...] = rolled
            for i in range(nseq):
                fix_scr[i * rps:i * rps + 1, :] = shift_ref[i]
                nshift_ref[i] = zr[i * rps + tv - 1:i * rps + tv, :]
            prev = fix_scr[...]
        return r0, zr, prev

    if pc <= rps:
        @pl.when(step == 0)
        def _():
            for i in range(nseq):
                prev_scr[i:i + 1, :] = shift_ref[i]

    def prep_chunk(ci, carry):
        r0, zr, prev = shifted_chunk(ci)
        rows = pl.ds(r0, pc)
        _rwkv_prep_rows(zr, prev, rows)
        return carry

    def _rwkv_prep_rows(zr, prev, rows):
        nrow = zr.shape[0]
        zs = zr + (prev - zr) * mu_ref[...]
        r = zs[:, 0:256]
        k = zs[:, 256:512]
        v = zs[:, 512:768]
        lo = zs[:, 768:1024]
        w0, a0, k_k, k_a, r_k = (par_ref[i:i + 1, :] for i in range(5))
        lo_wa, lo_g = lo[:, 0:128], lo[:, 128:256]
        wl = w0 + _dot(jnp.tanh(lo_wa).astype(BF16), wup_ref[0:128, :])
        wexp = -(jnp.maximum(-wl, 0.0) + jnp.log1p(jnp.exp(-jnp.abs(wl)))) - 0.5
        lw = -jnp.exp(wexp)
        a = jax.nn.sigmoid(a0 + _dot(lo_wa.astype(BF16), aup_ref[0:128, :]))
        g_scr[rows, :] = _dot(jax.nn.sigmoid(lo_g).astype(BF16), gup_ref[128:256, :])
        kkp = k * k_k
        nrm = jnp.sqrt(_seg_dot(kkp * kkp, ovv_ref[...], 2))
        kk = kkp / jnp.maximum(nrm, 1e-12)
        k2 = k * (1.0 + (a - 1.0) * k_a)
        bonus_scr[rows, :] = _seg_dot(r * k2 * r_k, ovv_ref[...], 3) * v
        ka = kk * a
        if tv < rps:
            okrow = lax.rem(lax.broadcasted_iota(jnp.int32, (nrow, 1), 0), rps) < tv
            lw, k2, v, kk, ka = (jnp.where(okrow, t_, 0.0) for t_ in (lw, k2, v, kk, ka))
        r_scr[rows, :] = r
        lw_scr[rows, :] = lw
        k_scr[rows, :] = k2
        v_scr[rows, :] = v
        kk_scr[rows, :] = kk
        ka_scr[rows, :] = ka

    nchunk = (nseq * rps) // pc
    if nchunk == 1:
        prep_chunk(0, 0)
    else:
        lax.fori_loop(0, nchunk, prep_chunk, 0)

    hp = 2
    ng = N_HEADS // hp
    gw = hp * 64
    rowc = lax.broadcasted_iota(jnp.int32, (c, 1), 0)
    src2 = lax.broadcasted_iota(jnp.int32, (1, 2 * hp * c), 1) & (c - 1)
    src = src2[:, 0:hp * c]
    strict = rowc > src
    incl = rowc >= src

    def sub_blocks(lane_scrs, row0s):
        stage_state(*stage_free(lane_scrs, row0s))

    def stage_free(lane_scrs, row0s):
        hm = hmask_ref[...]
        tile = lambda t_: jnp.concatenate([t_] * hp, axis=0) * hm
        nc = hp * c
        nsolve = min(tv, c - 1)
        units, cx = [], {}
        for i, r0 in enumerate(row0s):
            rows = pl.ds(r0, c)
            lw_, r_, k_, v_, kk_, ka_ = (s_[rows, :] for s_ in (lw_scr, r_scr, k_scr, v_scr, kk_scr, ka_scr))
            gam = _cumsum_rows(lw_, rowc)
            gl = gam[c - 1:c, :]
            ginv = jnp.exp(-gam)
            gend = jnp.exp(gl - gam)
            lh, ll = _split2(jnp.concatenate([kk_ * jnp.exp(gam - lw_), r_ * jnp.exp(gam)], axis=0))
            a_, kq_ = ka_ * ginv, k_ * ginv
            ae_, ke_ = ka_ * gend, k_ * gend
            for p in range(ng):
                ls = slice(p * gw, (p + 1) * gw)
                rh, rl = _split2(jnp.concatenate([tile(a_[:, ls]), tile(kq_[:, ls])], axis=0))
                units.append((i, p))
                cx[i, p] = dict(rows=rows, ls=ls, v=v_[:, ls], egl=jnp.exp(gl[:, ls]), lh=lh[:, ls], ll=ll[:, ls],
                                rh=rh, rl=rl, vbig=tile(v_[:, ls]).astype(BF16), scr=lane_scrs[i][p],
                                end=jnp.concatenate([ae_[:, ls], ke_[:, ls]], axis=0).astype(BF16))
        g2 = {u_: _dot_nt(jnp.concatenate([cx[u_]["lh"], cx[u_]["ll"]], axis=0), cx[u_]["rh"]) for u_ in units}
        g3 = {u_: _dot_nt(cx[u_]["lh"], cx[u_]["rl"]) for u_ in units}
        m_ab, m_ra, m_rk, bkv = {}, {}, {}, {}
        for u_ in units:
            gm = g2[u_][0:2 * c] + g2[u_][2 * c:4 * c] + g3[u_]
            m_ab[u_] = jnp.where(strict, gm[0:c, 0:nc], 0.0)
            m_ra[u_] = jnp.where(incl, gm[c:2 * c, 0:nc], 0.0)
            m_rk[u_] = jnp.where(incl, gm[c:2 * c, nc:2 * nc], 0.0)
            bkv[u_] = _dot(jnp.where(strict, gm[0:c, nc:2 * nc], 0.0).astype(BF16), cx[u_]["vbig"])
        m2 = {u_: jnp.concatenate(_split2(m_ab[u_]), axis=1) for u_ in units}
        cols = _dot(jnp.concatenate([jnp.where(src2 == s, m2[u_], jnp.zeros_like(m2[u_]))
                                     for u_ in units for s in range(nsolve)], axis=0), ehead_ref[...])
        return units, cx, m_ra, m_rk, bkv, cols

    def stage_state(units, cx, m_ra, m_rk, bkv, cols):
        nsolve = min(tv, c - 1)
        tile = lambda t_: jnp.concatenate([t_] * hp, axis=0) * hmask_ref[...]
        x0 = {u_: _dot_nt(cx[u_]["lh"], cx[u_]["scr"][...].astype(BF16)) for u_ in units}
        usol = {u_: x0[u_][0:c] + bkv[u_] for u_ in units}
        for s in range(nsolve):
            for n_, u_ in enumerate(units):
                o_ = (n_ * nsolve + s) * c
                usol[u_] = usol[u_] - cols[o_:o_ + c] * usol[u_][s:s + 1, :]
        ys = {u_: _dot(jnp.concatenate([-m_ra[u_], m_rk[u_]], axis=1).astype(BF16),
                       jnp.concatenate([tile(usol[u_]).astype(BF16), cx[u_]["vbig"]], axis=0)) for u_ in units}
        upd = {u_: _dot_tn(jnp.concatenate([-usol[u_], cx[u_]["v"]], axis=0).astype(BF16), cx[u_]["end"])
               for u_ in units}
        for u_ in units:
            x = cx[u_]
            y_scr[x["rows"], x["ls"]] = x0[u_][c:2 * c] + ys[u_]
            x["scr"][...] = x["scr"][...] * x["egl"] + upd[u_] * mask_ref[...]

    def load_state(scrs, seq):
        for p, scr in enumerate(scrs):
            scr[...] = jnp.concatenate([st_in_ref[seq, p * gw:(p + 1) * gw, :]] * hp, axis=1) * mask_ref[...]

    def store_state(scrs, seq):
        for p, scr in enumerate(scrs):
            st = scr[...]
            st_out_ref[seq, p * gw:(p + 1) * gw, :] = st[:, 0:64] + st[:, 64:128]

    lane_scrs = [st_scrs[i * ng:(i + 1) * ng] for i in range(len(st_scrs) // ng)]
    _run_lanes(lane_scrs, nseq, rps, nsub, c, nst, load_state, store_state, sub_blocks)

    def finish_chunk(ci, carry):
        rows = pl.ds(ci * pc if isinstance(ci, int) else pl.multiple_of(ci * pc, 8), pc)
        y = y_scr[rows, :]
        mean = _seg_dot(y, ovv_ref[...], 3) * (1.0 / HEAD_V)
        d = y - mean
        var = _seg_dot(d * d, ovv_ref[...], 2) * (1.0 / HEAD_V)
        ln = d * lax.rsqrt(var + RWKV_LN_EPS) * par_ref[5:6, :] + par_ref[6:7, :]
        o_ref[rows, :] = (ln + bonus_scr[rows, :]) * g_scr[rows, :]
        return carry

    if nchunk == 1:
        finish_chunk(0, 0)
    else:
        lax.fori_loop(0, nchunk, finish_chunk, 0)


def _rwkv_call(z, mu, par, wup, aup, gup, ovv, hmask, ehead, mask, shift, st_in, o_prev, *, nrows, g):
    rblk = g.rows
    nsteps = g.nst
    base = g.base_blk
    pc = g.pc
    kern = functools.partial(_rwkv_kernel, c=g.c, nsub=g.nsub, tv=g.tv, nseq=g.nseq, rps=g.rps, nst=g.nst, pc=pc)
    const = lambda b, s: (0, 0)
    return pl.pallas_call(
        kern,
        grid=(g.ngrid, nsteps),
        in_specs=[
            pl.BlockSpec((rblk, SEC), lambda b, s: (base + b * nsteps + s, 1),
                         pipeline_mode=pl.Buffered(1) if g.lanes > 2 and g.full else None),
            pl.BlockSpec((1, SEC), const),
            pl.BlockSpec((8, 256), const),
            pl.BlockSpec((256, 256), const),
            pl.BlockSpec((256, 256), const),
            pl.BlockSpec((256, 256), const),
            pl.BlockSpec((256, 256), const),
            pl.BlockSpec((2 * g.c, 128), const),
            pl.BlockSpec((4 * g.c, 128), const),
            pl.BlockSpec((128, 128), const),
            pl.BlockSpec((g.nseq, 1, SEC), lambda b, s: (b, 0, 0)),
            pl.BlockSpec((g.nseq, 256, 64), lambda b, s: (b, 0, 0)),
            pl.BlockSpec(memory_space=pl.ANY),
        ],
        input_output_aliases={12: 0},
        out_specs=[
            pl.BlockSpec((rblk, 256), lambda b, s: (base + b * nsteps + s, 0)),
            pl.BlockSpec((g.nseq, 1, SEC), lambda b, s: (b, 0, 0)),
            pl.BlockSpec((g.nseq, 256, 64), lambda b, s: (b, 0, 0)),
        ],
        out_shape=[
            jax.ShapeDtypeStruct((nrows, 256), F32),
            jax.ShapeDtypeStruct((g.nb, 1, SEC), F32),
            jax.ShapeDtypeStruct((g.nb, 256, 64), F32),
        ],
        scratch_shapes=[pltpu.VMEM((max(8, g.nseq), SEC), F32), pltpu.VMEM((pc if pc > g.rps else 8, SEC), F32)]
        + [pltpu.VMEM((rblk, 256), F32)] * 9 + [pltpu.VMEM((128, 128), F32)] * (2 * g.lanes),
        compiler_params=_cparams("parallel", "arbitrary"),
    )(z, mu, par, wup, aup, gup, ovv, hmask, ehead, mask, shift, st_in, o_prev)


def _mlstm_kernel(z_ref, par_ref, bcol_ref, cw_ref, sel_ref, ovv_ref, cm_in, nm_in, mm_in, cv_in, o_prev_ref,
                  o_ref, cm_out, nm_out, mm_out, cv_out,
                  *scratch, chunks, nseq, rps, nst, pc):
    del o_prev_ref
    lanes = scratch[0].shape[0]

    def chunk_gates(scr, row0, c, tv):
        xbuf, cm_scr, nm_scr, mm_scr = scr
        rows = pl.ds(row0, c)
        mqk = z_ref[rows, 0:256]
        mv = z_ref[rows, 256:512]
        gi = z_ref[rows, 512:640]
        mo = z_ref[rows, 640:896]
        xbuf[8:8 + c, :] = mqk
        conv = par_ref[0:1, :] + xbuf[5:5 + c, :] * cw_ref[0:1, :]
        for j in range(1, CONV_W):
            conv = conv + xbuf[5 + j:5 + j + c, :] * cw_ref[j:j + 1, :]
        xbuf[5:8, :] = xbuf[8 + tv - 3:8 + tv, :]
        act = _silu(conv)
        q = act[:, 0:128]
        k = act[:, 128:256] * (MLSTM_DK ** -0.5)

        rowi = lax.broadcasted_iota(jnp.int32, (c, 1), 0)
        coli = lax.broadcasted_iota(jnp.int32, (1, c), 1)
        gcol = gi + par_ref[1:2, 0:128]
        lfc = _log_sigmoid(gcol)
        graw = None
        for part in _split3(gi):
            t_ = _dot_nt(sel_ref[...], part)
            graw = t_ if graw is None else graw + t_
        grow = graw + bcol_ref[...]
        lfr = _log_sigmoid(grow)
        if tv < c:
            gcol = jnp.where(rowi < tv, gcol, -jnp.inf)
            lfc = jnp.where(rowi < tv, lfc, 0.0)
            grow = jnp.where(coli < tv, grow, -jnp.inf)
            lfr = jnp.where(coli < tv, lfr, 0.0)
        b_col = _seg_dot_l(_tri(c).astype(BF16), lfc, 3)
        b_row = _seg_dot(lfr, _tri(c, upper=True).astype(BF16), 3)
        return dict(rows=rows, q=q, k=k, mv=mv, mo=mo, gcol=gcol, grow=grow, b_col=b_col, b_row=b_row,
                    mm=mm_scr[...], nm=nm_scr[...], cm=cm_scr[...])

    def chunk(scrs, row0s, c, tv):
        ctx = [chunk_gates(scr, row0, c, tv) for scr, row0 in zip(scrs, row0s)]
        causal = _tri(c)
        lane = lax.broadcasted_iota(jnp.int32, (1, 128), 1)
        items = [(l, h) for l in range(len(ctx)) for h in range(N_HEADS)]
        ks = lambda h: slice(h * MLSTM_DK, (h + 1) * MLSTM_DK)
        vs = lambda h: slice(h * HEAD_V, (h + 1) * HEAD_V)
        qh = {(l, h): ctx[l]["q"][:, ks(h)] for l, h in items}
        kh = {(l, h): ctx[l]["k"][:, ks(h)] for l, h in items}
        vh = {(l, h): ctx[l]["mv"][:, vs(h)].astype(BF16) for l, h in items}
        cmh = {(l, h): ctx[l]["cm"][ks(h), :] for l, h in items}
        nh = {(l, h): ctx[l]["nm"][0:1, ks(h)] for l, h in items}
        qk_raw = {it: _dot_nt(qh[it].astype(BF16), kh[it].astype(BF16)) for it in items}
        q_cm = {it: _dot(qh[it].astype(BF16), cmh[it].astype(BF16)) for it in items}
        bc, ic, mprev, m_t, sc, qk = {}, {}, {}, {}, {}, {}
        for it in items:
            l, h = it
            bc[it] = ctx[l]["b_col"][:, 4 + h:5 + h]
            br = ctx[l]["b_row"][4 + h:5 + h, :]
            ir = ctx[l]["grow"][h:h + 1, :]
            ic[it] = ctx[l]["gcol"][:, h:h + 1]
            dmat = jnp.where(causal, bc[it] - br + ir, -jnp.inf)
            mprev[it] = ctx[l]["mm"][0:1, h:h + 1]
            inter = bc[it] + mprev[it]
            m_t[it] = jnp.maximum(inter, jnp.max(dmat, axis=1, keepdims=True))
            sc[it] = jnp.exp(inter - m_t[it])
            qk[it] = qk_raw[it] * jnp.exp(dmat - m_t[it])
        num = {it: _dot(qk[it].astype(BF16), vh[it]) + sc[it] * q_cm[it] for it in items}
        qk_sum = _seg_dot(jnp.concatenate([qk[it] for it in items], axis=0), jnp.ones((c, HEAD_V), BF16), 2)
        qn_sum = _seg_dot(jnp.concatenate([qh[it] * nh[it] for it in items], axis=0),
                          jnp.ones((MLSTM_DK, HEAD_V), BF16), 2)
        kw, sl, m_new, hn = {}, {}, {}, {}
        for n_, it in enumerate(items):
            den = qk_sum[n_ * c:(n_ + 1) * c] + sc[it] * qn_sum[n_ * c:(n_ + 1) * c]
            hc = num[it] / jnp.maximum(jnp.abs(den), jnp.exp(-m_t[it]))
            m_new[it] = m_t[it][tv - 1:tv, :]
            bl = bc[it][tv - 1:tv, :]
            wl = jnp.exp(bl - bc[it] + ic[it] - m_new[it])
            sl[it] = jnp.exp(bl + mprev[it] - m_new[it])
            kw[it] = kh[it] * wl
            hn[it] = hc
        kv = {it: _dot_tn(kw[it].astype(BF16), vh[it]) for it in items}
        for l, (scr, cx) in enumerate(zip(scrs, ctx)):
            xbuf, cm_scr, nm_scr, mm_scr = scr
            heads = [(l, h) for h in range(N_HEADS)]
            cm_scr[...] = jnp.concatenate([sl[it] * cmh[it] + kv[it] for it in heads], axis=0)
            nm_scr[...] = jnp.concatenate([sl[it] * nh[it] + jnp.sum(kw[it], axis=0, keepdims=True)
                                           for it in heads], axis=1)
            mm_new = cx["mm"]
            for it in heads:
                mm_new = jnp.where(lane == it[1], m_new[it], mm_new)
            mm_scr[...] = mm_new
            o_ref[cx["rows"], :] = jnp.concatenate([hn[it] for it in heads], axis=1)

    def load_state(scr, seq):
        xbuf, cm_scr, nm_scr, mm_scr = scr
        cm_scr[...] = cm_in[seq]
        nm_scr[...] = nm_in[seq]
        mm_scr[...] = mm_in[seq]
        xbuf[0:8, :] = cv_in[seq]

    def store_state(scr, seq):
        xbuf, cm_scr, nm_scr, mm_scr = scr
        cm_out[seq] = cm_scr[...]
        nm_out[seq] = nm_scr[...]
        mm_out[seq] = mm_scr[...]
        cv_out[seq] = xbuf[0:8, :]

    assert nseq == lanes or nst == 1
    step = pl.program_id(1)

    def lane_scr(u):
        return tuple(s_.at[u] for s_ in scratch)

    def run_chunks(grp):
        for start, c, tv, count in chunks:
            def body(ci, carry2, start=start, c=c, tv=tv):
                firsts = [seq * rps + start + ci * c for _, seq in grp]
                chunk([scr for scr, _ in grp],
                      [f if isinstance(f, int) else pl.multiple_of(f, 8) for f in firsts], c, tv)
                return carry2

            if count == 1:
                body(0, 0)
            else:
                lax.fori_loop(0, count, body, 0)

    def seq_group(sg, carry):
        def lane(u, seq):
            scr = lane_scr(u)
            if nst == 1:
                load_state(scr, seq)
            else:
                pl.when(step == 0)(functools.partial(load_state, scr, seq))
            return scr

        def finish(scr, seq):
            if nst == 1:
                store_state(scr, seq)
            else:
                pl.when(step == nst - 1)(functools.partial(store_state, scr, seq))

        if max(c for _, c, _, _ in chunks) <= 16:
            grp = [(lane(u, sg * lanes + u), sg * lanes + u) for u in range(lanes)]
            run_chunks(grp)
            for scr, seq in grp:
                finish(scr, seq)
        else:
            def one_lane(u, carry2):
                seq = sg * lanes + u
                scr = lane(u, seq)
                run_chunks([(scr, seq)])
                finish(scr, seq)
                return carry2

            lax.fori_loop(0, lanes, one_lane, 0)
        return carry

    if nseq == lanes:
        seq_group(0, 0)
    else:
        lax.fori_loop(0, nseq // lanes, seq_group, 0)

    nchunk = (nseq * rps) // pc

    def finish_chunk(ci, carry):
        rows = pl.ds(ci * pc if isinstance(ci, int) else pl.multiple_of(ci * pc, 8), pc)
        o = o_ref[rows, :]
        ms = _seg_dot(o * o, ovv_ref[...], 2) * (1.0 / HEAD_V)
        o_ref[rows, :] = o * lax.rsqrt(ms + EPS) * par_ref[2:3, :] * jax.nn.sigmoid(z_ref[rows, 640:896])
        return carry

    if nchunk == 1:
        finish_chunk(0, 0)
    else:
        lax.fori_loop(0, nchunk, finish_chunk, 0)


def _mlstm_call(z, par, bcol, cw, sel, ovv, cm, nm, mm, cv, o_prev, *, nrows, g):
    if g.full:
        cmain = max(d for d in (MLSTM_CHUNK, 64, 32, 16, 8) if d <= g.rps)
        lead = g.rps % cmain
        chunks = ([(0, lead, lead, 1)] if lead else []) + [(lead, cmain, cmain, g.rps // cmain)]
        assert lead % 8 == 0
    else:
        chunks = [(0, g.rps, g.tv, 1)]
    cmax = max(c for _, c, _, _ in chunks)
    kern = functools.partial(_mlstm_kernel, chunks=tuple(chunks), nseq=g.nseq, rps=g.rps, nst=g.nst, pc=g.pc)
    const = lambda b, s: (0, 0)
    st3 = lambda shp: pl.BlockSpec((g.nseq,) + shp, lambda b, s: (b, 0, 0))
    return pl.pallas_call(
        kern,
        grid=(g.ngrid, g.nst),
        in_specs=[
            pl.BlockSpec((g.rows, SEC), lambda b, s: (g.base_blk + b * g.nst + s, 3)),
            pl.BlockSpec((8, 256), const),
            pl.BlockSpec((8, 1), const),
            pl.BlockSpec((8, 256), const),
            pl.BlockSpec((8, 128), const),
            pl.BlockSpec((256, 256), const),
            st3((128, 64)), st3((1, 128)), st3((1, 128)), st3((8, 256)),
            pl.BlockSpec(memory_space=pl.ANY),
        ],
        input_output_aliases={10: 0},
        out_specs=[
            pl.BlockSpec((g.rows, 256), lambda b, s: (g.base_blk + b * g.nst + s, 0)),
            st3((128, 64)), st3((1, 128)), st3((1, 128)), st3((8, 256)),
        ],
        out_shape=[
            jax.ShapeDtypeStruct((nrows, 256), F32),
            jax.ShapeDtypeStruct((g.nb, 128, 64), F32),
            jax.ShapeDtypeStruct((g.nb, 1, 128), F32),
            jax.ShapeDtypeStruct((g.nb, 1, 128), F32),
            jax.ShapeDtypeStruct((g.nb, 8, 256), F32),
        ],
        scratch_shapes=[
            pltpu.VMEM((g.lanes, 8 + cmax, 256), F32),
            pltpu.VMEM((g.lanes, 128, 64), F32),
            pltpu.VMEM((g.lanes, 1, 128), F32),
            pltpu.VMEM((g.lanes, 1, 128), F32),
        ],
        compiler_params=_cparams("parallel", "arbitrary"),
    )(z, par, bcol, cw, sel, ovv, cm, nm, mm, cv, o_prev)


def _merge_kernel(o0_ref, o1_ref, o2_ref, o3_ref, x_ref, g_ref, wg_ref, wb_ref, wout_ref, out_ref):
    h = _rms_bf16(x_ref[...], g_ref[...])
    mix = None
    for n, o_ref in enumerate((o0_ref, o1_ref, o2_ref, o3_ref)):
        gate = jax.nn.sigmoid(_dot(h, wg_ref[:, n * D_MODEL:(n + 1) * D_MODEL]))
        t = _dot(o_ref[...].astype(BF16), wb_ref[n]) * gate
        mix = t if mix is None else mix + t
    out_ref[...] = x_ref[...] + _dot(mix.astype(BF16), wout_ref[...])


def _merge_call(outs, x, g, wg, wb, wout, tm, o_tile):
    n = x.shape[0]
    o_spec = pl.BlockSpec((tm, 256), lambda i: (o_tile(i), 0))
    return pl.pallas_call(
        _merge_kernel,
        grid=(n // tm,),
        in_specs=[
            o_spec, o_spec, o_spec, o_spec,
            pl.BlockSpec((tm, D_MODEL), lambda i: (i, 0)),
            pl.BlockSpec((1, D_MODEL), lambda i: (0, 0)),
            pl.BlockSpec((D_MODEL, 4 * D_MODEL), lambda i: (0, 0)),
            pl.BlockSpec((4, 256, D_MODEL), lambda i: (0, 0, 0)),
            pl.BlockSpec((D_MODEL, D_MODEL), lambda i: (0, 0)),
        ],
        out_specs=pl.BlockSpec((tm, D_MODEL), lambda i: (i, 0)),
        out_shape=jax.ShapeDtypeStruct((n, D_MODEL), F32),
        compiler_params=_cparams("parallel"),
    )(*outs, x, g, wg, wb, wout)


def _mlp_kernel(x_ref, g_ref, gf_ref, wup_ref, wdn_ref, out_ref, h_scr, acc_scr, *, final):
    j = pl.program_id(1)

    @pl.when(j == 0)
    def _():
        h_scr[...] = _rms_bf16(x_ref[...], g_ref[...])
        acc_scr[...] = jnp.zeros_like(acc_scr)

    u = jnp.maximum(_dot(h_scr[...], wup_ref[...]), 0.0)
    acc_scr[...] += _dot((u * u).astype(BF16), wdn_ref[...])

    @pl.when(j == pl.num_programs(1) - 1)
    def _():
        y = x_ref[...] + acc_scr[...]
        if final:
            y = y * lax.rsqrt(jnp.mean(y * y, axis=-1, keepdims=True) + EPS) * gf_ref[...]
        out_ref[...] = y


def _mlp_call(x, g, gf, wup, wdn, tm, final):
    n = x.shape[0]
    tf = 1024
    return pl.pallas_call(
        functools.partial(_mlp_kernel, final=final),
        grid=(n // tm, D_FF // tf),
        in_specs=[
            pl.BlockSpec((tm, D_MODEL), lambda i, j: (i, 0)),
            pl.BlockSpec((1, D_MODEL), lambda i, j: (0, 0)),
            pl.BlockSpec((1, D_MODEL), lambda i, j: (0, 0)),
            pl.BlockSpec((D_MODEL, tf), lambda i, j: (0, j)),
            pl.BlockSpec((tf, D_MODEL), lambda i, j: (j, 0)),
        ],
        out_specs=pl.BlockSpec((tm, D_MODEL), lambda i, j: (i, 0)),
        out_shape=jax.ShapeDtypeStruct((n, D_MODEL), F32),
        scratch_shapes=[pltpu.VMEM((tm, D_MODEL), BF16), pltpu.VMEM((tm, D_MODEL), F32)],
        compiler_params=_cparams("parallel", "arbitrary"),
    )(x, g, gf, wup, wdn)


def _seg_ones(rows_per_head, cols_per_head):
    r = np.arange(N_HEADS * rows_per_head)[:, None] // rows_per_head
    c = np.arange(N_HEADS * cols_per_head)[None, :] // cols_per_head
    return (r == c).astype(np.float32)


def _rows(*vecs, width=256, nrows=8):
    rows = []
    for v in vecs:
        v = jnp.asarray(v, F32).reshape(-1)
        rows.append(jnp.pad(v, (0, width - v.shape[0])))
    rows += [jnp.zeros((width,), F32)] * (nrows - len(rows))
    return jnp.stack(rows)


def _layout_w_in(w):
    d = w.shape[0]
    zeros = lambda n: jnp.zeros((d, n), w.dtype)
    gl0, ml0, gt0 = 2048, 2832, 3608
    parts = [
        w[:, 0:2048],
        w[:, gl0:gl0 + 512], w[:, gl0 + 512:gl0 + 528], zeros(112), w[:, gl0 + 528:gl0 + 784], zeros(128),
        w[:, ml0:ml0 + 512], w[:, ml0 + 512:ml0 + 520], zeros(120), w[:, ml0 + 520:ml0 + 776], zeros(128),
    ]
    wz = jnp.concatenate(parts, axis=1)
    assert wz.shape[1] == Z_COLS
    return wz.astype(BF16), w[:, gt0:gt0 + 4 * D_MODEL].astype(BF16)


class _Group:
    def __init__(self, nb, trow, tv, row0, max_seq):
        self.nb, self.trow, self.tv, self.row0 = nb, trow, tv, row0
        self.full = tv == trow
        self.c = SUB if self.full else trow
        self.tvs = min(tv, self.c)
        if self.full:
            nblk = trow // self.c
            self.rps = self.c * max(d for d in range(1, min(nblk, 64) + 1) if nblk % d == 0)
        else:
            self.rps = trow
        self.nst = trow // self.rps
        self.nsub = self.rps // self.c
        self.nseq = max(d for d in range(1, max_seq + 1) if nb % d == 0 and row0 % (d * self.rps) == 0)
        self.lanes = (8 if not self.full and self.nseq % 8 == 0 else 4 if self.nseq % 4 == 0
                      else 2 if self.nseq % 2 == 0 else 1)
        assert self.nseq == self.lanes or self.nst == 1
        self.rows = self.nseq * self.rps
        self.ngrid = nb // self.nseq
        self.base_blk = row0 // self.rows
        self.pc = _pick_tile(self.rps, 384) if self.rps >= 64 else self.rows


def kernel(x_prompt, x_sample, state_hgrn, state_rwkv, state_rwkv_shift, state_gla, state_mlstm_c, state_mlstm_n, state_mlstm_m, state_mlstm_conv, meta_tokens, norm_mix, norm_mlp, norm_final, w_in, hgrn_lb, hgrn_norm, rwkv_mu, rwkv_w0, rwkv_w_up, rwkv_a0, rwkv_a_up, rwkv_g_up, rwkv_k_k, rwkv_k_a, rwkv_r_k, rwkv_ln_w, rwkv_ln_b, gla_gate_up, gla_gate_b, gla_norm, mlstm_conv_w, mlstm_conv_b, mlstm_i_b, mlstm_f_b, mlstm_norm, w_branch, w_out, w_up, w_down):
    depth = w_in.shape[0]
    bp, seq, _ = x_prompt.shape
    bs, dseq, _ = x_sample.shape
    tp = N_META + seq
    assert tp % SUB == 0 and CONV_W - 1 <= dseq <= SAMPLE_PAD
    dt = x_prompt.dtype

    gp = _Group(bp, tp, tp, 0, 4)
    gs = _Group(bs, SAMPLE_PAD, dseq, bp * tp, 16)
    tm = gp.rps
    xs = jnp.pad(x_sample, ((0, 0), (0, SAMPLE_PAD - dseq), (0, 0))).reshape(bs * SAMPLE_PAD, D_MODEL)
    n_real = bp * tp + bs * SAMPLE_PAD
    n_rows = -(-n_real // tm) * tm
    pieces = []
    for b in range(bp):
        pieces += [meta_tokens.astype(dt), x_prompt[b]]
    x = jnp.concatenate(pieces + [xs, jnp.zeros((n_rows - n_real, D_MODEL), dt)], axis=0).astype(F32)

    def mixer_tile(i):
        b, s = i // gp.nst, i % gp.nst
        p = (b // gp.nseq) * (gp.nst * gp.nseq) + s * gp.nseq + b % gp.nseq
        return jnp.where(i < bp * gp.nst, p, i)

    tm_mlp = 2 * tm if (n_rows // tm) % 2 == 0 and 2 * tm <= 1536 else tm
    merge_split = 2 if tm % 16 == 0 and tm >= 256 else 1

    ones_vv = jnp.asarray(_seg_ones(64, 64), BF16)
    ones_gl = jnp.asarray(_seg_ones(GLA_DK, 64), BF16)
    mask_gl = jnp.asarray(_seg_ones(64, GLA_DK), F32)
    sel8 = jnp.asarray(np.eye(8, 128, dtype=np.float32), BF16)
    hmask = {g.c: jnp.asarray(np.kron(np.eye(2), np.ones((g.c, 64))), F32) for g in (gp, gs)}
    ehead = {c_: jnp.concatenate([m_, m_], axis=0).astype(BF16) for c_, m_ in hmask.items()}
    mask_pair = jnp.asarray(np.kron(np.eye(2), np.ones((64, 64))), F32)

    lb_cs = jnp.cumsum(jax.nn.softmax(hgrn_lb.astype(F32), axis=0), axis=0)
    lb_all = lb_cs - lb_cs[:1]

    def init_states(states, nb, zero):
        s_hg, s_rw, shift, s_gl, c_ml, n_ml, m_ml, conv = states
        if zero:
            z = lambda *shp: jnp.zeros((depth, nb) + shp, F32)
            return dict(hg=z(256, 64), rw=z(256, 64), shift=z(1, SEC), gl=z(256, GLA_DK), cm=z(128, 64),
                        nm=z(1, 128), mm=z(1, 128), cv=z(8, 256))
        f = lambda a: a.astype(F32)
        return dict(
            hg=f(s_hg).transpose(0, 1, 2, 4, 3).reshape(depth, nb, 256, 64),
            rw=f(s_rw).reshape(depth, nb, 256, 64),
            shift=f(shift).reshape(depth, nb, 1, SEC),
            gl=f(s_gl).transpose(0, 1, 2, 4, 3).reshape(depth, nb, 256, GLA_DK),
            cm=f(c_ml).reshape(depth, nb, 128, 64),
            nm=f(n_ml).reshape(depth, nb, 1, 128),
            mm=jnp.pad(f(m_ml), ((0, 0), (0, 0), (0, 124))).reshape(depth, nb, 1, 128),
            cv=jnp.pad(f(conv), ((0, 0), (0, 0), (5, 0), (0, 0))),
        )

    sample_states = (state_hgrn, state_rwkv, state_rwkv_shift, state_gla,
                     state_mlstm_c, state_mlstm_n, state_mlstm_m, state_mlstm_conv)
    st_p = init_states(sample_states, bp, True)
    st_s = init_states(sample_states, bs, False)
    new_p = {k: [] for k in st_p}
    new_s = {k: [] for k in st_s}

    for l in range(depth):
        wz, wg = _layout_w_in(w_in[l])
        g_mix = norm_mix[l].reshape(1, D_MODEL).astype(F32)
        z = _in_proj(x, g_mix, wz, tm, mixer_tile)

        lb = lb_all[l]
        par_hg = _rows(jnp.log(lb), jnp.log1p(-lb), 1.0 - lb, hgrn_norm[l])
        par_gl = _rows(gla_gate_b[l], jnp.zeros((1,)), jnp.zeros((1,)), gla_norm[l])
        gup_gl = jnp.pad(gla_gate_up[l], ((0, 128 - gla_gate_up.shape[1]), (0, 0))).astype(BF16)
        gup_dummy = jnp.zeros((128, 128), BF16)
        par_rw = _rows(rwkv_w0[l], rwkv_a0[l], rwkv_k_k[l], rwkv_k_a[l], rwkv_r_k[l], rwkv_ln_w[l], rwkv_ln_b[l])
        nw, na = rwkv_w_up.shape[1], rwkv_a_up.shape[1]
        assert nw + na == 128 and rwkv_g_up.shape[1] == 128
        wup_p = jnp.pad(rwkv_w_up[l], ((0, 256 - nw), (0, 0))).astype(BF16)
        aup_p = jnp.pad(rwkv_a_up[l], ((nw, 256 - nw - na), (0, 0))).astype(BF16)
        gup_p = jnp.pad(rwkv_g_up[l], ((nw + na, 0), (0, 0))).astype(BF16)
        mu = rwkv_mu[l].reshape(1, SEC).astype(F32)
        gate_b = jnp.concatenate([mlstm_i_b[l], mlstm_f_b[l]]).astype(F32)
        par_ml = _rows(mlstm_conv_b[l], gate_b, mlstm_norm[l])
        bcol_ml = gate_b.reshape(8, 1)
        cw_ml = _rows(*[mlstm_conv_w[l, j] for j in range(CONV_W)])

        o_hg = o_rw = o_gl = o_ml = None
        for g, st, new in ((gp, st_p, new_p), (gs, st_s, new_s)):
            prev = lambda o: jnp.zeros((n_rows, 256), F32) if o is None else o
            o_hg, s_hg = _gla_call(z, 0, par_hg, gup_dummy, ones_vv, ones_vv, mask_pair, st["hg"][l], prev(o_hg),
                                   mode="hgrn", dk=64, nrows=n_rows, g=g)
            o_gl, s_gl = _gla_call(z, 2, par_gl, gup_gl, ones_gl, ones_vv, mask_gl, st["gl"][l], prev(o_gl),
                                   mode="gla", dk=GLA_DK, nrows=n_rows, g=g)
            o_rw, nshift, s_rw = _rwkv_call(z, mu, par_rw, wup_p, aup_p, gup_p, ones_vv, hmask[g.c], ehead[g.c],
                                            mask_pair, st["shift"][l], st["rw"][l], prev(o_rw), nrows=n_rows, g=g)
            o_ml, s_cm, s_nm, s_mm, s_cv = _mlstm_call(z, par_ml, bcol_ml, cw_ml, sel8, ones_vv, st["cm"][l], st["nm"][l],
                                                        st["mm"][l], st["cv"][l], prev(o_ml), nrows=n_rows, g=g)
            for key, val in (("hg", s_hg), ("rw", s_rw), ("shift", nshift), ("gl", s_gl), ("cm", s_cm),
                             ("nm", s_nm), ("mm", s_mm), ("cv", s_cv)):
                new[key].append(val)

        x = _merge_call((o_hg, o_rw, o_gl, o_ml), x, g_mix, wg, w_branch[l].astype(BF16), w_out[l].astype(BF16),
                        tm // merge_split, lambda i: mixer_tile(i // merge_split) * merge_split + i % merge_split)
        x = _mlp_call(x, norm_mlp[l].reshape(1, D_MODEL).astype(F32), norm_final.reshape(1, D_MODEL).astype(F32),
                      w_up[l].astype(BF16), w_down[l].astype(BF16), tm_mlp, final=(l == depth - 1))

    y_prompt = jnp.stack([x[b * tp + N_META:(b + 1) * tp] for b in range(bp)]).astype(dt)
    y_sample = x[gs.row0:gs.row0 + bs * SAMPLE_PAD].reshape(bs, SAMPLE_PAD, D_MODEL)[:, :dseq].astype(dt)

    def finish(new, nb):
        st = {k: jnp.stack(v) for k, v in new.items()}
        return (
            st["hg"].reshape(depth, nb, N_HEADS, 64, 64).transpose(0, 1, 2, 4, 3),
            st["rw"].reshape(depth, nb, N_HEADS, 64, 64),
            st["shift"].reshape(depth, nb, SEC),
            st["gl"].reshape(depth, nb, N_HEADS, 64, GLA_DK).transpose(0, 1, 2, 4, 3),
            st["cm"].reshape(depth, nb, N_HEADS, MLSTM_DK, 64),
            st["nm"].reshape(depth, nb, N_HEADS, MLSTM_DK),
            st["mm"].reshape(depth, nb, 128)[:, :, :N_HEADS],
            st["cv"][:, :, 5:8, :],
        )

    outs_p = tuple(a.astype(dt) for a in finish(new_p, bp))
    outs_s = tuple(a.astype(dt) for a in finish(new_s, bs))
    return (y_prompt, y_sample) + outs_p + outs_s
```

```python
import functools

import numpy as np
import jax
import jax.numpy as jnp
from jax import lax
from jax.experimental import pallas as pl
from jax.experimental.pallas import tpu as pltpu

F32 = jnp.float32
BF16 = jnp.bfloat16

D_MODEL = 1024
N_META = 16
N_HEADS = 4
HEAD_V = 64
GLA_DK = 32
MLSTM_DK = 32
GLA_GATE_TAU = 16.0
RWKV_LN_EPS = 64e-5
CONV_W = 4
D_FF = 4 * D_MODEL
EPS = 1e-6

SEC = 1024
Z_COLS = 4 * SEC
SUB = 16
SAMPLE_PAD = 8
MLSTM_CHUNK = 128
VMEM_LIMIT = 56 * 1024 * 1024


def _cparams(*sem):
    return pltpu.CompilerParams(dimension_semantics=sem, vmem_limit_bytes=VMEM_LIMIT)


def _pick_tile(n, max_tile, mult=8):
    best = None
    t = mult
    while t <= min(n, max_tile):
        if n % t == 0:
            best = t
        t += mult
    assert best is not None, (n, max_tile, mult)
    return best


def _split2(x):
    hi = x.astype(BF16)
    lo = (x - hi.astype(F32)).astype(BF16)
    return hi, lo


def _split3(x):
    hi = x.astype(BF16)
    r1 = x - hi.astype(F32)
    mid = r1.astype(BF16)
    lo = (r1 - mid.astype(F32)).astype(BF16)
    return hi, mid, lo


def _dot(a, b):
    return jnp.dot(a, b, preferred_element_type=F32)


def _dot_nt(a, b):
    return lax.dot_general(a, b, (((1,), (1,)), ((), ())), preferred_element_type=F32)


def _dot_tn(a, b):
    return lax.dot_general(a, b, (((0,), (0,)), ((), ())), preferred_element_type=F32)


def _seg_dot(x, m_bf16, parts):
    ps = _split2(x) if parts == 2 else _split3(x)
    acc = _dot(ps[0], m_bf16)
    for p in ps[1:]:
        acc = acc + _dot(p, m_bf16)
    return acc


def _seg_dot_l(m_bf16, x, parts):
    ps = _split2(x) if parts == 2 else _split3(x)
    acc = _dot(m_bf16, ps[0])
    for p in ps[1:]:
        acc = acc + _dot(m_bf16, p)
    return acc


def _log_sigmoid(x):
    return jnp.minimum(x, 0.0) - jnp.log1p(jnp.exp(-jnp.abs(x)))


def _silu(x):
    return x * jax.nn.sigmoid(x)


def _tri(n, upper=False):
    r = lax.broadcasted_iota(jnp.int32, (n, n), 0)
    c = lax.broadcasted_iota(jnp.int32, (n, n), 1)
    return (r <= c) if upper else (r >= c)


def _cumsum_rows(x, rowi):
    d = 1
    while d < x.shape[0]:
        x = x + jnp.where(rowi >= d, pltpu.roll(x, d, axis=0), 0.0)
        d *= 2
    return x


def _rms_bf16(x, g):
    return (x * lax.rsqrt(jnp.mean(x * x, axis=-1, keepdims=True) + EPS) * g).astype(BF16)


def _in_proj_kernel(x_ref, g_ref, w_ref, z_ref):
    h = _rms_bf16(x_ref[...], g_ref[...])
    for j in range(Z_COLS // SEC):
        z_ref[:, j * SEC:(j + 1) * SEC] = _dot(h, w_ref[:, j * SEC:(j + 1) * SEC])


def _in_proj(x, g, w, tm, out_tile):
    n = x.shape[0]
    return pl.pallas_call(
        _in_proj_kernel,
        grid=(n // tm,),
        in_specs=[
            pl.BlockSpec((tm, D_MODEL), lambda i: (i, 0)),
            pl.BlockSpec((1, D_MODEL), lambda i: (0, 0)),
            pl.BlockSpec((D_MODEL, Z_COLS), lambda i: (0, 0)),
        ],
        out_specs=pl.BlockSpec((tm, Z_COLS), lambda i: (out_tile(i), 0)),
        out_shape=jax.ShapeDtypeStruct((n, Z_COLS), F32),
        compiler_params=_cparams("parallel"),
    )(x, g, w)


def _gla_kernel(z_ref, par_ref, gup_ref, okv_ref, ovv_ref, mask_ref, st_in_ref, o_prev_ref,
                o_ref, st_out_ref, *st_scrs, mode, c, nsub, tv, dk, nseq, rps, nst, pc):
    del o_prev_ref
    voff = 512 if mode == "hgrn" else 256
    rowi = lax.broadcasted_iota(jnp.int32, (c, 1), 0)
    valid = rowi < tv
    hpg = 128 // dk
    groups = [(slice(i * hpg * HEAD_V, (i + 1) * hpg * HEAD_V), slice(i * 128, (i + 1) * 128))
              for i in range(N_HEADS // hpg)]

    def sub_block(st_scr, r0):
        rows = pl.ds(r0, c)
        if mode == "hgrn":
            q = z_ref[rows, 0:256]
            hf = z_ref[rows, 256:512]
            a = par_ref[0:1, :]
            cc = par_ref[1:2, :] + _log_sigmoid(hf)
            loga = jnp.maximum(a, cc) + jnp.log1p(jnp.exp(-jnp.abs(a - cc)))
            k = par_ref[2:3, :] * jax.nn.sigmoid(-hf)
        else:
            q = z_ref[rows, 0:128] * (GLA_DK ** -0.5)
            k = z_ref[rows, 128:256]
            ga = z_ref[rows, 512:640]
            gl = _dot(ga.astype(BF16), gup_ref[...]) + par_ref[0:1, 0:128]
            loga = _log_sigmoid(gl) * (1.0 / GLA_GATE_TAU)
        v = z_ref[rows, voff:voff + 256]
        if tv < c:
            loga = jnp.where(valid, loga, 0.0)
            k = jnp.where(valid, k, 0.0)
            v = jnp.where(valid, v, 0.0)
        b = _cumsum_rows(loga, rowi)

        qe = (q * jnp.exp(b)).astype(BF16)
        o = jnp.concatenate([_dot_nt(qe[:, ks], scr[...].astype(BF16))
                             for scr, (_, ks) in zip(st_scr, groups)], axis=1)

        lo_rows = [8 * (s // 8) for s in range(tv)]
        pieces = []
        for s in range(tv):
            rs = slice(lo_rows[s], c)
            e = jnp.exp(jnp.where(rowi[rs] >= s, b[rs] - b[s:s + 1, :], -jnp.inf))
            pieces.append(q[rs] * e * k[s:s + 1, :])
        return rows, lo_rows, k, v, b, o, jnp.concatenate(pieces, axis=0)

    def sub_block_finish(st_scr, part, sm):
        rows, lo_rows, k, v, b, o, _ = part
        acc = {}
        off = 0
        for s in range(tv):
            n = c - lo_rows[s]
            contrib = sm[off:off + n] * v[s:s + 1, :]
            off += n
            acc[lo_rows[s]] = contrib if lo_rows[s] not in acc else acc[lo_rows[s]] + contrib
        for lo, a_ in acc.items():
            o = o + (a_ if lo == 0 else jnp.concatenate([jnp.zeros((lo, 256), F32), a_], axis=0))

        bl = b[c - 1:c, :]
        kdec = (k * jnp.exp(bl - b)).astype(BF16)
        vb = v.astype(BF16)
        ebl = jnp.exp(bl)
        upd = [_dot_tn(vb[:, vs], kdec[:, ks]) for vs, ks in groups]
        for scr, u_, (_, ks) in zip(st_scr, upd, groups):
            scr[...] = scr[...] * ebl[:, ks] + u_ * mask_ref[...]
        o_ref[rows, :] = o

    def sub_blocks(scrs, row0s):
        parts = [sub_block(scr, r0) for scr, r0 in zip(scrs, row0s)]
        sm = _dot(jnp.concatenate([p[-1] for p in parts], axis=0).astype(BF16), okv_ref[...])
        n = parts[0][-1].shape[0]
        for i, (scr, part) in enumerate(zip(scrs, parts)):
            sub_block_finish(scr, part, sm[i * n:(i + 1) * n])

    def load_state(st_scr, seq):
        for scr, (vs, _) in zip(st_scr, groups):
            scr[...] = jnp.concatenate([st_in_ref[seq, vs, :]] * hpg, axis=1) * mask_ref[...]

    def store_state(st_scr, seq):
        for scr, (vs, _) in zip(st_scr, groups):
            st = scr[...]
            acc = st[:, 0:dk]
            for h in range(1, hpg):
                acc = acc + st[:, h * dk:(h + 1) * dk]
            st_out_ref[seq, vs, :] = acc

    ng = len(groups)
    lane_scrs = [st_scrs[i * ng:(i + 1) * ng] for i in range(len(st_scrs) // ng)]
    _run_lanes(lane_scrs, nseq, rps, nsub, c, nst, load_state, store_state, sub_blocks)

    goff = 768 if mode == "hgrn" else 640
    nchunk = (nseq * rps) // pc

    def finish_chunk(ci, carry):
        rows = pl.ds(ci * pc if isinstance(ci, int) else pl.multiple_of(ci * pc, 8), pc)
        o = o_ref[rows, :]
        ms = _seg_dot(o * o, ovv_ref[...], 2) * (1.0 / HEAD_V)
        o_ref[rows, :] = o * lax.rsqrt(ms + EPS) * par_ref[3:4, :] * _silu(z_ref[rows, goff:goff + 256])
        return carry

    if nchunk == 1:
        finish_chunk(0, 0)
    else:
        lax.fori_loop(0, nchunk, finish_chunk, 0)


def _run_lanes(scrs, nseq, rps, nsub, c, nst, load_state, store_state, sub_blocks):
    lanes = len(scrs)
    assert nseq == lanes or nst == 1
    step = pl.program_id(1)

    def seq_group(sg, carry):
        seqs = [sg * lanes + u for u in range(lanes)]
        for scr, seq in zip(scrs, seqs):
            if nst == 1:
                load_state(scr, seq)
            else:
                pl.when(step == 0)(functools.partial(load_state, scr, seq))

        def blocks(j, carry2):
            starts = [seq * rps + j * c for seq in seqs]
            sub_blocks(scrs, [s_ if isinstance(s_, int) else pl.multiple_of(s_, 8) for s_ in starts])
            return carry2

        if nsub == 1:
            blocks(0, 0)
        else:
            lax.fori_loop(0, nsub, blocks, 0, unroll=3)
        for scr, seq in zip(scrs, seqs):
            if nst == 1:
                store_state(scr, seq)
            else:
                pl.when(step == nst - 1)(functools.partial(store_state, scr, seq))
        return carry

    if nseq == lanes:
        seq_group(0, 0)
    else:
        lax.fori_loop(0, nseq // lanes, seq_group, 0)


def _gla_call(z, sec, par, gup, okv, ovv, mask, st_in, o_prev, *, mode, dk, nrows, g):
    hk = N_HEADS * dk
    grows = (128 // dk) * HEAD_V
    kern = functools.partial(_gla_kernel, mode=mode, c=g.c, nsub=g.nsub, tv=g.tvs, dk=dk,
                             nseq=g.nseq, rps=g.rps, nst=g.nst, pc=g.pc)
    const = lambda b, s: (0, 0)
    return pl.pallas_call(
        kern,
        grid=(g.ngrid, g.nst),
        in_specs=[
            pl.BlockSpec((g.rows, SEC), lambda b, s: (g.base_blk + b * g.nst + s, sec)),
            pl.BlockSpec((8, 256), const),
            pl.BlockSpec((128, 128), const),
            pl.BlockSpec((hk, 256), const),
            pl.BlockSpec((256, 256), const),
            pl.BlockSpec((grows, 128), const),
            pl.BlockSpec((g.nseq, 256, dk), lambda b, s: (b, 0, 0)),
            pl.BlockSpec(memory_space=pl.ANY),
        ],
        input_output_aliases={7: 0},
        out_specs=[
            pl.BlockSpec((g.rows, 256), lambda b, s: (g.base_blk + b * g.nst + s, 0)),
            pl.BlockSpec((g.nseq, 256, dk), lambda b, s: (b, 0, 0)),
        ],
        out_shape=[
            jax.ShapeDtypeStruct((nrows, 256), F32),
            jax.ShapeDtypeStruct((g.nb, 256, dk), F32),
        ],
        scratch_shapes=[pltpu.VMEM((grows, 128), F32)] * (g.lanes * (256 // grows)),
        compiler_params=_cparams("parallel", "arbitrary"),
    )(z, par, gup, okv, ovv, mask, st_in, o_prev)


def _rwkv_kernel(z_ref, mu_ref, par_ref, wup_ref, aup_ref, gup_ref, ovv_ref, hmask_ref, ehead_ref, mask_ref,
                 shift_ref, st_in_ref, o_prev_ref,
                 o_ref, nshift_ref, st_out_ref,
                 prev_scr, fix_scr, r_scr, lw_scr, k_scr, v_scr, kk_scr, ka_scr, g_scr, bonus_scr, y_scr, *st_scrs,
                 c, nsub, tv, nseq, rps, nst, pc):
    del o_prev_ref
    step = pl.program_id(1)

    def shifted_chunk(ci):
        r0 = ci * pc if isinstance(ci, int) else pl.multiple_of(ci * pc, 8)
        zr = z_ref[pl.ds(r0, pc), :]
        rolled = pltpu.roll(zr, 1, axis=0)
        if pc <= rps:
            cps = rps // pc
            seq = ci // cps
            rowi = lax.broadcasted_iota(jnp.int32, (pc, 1), 0)
            before = z_ref[pl.ds(pl.multiple_of(jnp.maximum(r0 - 8, 0), 8), 8), :][7:8, :]
            first = jnp.where(ci % cps == 0, prev_scr[pl.ds(seq, 1), :], before)
            prev = jnp.where(rowi == 0, first, rolled)
            prev_scr[pl.ds(seq, 1), :] = zr[pc - 1:pc, :]
            nshift_ref[seq] = zr[pc - 1:pc, :]
        else:
            fix_scr[...] = rolled
            for i in range(nseq):
                fix_scr[i * rps:i * rps + 1, :] = shift_ref[i]
                nshift_ref[i] = zr[i * rps + tv - 1:i * rps + tv, :]
            prev = fix_scr[...]
        return r0, zr, prev

    if pc <= rps:
        @pl.when(step == 0)
        def _():
            for i in range(nseq):
                prev_scr[i:i + 1, :] = shift_ref[i]

    def prep_chunk(ci, carry):
        r0, zr, prev = shifted_chunk(ci)
        rows = pl.ds(r0, pc)
        _rwkv_prep_rows(zr, prev, rows)
        return carry

    def _rwkv_prep_rows(zr, prev, rows):
        nrow = zr.shape[0]
        zs = zr + (prev - zr) * mu_ref[...]
        r = zs[:, 0:256]
        k = zs[:, 256:512]
        v = zs[:, 512:768]
        lo = zs[:, 768:1024]
        w0, a0, k_k, k_a, r_k = (par_ref[i:i + 1, :] for i in range(5))
        lo_wa, lo_g = lo[:, 0:128], lo[:, 128:256]
        wl = w0 + _dot(jnp.tanh(lo_wa).astype(BF16), wup_ref[0:128, :])
        wexp = -(jnp.maximum(-wl, 0.0) + jnp.log1p(jnp.exp(-jnp.abs(wl)))) - 0.5
        lw = -jnp.exp(wexp)
        a = jax.nn.sigmoid(a0 + _dot(lo_wa.astype(BF16), aup_ref[0:128, :]))
        g_scr[rows, :] = _dot(jax.nn.sigmoid(lo_g).astype(BF16), gup_ref[128:256, :])
        kkp = k * k_k
        nrm = jnp.sqrt(_seg_dot(kkp * kkp, ovv_ref[...], 2))
        kk = kkp / jnp.maximum(nrm, 1e-12)
        k2 = k * (1.0 + (a - 1.0) * k_a)
        bonus_scr[rows, :] = _seg_dot(r * k2 * r_k, ovv_ref[...], 3) * v
        ka = kk * a
        if tv < rps:
            okrow = lax.rem(lax.broadcasted_iota(jnp.int32, (nrow, 1), 0), rps) < tv
            lw, k2, v, kk, ka = (jnp.where(okrow, t_, 0.0) for t_ in (lw, k2, v, kk, ka))
        r_scr[rows, :] = r
        lw_scr[rows, :] = lw
        k_scr[rows, :] = k2
        v_scr[rows, :] = v
        kk_scr[rows, :] = kk
        ka_scr[rows, :] = ka

    nchunk = (nseq * rps) // pc
    if nchunk == 1:
        prep_chunk(0, 0)
    else:
        lax.fori_loop(0, nchunk, prep_chunk, 0)

    hp = 2
    ng = N_HEADS // hp
    gw = hp * 64
    rowc = lax.broadcasted_iota(jnp.int32, (c, 1), 0)
    src2 = lax.broadcasted_iota(jnp.int32, (1, 2 * hp * c), 1) & (c - 1)
    src = src2[:, 0:hp * c]
    strict = rowc > src
    incl = rowc >= src

    def sub_blocks(lane_scrs, row0s):
        stage_state(*stage_free(lane_scrs, row0s))

    def stage_free(lane_scrs, row0s):
        hm = hmask_ref[...]
        tile = lambda t_: jnp.concatenate([t_] * hp, axis=0) * hm
        nc = hp * c
        nsolve = min(tv, c - 1)
        units, cx = [], {}
        for i, r0 in enumerate(row0s):
            rows = pl.ds(r0, c)
            lw_, r_, k_, v_, kk_, ka_ = (s_[rows, :] for s_ in (lw_scr, r_scr, k_scr, v_scr, kk_scr, ka_scr))
            gam = _cumsum_rows(lw_, rowc)
            gl = gam[c - 1:c, :]
            ginv = jnp.exp(-gam)
            gend = jnp.exp(gl - gam)
            lh, ll = _split2(jnp.concatenate([kk_ * jnp.exp(gam - lw_), r_ * jnp.exp(gam)], axis=0))
            a_, kq_ = ka_ * ginv, k_ * ginv
            ae_, ke_ = ka_ * gend, k_ * gend
            for p in range(ng):
                ls = slice(p * gw, (p + 1) * gw)
                rh, rl = _split2(jnp.concatenate([tile(a_[:, ls]), tile(kq_[:, ls])], axis=0))
                units.append((i, p))
                cx[i, p] = dict(rows=rows, ls=ls, v=v_[:, ls], egl=jnp.exp(gl[:, ls]), lh=lh[:, ls], ll=ll[:, ls],
                                rh=rh, rl=rl, vbig=tile(v_[:, ls]).astype(BF16), scr=lane_scrs[i][p],
                                end=jnp.concatenate([ae_[:, ls], ke_[:, ls]], axis=0).astype(BF16))
        g2 = {u_: _dot_nt(jnp.concatenate([cx[u_]["lh"], cx[u_]["ll"]], axis=0), cx[u_]["rh"]) for u_ in units}
        g3 = {u_: _dot_nt(cx[u_]["lh"], cx[u_]["rl"]) for u_ in units}
        m_ab, m_ra, m_rk, bkv = {}, {}, {}, {}
        for u_ in units:
            gm = g2[u_][0:2 * c] + g2[u_][2 * c:4 * c] + g3[u_]
            m_ab[u_] = jnp.where(strict, gm[0:c, 0:nc], 0.0)
            m_ra[u_] = jnp.where(incl, gm[c:2 * c, 0:nc], 0.0)
            m_rk[u_] = jnp.where(incl, gm[c:2 * c, nc:2 * nc], 0.0)
            bkv[u_] = _dot(jnp.where(strict, gm[0:c, nc:2 * nc], 0.0).astype(BF16), cx[u_]["vbig"])
        m2 = {u_: jnp.concatenate(_split2(m_ab[u_]), axis=1) for u_ in units}
        cols = _dot(jnp.concatenate([jnp.where(src2 == s, m2[u_], jnp.zeros_like(m2[u_]))
                                     for u_ in units for s in range(nsolve)], axis=0), ehead_ref[...])
        return units, cx, m_ra, m_rk, bkv, cols

    def stage_state(units, cx, m_ra, m_rk, bkv, cols):
        nsolve = min(tv, c - 1)
        tile = lambda t_: jnp.concatenate([t_] * hp, axis=0) * hmask_ref[...]
        x0 = {u_: _dot_nt(cx[u_]["lh"], cx[u_]["scr"][...].astype(BF16)) for u_ in units}
        usol = {u_: x0[u_][0:c] + bkv[u_] for u_ in units}
        for s in range(nsolve):
            for n_, u_ in enumerate(units):
                o_ = (n_ * nsolve + s) * c
                usol[u_] = usol[u_] - cols[o_:o_ + c] * usol[u_][s:s + 1, :]
        ys = {u_: _dot(jnp.concatenate([-m_ra[u_], m_rk[u_]], axis=1).astype(BF16),
                       jnp.concatenate([tile(usol[u_]).astype(BF16), cx[u_]["vbig"]], axis=0)) for u_ in units}
        upd = {u_: _dot_tn(jnp.concatenate([-usol[u_], cx[u_]["v"]], axis=0).astype(BF16), cx[u_]["end"])
               for u_ in units}
        for u_ in units:
            x = cx[u_]
            y_scr[x["rows"], x["ls"]] = x0[u_][c:2 * c] + ys[u_]
            x["scr"][...] = x["scr"][...] * x["egl"] + upd[u_] * mask_ref[...]

    def load_state(scrs, seq):
        for p, scr in enumerate(scrs):
            scr[...] = jnp.concatenate([st_in_ref[seq, p * gw:(p + 1) * gw, :]] * hp, axis=1) * mask_ref[...]

    def store_state(scrs, seq):
        for p, scr in enumerate(scrs):
            st = scr[...]
            st_out_ref[seq, p * gw:(p + 1) * gw, :] = st[:, 0:64] + st[:, 64:128]

    lane_scrs = [st_scrs[i * ng:(i + 1) * ng] for i in range(len(st_scrs) // ng)]
    _run_lanes(lane_scrs, nseq, rps, nsub, c, nst, load_state, store_state, sub_blocks)

    def finish_chunk(ci, carry):
        rows = pl.ds(ci * pc if isinstance(ci, int) else pl.multiple_of(ci * pc, 8), pc)
        y = y_scr[rows, :]
        mean = _seg_dot(y, ovv_ref[...], 3) * (1.0 / HEAD_V)
        d = y - mean
        var = _seg_dot(d * d, ovv_ref[...], 2) * (1.0 / HEAD_V)
        ln = d * lax.rsqrt(var + RWKV_LN_EPS) * par_ref[5:6, :] + par_ref[6:7, :]
        o_ref[rows, :] = (ln + bonus_scr[rows, :]) * g_scr[rows, :]
        return carry

    if nchunk == 1:
        finish_chunk(0, 0)
    else:
        lax.fori_loop(0, nchunk, finish_chunk, 0)


def _rwkv_call(z, mu, par, wup, aup, gup, ovv, hmask, ehead, mask, shift, st_in, o_prev, *, nrows, g):
    rblk = g.rows
    nsteps = g.nst
    base = g.base_blk
    pc = g.pc
    kern = functools.partial(_rwkv_kernel, c=g.c, nsub=g.nsub, tv=g.tv, nseq=g.nseq, rps=g.rps, nst=g.nst, pc=pc)
    const = lambda b, s: (0, 0)
    return pl.pallas_call(
        kern,
        grid=(g.ngrid, nsteps),
        in_specs=[
            pl.BlockSpec((rblk, SEC), lambda b, s: (base + b * nsteps + s, 1),
                         pipeline_mode=pl.Buffered(1) if g.lanes > 2 and g.full else None),
            pl.BlockSpec((1, SEC), const),
            pl.BlockSpec((8, 256), const),
            pl.BlockSpec((256, 256), const),
            pl.BlockSpec((256, 256), const),
            pl.BlockSpec((256, 256), const),
            pl.BlockSpec((256, 256), const),
            pl.BlockSpec((2 * g.c, 128), const),
            pl.BlockSpec((4 * g.c, 128), const),
            pl.BlockSpec((128, 128), const),
            pl.BlockSpec((g.nseq, 1, SEC), lambda b, s: (b, 0, 0)),
            pl.BlockSpec((g.nseq, 256, 64), lambda b, s: (b, 0, 0)),
            pl.BlockSpec(memory_space=pl.ANY),
        ],
        input_output_aliases={12: 0},
        out_specs=[
            pl.BlockSpec((rblk, 256), lambda b, s: (base + b * nsteps + s, 0)),
            pl.BlockSpec((g.nseq, 1, SEC), lambda b, s: (b, 0, 0)),
            pl.BlockSpec((g.nseq, 256, 64), lambda b, s: (b, 0, 0)),
        ],
        out_shape=[
            jax.ShapeDtypeStruct((nrows, 256), F32),
            jax.ShapeDtypeStruct((g.nb, 1, SEC), F32),
            jax.ShapeDtypeStruct((g.nb, 256, 64), F32),
        ],
        scratch_shapes=[pltpu.VMEM((max(8, g.nseq), SEC), F32), pltpu.VMEM((pc if pc > g.rps else 8, SEC), F32)]
        + [pltpu.VMEM((rblk, 256), F32)] * 9 + [pltpu.VMEM((128, 128), F32)] * (2 * g.lanes),
        compiler_params=_cparams("parallel", "arbitrary"),
    )(z, mu, par, wup, aup, gup, ovv, hmask, ehead, mask, shift, st_in, o_prev)


def _mlstm_kernel(z_ref, par_ref, bcol_ref, cw_ref, sel_ref, ovv_ref, cm_in, nm_in, mm_in, cv_in, o_prev_ref,
                  o_ref, cm_out, nm_out, mm_out, cv_out,
                  *scratch, chunks, nseq, rps, nst, pc):
    del o_prev_ref
    lanes = scratch[0].shape[0]

    def chunk_gates(scr, row0, c, tv):
        xbuf, cm_scr, nm_scr, mm_scr = scr
        rows = pl.ds(row0, c)
        mqk = z_ref[rows, 0:256]
        mv = z_ref[rows, 256:512]
        gi = z_ref[rows, 512:640]
        mo = z_ref[rows, 640:896]
        xbuf[8:8 + c, :] = mqk
        conv = par_ref[0:1, :] + xbuf[5:5 + c, :] * cw_ref[0:1, :]
        for j in range(1, CONV_W):
            conv = conv + xbuf[5 + j:5 + j + c, :] * cw_ref[j:j + 1, :]
        xbuf[5:8, :] = xbuf[8 + tv - 3:8 + tv, :]
        act = _silu(conv)
        q = act[:, 0:128]
        k = act[:, 128:256] * (MLSTM_DK ** -0.5)

        rowi = lax.broadcasted_iota(jnp.int32, (c, 1), 0)
        coli = lax.broadcasted_iota(jnp.int32, (1, c), 1)
        gcol = gi + par_ref[1:2, 0:128]
        lfc = _log_sigmoid(gcol)
        graw = None
        for part in _split3(gi):
            t_ = _dot_nt(sel_ref[...], part)
            graw = t_ if graw is None else graw + t_
        grow = graw + bcol_ref[...]
        lfr = _log_sigmoid(grow)
        if tv < c:
            gcol = jnp.where(rowi < tv, gcol, -jnp.inf)
            lfc = jnp.where(rowi < tv, lfc, 0.0)
            grow = jnp.where(coli < tv, grow, -jnp.inf)
            lfr = jnp.where(coli < tv, lfr, 0.0)
        b_col = _seg_dot_l(_tri(c).astype(BF16), lfc, 3)
        b_row = _seg_dot(lfr, _tri(c, upper=True).astype(BF16), 3)
        return dict(rows=rows, q=q, k=k, mv=mv, mo=mo, gcol=gcol, grow=grow, b_col=b_col, b_row=b_row,
                    mm=mm_scr[...], nm=nm_scr[...], cm=cm_scr[...])

    def chunk(scrs, row0s, c, tv):
        ctx = [chunk_gates(scr, row0, c, tv) for scr, row0 in zip(scrs, row0s)]
        causal = _tri(c)
        lane = lax.broadcasted_iota(jnp.int32, (1, 128), 1)
        items = [(l, h) for l in range(len(ctx)) for h in range(N_HEADS)]
        ks = lambda h: slice(h * MLSTM_DK, (h + 1) * MLSTM_DK)
        vs = lambda h: slice(h * HEAD_V, (h + 1) * HEAD_V)
        qh = {(l, h): ctx[l]["q"][:, ks(h)] for l, h in items}
        kh = {(l, h): ctx[l]["k"][:, ks(h)] for l, h in items}
        vh = {(l, h): ctx[l]["mv"][:, vs(h)].astype(BF16) for l, h in items}
        cmh = {(l, h): ctx[l]["cm"][ks(h), :] for l, h in items}
        nh = {(l, h): ctx[l]["nm"][0:1, ks(h)] for l, h in items}
        qk_raw = {it: _dot_nt(qh[it].astype(BF16), kh[it].astype(BF16)) for it in items}
        q_cm = {it: _dot(qh[it].astype(BF16), cmh[it].astype(BF16)) for it in items}
        bc, ic, mprev, m_t, sc, qk = {}, {}, {}, {}, {}, {}
        for it in items:
            l, h = it
            bc[it] = ctx[l]["b_col"][:, 4 + h:5 + h]
            br = ctx[l]["b_row"][4 + h:5 + h, :]
            ir = ctx[l]["grow"][h:h + 1, :]
            ic[it] = ctx[l]["gcol"][:, h:h + 1]
            dmat = jnp.where(causal, bc[it] - br + ir, -jnp.inf)
            mprev[it] = ctx[l]["mm"][0:1, h:h + 1]
            inter = bc[it] + mprev[it]
            m_t[it] = jnp.maximum(inter, jnp.max(dmat, axis=1, keepdims=True))
            sc[it] = jnp.exp(inter - m_t[it])
            qk[it] = qk_raw[it] * jnp.exp(dmat - m_t[it])
        num = {it: _dot(qk[it].astype(BF16), vh[it]) + sc[it] * q_cm[it] for it in items}
        qk_sum = _seg_dot(jnp.concatenate([qk[it] for it in items], axis=0), jnp.ones((c, HEAD_V), BF16), 2)
        qn_sum = _seg_dot(jnp.concatenate([qh[it] * nh[it] for it in items], axis=0),
                          jnp.ones((MLSTM_DK, HEAD_V), BF16), 2)
        kw, sl, m_new, hn = {}, {}, {}, {}
        for n_, it in enumerate(items):
            den = qk_sum[n_ * c:(n_ + 1) * c] + sc[it] * qn_sum[n_ * c:(n_ + 1) * c]
            hc = num[it] / jnp.maximum(jnp.abs(den), jnp.exp(-m_t[it]))
            m_new[it] = m_t[it][tv - 1:tv, :]
            bl = bc[it][tv - 1:tv, :]
            wl = jnp.exp(bl - bc[it] + ic[it] - m_new[it])
            sl[it] = jnp.exp(bl + mprev[it] - m_new[it])
            kw[it] = kh[it] * wl
            hn[it] = hc
        kv = {it: _dot_tn(kw[it].astype(BF16), vh[it]) for it in items}
        for l, (scr, cx) in enumerate(zip(scrs, ctx)):
            xbuf, cm_scr, nm_scr, mm_scr = scr
            heads = [(l, h) for h in range(N_HEADS)]
            cm_scr[...] = jnp.concatenate([sl[it] * cmh[it] + kv[it] for it in heads], axis=0)
            nm_scr[...] = jnp.concatenate([sl[it] * nh[it] + jnp.sum(kw[it], axis=0, keepdims=True)
                                           for it in heads], axis=1)
            mm_new = cx["mm"]
            for it in heads:
                mm_new = jnp.where(lane == it[1], m_new[it], mm_new)
            mm_scr[...] = mm_new
            o_ref[cx["rows"], :] = jnp.concatenate([hn[it] for it in heads], axis=1)

    def load_state(scr, seq):
        xbuf, cm_scr, nm_scr, mm_scr = scr
        cm_scr[...] = cm_in[seq]
        nm_scr[...] = nm_in[seq]
        mm_scr[...] = mm_in[seq]
        xbuf[0:8, :] = cv_in[seq]

    def store_state(scr, seq):
        xbuf, cm_scr, nm_scr, mm_scr = scr
        cm_out[seq] = cm_scr[...]
        nm_out[seq] = nm_scr[...]
        mm_out[seq] = mm_scr[...]
        cv_out[seq] = xbuf[0:8, :]

    assert nseq == lanes or nst == 1
    step = pl.program_id(1)

    def lane_scr(u):
        return tuple(s_.at[u] for s_ in scratch)

    def run_chunks(grp):
        for start, c, tv, count in chunks:
            def body(ci, carry2, start=start, c=c, tv=tv):
                firsts = [seq * rps + start + ci * c for _, seq in grp]
                chunk([scr for scr, _ in grp],
                      [f if isinstance(f, int) else pl.multiple_of(f, 8) for f in firsts], c, tv)
                return carry2

            if count == 1:
                body(0, 0)
            else:
                lax.fori_loop(0, count, body, 0)

    def seq_group(sg, carry):
        def lane(u, seq):
            scr = lane_scr(u)
            if nst == 1:
                load_state(scr, seq)
            else:
                pl.when(step == 0)(functools.partial(load_state, scr, seq))
            return scr

        def finish(scr, seq):
            if nst == 1:
                store_state(scr, seq)
            else:
                pl.when(step == nst - 1)(functools.partial(store_state, scr, seq))

        if max(c for _, c, _, _ in chunks) <= 16:
            grp = [(lane(u, sg * lanes + u), sg * lanes + u) for u in range(lanes)]
            run_chunks(grp)
            for scr, seq in grp:
                finish(scr, seq)
        else:
            def one_lane(u, carry2):
                seq = sg * lanes + u
                scr = lane(u, seq)
                run_chunks([(scr, seq)])
                finish(scr, seq)
                return carry2

            lax.fori_loop(0, lanes, one_lane, 0)
        return carry

    if nseq == lanes:
        seq_group(0, 0)
    else:
        lax.fori_loop(0, nseq // lanes, seq_group, 0)

    nchunk = (nseq * rps) // pc

    def finish_chunk(ci, carry):
        rows = pl.ds(ci * pc if isinstance(ci, int) else pl.multiple_of(ci * pc, 8), pc)
        o = o_ref[rows, :]
        ms = _seg_dot(o * o, ovv_ref[...], 2) * (1.0 / HEAD_V)
        o_ref[rows, :] = o * lax.rsqrt(ms + EPS) * par_ref[2:3, :] * jax.nn.sigmoid(z_ref[rows, 640:896])
        return carry

    if nchunk == 1:
        finish_chunk(0, 0)
    else:
        lax.fori_loop(0, nchunk, finish_chunk, 0)


def _mlstm_call(z, par, bcol, cw, sel, ovv, cm, nm, mm, cv, o_prev, *, nrows, g):
    if g.full:
        cmain = max(d for d in (MLSTM_CHUNK, 64, 32, 16, 8) if d <= g.rps)
        lead = g.rps % cmain
        chunks = ([(0, lead, lead, 1)] if lead else []) + [(lead, cmain, cmain, g.rps // cmain)]
        assert lead % 8 == 0
    else:
        chunks = [(0, g.rps, g.tv, 1)]
    cmax = max(c for _, c, _, _ in chunks)
    kern = functools.partial(_mlstm_kernel, chunks=tuple(chunks), nseq=g.nseq, rps=g.rps, nst=g.nst, pc=g.pc)
    const = lambda b, s: (0, 0)
    st3 = lambda shp: pl.BlockSpec((g.nseq,) + shp, lambda b, s: (b, 0, 0))
    return pl.pallas_call(
        kern,
        grid=(g.ngrid, g.nst),
        in_specs=[
            pl.BlockSpec((g.rows, SEC), lambda b, s: (g.base_blk + b * g.nst + s, 3)),
            pl.BlockSpec((8, 256), const),
            pl.BlockSpec((8, 1), const),
            pl.BlockSpec((8, 256), const),
            pl.BlockSpec((8, 128), const),
            pl.BlockSpec((256, 256), const),
            st3((128, 64)), st3((1, 128)), st3((1, 128)), st3((8, 256)),
            pl.BlockSpec(memory_space=pl.ANY),
        ],
        input_output_aliases={10: 0},
        out_specs=[
            pl.BlockSpec((g.rows, 256), lambda b, s: (g.base_blk + b * g.nst + s, 0)),
            st3((128, 64)), st3((1, 128)), st3((1, 128)), st3((8, 256)),
        ],
        out_shape=[
            jax.ShapeDtypeStruct((nrows, 256), F32),
            jax.ShapeDtypeStruct((g.nb, 128, 64), F32),
            jax.ShapeDtypeStruct((g.nb, 1, 128), F32),
            jax.ShapeDtypeStruct((g.nb, 1, 128), F32),
            jax.ShapeDtypeStruct((g.nb, 8, 256), F32),
        ],
        scratch_shapes=[
            pltpu.VMEM((g.lanes, 8 + cmax, 256), F32),
            pltpu.VMEM((g.lanes, 128, 64), F32),
            pltpu.VMEM((g.lanes, 1, 128), F32),
            pltpu.VMEM((g.lanes, 1, 128), F32),
        ],
        compiler_params=_cparams("parallel", "arbitrary"),
    )(z, par, bcol, cw, sel, ovv, cm, nm, mm, cv, o_prev)


def _merge_kernel(o0_ref, o1_ref, o2_ref, o3_ref, x_ref, g_ref, wg_ref, wb_ref, wout_ref, out_ref):
    h = _rms_bf16(x_ref[...], g_ref[...])
    mix = None
    for n, o_ref in enumerate((o0_ref, o1_ref, o2_ref, o3_ref)):
        gate = jax.nn.sigmoid(_dot(h, wg_ref[:, n * D_MODEL:(n + 1) * D_MODEL]))
        t = _dot(o_ref[...].astype(BF16), wb_ref[n]) * gate
        mix = t if mix is None else mix + t
    out_ref[...] = x_ref[...] + _dot(mix.astype(BF16), wout_ref[...])


def _merge_call(outs, x, g, wg, wb, wout, tm, o_tile):
    n = x.shape[0]
    o_spec = pl.BlockSpec((tm, 256), lambda i: (o_tile(i), 0))
    return pl.pallas_call(
        _merge_kernel,
        grid=(n // tm,),
        in_specs=[
            o_spec, o_spec, o_spec, o_spec,
            pl.BlockSpec((tm, D_MODEL), lambda i: (i, 0)),
            pl.BlockSpec((1, D_MODEL), lambda i: (0, 0)),
            pl.BlockSpec((D_MODEL, 4 * D_MODEL), lambda i: (0, 0)),
            pl.BlockSpec((4, 256, D_MODEL), lambda i: (0, 0, 0)),
            pl.BlockSpec((D_MODEL, D_MODEL), lambda i: (0, 0)),
        ],
        out_specs=pl.BlockSpec((tm, D_MODEL), lambda i: (i, 0)),
        out_shape=jax.ShapeDtypeStruct((n, D_MODEL), F32),
        compiler_params=_cparams("parallel"),
    )(*outs, x, g, wg, wb, wout)


def _mlp_kernel(x_ref, g_ref, gf_ref, wup_ref, wdn_ref, out_ref, *, final):
    x = x_ref[...]
    h = _rms_bf16(x, g_ref[...])
    y = x
    for j in range(D_FF // D_MODEL):
        cols = slice(j * D_MODEL, (j + 1) * D_MODEL)
        u = jnp.maximum(_dot(h, wup_ref[:, cols]), 0.0)
        y = y + _dot((u * u).astype(BF16), wdn_ref[cols, :])
    if final:
        y = y * lax.rsqrt(jnp.mean(y * y, axis=-1, keepdims=True) + EPS) * gf_ref[...]
    out_ref[...] = y


def _mlp_call(x, g, gf, wup, wdn, tm, final):
    n = x.shape[0]
    return pl.pallas_call(
        functools.partial(_mlp_kernel, final=final),
        grid=(n // tm,),
        in_specs=[
            pl.BlockSpec((tm, D_MODEL), lambda i: (i, 0)),
            pl.BlockSpec((1, D_MODEL), lambda i: (0, 0)),
            pl.BlockSpec((1, D_MODEL), lambda i: (0, 0)),
            pl.BlockSpec((D_MODEL, D_FF), lambda i: (0, 0)),
            pl.BlockSpec((D_FF, D_MODEL), lambda i: (0, 0)),
        ],
        out_specs=pl.BlockSpec((tm, D_MODEL), lambda i: (i, 0)),
        out_shape=jax.ShapeDtypeStruct((n, D_MODEL), F32),
        compiler_params=_cparams("parallel"),
    )(x, g, gf, wup, wdn)


def _seg_ones(rows_per_head, cols_per_head):
    r = np.arange(N_HEADS * rows_per_head)[:, None] // rows_per_head
    c = np.arange(N_HEADS * cols_per_head)[None, :] // cols_per_head
    return (r == c).astype(np.float32)


def _rows(*vecs, width=256, nrows=8):
    rows = []
    for v in vecs:
        v = jnp.asarray(v, F32).reshape(-1)
        rows.append(jnp.pad(v, (0, width - v.shape[0])))
    rows += [jnp.zeros((width,), F32)] * (nrows - len(rows))
    return jnp.stack(rows)


def _layout_w_in(w):
    d = w.shape[0]
    zeros = lambda n: jnp.zeros((d, n), w.dtype)
    gl0, ml0, gt0 = 2048, 2832, 3608
    parts = [
        w[:, 0:2048],
        w[:, gl0:gl0 + 512], w[:, gl0 + 512:gl0 + 528], zeros(112), w[:, gl0 + 528:gl0 + 784], zeros(128),
        w[:, ml0:ml0 + 512], w[:, ml0 + 512:ml0 + 520], zeros(120), w[:, ml0 + 520:ml0 + 776], zeros(128),
    ]
    wz = jnp.concatenate(parts, axis=1)
    assert wz.shape[1] == Z_COLS
    return wz.astype(BF16), w[:, gt0:gt0 + 4 * D_MODEL].astype(BF16)


class _Group:
    def __init__(self, nb, trow, tv, row0, max_seq):
        self.nb, self.trow, self.tv, self.row0 = nb, trow, tv, row0
        self.full = tv == trow
        self.c = SUB if self.full else trow
        self.tvs = min(tv, self.c)
        if self.full:
            nblk = trow // self.c
            self.rps = self.c * max(d for d in range(1, min(nblk, 64) + 1) if nblk % d == 0)
        else:
            self.rps = trow
        self.nst = trow // self.rps
        self.nsub = self.rps // self.c
        self.nseq = max(d for d in range(1, max_seq + 1) if nb % d == 0 and row0 % (d * self.rps) == 0)
        self.lanes = (8 if not self.full and self.nseq % 8 == 0 else 4 if self.nseq % 4 == 0
                      else 2 if self.nseq % 2 == 0 else 1)
        assert self.nseq == self.lanes or self.nst == 1
        self.rows = self.nseq * self.rps
        self.ngrid = nb // self.nseq
        self.base_blk = row0 // self.rows
        self.pc = _pick_tile(self.rps, 384) if self.rps >= 64 else self.rows


def kernel(x_prompt, x_sample, state_hgrn, state_rwkv, state_rwkv_shift, state_gla, state_mlstm_c, state_mlstm_n, state_mlstm_m, state_mlstm_conv, meta_tokens, norm_mix, norm_mlp, norm_final, w_in, hgrn_lb, hgrn_norm, rwkv_mu, rwkv_w0, rwkv_w_up, rwkv_a0, rwkv_a_up, rwkv_g_up, rwkv_k_k, rwkv_k_a, rwkv_r_k, rwkv_ln_w, rwkv_ln_b, gla_gate_up, gla_gate_b, gla_norm, mlstm_conv_w, mlstm_conv_b, mlstm_i_b, mlstm_f_b, mlstm_norm, w_branch, w_out, w_up, w_down):
    depth = w_in.shape[0]
    bp, seq, _ = x_prompt.shape
    bs, dseq, _ = x_sample.shape
    tp = N_META + seq
    assert tp % SUB == 0 and CONV_W - 1 <= dseq <= SAMPLE_PAD
    dt = x_prompt.dtype

    gp = _Group(bp, tp, tp, 0, 4)
    gs = _Group(bs, SAMPLE_PAD, dseq, bp * tp, 16)
    tm = gp.rps
    xs = jnp.pad(x_sample, ((0, 0), (0, SAMPLE_PAD - dseq), (0, 0))).reshape(bs * SAMPLE_PAD, D_MODEL)
    n_real = bp * tp + bs * SAMPLE_PAD
    n_rows = -(-n_real // tm) * tm
    pieces = []
    for b in range(bp):
        pieces += [meta_tokens.astype(dt), x_prompt[b]]
    x = jnp.concatenate(pieces + [xs, jnp.zeros((n_rows - n_real, D_MODEL), dt)], axis=0).astype(F32)

    def mixer_tile(i):
        b, s = i // gp.nst, i % gp.nst
        p = (b // gp.nseq) * (gp.nst * gp.nseq) + s * gp.nseq + b % gp.nseq
        return jnp.where(i < bp * gp.nst, p, i)

    tm_mlp = tm // 2 if tm % 16 == 0 and tm >= 256 else tm
    merge_split = 2 if tm % 16 == 0 and tm >= 256 else 1

    ones_vv = jnp.asarray(_seg_ones(64, 64), BF16)
    ones_gl = jnp.asarray(_seg_ones(GLA_DK, 64), BF16)
    mask_gl = jnp.asarray(_seg_ones(64, GLA_DK), F32)
    sel8 = jnp.asarray(np.eye(8, 128, dtype=np.float32), BF16)
    hmask = {g.c: jnp.asarray(np.kron(np.eye(2), np.ones((g.c, 64))), F32) for g in (gp, gs)}
    ehead = {c_: jnp.concatenate([m_, m_], axis=0).astype(BF16) for c_, m_ in hmask.items()}
    mask_pair = jnp.asarray(np.kron(np.eye(2), np.ones((64, 64))), F32)

    lb_cs = jnp.cumsum(jax.nn.softmax(hgrn_lb.astype(F32), axis=0), axis=0)
    lb_all = lb_cs - lb_cs[:1]

    def init_states(states, nb, zero):
        s_hg, s_rw, shift, s_gl, c_ml, n_ml, m_ml, conv = states
        if zero:
            z = lambda *shp: jnp.zeros((depth, nb) + shp, F32)
            return dict(hg=z(256, 64), rw=z(256, 64), shift=z(1, SEC), gl=z(256, GLA_DK), cm=z(128, 64),
                        nm=z(1, 128), mm=z(1, 128), cv=z(8, 256))
        f = lambda a: a.astype(F32)
        return dict(
            hg=f(s_hg).transpose(0, 1, 2, 4, 3).reshape(depth, nb, 256, 64),
            rw=f(s_rw).reshape(depth, nb, 256, 64),
            shift=f(shift).reshape(depth, nb, 1, SEC),
            gl=f(s_gl).transpose(0, 1, 2, 4, 3).reshape(depth, nb, 256, GLA_DK),
            cm=f(c_ml).reshape(depth, nb, 128, 64),
            nm=f(n_ml).reshape(depth, nb, 1, 128),
            mm=jnp.pad(f(m_ml), ((0, 0), (0, 0), (0, 124))).reshape(depth, nb, 1, 128),
            cv=jnp.pad(f(conv), ((0, 0), (0, 0), (5, 0), (0, 0))),
        )

    sample_states = (state_hgrn, state_rwkv, state_rwkv_shift, state_gla,
                     state_mlstm_c, state_mlstm_n, state_mlstm_m, state_mlstm_conv)
    st_p = init_states(sample_states, bp, True)
    st_s = init_states(sample_states, bs, False)
    new_p = {k: [] for k in st_p}
    new_s = {k: [] for k in st_s}

    for l in range(depth):
        wz, wg = _layout_w_in(w_in[l])
        g_mix = norm_mix[l].reshape(1, D_MODEL).astype(F32)
        z = _in_proj(x, g_mix, wz, tm, mixer_tile)

        lb = lb_all[l]
        par_hg = _rows(jnp.log(lb), jnp.log1p(-lb), 1.0 - lb, hgrn_norm[l])
        par_gl = _rows(gla_gate_b[l], jnp.zeros((1,)), jnp.zeros((1,)), gla_norm[l])
        gup_gl = jnp.pad(gla_gate_up[l], ((0, 128 - gla_gate_up.shape[1]), (0, 0))).astype(BF16)
        gup_dummy = jnp.zeros((128, 128), BF16)
        par_rw = _rows(rwkv_w0[l], rwkv_a0[l], rwkv_k_k[l], rwkv_k_a[l], rwkv_r_k[l], rwkv_ln_w[l], rwkv_ln_b[l])
        nw, na = rwkv_w_up.shape[1], rwkv_a_up.shape[1]
        assert nw + na == 128 and rwkv_g_up.shape[1] == 128
        wup_p = jnp.pad(rwkv_w_up[l], ((0, 256 - nw), (0, 0))).astype(BF16)
        aup_p = jnp.pad(rwkv_a_up[l], ((nw, 256 - nw - na), (0, 0))).astype(BF16)
        gup_p = jnp.pad(rwkv_g_up[l], ((nw + na, 0), (0, 0))).astype(BF16)
        mu = rwkv_mu[l].reshape(1, SEC).astype(F32)
        gate_b = jnp.concatenate([mlstm_i_b[l], mlstm_f_b[l]]).astype(F32)
        par_ml = _rows(mlstm_conv_b[l], gate_b, mlstm_norm[l])
        bcol_ml = gate_b.reshape(8, 1)
        cw_ml = _rows(*[mlstm_conv_w[l, j] for j in range(CONV_W)])

        o_hg = o_rw = o_gl = o_ml = None
        for g, st, new in ((gp, st_p, new_p), (gs, st_s, new_s)):
            prev = lambda o: jnp.zeros((n_rows, 256), F32) if o is None else o
            o_hg, s_hg = _gla_call(z, 0, par_hg, gup_dummy, ones_vv, ones_vv, mask_pair, st["hg"][l], prev(o_hg),
                                   mode="hgrn", dk=64, nrows=n_rows, g=g)
            o_gl, s_gl = _gla_call(z, 2, par_gl, gup_gl, ones_gl, ones_vv, mask_gl, st["gl"][l], prev(o_gl),
                                   mode="gla", dk=GLA_DK, nrows=n_rows, g=g)
            o_rw, nshift, s_rw = _rwkv_call(z, mu, par_rw, wup_p, aup_p, gup_p, ones_vv, hmask[g.c], ehead[g.c],
                                            mask_pair, st["shift"][l], st["rw"][l], prev(o_rw), nrows=n_rows, g=g)
            o_ml, s_cm, s_nm, s_mm, s_cv = _mlstm_call(z, par_ml, bcol_ml, cw_ml, sel8, ones_vv, st["cm"][l], st["nm"][l],
                                                        st["mm"][l], st["cv"][l], prev(o_ml), nrows=n_rows, g=g)
            for key, val in (("hg", s_hg), ("rw", s_rw), ("shift", nshift), ("gl", s_gl), ("cm", s_cm),
                             ("nm", s_nm), ("mm", s_mm), ("cv", s_cv)):
                new[key].append(val)

        x = _merge_call((o_hg, o_rw, o_gl, o_ml), x, g_mix, wg, w_branch[l].astype(BF16), w_out[l].astype(BF16),
                        tm // merge_split, lambda i: mixer_tile(i // merge_split) * merge_split + i % merge_split)
        x = _mlp_call(x, norm_mlp[l].reshape(1, D_MODEL).astype(F32), norm_final.reshape(1, D_MODEL).astype(F32),
                      w_up[l].astype(BF16), w_down[l].astype(BF16), tm_mlp, final=(l == depth - 1))

    y_prompt = jnp.stack([x[b * tp + N_META:(b + 1) * tp] for b in range(bp)]).astype(dt)
    y_sample = x[gs.row0:gs.row0 + bs * SAMPLE_PAD].reshape(bs, SAMPLE_PAD, D_MODEL)[:, :dseq].astype(dt)

    def finish(new, nb):
        st = {k: jnp.stack(v) for k, v in new.items()}
        return (
            st["hg"].reshape(depth, nb, N_HEADS, 64, 64).transpose(0, 1, 2, 4, 3),
            st["rw"].reshape(depth, nb, N_HEADS, 64, 64),
            st["shift"].reshape(depth, nb, SEC),
            st["gl"].reshape(depth, nb, N_HEADS, 64, GLA_DK).transpose(0, 1, 2, 4, 3),
            st["cm"].reshape(depth, nb, N_HEADS, MLSTM_DK, 64),
            st["nm"].reshape(depth, nb, N_HEADS, MLSTM_DK),
            st["mm"].reshape(depth, nb, 128)[:, :, :N_HEADS],
            st["cv"][:, :, 5:8, :],
        )

    outs_p = tuple(a.astype(dt) for a in finish(new_p, bp))
    outs_s = tuple(a.astype(dt) for a in finish(new_s, bs))
    return (y_prompt, y_sample) + outs_p + outs_s
```

```python
import functools

import numpy as np
import jax
import jax.numpy as jnp
from jax import lax
from jax.experimental import pallas as pl
from jax.experimental.pallas import tpu as pltpu

F32 = jnp.float32
BF16 = jnp.bfloat16

D_MODEL = 1024
N_META = 16
N_HEADS = 4
HEAD_V = 64
GLA_DK = 32
MLSTM_DK = 32
GLA_GATE_TAU = 16.0
RWKV_LN_EPS = 64e-5
CONV_W = 4
D_FF = 4 * D_MODEL
EPS = 1e-6

SEC = 1024
Z_COLS = 4 * SEC
SUB = 16
SAMPLE_PAD = 8
MLSTM_CHUNK = 128
VMEM_LIMIT = 60 * 1024 * 1024


def _cparams(*sem):
    return pltpu.CompilerParams(dimension_semantics=sem, vmem_limit_bytes=VMEM_LIMIT)


def _pick_tile(n, max_tile, mult=8):
    best = None
    t = mult
    while t <= min(n, max_tile):
        if n % t == 0:
            best = t
        t += mult
    assert best is not None, (n, max_tile, mult)
    return best


def _split2(x):
    hi = x.astype(BF16)
    lo = (x - hi.astype(F32)).astype(BF16)
    return hi, lo


def _split3(x):
    hi = x.astype(BF16)
    r1 = x - hi.astype(F32)
    mid = r1.astype(BF16)
    lo = (r1 - mid.astype(F32)).astype(BF16)
    return hi, mid, lo


def _dot(a, b):
    return jnp.dot(a, b, preferred_element_type=F32)


def _dot_nt(a, b):
    return lax.dot_general(a, b, (((1,), (1,)), ((), ())), preferred_element_type=F32)


def _dot_tn(a, b):
    return lax.dot_general(a, b, (((0,), (0,)), ((), ())), preferred_element_type=F32)


def _seg_dot(x, m_bf16, parts):
    ps = _split2(x) if parts == 2 else _split3(x)
    acc = _dot(ps[0], m_bf16)
    for p in ps[1:]:
        acc = acc + _dot(p, m_bf16)
    return acc


def _seg_dot_l(m_bf16, x, parts):
    ps = _split2(x) if parts == 2 else _split3(x)
    acc = _dot(m_bf16, ps[0])
    for p in ps[1:]:
        acc = acc + _dot(m_bf16, p)
    return acc


def _log_sigmoid(x):
    return jnp.minimum(x, 0.0) - jnp.log1p(jnp.exp(-jnp.abs(x)))


def _silu(x):
    return x * jax.nn.sigmoid(x)


def _tri(n, upper=False):
    r = lax.broadcasted_iota(jnp.int32, (n, n), 0)
    c = lax.broadcasted_iota(jnp.int32, (n, n), 1)
    return (r <= c) if upper else (r >= c)


def _cumsum_rows(x, rowi):
    d = 1
    while d < x.shape[0]:
        x = x + jnp.where(rowi >= d, pltpu.roll(x, d, axis=0), 0.0)
        d *= 2
    return x


def _rms_bf16(x, g):
    return (x * lax.rsqrt(jnp.mean(x * x, axis=-1, keepdims=True) + EPS) * g).astype(BF16)


def _in_proj_kernel(x_ref, g_ref, w_ref, z_ref):
    h = _rms_bf16(x_ref[...], g_ref[...])
    for j in range(Z_COLS // SEC):
        z_ref[:, j * SEC:(j + 1) * SEC] = _dot(h, w_ref[:, j * SEC:(j + 1) * SEC])


def _in_proj(x, g, w, tm, out_tile):
    n = x.shape[0]
    return pl.pallas_call(
        _in_proj_kernel,
        grid=(n // tm,),
        in_specs=[
            pl.BlockSpec((tm, D_MODEL), lambda i: (i, 0)),
            pl.BlockSpec((1, D_MODEL), lambda i: (0, 0)),
            pl.BlockSpec((D_MODEL, Z_COLS), lambda i: (0, 0)),
        ],
        out_specs=pl.BlockSpec((tm, Z_COLS), lambda i: (out_tile(i), 0)),
        out_shape=jax.ShapeDtypeStruct((n, Z_COLS), F32),
        compiler_params=_cparams("parallel"),
    )(x, g, w)


def _gla_kernel(z_ref, par_ref, gup_ref, okv_ref, ovv_ref, mask_ref, st_in_ref, o_prev_ref,
                o_ref, st_out_ref, *st_scrs, mode, c, nsub, tv, dk, nseq, rps, nst, pc):
    del o_prev_ref
    voff = 512 if mode == "hgrn" else 256
    rowi = lax.broadcasted_iota(jnp.int32, (c, 1), 0)
    valid = rowi < tv
    hpg = 128 // dk
    groups = [(slice(i * hpg * HEAD_V, (i + 1) * hpg * HEAD_V), slice(i * 128, (i + 1) * 128))
              for i in range(N_HEADS // hpg)]

    def sub_block(st_scr, r0):
        rows = pl.ds(r0, c)
        if mode == "hgrn":
            q = z_ref[rows, 0:256]
            hf = z_ref[rows, 256:512]
            a = par_ref[0:1, :]
            cc = par_ref[1:2, :] + _log_sigmoid(hf)
            loga = jnp.maximum(a, cc) + jnp.log1p(jnp.exp(-jnp.abs(a - cc)))
            k = par_ref[2:3, :] * jax.nn.sigmoid(-hf)
        else:
            q = z_ref[rows, 0:128] * (GLA_DK ** -0.5)
            k = z_ref[rows, 128:256]
            ga = z_ref[rows, 512:640]
            gl = _dot(ga.astype(BF16), gup_ref[...]) + par_ref[0:1, 0:128]
            loga = _log_sigmoid(gl) * (1.0 / GLA_GATE_TAU)
        v = z_ref[rows, voff:voff + 256]
        if tv < c:
            loga = jnp.where(valid, loga, 0.0)
            k = jnp.where(valid, k, 0.0)
            v = jnp.where(valid, v, 0.0)
        b = _cumsum_rows(loga, rowi)

        qe = (q * jnp.exp(b)).astype(BF16)
        o = jnp.concatenate([_dot_nt(qe[:, ks], scr[...].astype(BF16))
                             for scr, (_, ks) in zip(st_scr, groups)], axis=1)

        lo_rows = [8 * (s // 8) for s in range(tv)]
        pieces = []
        for s in range(tv):
            rs = slice(lo_rows[s], c)
            e = jnp.exp(jnp.where(rowi[rs] >= s, b[rs] - b[s:s + 1, :], -jnp.inf))
            pieces.append(q[rs] * e * k[s:s + 1, :])
        return rows, lo_rows, k, v, b, o, jnp.concatenate(pieces, axis=0)

    def sub_block_finish(st_scr, part, sm):
        rows, lo_rows, k, v, b, o, _ = part
        acc = {}
        off = 0
        for s in range(tv):
            n = c - lo_rows[s]
            contrib = sm[off:off + n] * v[s:s + 1, :]
            off += n
            acc[lo_rows[s]] = contrib if lo_rows[s] not in acc else acc[lo_rows[s]] + contrib
        for lo, a_ in acc.items():
            o = o + (a_ if lo == 0 else jnp.concatenate([jnp.zeros((lo, 256), F32), a_], axis=0))

        bl = b[c - 1:c, :]
        kdec = (k * jnp.exp(bl - b)).astype(BF16)
        vb = v.astype(BF16)
        ebl = jnp.exp(bl)
        upd = [_dot_tn(vb[:, vs], kdec[:, ks]) for vs, ks in groups]
        for scr, u_, (_, ks) in zip(st_scr, upd, groups):
            scr[...] = scr[...] * ebl[:, ks] + u_ * mask_ref[...]
        o_ref[rows, :] = o

    def sub_blocks(scrs, row0s):
        parts = [sub_block(scr, r0) for scr, r0 in zip(scrs, row0s)]
        sm = _dot(jnp.concatenate([p[-1] for p in parts], axis=0).astype(BF16), okv_ref[...])
        n = parts[0][-1].shape[0]
        for i, (scr, part) in enumerate(zip(scrs, parts)):
            sub_block_finish(scr, part, sm[i * n:(i + 1) * n])

    def load_state(st_scr, seq):
        for scr, (vs, _) in zip(st_scr, groups):
            scr[...] = jnp.concatenate([st_in_ref[seq, vs, :]] * hpg, axis=1) * mask_ref[...]

    def store_state(st_scr, seq):
        for scr, (vs, _) in zip(st_scr, groups):
            st = scr[...]
            acc = st[:, 0:dk]
            for h in range(1, hpg):
                acc = acc + st[:, h * dk:(h + 1) * dk]
            st_out_ref[seq, vs, :] = acc

    ng = len(groups)
    lane_scrs = [st_scrs[i * ng:(i + 1) * ng] for i in range(len(st_scrs) // ng)]
    _run_lanes(lane_scrs, nseq, rps, nsub, c, nst, load_state, store_state, sub_blocks)

    goff = 768 if mode == "hgrn" else 640
    nchunk = (nseq * rps) // pc

    def finish_chunk(ci, carry):
        rows = pl.ds(ci * pc if isinstance(ci, int) else pl.multiple_of(ci * pc, 8), pc)
        o = o_ref[rows, :]
        ms = _seg_dot(o * o, ovv_ref[...], 2) * (1.0 / HEAD_V)
        o_ref[rows, :] = o * lax.rsqrt(ms + EPS) * par_ref[3:4, :] * _silu(z_ref[rows, goff:goff + 256])
        return carry

    if nchunk == 1:
        finish_chunk(0, 0)
    else:
        lax.fori_loop(0, nchunk, finish_chunk, 0)


def _run_lanes(scrs, nseq, rps, nsub, c, nst, load_state, store_state, sub_blocks):
    lanes = len(scrs)
    assert nseq == lanes or nst == 1
    step = pl.program_id(1)

    def seq_group(sg, carry):
        seqs = [sg * lanes + u for u in range(lanes)]
        for scr, seq in zip(scrs, seqs):
            if nst == 1:
                load_state(scr, seq)
            else:
                pl.when(step == 0)(functools.partial(load_state, scr, seq))

        def blocks(j, carry2):
            starts = [seq * rps + j * c for seq in seqs]
            sub_blocks(scrs, [s_ if isinstance(s_, int) else pl.multiple_of(s_, 8) for s_ in starts])
            return carry2

        if nsub == 1:
            blocks(0, 0)
        else:
            lax.fori_loop(0, nsub, blocks, 0, unroll=3)
        for scr, seq in zip(scrs, seqs):
            if nst == 1:
                store_state(scr, seq)
            else:
                pl.when(step == nst - 1)(functools.partial(store_state, scr, seq))
        return carry

    if nseq == lanes:
        seq_group(0, 0)
    else:
        lax.fori_loop(0, nseq // lanes, seq_group, 0)


def _gla_call(z, sec, par, gup, okv, ovv, mask, st_in, o_prev, *, mode, dk, nrows, g):
    hk = N_HEADS * dk
    grows = (128 // dk) * HEAD_V
    kern = functools.partial(_gla_kernel, mode=mode, c=g.c, nsub=g.nsub, tv=g.tvs, dk=dk,
                             nseq=g.nseq, rps=g.rps, nst=g.nst, pc=g.pc)
    const = lambda b, s: (0, 0)
    return pl.pallas_call(
        kern,
        grid=(g.ngrid, g.nst),
        in_specs=[
            pl.BlockSpec((g.rows, SEC), lambda b, s: (g.base_blk + b * g.nst + s, sec)),
            pl.BlockSpec((8, 256), const),
            pl.BlockSpec((128, 128), const),
            pl.BlockSpec((hk, 256), const),
            pl.BlockSpec((256, 256), const),
            pl.BlockSpec((grows, 128), const),
            pl.BlockSpec((g.nseq, 256, dk), lambda b, s: (b, 0, 0)),
            pl.BlockSpec(memory_space=pl.ANY),
        ],
        input_output_aliases={7: 0},
        out_specs=[
            pl.BlockSpec((g.rows, 256), lambda b, s: (g.base_blk + b * g.nst + s, 0)),
            pl.BlockSpec((g.nseq, 256, dk), lambda b, s: (b, 0, 0)),
        ],
        out_shape=[
            jax.ShapeDtypeStruct((nrows, 256), F32),
            jax.ShapeDtypeStruct((g.nb, 256, dk), F32),
        ],
        scratch_shapes=[pltpu.VMEM((grows, 128), F32)] * (g.lanes * (256 // grows)),
        compiler_params=_cparams("parallel", "arbitrary"),
    )(z, par, gup, okv, ovv, mask, st_in, o_prev)


def _rwkv_kernel(z_ref, mu_ref, par_ref, wup_ref, aup_ref, gup_ref, ovv_ref, hmask_ref, ehead_ref, mask_ref,
                 shift_ref, st_in_ref, o_prev_ref,
                 o_ref, nshift_ref, st_out_ref,
                 prev_scr, fix_scr, r_scr, lw_scr, k_scr, v_scr, kk_scr, ka_scr, g_scr, bonus_scr, y_scr, *st_scrs,
                 c, nsub, tv, nseq, rps, nst, pc):
    del o_prev_ref
    step = pl.program_id(1)

    def shifted_chunk(ci):
        r0 = ci * pc if isinstance(ci, int) else pl.multiple_of(ci * pc, 8)
        zr = z_ref[pl.ds(r0, pc), :]
        rolled = pltpu.roll(zr, 1, axis=0)
        if pc <= rps:
            cps = rps // pc
            seq = ci // cps
            rowi = lax.broadcasted_iota(jnp.int32, (pc, 1), 0)
            before = z_ref[pl.ds(pl.multiple_of(jnp.maximum(r0 - 8, 0), 8), 8), :][7:8, :]
            first = jnp.where(ci % cps == 0, prev_scr[pl.ds(seq, 1), :], before)
            prev = jnp.where(rowi == 0, first, rolled)
            prev_scr[pl.ds(seq, 1), :] = zr[pc - 1:pc, :]
            nshift_ref[seq] = zr[pc - 1:pc, :]
        else:
            fix_scr[...] = rolled
            for i in range(nseq):
                fix_scr[i * rps:i * rps + 1, :] = shift_ref[i]
                nshift_ref[i] = zr[i * rps + tv - 1:i * rps + tv, :]
            prev = fix_scr[...]
        return r0, zr, prev

    if pc <= rps:
        @pl.when(step == 0)
        def _():
            for i in range(nseq):
                prev_scr[i:i + 1, :] = shift_ref[i]

    def prep_chunk(ci, carry):
        r0, zr, prev = shifted_chunk(ci)
        rows = pl.ds(r0, pc)
        _rwkv_prep_rows(zr, prev, rows)
        return carry

    def _rwkv_prep_rows(zr, prev, rows):
        nrow = zr.shape[0]
        zs = zr + (prev - zr) * mu_ref[...]
        r = zs[:, 0:256]
        k = zs[:, 256:512]
        v = zs[:, 512:768]
        lo = zs[:, 768:1024]
        w0, a0, k_k, k_a, r_k = (par_ref[i:i + 1, :] for i in range(5))
        lo_wa, lo_g = lo[:, 0:128], lo[:, 128:256]
        wl = w0 + _dot(jnp.tanh(lo_wa).astype(BF16), wup_ref[0:128, :])
        wexp = -(jnp.maximum(-wl, 0.0) + jnp.log1p(jnp.exp(-jnp.abs(wl)))) - 0.5
        lw = -jnp.exp(wexp)
        a = jax.nn.sigmoid(a0 + _dot(lo_wa.astype(BF16), aup_ref[0:128, :]))
        g_scr[rows, :] = _dot(jax.nn.sigmoid(lo_g).astype(BF16), gup_ref[128:256, :])
        kkp = k * k_k
        nrm = jnp.sqrt(_seg_dot(kkp * kkp, ovv_ref[...], 2))
        kk = kkp / jnp.maximum(nrm, 1e-12)
        k2 = k * (1.0 + (a - 1.0) * k_a)
        bonus_scr[rows, :] = _seg_dot(r * k2 * r_k, ovv_ref[...], 3) * v
        ka = kk * a
        if tv < rps:
            okrow = lax.rem(lax.broadcasted_iota(jnp.int32, (nrow, 1), 0), rps) < tv
            lw, k2, v, kk, ka = (jnp.where(okrow, t_, 0.0) for t_ in (lw, k2, v, kk, ka))
        r_scr[rows, :] = r
        lw_scr[rows, :] = lw
        k_scr[rows, :] = k2
        v_scr[rows, :] = v
        kk_scr[rows, :] = kk
        ka_scr[rows, :] = ka

    nchunk = (nseq * rps) // pc
    if nchunk == 1:
        prep_chunk(0, 0)
    else:
        lax.fori_loop(0, nchunk, prep_chunk, 0)

    hp = 2
    ng = N_HEADS // hp
    gw = hp * 64
    rowc = lax.broadcasted_iota(jnp.int32, (c, 1), 0)
    src2 = lax.broadcasted_iota(jnp.int32, (1, 2 * hp * c), 1) & (c - 1)
    src = src2[:, 0:hp * c]
    strict = rowc > src
    incl = rowc >= src

    def sub_blocks(lane_scrs, row0s):
        stage_state(*stage_free(lane_scrs, row0s))

    def stage_free(lane_scrs, row0s):
        hm = hmask_ref[...]
        tile = lambda t_: jnp.concatenate([t_] * hp, axis=0) * hm
        nc = hp * c
        nsolve = min(tv, c - 1)
        units, cx = [], {}
        for i, r0 in enumerate(row0s):
            rows = pl.ds(r0, c)
            lw_, r_, k_, v_, kk_, ka_ = (s_[rows, :] for s_ in (lw_scr, r_scr, k_scr, v_scr, kk_scr, ka_scr))
            gam = _cumsum_rows(lw_, rowc)
            gl = gam[c - 1:c, :]
            ginv = jnp.exp(-gam)
            gend = jnp.exp(gl - gam)
            lh, ll = _split2(jnp.concatenate([kk_ * jnp.exp(gam - lw_), r_ * jnp.exp(gam)], axis=0))
            a_, kq_ = ka_ * ginv, k_ * ginv
            ae_, ke_ = ka_ * gend, k_ * gend
            for p in range(ng):
                ls = slice(p * gw, (p + 1) * gw)
                rh, rl = _split2(jnp.concatenate([tile(a_[:, ls]), tile(kq_[:, ls])], axis=0))
                units.append((i, p))
                cx[i, p] = dict(rows=rows, ls=ls, v=v_[:, ls], egl=jnp.exp(gl[:, ls]), lh=lh[:, ls], ll=ll[:, ls],
                                rh=rh, rl=rl, vbig=tile(v_[:, ls]).astype(BF16), scr=lane_scrs[i][p],
                                end=jnp.concatenate([ae_[:, ls], ke_[:, ls]], axis=0).astype(BF16))
        g2 = {u_: _dot_nt(jnp.concatenate([cx[u_]["lh"], cx[u_]["ll"]], axis=0), cx[u_]["rh"]) for u_ in units}
        g3 = {u_: _dot_nt(cx[u_]["lh"], cx[u_]["rl"]) for u_ in units}
        m_ab, m_ra, m_rk, bkv = {}, {}, {}, {}
        for u_ in units:
            gm = g2[u_][0:2 * c] + g2[u_][2 * c:4 * c] + g3[u_]
            m_ab[u_] = jnp.where(strict, gm[0:c, 0:nc], 0.0)
            m_ra[u_] = jnp.where(incl, gm[c:2 * c, 0:nc], 0.0)
            m_rk[u_] = jnp.where(incl, gm[c:2 * c, nc:2 * nc], 0.0)
            bkv[u_] = _dot(jnp.where(strict, gm[0:c, nc:2 * nc], 0.0).astype(BF16), cx[u_]["vbig"])
        m2 = {u_: jnp.concatenate(_split2(m_ab[u_]), axis=1) for u_ in units}
        cols = _dot(jnp.concatenate([jnp.where(src2 == s, m2[u_], jnp.zeros_like(m2[u_]))
                                     for u_ in units for s in range(nsolve)], axis=0), ehead_ref[...])
        return units, cx, m_ra, m_rk, bkv, cols

    def stage_state(units, cx, m_ra, m_rk, bkv, cols):
        nsolve = min(tv, c - 1)
        tile = lambda t_: jnp.concatenate([t_] * hp, axis=0) * hmask_ref[...]
        x0 = {u_: _dot_nt(cx[u_]["lh"], cx[u_]["scr"][...].astype(BF16)) for u_ in units}
        usol = {u_: x0[u_][0:c] + bkv[u_] for u_ in units}
        for s in range(nsolve):
            for n_, u_ in enumerate(units):
                o_ = (n_ * nsolve + s) * c
                usol[u_] = usol[u_] - cols[o_:o_ + c] * usol[u_][s:s + 1, :]
        ys = {u_: _dot(jnp.concatenate([-m_ra[u_], m_rk[u_]], axis=1).astype(BF16),
                       jnp.concatenate([tile(usol[u_]).astype(BF16), cx[u_]["vbig"]], axis=0)) for u_ in units}
        upd = {u_: _dot_tn(jnp.concatenate([-usol[u_], cx[u_]["v"]], axis=0).astype(BF16), cx[u_]["end"])
               for u_ in units}
        for u_ in units:
            x = cx[u_]
            y_scr[x["rows"], x["ls"]] = x0[u_][c:2 * c] + ys[u_]
            x["scr"][...] = x["scr"][...] * x["egl"] + upd[u_] * mask_ref[...]

    def load_state(scrs, seq):
        for p, scr in enumerate(scrs):
            scr[...] = jnp.concatenate([st_in_ref[seq, p * gw:(p + 1) * gw, :]] * hp, axis=1) * mask_ref[...]

    def store_state(scrs, seq):
        for p, scr in enumerate(scrs):
            st = scr[...]
            st_out_ref[seq, p * gw:(p + 1) * gw, :] = st[:, 0:64] + st[:, 64:128]

    lane_scrs = [st_scrs[i * ng:(i + 1) * ng] for i in range(len(st_scrs) // ng)]
    _run_lanes(lane_scrs, nseq, rps, nsub, c, nst, load_state, store_state, sub_blocks)

    def finish_chunk(ci, carry):
        rows = pl.ds(ci * pc if isinstance(ci, int) else pl.multiple_of(ci * pc, 8), pc)
        y = y_scr[rows, :]
        mean = _seg_dot(y, ovv_ref[...], 3) * (1.0 / HEAD_V)
        d = y - mean
        var = _seg_dot(d * d, ovv_ref[...], 2) * (1.0 / HEAD_V)
        ln = d * lax.rsqrt(var + RWKV_LN_EPS) * par_ref[5:6, :] + par_ref[6:7, :]
        o_ref[rows, :] = (ln + bonus_scr[rows, :]) * g_scr[rows, :]
        return carry

    if nchunk == 1:
        finish_chunk(0, 0)
    else:
        lax.fori_loop(0, nchunk, finish_chunk, 0)


def _rwkv_call(z, mu, par, wup, aup, gup, ovv, hmask, ehead, mask, shift, st_in, o_prev, *, nrows, g):
    rblk = g.rows
    nsteps = g.nst
    base = g.base_blk
    pc = g.pc
    kern = functools.partial(_rwkv_kernel, c=g.c, nsub=g.nsub, tv=g.tv, nseq=g.nseq, rps=g.rps, nst=g.nst, pc=pc)
    const = lambda b, s: (0, 0)
    return pl.pallas_call(
        kern,
        grid=(g.ngrid, nsteps),
        in_specs=[
            pl.BlockSpec((rblk, SEC), lambda b, s: (base + b * nsteps + s, 1)),
            pl.BlockSpec((1, SEC), const),
            pl.BlockSpec((8, 256), const),
            pl.BlockSpec((256, 256), const),
            pl.BlockSpec((256, 256), const),
            pl.BlockSpec((256, 256), const),
            pl.BlockSpec((256, 256), const),
            pl.BlockSpec((2 * g.c, 128), const),
            pl.BlockSpec((4 * g.c, 128), const),
            pl.BlockSpec((128, 128), const),
            pl.BlockSpec((g.nseq, 1, SEC), lambda b, s: (b, 0, 0)),
            pl.BlockSpec((g.nseq, 256, 64), lambda b, s: (b, 0, 0)),
            pl.BlockSpec(memory_space=pl.ANY),
        ],
        input_output_aliases={12: 0},
        out_specs=[
            pl.BlockSpec((rblk, 256), lambda b, s: (base + b * nsteps + s, 0)),
            pl.BlockSpec((g.nseq, 1, SEC), lambda b, s: (b, 0, 0)),
            pl.BlockSpec((g.nseq, 256, 64), lambda b, s: (b, 0, 0)),
        ],
        out_shape=[
            jax.ShapeDtypeStruct((nrows, 256), F32),
            jax.ShapeDtypeStruct((g.nb, 1, SEC), F32),
            jax.ShapeDtypeStruct((g.nb, 256, 64), F32),
        ],
        scratch_shapes=[pltpu.VMEM((max(8, g.nseq), SEC), F32), pltpu.VMEM((pc if pc > g.rps else 8, SEC), F32)]
        + [pltpu.VMEM((rblk, 256), F32)] * 9 + [pltpu.VMEM((128, 128), F32)] * (2 * g.lanes),
        compiler_params=_cparams("parallel", "arbitrary"),
    )(z, mu, par, wup, aup, gup, ovv, hmask, ehead, mask, shift, st_in, o_prev)


def _mlstm_kernel(z_ref, par_ref, bcol_ref, cw_ref, sel_ref, ovv_ref, cm_in, nm_in, mm_in, cv_in, o_prev_ref,
                  o_ref, cm_out, nm_out, mm_out, cv_out,
                  *scratch, chunks, nseq, rps, nst, pc):
    del o_prev_ref
    lanes = scratch[0].shape[0]

    def chunk_gates(scr, row0, c, tv):
        xbuf, cm_scr, nm_scr, mm_scr = scr
        rows = pl.ds(row0, c)
        mqk = z_ref[rows, 0:256]
        mv = z_ref[rows, 256:512]
        gi = z_ref[rows, 512:640]
        mo = z_ref[rows, 640:896]
        xbuf[8:8 + c, :] = mqk
        conv = par_ref[0:1, :] + xbuf[5:5 + c, :] * cw_ref[0:1, :]
        for j in range(1, CONV_W):
            conv = conv + xbuf[5 + j:5 + j + c, :] * cw_ref[j:j + 1, :]
        xbuf[5:8, :] = xbuf[8 + tv - 3:8 + tv, :]
        act = _silu(conv)
        q = act[:, 0:128]
        k = act[:, 128:256] * (MLSTM_DK ** -0.5)

        rowi = lax.broadcasted_iota(jnp.int32, (c, 1), 0)
        coli = lax.broadcasted_iota(jnp.int32, (1, c), 1)
        gcol = gi + par_ref[1:2, 0:128]
        lfc = _log_sigmoid(gcol)
        graw = None
        for part in _split3(gi):
            t_ = _dot_nt(sel_ref[...], part)
            graw = t_ if graw is None else graw + t_
        grow = graw + bcol_ref[...]
        lfr = _log_sigmoid(grow)
        if tv < c:
            gcol = jnp.where(rowi < tv, gcol, -jnp.inf)
            lfc = jnp.where(rowi < tv, lfc, 0.0)
            grow = jnp.where(coli < tv, grow, -jnp.inf)
            lfr = jnp.where(coli < tv, lfr, 0.0)
        b_col = _seg_dot_l(_tri(c).astype(BF16), lfc, 3)
        b_row = _seg_dot(lfr, _tri(c, upper=True).astype(BF16), 3)
        return dict(rows=rows, q=q, k=k, mv=mv, mo=mo, gcol=gcol, grow=grow, b_col=b_col, b_row=b_row,
                    mm=mm_scr[...], nm=nm_scr[...], cm=cm_scr[...])

    def chunk(scrs, row0s, c, tv):
        ctx = [chunk_gates(scr, row0, c, tv) for scr, row0 in zip(scrs, row0s)]
        causal = _tri(c)
        lane = lax.broadcasted_iota(jnp.int32, (1, 128), 1)
        items = [(l, h) for l in range(len(ctx)) for h in range(N_HEADS)]
        ks = lambda h: slice(h * MLSTM_DK, (h + 1) * MLSTM_DK)
        vs = lambda h: slice(h * HEAD_V, (h + 1) * HEAD_V)
        qh = {(l, h): ctx[l]["q"][:, ks(h)] for l, h in items}
        kh = {(l, h): ctx[l]["k"][:, ks(h)] for l, h in items}
        vh = {(l, h): ctx[l]["mv"][:, vs(h)].astype(BF16) for l, h in items}
        cmh = {(l, h): ctx[l]["cm"][ks(h), :] for l, h in items}
        nh = {(l, h): ctx[l]["nm"][0:1, ks(h)] for l, h in items}
        qk_raw = {it: _dot_nt(qh[it].astype(BF16), kh[it].astype(BF16)) for it in items}
        q_cm = {it: _dot(qh[it].astype(BF16), cmh[it].astype(BF16)) for it in items}
        bc, ic, mprev, m_t, sc, qk = {}, {}, {}, {}, {}, {}
        for it in items:
            l, h = it
            bc[it] = ctx[l]["b_col"][:, 4 + h:5 + h]
            br = ctx[l]["b_row"][4 + h:5 + h, :]
            ir = ctx[l]["grow"][h:h + 1, :]
            ic[it] = ctx[l]["gcol"][:, h:h + 1]
            dmat = jnp.where(causal, bc[it] - br + ir, -jnp.inf)
            mprev[it] = ctx[l]["mm"][0:1, h:h + 1]
            inter = bc[it] + mprev[it]
            m_t[it] = jnp.maximum(inter, jnp.max(dmat, axis=1, keepdims=True))
            sc[it] = jnp.exp(inter - m_t[it])
            qk[it] = qk_raw[it] * jnp.exp(dmat - m_t[it])
        num = {it: _dot(qk[it].astype(BF16), vh[it]) + sc[it] * q_cm[it] for it in items}
        qk_sum = _seg_dot(jnp.concatenate([qk[it] for it in items], axis=0), jnp.ones((c, HEAD_V), BF16), 2)
        qn_sum = _seg_dot(jnp.concatenate([qh[it] * nh[it] for it in items], axis=0),
                          jnp.ones((MLSTM_DK, HEAD_V), BF16), 2)
        kw, sl, m_new, hn = {}, {}, {}, {}
        for n_, it in enumerate(items):
            den = qk_sum[n_ * c:(n_ + 1) * c] + sc[it] * qn_sum[n_ * c:(n_ + 1) * c]
            hc = num[it] / jnp.maximum(jnp.abs(den), jnp.exp(-m_t[it]))
            m_new[it] = m_t[it][tv - 1:tv, :]
            bl = bc[it][tv - 1:tv, :]
            wl = jnp.exp(bl - bc[it] + ic[it] - m_new[it])
            sl[it] = jnp.exp(bl + mprev[it] - m_new[it])
            kw[it] = kh[it] * wl
            hn[it] = hc
        kv = {it: _dot_tn(kw[it].astype(BF16), vh[it]) for it in items}
        for l, (scr, cx) in enumerate(zip(scrs, ctx)):
            xbuf, cm_scr, nm_scr, mm_scr = scr
            heads = [(l, h) for h in range(N_HEADS)]
            cm_scr[...] = jnp.concatenate([sl[it] * cmh[it] + kv[it] for it in heads], axis=0)
            nm_scr[...] = jnp.concatenate([sl[it] * nh[it] + jnp.sum(kw[it], axis=0, keepdims=True)
                                           for it in heads], axis=1)
            mm_new = cx["mm"]
            for it in heads:
                mm_new = jnp.where(lane == it[1], m_new[it], mm_new)
            mm_scr[...] = mm_new
            o_ref[cx["rows"], :] = jnp.concatenate([hn[it] for it in heads], axis=1)

    def load_state(scr, seq):
        xbuf, cm_scr, nm_scr, mm_scr = scr
        cm_scr[...] = cm_in[seq]
        nm_scr[...] = nm_in[seq]
        mm_scr[...] = mm_in[seq]
        xbuf[0:8, :] = cv_in[seq]

    def store_state(scr, seq):
        xbuf, cm_scr, nm_scr, mm_scr = scr
        cm_out[seq] = cm_scr[...]
        nm_out[seq] = nm_scr[...]
        mm_out[seq] = mm_scr[...]
        cv_out[seq] = xbuf[0:8, :]

    assert nseq == lanes or nst == 1
    step = pl.program_id(1)

    def lane_scr(u):
        return tuple(s_.at[u] for s_ in scratch)

    def run_chunks(grp):
        for start, c, tv, count in chunks:
            def body(ci, carry2, start=start, c=c, tv=tv):
                firsts = [seq * rps + start + ci * c for _, seq in grp]
                chunk([scr for scr, _ in grp],
                      [f if isinstance(f, int) else pl.multiple_of(f, 8) for f in firsts], c, tv)
                return carry2

            if count == 1:
                body(0, 0)
            else:
                lax.fori_loop(0, count, body, 0)

    def seq_group(sg, carry):
        def lane(u, seq):
            scr = lane_scr(u)
            if nst == 1:
                load_state(scr, seq)
            else:
                pl.when(step == 0)(functools.partial(load_state, scr, seq))
            return scr

        def finish(scr, seq):
            if nst == 1:
                store_state(scr, seq)
            else:
                pl.when(step == nst - 1)(functools.partial(store_state, scr, seq))

        if max(c for _, c, _, _ in chunks) <= 16:
            grp = [(lane(u, sg * lanes + u), sg * lanes + u) for u in range(lanes)]
            run_chunks(grp)
            for scr, seq in grp:
                finish(scr, seq)
        else:
            def one_lane(u, carry2):
                seq = sg * lanes + u
                scr = lane(u, seq)
                run_chunks([(scr, seq)])
                finish(scr, seq)
                return carry2

            lax.fori_loop(0, lanes, one_lane, 0)
        return carry

    if nseq == lanes:
        seq_group(0, 0)
    else:
        lax.fori_loop(0, nseq // lanes, seq_group, 0)

    nchunk = (nseq * rps) // pc

    def finish_chunk(ci, carry):
        rows = pl.ds(ci * pc if isinstance(ci, int) else pl.multiple_of(ci * pc, 8), pc)
        o = o_ref[rows, :]
        ms = _seg_dot(o * o, ovv_ref[...], 2) * (1.0 / HEAD_V)
        o_ref[rows, :] = o * lax.rsqrt(ms + EPS) * par_ref[2:3, :] * jax.nn.sigmoid(z_ref[rows, 640:896])
        return carry

    if nchunk == 1:
        finish_chunk(0, 0)
    else:
        lax.fori_loop(0, nchunk, finish_chunk, 0)


def _mlstm_call(z, par, bcol, cw, sel, ovv, cm, nm, mm, cv, o_prev, *, nrows, g):
    if g.full:
        cmain = max(d for d in (MLSTM_CHUNK, 64, 32, 16, 8) if d <= g.rps)
        lead = g.rps % cmain
        chunks = ([(0, lead, lead, 1)] if lead else []) + [(lead, cmain, cmain, g.rps // cmain)]
        assert lead % 8 == 0
    else:
        chunks = [(0, g.rps, g.tv, 1)]
    cmax = max(c for _, c, _, _ in chunks)
    kern = functools.partial(_mlstm_kernel, chunks=tuple(chunks), nseq=g.nseq, rps=g.rps, nst=g.nst, pc=g.pc)
    const = lambda b, s: (0, 0)
    st3 = lambda shp: pl.BlockSpec((g.nseq,) + shp, lambda b, s: (b, 0, 0))
    return pl.pallas_call(
        kern,
        grid=(g.ngrid, g.nst),
        in_specs=[
            pl.BlockSpec((g.rows, SEC), lambda b, s: (g.base_blk + b * g.nst + s, 3)),
            pl.BlockSpec((8, 256), const),
            pl.BlockSpec((8, 1), const),
            pl.BlockSpec((8, 256), const),
            pl.BlockSpec((8, 128), const),
            pl.BlockSpec((256, 256), const),
            st3((128, 64)), st3((1, 128)), st3((1, 128)), st3((8, 256)),
            pl.BlockSpec(memory_space=pl.ANY),
        ],
        input_output_aliases={10: 0},
        out_specs=[
            pl.BlockSpec((g.rows, 256), lambda b, s: (g.base_blk + b * g.nst + s, 0)),
            st3((128, 64)), st3((1, 128)), st3((1, 128)), st3((8, 256)),
        ],
        out_shape=[
            jax.ShapeDtypeStruct((nrows, 256), F32),
            jax.ShapeDtypeStruct((g.nb, 128, 64), F32),
            jax.ShapeDtypeStruct((g.nb, 1, 128), F32),
            jax.ShapeDtypeStruct((g.nb, 1, 128), F32),
            jax.ShapeDtypeStruct((g.nb, 8, 256), F32),
        ],
        scratch_shapes=[
            pltpu.VMEM((g.lanes, 8 + cmax, 256), F32),
            pltpu.VMEM((g.lanes, 128, 64), F32),
            pltpu.VMEM((g.lanes, 1, 128), F32),
            pltpu.VMEM((g.lanes, 1, 128), F32),
        ],
        compiler_params=_cparams("parallel", "arbitrary"),
    )(z, par, bcol, cw, sel, ovv, cm, nm, mm, cv, o_prev)


def _merge_kernel(o0_ref, o1_ref, o2_ref, o3_ref, x_ref, g_ref, wg_ref, wb_ref, wout_ref, out_ref):
    h = _rms_bf16(x_ref[...], g_ref[...])
    mix = None
    for n, o_ref in enumerate((o0_ref, o1_ref, o2_ref, o3_ref)):
        gate = jax.nn.sigmoid(_dot(h, wg_ref[:, n * D_MODEL:(n + 1) * D_MODEL]))
        t = _dot(o_ref[...].astype(BF16), wb_ref[n]) * gate
        mix = t if mix is None else mix + t
    out_ref[...] = x_ref[...] + _dot(mix.astype(BF16), wout_ref[...])


def _merge_call(outs, x, g, wg, wb, wout, tm, o_tile):
    n = x.shape[0]
    o_spec = pl.BlockSpec((tm, 256), lambda i: (o_tile(i), 0))
    return pl.pallas_call(
        _merge_kernel,
        grid=(n // tm,),
        in_specs=[
            o_spec, o_spec, o_spec, o_spec,
            pl.BlockSpec((tm, D_MODEL), lambda i: (i, 0)),
            pl.BlockSpec((1, D_MODEL), lambda i: (0, 0)),
            pl.BlockSpec((D_MODEL, 4 * D_MODEL), lambda i: (0, 0)),
            pl.BlockSpec((4, 256, D_MODEL), lambda i: (0, 0, 0)),
            pl.BlockSpec((D_MODEL, D_MODEL), lambda i: (0, 0)),
        ],
        out_specs=pl.BlockSpec((tm, D_MODEL), lambda i: (i, 0)),
        out_shape=jax.ShapeDtypeStruct((n, D_MODEL), F32),
        compiler_params=_cparams("parallel"),
    )(*outs, x, g, wg, wb, wout)


def _mlp_kernel(x_ref, g_ref, gf_ref, wup_ref, wdn_ref, out_ref, *, final):
    x = x_ref[...]
    h = _rms_bf16(x, g_ref[...])
    y = x
    for j in range(D_FF // D_MODEL):
        cols = slice(j * D_MODEL, (j + 1) * D_MODEL)
        u = jnp.maximum(_dot(h, wup_ref[:, cols]), 0.0)
        y = y + _dot((u * u).astype(BF16), wdn_ref[cols, :])
    if final:
        y = y * lax.rsqrt(jnp.mean(y * y, axis=-1, keepdims=True) + EPS) * gf_ref[...]
    out_ref[...] = y


def _mlp_call(x, g, gf, wup, wdn, tm, final):
    n = x.shape[0]
    return pl.pallas_call(
        functools.partial(_mlp_kernel, final=final),
        grid=(n // tm,),
        in_specs=[
            pl.BlockSpec((tm, D_MODEL), lambda i: (i, 0)),
            pl.BlockSpec((1, D_MODEL), lambda i: (0, 0)),
            pl.BlockSpec((1, D_MODEL), lambda i: (0, 0)),
            pl.BlockSpec((D_MODEL, D_FF), lambda i: (0, 0)),
            pl.BlockSpec((D_FF, D_MODEL), lambda i: (0, 0)),
        ],
        out_specs=pl.BlockSpec((tm, D_MODEL), lambda i: (i, 0)),
        out_shape=jax.ShapeDtypeStruct((n, D_MODEL), F32),
        compiler_params=_cparams("parallel"),
    )(x, g, gf, wup, wdn)


def _seg_ones(rows_per_head, cols_per_head):
    r = np.arange(N_HEADS * rows_per_head)[:, None] // rows_per_head
    c = np.arange(N_HEADS * cols_per_head)[None, :] // cols_per_head
    return (r == c).astype(np.float32)


def _rows(*vecs, width=256, nrows=8):
    rows = []
    for v in vecs:
        v = jnp.asarray(v, F32).reshape(-1)
        rows.append(jnp.pad(v, (0, width - v.shape[0])))
    rows += [jnp.zeros((width,), F32)] * (nrows - len(rows))
    return jnp.stack(rows)


def _layout_w_in(w):
    d = w.shape[0]
    zeros = lambda n: jnp.zeros((d, n), w.dtype)
    gl0, ml0, gt0 = 2048, 2832, 3608
    parts = [
        w[:, 0:2048],
        w[:, gl0:gl0 + 512], w[:, gl0 + 512:gl0 + 528], zeros(112), w[:, gl0 + 528:gl0 + 784], zeros(128),
        w[:, ml0:ml0 + 512], w[:, ml0 + 512:ml0 + 520], zeros(120), w[:, ml0 + 520:ml0 + 776], zeros(128),
    ]
    wz = jnp.concatenate(parts, axis=1)
    assert wz.shape[1] == Z_COLS
    return wz.astype(BF16), w[:, gt0:gt0 + 4 * D_MODEL].astype(BF16)


class _Group:
    def __init__(self, nb, trow, tv, row0, max_seq):
        self.nb, self.trow, self.tv, self.row0 = nb, trow, tv, row0
        self.full = tv == trow
        self.c = SUB if self.full else trow
        self.tvs = min(tv, self.c)
        if self.full:
            nblk = trow // self.c
            self.rps = self.c * max(d for d in range(1, min(nblk, 64) + 1) if nblk % d == 0)
        else:
            self.rps = trow
        self.nst = trow // self.rps
        self.nsub = self.rps // self.c
        self.nseq = max(d for d in range(1, max_seq + 1) if nb % d == 0 and row0 % (d * self.rps) == 0)
        self.lanes = (8 if not self.full and self.nseq % 8 == 0 else 4 if self.nseq % 4 == 0
                      else 2 if self.nseq % 2 == 0 else 1)
        assert self.nseq == self.lanes or self.nst == 1
        self.rows = self.nseq * self.rps
        self.ngrid = nb // self.nseq
        self.base_blk = row0 // self.rows
        self.pc = _pick_tile(self.rps, 384) if self.rps >= 64 else self.rows


def kernel(x_prompt, x_sample, state_hgrn, state_rwkv, state_rwkv_shift, state_gla, state_mlstm_c, state_mlstm_n, state_mlstm_m, state_mlstm_conv, meta_tokens, norm_mix, norm_mlp, norm_final, w_in, hgrn_lb, hgrn_norm, rwkv_mu, rwkv_w0, rwkv_w_up, rwkv_a0, rwkv_a_up, rwkv_g_up, rwkv_k_k, rwkv_k_a, rwkv_r_k, rwkv_ln_w, rwkv_ln_b, gla_gate_up, gla_gate_b, gla_norm, mlstm_conv_w, mlstm_conv_b, mlstm_i_b, mlstm_f_b, mlstm_norm, w_branch, w_out, w_up, w_down):
    depth = w_in.shape[0]
    bp, seq, _ = x_prompt.shape
    bs, dseq, _ = x_sample.shape
    tp = N_META + seq
    assert tp % SUB == 0 and CONV_W - 1 <= dseq <= SAMPLE_PAD
    dt = x_prompt.dtype

    gp = _Group(bp, tp, tp, 0, 4)
    gs = _Group(bs, SAMPLE_PAD, dseq, bp * tp, 16)
    tm = gp.rps
    xs = jnp.pad(x_sample, ((0, 0), (0, SAMPLE_PAD - dseq), (0, 0))).reshape(bs * SAMPLE_PAD, D_MODEL)
    n_real = bp * tp + bs * SAMPLE_PAD
    n_rows = -(-n_real // tm) * tm
    pieces = []
    for b in range(bp):
        pieces += [meta_tokens.astype(dt), x_prompt[b]]
    x = jnp.concatenate(pieces + [xs, jnp.zeros((n_rows - n_real, D_MODEL), dt)], axis=0).astype(F32)

    def mixer_tile(i):
        b, s = i // gp.nst, i % gp.nst
        p = (b // gp.nseq) * (gp.nst * gp.nseq) + s * gp.nseq + b % gp.nseq
        return jnp.where(i < bp * gp.nst, p, i)

    tm_mlp = tm // 2 if tm % 16 == 0 and tm >= 256 else tm
    merge_split = 2 if tm % 16 == 0 and tm >= 256 else 1

    ones_vv = jnp.asarray(_seg_ones(64, 64), BF16)
    ones_gl = jnp.asarray(_seg_ones(GLA_DK, 64), BF16)
    mask_gl = jnp.asarray(_seg_ones(64, GLA_DK), F32)
    sel8 = jnp.asarray(np.eye(8, 128, dtype=np.float32), BF16)
    hmask = {g.c: jnp.asarray(np.kron(np.eye(2), np.ones((g.c, 64))), F32) for g in (gp, gs)}
    ehead = {c_: jnp.concatenate([m_, m_], axis=0).astype(BF16) for c_, m_ in hmask.items()}
    mask_pair = jnp.asarray(np.kron(np.eye(2), np.ones((64, 64))), F32)

    lb_cs = jnp.cumsum(jax.nn.softmax(hgrn_lb.astype(F32), axis=0), axis=0)
    lb_all = lb_cs - lb_cs[:1]

    def init_states(states, nb, zero):
        s_hg, s_rw, shift, s_gl, c_ml, n_ml, m_ml, conv = states
        if zero:
            z = lambda *shp: jnp.zeros((depth, nb) + shp, F32)
            return dict(hg=z(256, 64), rw=z(256, 64), shift=z(1, SEC), gl=z(256, GLA_DK), cm=z(128, 64),
                        nm=z(1, 128), mm=z(1, 128), cv=z(8, 256))
        f = lambda a: a.astype(F32)
        return dict(
            hg=f(s_hg).transpose(0, 1, 2, 4, 3).reshape(depth, nb, 256, 64),
            rw=f(s_rw).reshape(depth, nb, 256, 64),
            shift=f(shift).reshape(depth, nb, 1, SEC),
            gl=f(s_gl).transpose(0, 1, 2, 4, 3).reshape(depth, nb, 256, GLA_DK),
            cm=f(c_ml).reshape(depth, nb, 128, 64),
            nm=f(n_ml).reshape(depth, nb, 1, 128),
            mm=jnp.pad(f(m_ml), ((0, 0), (0, 0), (0, 124))).reshape(depth, nb, 1, 128),
            cv=jnp.pad(f(conv), ((0, 0), (0, 0), (5, 0), (0, 0))),
        )

    sample_states = (state_hgrn, state_rwkv, state_rwkv_shift, state_gla,
                     state_mlstm_c, state_mlstm_n, state_mlstm_m, state_mlstm_conv)
    st_p = init_states(sample_states, bp, True)
    st_s = init_states(sample_states, bs, False)
    new_p = {k: [] for k in st_p}
    new_s = {k: [] for k in st_s}

    for l in range(depth):
        wz, wg = _layout_w_in(w_in[l])
        g_mix = norm_mix[l].reshape(1, D_MODEL).astype(F32)
        z = _in_proj(x, g_mix, wz, tm, mixer_tile)

        lb = lb_all[l]
        par_hg = _rows(jnp.log(lb), jnp.log1p(-lb), 1.0 - lb, hgrn_norm[l])
        par_gl = _rows(gla_gate_b[l], jnp.zeros((1,)), jnp.zeros((1,)), gla_norm[l])
        gup_gl = jnp.pad(gla_gate_up[l], ((0, 128 - gla_gate_up.shape[1]), (0, 0))).astype(BF16)
        gup_dummy = jnp.zeros((128, 128), BF16)
        par_rw = _rows(rwkv_w0[l], rwkv_a0[l], rwkv_k_k[l], rwkv_k_a[l], rwkv_r_k[l], rwkv_ln_w[l], rwkv_ln_b[l])
        nw, na = rwkv_w_up.shape[1], rwkv_a_up.shape[1]
        assert nw + na == 128 and rwkv_g_up.shape[1] == 128
        wup_p = jnp.pad(rwkv_w_up[l], ((0, 256 - nw), (0, 0))).astype(BF16)
        aup_p = jnp.pad(rwkv_a_up[l], ((nw, 256 - nw - na), (0, 0))).astype(BF16)
        gup_p = jnp.pad(rwkv_g_up[l], ((nw + na, 0), (0, 0))).astype(BF16)
        mu = rwkv_mu[l].reshape(1, SEC).astype(F32)
        gate_b = jnp.concatenate([mlstm_i_b[l], mlstm_f_b[l]]).astype(F32)
        par_ml = _rows(mlstm_conv_b[l], gate_b, mlstm_norm[l])
        bcol_ml = gate_b.reshape(8, 1)
        cw_ml = _rows(*[mlstm_conv_w[l, j] for j in range(CONV_W)])

        o_hg = o_rw = o_gl = o_ml = None
        for g, st, new in ((gp, st_p, new_p), (gs, st_s, new_s)):
            prev = lambda o: jnp.zeros((n_rows, 256), F32) if o is None else o
            o_hg, s_hg = _gla_call(z, 0, par_hg, gup_dummy, ones_vv, ones_vv, mask_pair, st["hg"][l], prev(o_hg),
                                   mode="hgrn", dk=64, nrows=n_rows, g=g)
            o_gl, s_gl = _gla_call(z, 2, par_gl, gup_gl, ones_gl, ones_vv, mask_gl, st["gl"][l], prev(o_gl),
                                   mode="gla", dk=GLA_DK, nrows=n_rows, g=g)
            o_rw, nshift, s_rw = _rwkv_call(z, mu, par_rw, wup_p, aup_p, gup_p, ones_vv, hmask[g.c], ehead[g.c],
                                            mask_pair, st["shift"][l], st["rw"][l], prev(o_rw), nrows=n_rows, g=g)
            o_ml, s_cm, s_nm, s_mm, s_cv = _mlstm_call(z, par_ml, bcol_ml, cw_ml, sel8, ones_vv, st["cm"][l], st["nm"][l],
                                                        st["mm"][l], st["cv"][l], prev(o_ml), nrows=n_rows, g=g)
            for key, val in (("hg", s_hg), ("rw", s_rw), ("shift", nshift), ("gl", s_gl), ("cm", s_cm),
                             ("nm", s_nm), ("mm", s_mm), ("cv", s_cv)):
                new[key].append(val)

        x = _merge_call((o_hg, o_rw, o_gl, o_ml), x, g_mix, wg, w_branch[l].astype(BF16), w_out[l].astype(BF16),
                        tm // merge_split, lambda i: mixer_tile(i // merge_split) * merge_split + i % merge_split)
        x = _mlp_call(x, norm_mlp[l].reshape(1, D_MODEL).astype(F32), norm_final.reshape(1, D_MODEL).astype(F32),
                      w_up[l].astype(BF16), w_down[l].astype(BF16), tm_mlp, final=(l == depth - 1))

    y_prompt = jnp.stack([x[b * tp + N_META:(b + 1) * tp] for b in range(bp)]).astype(dt)
    y_sample = x[gs.row0:gs.row0 + bs * SAMPLE_PAD].reshape(bs, SAMPLE_PAD, D_MODEL)[:, :dseq].astype(dt)

    def finish(new, nb):
        st = {k: jnp.stack(v) for k, v in new.items()}
        return (
            st["hg"].reshape(depth, nb, N_HEADS, 64, 64).transpose(0, 1, 2, 4, 3),
            st["rw"].reshape(depth, nb, N_HEADS, 64, 64),
            st["shift"].reshape(depth, nb, SEC),
            st["gl"].reshape(depth, nb, N_HEADS, 64, GLA_DK).transpose(0, 1, 2, 4, 3),
            st["cm"].reshape(depth, nb, N_HEADS, MLSTM_DK, 64),
            st["nm"].reshape(depth, nb, N_HEADS, MLSTM_DK),
            st["mm"].reshape(depth, nb, 128)[:, :, :N_HEADS],
            st["cv"][:, :, 5:8, :],
        )

    outs_p = tuple(a.astype(dt) for a in finish(new_p, bp))
    outs_s = tuple(a.astype(dt) for a in finish(new_s, bs))
    return (y_prompt, y_sample) + outs_p + outs_s
```
